```python
import math
import jax, jax.numpy as jnp
from jax import lax
import numpy as np

D_MODEL = 1024
BATCH = 2
SEQ = 16384
DEPTH = 2

GRID_W = 64
CTX_LEN = 256
EPS = 1e-6

GROUP_W = D_MODEL // 4

MLA_HEADS = 4
MLA_NOPE = 64
MLA_ROPE = 32
MLA_V = GROUP_W // MLA_HEADS
MLA_Q_LORA = 256
MLA_KV_LORA = 128
MLA_SCALE = (MLA_NOPE + MLA_ROPE) ** -0.5
ROPE_THETA = 10000.0
Q_BLOCK = 128

HY_W = GROUP_W
HY_EMB = 33
HY_FFN = 64
HY_TARGET = 1e-2
HY_FAST_PCT = 0.3
HY_SLOW_PCT = 1.5

GLA_HEADS = 4
GLA_DK = 32
GLA_DV = GROUP_W // GLA_HEADS
GLA_LR = 16
GLA_TAU = 16.0
GLA_CHUNK = 64

FN_W = GROUP_W

N_EXPERTS = 32
TOP_K = 4
D_FF = D_MODEL
SWIGLU_ALPHA = 1.702
SWIGLU_LIMIT = 7.0
MOE_BLOCK = 128

IN_WIDTHS = (MLA_Q_LORA, MLA_KV_LORA, MLA_ROPE,
             3 * HY_W,
             GLA_HEADS * GLA_DK, GLA_HEADS * GLA_DK, GLA_HEADS * GLA_DV, 2 * GLA_LR, GLA_HEADS * GLA_DV,
             FN_W)
D_IN = sum(IN_WIDTHS)

kernel_name = "hybrid_mla_hyena_gla_fnet_moe_dit"


def _rms(x, g):
    xf = x.astype(jnp.float32)
    y = xf * lax.rsqrt(jnp.mean(xf * xf, axis=-1, keepdims=True) + EPS)
    return (y * g.astype(jnp.float32)).astype(x.dtype)


def _modulate(x, g, shift, scale):
    return _rms(x, g) * (1.0 + scale) + shift


def _axial_rope(rows):
    row = jnp.repeat(jnp.arange(rows, dtype=jnp.float32), GRID_W)
    col = jnp.tile(jnp.arange(GRID_W, dtype=jnp.float32), rows)
    n_freq = MLA_ROPE // 4
    inv = ROPE_THETA ** (-jnp.arange(n_freq, dtype=jnp.float32) / n_freq)
    ang = jnp.concatenate([row[:, None] * inv, col[:, None] * inv], axis=-1)
    return jnp.cos(ang), jnp.sin(ang)


def _rope(x, cos, sin):
    half = MLA_ROPE // 2
    x1, x2 = x[..., :half], x[..., half:]
    c, s = cos[:, None, :], sin[:, None, :]
    return jnp.concatenate([x1 * c - x2 * s, x2 * c + x1 * s], axis=-1)


def _mla_qkv(cq, ckv, kr, p, cos, sin):
    B, L, _ = cq.shape
    q = (_rms(cq, p['mla_q_g']) @ p['mla_w_uq']).reshape(B, L, MLA_HEADS, MLA_NOPE + MLA_ROPE)
    kv = (_rms(ckv, p['mla_kv_g']) @ p['mla_w_ukv']).reshape(B, L, MLA_HEADS, MLA_NOPE + MLA_V)
    q_nope, q_rope = q[..., :MLA_NOPE], q[..., MLA_NOPE:]
    k_nope, v = kv[..., :MLA_NOPE], kv[..., MLA_NOPE:]
    kr = kr[:, :, None, :]
    if cos is not None:
        q_rope = _rope(q_rope, cos, sin)
        kr = _rope(kr, cos, sin)
    q = jnp.concatenate([q_nope, q_rope], axis=-1)
    k = jnp.concatenate([k_nope, jnp.broadcast_to(kr, (B, L, MLA_HEADS, MLA_ROPE)).astype(k_nope.dtype)], axis=-1)
    return q, k, v


def _attend(q, k, v):
    s = jnp.einsum('bqhd,bkhd->bhqk', q, k).astype(jnp.float32) * MLA_SCALE
    w = jax.nn.softmax(s, axis=-1).astype(v.dtype)
    return jnp.einsum('bhqk,bkhd->bqhd', w, v)


def _attend_blocked(q, k, v):
    B, L, H, dq = q.shape
    nb = L // Q_BLOCK
    qb = q.reshape(B, nb, Q_BLOCK, H, dq).transpose(1, 0, 2, 3, 4)
    ob = lax.map(lambda qi: _attend(qi, k, v), qb)
    return ob.transpose(1, 0, 2, 3, 4).reshape(B, L, H, v.shape[-1])


def _short_conv(u, w, b):
    up = jnp.pad(u, ((0, 0), (1, 1), (0, 0)))
    return up[:, :-2] * w[0] + up[:, 1:-1] * w[1] + up[:, 2:] * w[2] + b


def _hyena_filter(L, p):
    t = jnp.linspace(0.0, 1.0, L, dtype=jnp.float32)[:, None]
    bands = (HY_EMB - 1) // 2
    f = jnp.linspace(1e-4, bands - 1, bands, dtype=jnp.float32)
    w = 2.0 * math.pi * jnp.arange(L, dtype=jnp.float32) / L
    fw = w[:, None] * f[None, :]
    z = jnp.concatenate([t, jnp.cos(fw), -jnp.sin(fw)], axis=-1)
    h = jnp.sin(p['hy_freq'] * (z @ p['hy_w1'] + p['hy_b1']))
    h = jnp.sin(p['hy_freq'] * (h @ p['hy_w2'] + p['hy_b2']))
    h = (h @ p['hy_w3']).astype(jnp.float32).reshape(L, 2, HY_W)
    max_decay = math.log(HY_TARGET) / HY_FAST_PCT
    min_decay = math.log(HY_TARGET) / HY_SLOW_PCT
    deltas = jnp.abs(jnp.linspace(min_decay, max_decay, HY_W, dtype=jnp.float32))
    h = h * jnp.exp(-t * deltas)[:, None, :]
    h_f, h_b = h[:, 0], h[:, 1]
    k2 = jnp.concatenate([h_f, jnp.zeros((1, HY_W), jnp.float32), h_b[1:][::-1]], axis=0)
    return k2 * lax.rsqrt(jnp.sum(k2 * k2, axis=0, keepdims=True))


def _hyena(z, p):
    L = z.shape[1]
    u = _short_conv(z, p['hy_conv_w'], p['hy_conv_b'])
    x0, x1, v = jnp.split(u, 3, axis=-1)
    k2 = _hyena_filter(L, p)
    vg = (v * x1).astype(jnp.float32)
    y = jnp.fft.irfft(jnp.fft.rfft(vg, n=2 * L, axis=1) * jnp.fft.rfft(k2, axis=0)[None], n=2 * L, axis=1)[:, :L]
    y = y + vg * p['hy_bias'].astype(jnp.float32)
    return y.astype(z.dtype) * x0


def _gla_scan(q, k, v, g, s0):
    B2, L, H, dk = q.shape
    n = L // GLA_CHUNK
    def chunks(a):
        return a.astype(jnp.float32).reshape(B2, n, GLA_CHUNK, H, a.shape[-1]).swapaxes(0, 1)
    mask = jnp.tril(jnp.ones((GLA_CHUNK, GLA_CHUNK), bool))[None, :, :, None, None]
    def step(S, inp):
        qc, kc, vc, gc = inp
        b = jnp.cumsum(gc, axis=1)
        o_inter = jnp.einsum('bchk,bhkv->bchv', qc * jnp.exp(b), S)
        diff = jnp.where(mask, b[:, :, None] - b[:, None, :], -jnp.inf)
        A = jnp.einsum('bihk,bjhk,bijhk->bhij', qc, kc, jnp.exp(diff))
        o_intra = jnp.einsum('bhij,bjhv->bihv', A, vc)
        b_last = b[:, -1]
        S_new = S * jnp.exp(b_last)[..., None] + jnp.einsum('bchk,bchv->bhkv', kc * jnp.exp(b_last[:, None] - b), vc)
        return S_new, o_inter + o_intra
    S, o = lax.scan(step, s0, (chunks(q), chunks(k), chunks(v), chunks(g)))
    return o.swapaxes(0, 1).reshape(B2, L, H, v.shape[-1]), S


def _gla_bidir(q, k, v, glr, p, s0):
    B, L, _ = q.shape
    q = q.reshape(B, L, GLA_HEADS, GLA_DK) * (GLA_DK ** -0.5)
    k = k.reshape(B, L, GLA_HEADS, GLA_DK)
    v = v.reshape(B, L, GLA_HEADS, GLA_DV)
    gl = glr.reshape(B, L, 2, GLA_LR)
    gk = jnp.einsum('bldr,drk->bldk', gl, p['gla_gk_w']) + p['gla_gk_b']
    g = (jax.nn.log_sigmoid(gk.astype(jnp.float32)) / GLA_TAU).reshape(B, L, 2, GLA_HEADS, GLA_DK)
    both = lambda a: jnp.concatenate([a, a[:, ::-1]], axis=0)
    g2 = jnp.concatenate([g[:, :, 0], g[:, ::-1, 1]], axis=0)
    o, s = _gla_scan(both(q), both(k), both(v), g2, s0)
    return o[:B] + o[B:, ::-1], s


def _fnet(z, w):
    y = jnp.fft.fft2(z.astype(jnp.float32), axes=(1, 2), norm='ortho').real
    return y.astype(z.dtype) @ w


def _mix_out(a, hy, o_gla, og, fn, p):
    B, L = hy.shape[:2]
    gA, gB, gC, gD = jnp.split(p['mix_g'], 4)
    a = _rms(a, gA.reshape(MLA_HEADS, MLA_V)).reshape(B, L, GROUP_W)
    hy = _rms(hy, gB)
    o = _rms(o_gla, gC.reshape(GLA_HEADS, GLA_DV)).reshape(B, L, GROUP_W) * jax.nn.silu(og)
    fn = _rms(fn, gD)
    y = jnp.concatenate([a, hy, o.astype(hy.dtype), fn], axis=-1)
    return y @ p['w_out']


def _moe(h, p):
    T, D = h.shape
    logits = (h @ p['router_w']).astype(jnp.float32) + p['router_b'].astype(jnp.float32)
    top_v, top_i = lax.top_k(logits, TOP_K)
    gates = jax.nn.softmax(top_v, axis=-1)
    n_assign = T * TOP_K
    e_flat = top_i.reshape(-1)
    tok_flat = jnp.arange(n_assign, dtype=jnp.int32) // TOP_K
    w_flat = gates.reshape(-1)
    order = jnp.argsort(e_flat)
    e_s, tok_s, w_s = e_flat[order], tok_flat[order], w_flat[order]
    counts = jnp.bincount(e_flat, length=N_EXPERTS)
    start = jnp.cumsum(counts) - counts
    padded = (counts + MOE_BLOCK - 1) // MOE_BLOCK * MOE_BLOCK
    pend = jnp.cumsum(padded)
    pstart = pend - padded
    dest = pstart[e_s] + (jnp.arange(n_assign, dtype=jnp.int32) - start[e_s])
    n_rows = ((n_assign + MOE_BLOCK - 1) // MOE_BLOCK + N_EXPERTS) * MOE_BLOCK
    n_blocks = n_rows // MOE_BLOCK
    row_tok = jnp.full((n_rows,), T, jnp.int32).at[dest].set(tok_s)
    row_w = jnp.zeros((n_rows,), jnp.float32).at[dest].set(w_s)
    blk_e = jnp.minimum(jnp.searchsorted(pend, jnp.arange(n_blocks, dtype=pend.dtype) * MOE_BLOCK, side='right'),
                        N_EXPERTS - 1).astype(jnp.int32)
    h_pad = jnp.concatenate([h, jnp.zeros((1, D), h.dtype)], axis=0)
    def expert_block(args):
        rows, ei = args
        xi = h_pad[rows]
        gu = xi @ p['moe_w_gu'][ei] + p['moe_b_gu'][ei]
        gate = jnp.minimum(gu[:, ::2], SWIGLU_LIMIT)
        up = jnp.clip(gu[:, 1::2], -SWIGLU_LIMIT, SWIGLU_LIMIT)
        glu = gate * jax.nn.sigmoid(gate * SWIGLU_ALPHA)
        return ((up + 1.0) * glu) @ p['moe_w_down'][ei] + p['moe_b_down'][ei]
    yb = lax.map(expert_block, (row_tok.reshape(n_blocks, MOE_BLOCK), blk_e)).reshape(n_rows, D)
    out = jnp.zeros((T + 1, D), yb.dtype).at[row_tok].add(yb * row_w[:, None].astype(yb.dtype))
    return out[:T]


def _layer(x_lat, x_ctx, mod_lat, mod_ctx, p, cos, sin, need_ctx_out):
    B, L, D = x_lat.shape
    Lc = x_ctx.shape[1]
    sh1, sc1, g1, sh2, sc2, g2 = jnp.split(mod_lat[:, None, :], 6, axis=-1)
    csh1, csc1, cg1, csh2, csc2, cg2 = jnp.split(mod_ctx, 6, axis=-1)
    idx = np.cumsum(IN_WIDTHS)[:-1].tolist()
    z_lat = _modulate(x_lat, p['norm1_g'], sh1, sc1) @ p['w_in']
    z_ctx = _modulate(x_ctx, p['norm1_g'], csh1, csc1) @ p['w_in']
    cq_l, ckv_l, kr_l, hy_l, gq_l, gk_l, gv_l, glr_l, og_l, fn_l = jnp.split(z_lat, idx, axis=-1)
    cq_c, ckv_c, kr_c, hy_c, gq_c, gk_c, gv_c, glr_c, og_c, fn_c = jnp.split(z_ctx, idx, axis=-1)
    q_l, k_l, v_l = _mla_qkv(cq_l, ckv_l, kr_l, p, cos, sin)
    q_c, k_c, v_c = _mla_qkv(cq_c, ckv_c, kr_c, p, None, None)
    a_l = _attend_blocked(q_l, jnp.concatenate([k_l, k_c], axis=1), jnp.concatenate([v_l, v_c], axis=1))
    s0 = jnp.zeros((2 * B, GLA_HEADS, GLA_DK, GLA_DV), jnp.float32)
    o_c, s_c = _gla_bidir(gq_c, gk_c, gv_c, glr_c, p, s0)
    o_l, _ = _gla_bidir(gq_l, gk_l, gv_l, glr_l, p, s_c)
    y_lat = _mix_out(a_l, _hyena(hy_l, p), o_l, og_l, _fnet(fn_l, p['fnet_w']), p)
    x_lat = x_lat + g1 * y_lat
    if need_ctx_out:
        a_c = _attend(q_c, k_c, v_c)
        y_ctx = _mix_out(a_c, _hyena(hy_c, p), o_c, og_c, _fnet(fn_c, p['fnet_w']), p)
        x_ctx = x_ctx + cg1 * y_ctx
        h2 = jnp.concatenate([_modulate(x_lat, p['norm2_g'], sh2, sc2).reshape(B * L, D),
                              _modulate(x_ctx, p['norm2_g'], csh2, csc2).reshape(B * Lc, D)], axis=0)
        f = _moe(h2, p)
        x_lat = x_lat + g2 * f[:B * L].reshape(B, L, D)
        x_ctx = x_ctx + cg2 * f[B * L:].reshape(B, Lc, D)
    else:
        f = _moe(_modulate(x_lat, p['norm2_g'], sh2, sc2).reshape(B * L, D), p)
        x_lat = x_lat + g2 * f.reshape(B, L, D)
    return x_lat, x_ctx


def setup_inputs(seed: int = 0) -> dict:
    key = jax.random.key(seed)
    ks = iter(jax.random.split(key, 40))
    def nrm(shape, s):
        return jax.random.normal(next(ks), shape, jnp.float32) * s
    def gain(shape):
        return 1.0 + nrm(shape, 0.02)
    Dd, E = D_MODEL, N_EXPERTS
    return {
        'x': nrm((BATCH, SEQ, Dd), 1.0),
        'c': nrm((BATCH, Dd), 1.0),
        'ctx': nrm((BATCH, CTX_LEN, Dd), 1.0),
        'c_ctx': nrm((Dd,), 1.0),
        'ada_w': nrm((DEPTH, Dd, 6 * Dd), 0.5 * Dd ** -0.5),
        'ada_b': nrm((DEPTH, 6 * Dd), 0.02),
        'norm1_g': gain((DEPTH, Dd)),
        'norm2_g': gain((DEPTH, Dd)),
        'w_in': nrm((DEPTH, Dd, D_IN), Dd ** -0.5),
        'mla_q_g': gain((DEPTH, MLA_Q_LORA)),
        'mla_w_uq': nrm((DEPTH, MLA_Q_LORA, MLA_HEADS * (MLA_NOPE + MLA_ROPE)), MLA_Q_LORA ** -0.5),
        'mla_kv_g': gain((DEPTH, MLA_KV_LORA)),
        'mla_w_ukv': nrm((DEPTH, MLA_KV_LORA, MLA_HEADS * (MLA_NOPE + MLA_V)), MLA_KV_LORA ** -0.5),
        'hy_conv_w': nrm((DEPTH, 3, 3 * HY_W), 3 ** -0.5),
        'hy_conv_b': nrm((DEPTH, 3 * HY_W), 0.02),
        'hy_w1': nrm((DEPTH, HY_EMB, HY_FFN), HY_EMB ** -0.5),
        'hy_b1': nrm((DEPTH, HY_FFN), 0.02),
        'hy_w2': nrm((DEPTH, HY_FFN, HY_FFN), HY_FFN ** -0.5),
        'hy_b2': nrm((DEPTH, HY_FFN), 0.02),
        'hy_w3': nrm((DEPTH, HY_FFN, 2 * HY_W), HY_FFN ** -0.5),
        'hy_freq': 1.0 + nrm((DEPTH, HY_FFN), 0.1),
        'hy_bias': nrm((DEPTH, HY_W), 1.0),
        'gla_gk_w': nrm((DEPTH, 2, GLA_LR, GLA_HEADS * GLA_DK), GLA_LR ** -0.5),
        'gla_gk_b': nrm((DEPTH, 2, GLA_HEADS * GLA_DK), 0.1),
        'fnet_w': nrm((DEPTH, FN_W, FN_W), FN_W ** -0.5),
        'mix_g': gain((DEPTH, 4 * GROUP_W)),
        'w_out': nrm((DEPTH, 4 * GROUP_W, Dd), (4 * GROUP_W) ** -0.5),
        'router_w': nrm((DEPTH, Dd, E), Dd ** -0.5),
        'router_b': nrm((DEPTH, E), 0.01),
        'moe_w_gu': nrm((DEPTH, E, Dd, 2 * D_FF), Dd ** -0.5),
        'moe_b_gu': nrm((DEPTH, E, 2 * D_FF), 0.02),
        'moe_w_down': nrm((DEPTH, E, D_FF, Dd), D_FF ** -0.5),
        'moe_b_down': nrm((DEPTH, E, Dd), 0.02),
        'final_g': gain((Dd,)),
    }


def reference(x, c, ctx, c_ctx, ada_w, ada_b, norm1_g, norm2_g, w_in, mla_q_g, mla_w_uq, mla_kv_g, mla_w_ukv,
              hy_conv_w, hy_conv_b, hy_w1, hy_b1, hy_w2, hy_b2, hy_w3, hy_freq, hy_bias, gla_gk_w, gla_gk_b,
              fnet_w, mix_g, w_out, router_w, router_b, moe_w_gu, moe_b_gu, moe_w_down, moe_b_down, final_g):
    n_lat = x.shape[1]
    rows = n_lat // GRID_W
    cos, sin = _axial_rope(rows)
    x_lat, x_ctx = x, ctx
    for i in range(DEPTH):
        p = {
            'norm1_g': norm1_g[i], 'norm2_g': norm2_g[i], 'w_in': w_in[i],
            'mla_q_g': mla_q_g[i], 'mla_w_uq': mla_w_uq[i], 'mla_kv_g': mla_kv_g[i], 'mla_w_ukv': mla_w_ukv[i],
            'hy_conv_w': hy_conv_w[i], 'hy_conv_b': hy_conv_b[i], 'hy_w1': hy_w1[i], 'hy_b1': hy_b1[i],
            'hy_w2': hy_w2[i], 'hy_b2': hy_b2[i], 'hy_w3': hy_w3[i], 'hy_freq': hy_freq[i], 'hy_bias': hy_bias[i],
            'gla_gk_w': gla_gk_w[i], 'gla_gk_b': gla_gk_b[i], 'fnet_w': fnet_w[i], 'mix_g': mix_g[i],
            'w_out': w_out[i], 'router_w': router_w[i], 'router_b': router_b[i],
            'moe_w_gu': moe_w_gu[i], 'moe_b_gu': moe_b_gu[i], 'moe_w_down': moe_w_down[i], 'moe_b_down': moe_b_down[i],
        }
        mod_lat = jax.nn.silu(c) @ ada_w[i] + ada_b[i]
        mod_ctx = jax.nn.silu(c_ctx) @ ada_w[i] + ada_b[i]
        x_lat, x_ctx = _layer(x_lat, x_ctx, mod_lat, mod_ctx, p, cos, sin, i < DEPTH - 1)
    return _rms(x_lat, final_g)
```

```python
import functools
import math

import numpy as np
import jax
import jax.numpy as jnp
from jax import lax
from jax.experimental import pallas as pl
from jax.experimental.pallas import tpu as pltpu

F32 = jnp.float32
BF16 = jnp.bfloat16
HIGHEST = lax.Precision.HIGHEST

EPS = 1e-6
D_MODEL = 1024
GROUP_W = 256
MLA_HEADS = 4
MLA_NOPE = 64
MLA_ROPE = 32
MLA_V = 64
MLA_Q_LORA = 256
MLA_KV_LORA = 128
MLA_SCALE = (MLA_NOPE + MLA_ROPE) ** -0.5
ROPE_THETA = 10000.0
GRID_W = 64
HY_EMB = 33
HY_FFN = 64
HY_TARGET = 1e-2
HY_FAST_PCT = 0.3
HY_SLOW_PCT = 1.5
GLA_HEADS = 4
GLA_DK = 32
GLA_DV = 64
GLA_LR = 16
GLA_TAU = 16.0
GLA_CHUNK = 64
N_EXPERTS = 32
TOP_K = 4
SWIGLU_ALPHA = 1.702
SWIGLU_LIMIT = 7.0

LANES = 128
TOK_TILE = 256
DFT_N2 = 128
MOE_BM = 256
VMEM_LIMIT = 56 * 1024 * 1024

_O_CQ, _O_CKV, _O_KRP, _O_KRS, _O_HY, _O_GQ, _O_GK, _O_GV, _O_GLR, _O_OG, _O_FN, _W_ALL = (
    0, 256, 384, 896, 1408, 2176, 2304, 2432, 2688, 2816, 3072, 3328)


def _dot(a, b):
    return jnp.dot(a.astype(BF16), b.astype(BF16), preferred_element_type=F32)


def _dot_hi(a, b):
    return jnp.dot(a, b, precision=HIGHEST, preferred_element_type=F32)


def _params(*sem):
    return pltpu.CompilerParams(dimension_semantics=sem, vmem_limit_bytes=VMEM_LIMIT)


def _rms_rows(x):
    return x * lax.rsqrt(jnp.mean(x * x, axis=-1, keepdims=True) + EPS)


def _full(shape):
    n = len(shape)
    return pl.BlockSpec(shape, lambda *_: (0,) * n)


def _mods_kernel(c_ref, w_ref, b_ref, o_ref):
    c = c_ref[...]
    s = c / (1.0 + jnp.exp(-c))
    o_ref[...] = _dot_hi(s, w_ref[...]) + b_ref[...]


def _mods(cc, ada_w, ada_b):
    depth, d, n = ada_w.shape
    tn = 1024
    return pl.pallas_call(
        _mods_kernel,
        grid=(depth, n // tn),
        in_specs=[pl.BlockSpec((8, d), lambda i, j: (0, 0)),
                  pl.BlockSpec((None, d, tn), lambda i, j: (i, 0, j)),
                  pl.BlockSpec((None, 1, tn), lambda i, j: (i, 0, j))],
        out_specs=pl.BlockSpec((None, 8, tn), lambda i, j: (i, 0, j)),
        out_shape=jax.ShapeDtypeStruct((depth, 8, n), F32),
        compiler_params=_params("arbitrary", "arbitrary"),
        name="ada_mods",
    )(cc, ada_w, ada_b.reshape(depth, 1, n))


def _proj_kernel(x_ref, mod_ref, g_ref, win_ref, qg_ref, wq_ref, kvg_ref, wkv_ref, gkw_ref, gkb_ref, csc_ref,
                 cq_ref, sq_ref, ck_ref, sk_ref,
                 q_out, k_out, v_out, hy_out, gqk_out, gv_out, gg_out, og_out, fre_out, fim_out):
    x = x_ref[...]
    mod = mod_ref[...]
    h = _rms_rows(x) * g_ref[...] * (1.0 + mod[1:2]) + mod[0:1]
    z = _dot(h, win_ref[...])

    nq = _rms_rows(z[:, _O_CQ:_O_CKV]) * qg_ref[...]
    qq = _dot(nq, wq_ref[...])
    nkv = _rms_rows(z[:, _O_CKV:_O_KRP]) * kvg_ref[...]
    kvu = _dot(nkv, wkv_ref[...])
    cq, sq, ck, sk = cq_ref[...], sq_ref[...], ck_ref[...], sk_ref[...]
    for hd in range(MLA_HEADS):
        a, b = hd * LANES, (hd + 1) * LANES
        q_out[:, a:b] = (qq[:, a:b] * cq + qq[:, 512 + a:512 + b] * sq).astype(BF16)
        k_out[:, a:b] = (kvu[:, a:b] + z[:, _O_KRP + a:_O_KRP + b] * ck
                         + z[:, _O_KRS + a:_O_KRS + b] * sk).astype(BF16)
    v_out[...] = kvu[:, 512:768].astype(BF16)

    hy_out[...] = z[:, _O_HY:_O_GQ]
    gqk_out[:, :LANES] = z[:, _O_GQ:_O_GK] * (GLA_DK ** -0.5)
    gqk_out[:, LANES:] = z[:, _O_GK:_O_GV]
    gv_out[...] = z[:, _O_GV:_O_GLR]
    gates = _dot(z[:, _O_GLR:_O_OG], gkw_ref[...]) + gkb_ref[...]
    gg_out[...] = (jnp.minimum(gates, 0.0) - jnp.log(1.0 + jnp.exp(-jnp.abs(gates)))) * (1.0 / GLA_TAU)
    og_out[...] = z[:, _O_OG:_O_FN]
    fcs = _dot_hi(z[:, _O_FN:_W_ALL], csc_ref[...])
    fre_out[...] = fcs[:, :GROUP_W]
    fim_out[...] = fcs[:, GROUP_W:]


def _proj(xs, mod, g1, win, qg, wq, kvg, wkv, gkw, gkb, csc, tabs, n_lat_tiles):
    B, S, D = xs.shape
    tm = TOK_TILE
    tok = lambda w: pl.BlockSpec((None, tm, w), lambda b, t: (b, t, 0))
    tab = pl.BlockSpec((tm, LANES), lambda b, t: (t, 0))
    shp = lambda w, dt: jax.ShapeDtypeStruct((B, S, w), dt)
    return pl.pallas_call(
        _proj_kernel,
        grid=(B, S // tm),
        in_specs=[tok(D),
                  pl.BlockSpec((None, 6, D), lambda b, t: (jnp.where(t >= n_lat_tiles, 2, b), 0, 0)),
                  _full((1, D)), _full(win.shape), _full((1, MLA_Q_LORA)), _full(wq.shape),
                  _full((1, MLA_KV_LORA)), _full(wkv.shape), _full(gkw.shape), _full(gkb.shape), _full(csc.shape),
                  tab, tab, tab, tab],
        out_specs=[tok(512), tok(512), tok(256), tok(768), tok(256), tok(256), tok(256), tok(256), tok(256), tok(256)],
        out_shape=[shp(512, BF16), shp(512, BF16), shp(256, BF16), shp(768, F32), shp(256, F32), shp(256, F32),
                   shp(256, F32), shp(256, F32), shp(256, F32), shp(256, F32)],
        compiler_params=_params("arbitrary", "arbitrary"),
        name="in_proj",
    )(xs, mod, g1, win, qg, wq, kvg, wkv, gkw, gkb, csc, *tabs)


def _attn_kernel(q_ref, k_ref, v_ref, o_ref, *, n_lat_tiles, tk, n_lat, n_ctx):
    qi = pl.program_id(2)
    tq = q_ref.shape[0]

    def head(hd, k0, n_chunks, width):
        q = q_ref[:, hd * LANES:(hd + 1) * LANES]

        def body(c, carry):
            m, l, acc = carry
            off = pl.multiple_of(k0 + c * width, 256)
            kc = k_ref[pl.ds(off, width), hd * LANES:(hd + 1) * LANES]
            vc = v_ref[pl.ds(off, width), hd * MLA_V:(hd + 1) * MLA_V]
            s = lax.dot_general(q, kc, (((1,), (1,)), ((), ())), preferred_element_type=F32)
            m_new = jnp.maximum(m, jnp.max(s, axis=-1, keepdims=True))
            alpha = jnp.exp(m - m_new)
            p = jnp.exp(s - m_new)
            l = alpha * l + jnp.sum(p, axis=-1, keepdims=True)
            acc = alpha * acc + jnp.dot(p.astype(BF16), vc, preferred_element_type=F32)
            return m_new, l, acc

        init = (jnp.full((tq, 1), -1e30, F32), jnp.zeros((tq, 1), F32), jnp.zeros((tq, MLA_V), F32))
        _, l, acc = lax.fori_loop(0, n_chunks, body, init)
        o_ref[:, hd * MLA_V:(hd + 1) * MLA_V] = acc / l

    @pl.when(qi < n_lat_tiles)
    def _():
        for hd in range(2):
            head(hd, 0, (n_lat + n_ctx) // tk, tk)

    @pl.when(qi >= n_lat_tiles)
    def _():
        for hd in range(2):
            head(hd, n_lat, 1, n_ctx)


def _attention(q, k, v, n_lat, n_ctx):
    B, S, _ = q.shape
    tq = TOK_TILE
    tk = 1280 if S % 1280 == 0 else 256
    kern = functools.partial(_attn_kernel, n_lat_tiles=n_lat // tq, tk=tk, n_lat=n_lat, n_ctx=n_ctx)
    return pl.pallas_call(
        kern,
        grid=(B, 2, S // tq),
        in_specs=[pl.BlockSpec((None, tq, 256), lambda b, p, t: (b, t, p)),
                  pl.BlockSpec((None, S, 256), lambda b, p, t: (b, 0, p)),
                  pl.BlockSpec((None, S, 128), lambda b, p, t: (b, 0, p))],
        out_specs=pl.BlockSpec((None, tq, 128), lambda b, p, t: (b, t, p)),
        out_shape=jax.ShapeDtypeStruct((B, S, 256), F32),
        compiler_params=_params("arbitrary", "arbitrary", "arbitrary"),
        name="mla_attention",
    )(q, k, v)


def _hy_pre_kernel(z_ref, zp_ref, zn_ref, w_ref, b_ref, vg_ref, x0_ref, *, n_tiles):
    i = pl.program_id(1)
    z = z_ref[...]
    tm = z.shape[0]
    rows = lax.broadcasted_iota(jnp.int32, z.shape, 0)
    prev_row = jnp.where(i == 0, 0.0, zp_ref[7:8, :])
    next_row = jnp.where(i == n_tiles - 1, 0.0, zn_ref[0:1, :])
    z_m = jnp.where(rows == 0, prev_row, pltpu.roll(z, 1, 0))
    z_p = jnp.where(rows == tm - 1, next_row, pltpu.roll(z, tm - 1, 0))
    w = w_ref[...]
    u = z_m * w[0:1] + z * w[1:2] + z_p * w[2:3] + b_ref[...]
    vg_ref[...] = u[:, 2 * GROUP_W:] * u[:, GROUP_W:2 * GROUP_W]
    x0_ref[...] = u[:, :GROUP_W]


def _hy_pre(hyz, conv_w, conv_b, row0, n_rows):
    B, S, W = hyz.shape
    tm = TOK_TILE
    nt = n_rows // tm
    t0, r8, last8 = row0 // tm, row0 // 8, S // 8 - 1
    kern = functools.partial(_hy_pre_kernel, n_tiles=nt)
    out = jax.ShapeDtypeStruct((B, n_rows, GROUP_W), F32)
    return pl.pallas_call(
        kern,
        grid=(B, nt),
        in_specs=[pl.BlockSpec((None, tm, W), lambda b, i: (b, t0 + i, 0)),
                  pl.BlockSpec((None, 8, W), lambda b, i: (b, jnp.maximum(r8 + i * (tm // 8) - 1, 0), 0)),
                  pl.BlockSpec((None, 8, W), lambda b, i: (b, jnp.minimum(r8 + (i + 1) * (tm // 8), last8), 0)),
                  _full((3, W)), _full((1, W))],
        out_specs=[pl.BlockSpec((None, tm, GROUP_W), lambda b, i: (b, i, 0))] * 2,
        out_shape=[out, out],
        compiler_params=_params("arbitrary", "arbitrary"),
        name="hyena_pre",
    )(hyz, hyz, hyz, conv_w, conv_b.reshape(1, W))


def _filter_kernel(fv_ref, w1_ref, b1_ref, w2_ref, b2_ref, w3_ref, fr_ref, dl_ref, h_ref, ss_ref, *, n_pos):
    i = pl.program_id(0)
    tl = h_ref.shape[0]
    pos = (lax.broadcasted_iota(jnp.int32, (tl, LANES), 0) + i * tl).astype(F32)
    lane = lax.broadcasted_iota(jnp.int32, (tl, LANES), 1)
    t = pos * (1.0 / (n_pos - 1))
    arg = (pos * (2.0 * math.pi / n_pos)) * fv_ref[...]
    feat = jnp.where(lane == 0, t, jnp.where(lane < 17, jnp.cos(arg), jnp.where(lane < HY_EMB, -jnp.sin(arg), 0.0)))
    fr = fr_ref[...]
    h = jnp.sin(fr * (_dot_hi(feat, w1_ref[...]) + b1_ref[...]))
    h = jnp.sin(fr * (_dot_hi(h, w2_ref[...]) + b2_ref[...]))
    h = _dot_hi(h, w3_ref[...]) * jnp.exp(-t[:, 0:1] * dl_ref[...])
    h_ref[...] = h

    @pl.when(i == 0)
    def _():
        ss_ref[...] = jnp.zeros_like(ss_ref)

    ss_ref[...] += jnp.sum(h * h, axis=0, keepdims=True)


def _hyena_filter(n_pos, w1, b1, w2, b2, w3, freq):
    tl = min(n_pos, 512)
    bands = (HY_EMB - 1) // 2
    f = np.linspace(1e-4, bands - 1, bands)
    fv = np.zeros((1, LANES), np.float32)
    fv[0, 1:17] = f
    fv[0, 17:33] = f
    w1p = jnp.zeros((LANES, HY_FFN), F32).at[:HY_EMB].set(w1)
    max_decay = math.log(HY_TARGET) / HY_FAST_PCT
    min_decay = math.log(HY_TARGET) / HY_SLOW_PCT
    deltas = np.abs(np.linspace(min_decay, max_decay, GROUP_W)).astype(np.float32)
    dl = jnp.asarray(np.concatenate([deltas, deltas])[None, :])
    kern = functools.partial(_filter_kernel, n_pos=n_pos)
    return pl.pallas_call(
        kern,
        grid=(n_pos // tl,),
        in_specs=[_full((1, LANES)), _full((LANES, HY_FFN)), _full((1, HY_FFN)), _full((HY_FFN, HY_FFN)),
                  _full((1, HY_FFN)), _full((HY_FFN, 2 * GROUP_W)), _full((1, HY_FFN)), _full((1, 2 * GROUP_W))],
        out_specs=[pl.BlockSpec((tl, 2 * GROUP_W), lambda i: (i, 0)), _full((1, 2 * GROUP_W))],
        out_shape=[jax.ShapeDtypeStruct((n_pos, 2 * GROUP_W), F32), jax.ShapeDtypeStruct((1, 2 * GROUP_W), F32)],
        compiler_params=_params("arbitrary"),
        name="hyena_filter",
    )(jnp.asarray(fv), w1p, b1.reshape(1, -1), w2, b2.reshape(1, -1), w3, freq.reshape(1, -1), dl)


def _circular_filter(h, ss):
    h_f, h_b = h[:, :GROUP_W], h[:, GROUP_W:]
    k2 = jnp.concatenate([h_f, jnp.zeros((1, GROUP_W), F32), h_b[1:][::-1]], axis=0)
    norm2 = ss[:, :GROUP_W] + ss[:, GROUP_W:] - h_b[0:1] * h_b[0:1]
    return k2, lax.rsqrt(norm2)


def _dft_mats(n, sign):
    j = np.arange(n)
    ang = 2.0 * np.pi * ((j[:, None] * j[None, :]) % n) / n
    return np.cos(ang), sign * np.sin(ang)


def _twiddle(n1, n2, sign):
    ang = 2.0 * np.pi * ((np.arange(n1)[:, None] * np.arange(n2)[None, :]) % (n1 * n2)) / (n1 * n2)
    return np.cos(ang), sign * np.sin(ang)


def _block_complex(re, im):
    return np.block([[re, -im], [im, re]])


def _stage1_kernel(m_ref, zr_ref, zi_ref, ar_ref, ai_ref):
    a = _dot_hi(m_ref[...], jnp.concatenate([zr_ref[...], zi_ref[...]], axis=0))
    half = ar_ref.shape[0]
    ar_ref[...] = a[:half]
    ai_ref[...] = a[half:]


def _stage1_real_kernel(m_ref, zr_ref, ar_ref, ai_ref):
    a = _dot_hi(m_ref[...], zr_ref[...])
    half = ar_ref.shape[0]
    ar_ref[...] = a[:half]
    ai_ref[...] = a[half:]


def _col_tile(cols):
    return 2048 if cols % 2048 == 0 else cols


def _hy_stage1(vg2, n1):
    _, r, cols = vg2.shape
    cr, ci = _dft_mats(n1, -1.0)
    m = jnp.asarray(_block_complex(cr[:, :r], ci[:, :r]), F32)
    tc = _col_tile(cols)
    out = jax.ShapeDtypeStruct((n1, cols), F32)
    return pl.pallas_call(
        _stage1_kernel,
        grid=(cols // tc,),
        in_specs=[_full(m.shape), pl.BlockSpec((None, r, tc), lambda j: (0, 0, j)),
                  pl.BlockSpec((None, r, tc), lambda j: (1, 0, j))],
        out_specs=[pl.BlockSpec((n1, tc), lambda j: (0, j))] * 2,
        out_shape=[out, out],
        compiler_params=_params("arbitrary"),
        name="hyena_fwd_stage1",
    )(m, vg2, vg2)


def _filter_stage1(k2v, n1):
    _, cols = k2v.shape
    cr, ci = _dft_mats(n1, -1.0)
    m = jnp.asarray(np.concatenate([cr, ci], axis=0), F32)
    tc = _col_tile(cols)
    out = jax.ShapeDtypeStruct((n1, cols), F32)
    return pl.pallas_call(
        _stage1_real_kernel,
        grid=(cols // tc,),
        in_specs=[_full(m.shape), pl.BlockSpec((n1, tc), lambda j: (0, j))],
        out_specs=[pl.BlockSpec((n1, tc), lambda j: (0, j))] * 2,
        out_shape=[out, out],
        compiler_params=_params("arbitrary"),
        name="hyena_filter_stage1",
    )(m, k2v)


def _twiddled(f_re, f_im, t_re, t_im):
    g_re = f_re * t_re - f_im * t_im
    g_im = f_re * t_im + f_im * t_re
    return jnp.concatenate([jnp.concatenate([g_re, -g_im], axis=1), jnp.concatenate([g_im, g_re], axis=1)], axis=0)


def _filter_stage2_kernel(ar_ref, ai_ref, twr_ref, twi_ref, fr_ref, fi_ref, sc_ref, kr_ref, ki_ref):
    g = _twiddled(fr_ref[...], fi_ref[...], twr_ref[...], twi_ref[...])
    x = _dot_hi(g, jnp.concatenate([ar_ref[...], ai_ref[...]], axis=0)) * sc_ref[...]
    n2 = kr_ref.shape[0]
    kr_ref[...] = x[:n2]
    ki_ref[...] = x[n2:]


def _filter_stage2(ar, ai, inv_norm, n1):
    n2 = DFT_N2
    c = ar.shape[1] // n2
    fr, fi = _dft_mats(n2, -1.0)
    twr, twi = _twiddle(n1, n2, -1.0)
    blk = pl.BlockSpec((None, n2, c), lambda k: (k, 0, 0))
    tw = pl.BlockSpec((None, 1, n2), lambda k: (k, 0, 0))
    out = jax.ShapeDtypeStruct((n1, n2, c), F32)
    return pl.pallas_call(
        _filter_stage2_kernel,
        grid=(n1,),
        in_specs=[blk, blk, tw, tw, _full((n2, n2)), _full((n2, n2)), _full((1, c))],
        out_specs=[blk, blk],
        out_shape=[out, out],
        compiler_params=_params("arbitrary"),
        name="hyena_filter_stage2",
    )(ar.reshape(n1, n2, c), ai.reshape(n1, n2, c), jnp.asarray(twr.reshape(n1, 1, n2), F32),
      jnp.asarray(twi.reshape(n1, 1, n2), F32), jnp.asarray(fr, F32), jnp.asarray(fi, F32), inv_norm)


def _hy_stage2_kernel(ar_ref, ai_ref, kr_ref, ki_ref, twr_ref, twi_ref, tcr_ref, tci_ref, fr_ref, fi_ref,
                      br_ref, bi_ref):
    f_re, f_im = fr_ref[...], fi_ref[...]
    n2 = f_re.shape[0]
    g = _twiddled(f_re, f_im, twr_ref[...], twi_ref[...])
    x = _dot_hi(g, jnp.concatenate([ar_ref[...], ai_ref[...]], axis=0))
    x_re, x_im = x[:n2], x[n2:]
    k_re, k_im = kr_ref[...], ki_ref[...]
    y = jnp.concatenate([x_re * k_re - x_im * k_im, x_re * k_im + x_im * k_re], axis=0)
    g_inv = _twiddled(f_re, -f_im, tcr_ref[...], -tci_ref[...])
    b = _dot_hi(g_inv, y)
    br_ref[...] = b[:n2]
    bi_ref[...] = b[n2:]


def _hy_stage2(ar, ai, kr, ki, n1):
    n2 = DFT_N2
    c = kr.shape[2]
    fr, fi = _dft_mats(n2, -1.0)
    twr, twi = _twiddle(n1, n2, -1.0)
    blk = pl.BlockSpec((None, n2, c), lambda k: (k, 0, 0))
    tw = pl.BlockSpec((None, 1, n2), lambda k: (k, 0, 0))
    twc = pl.BlockSpec((None, n2, 1), lambda k: (k, 0, 0))
    out = jax.ShapeDtypeStruct((n1, n2, c), F32)
    return pl.pallas_call(
        _hy_stage2_kernel,
        grid=(n1,),
        in_specs=[blk, blk, blk, blk, tw, tw, twc, twc, _full((n2, n2)), _full((n2, n2))],
        out_specs=[blk, blk],
        out_shape=[out, out],
        compiler_params=_params("arbitrary"),
        name="hyena_conv_stage2",
    )(ar.reshape(n1, n2, c), ai.reshape(n1, n2, c), kr, ki,
      jnp.asarray(twr.reshape(n1, 1, n2), F32), jnp.asarray(twi.reshape(n1, 1, n2), F32),
      jnp.asarray(twr.reshape(n1, n2, 1), F32), jnp.asarray(twi.reshape(n1, n2, 1), F32),
      jnp.asarray(fr, F32), jnp.asarray(fi, F32))


def _hy_stage3_kernel(m_ref, br_ref, bi_ref, vg0_ref, vg1_ref, x00_ref, x01_ref, bias_ref, o_ref):
    conv = _dot_hi(m_ref[...], jnp.concatenate([br_ref[...], bi_ref[...]], axis=0))
    r = vg0_ref.shape[0]
    bias = bias_ref[...]
    o_ref[0] = (conv[:r] + vg0_ref[...] * bias) * x00_ref[...]
    o_ref[1] = (conv[r:] + vg1_ref[...] * bias) * x01_ref[...]


def _hy_stage3(br, bi, vg2, x02, bias_cols, n1):
    _, r, cols = vg2.shape
    cr, ci = _dft_mats(n1, 1.0)
    m = jnp.asarray(_block_complex(cr[:r], ci[:r]) / (n1 * DFT_N2), F32)
    tc = _col_tile(cols)
    plane = lambda p: pl.BlockSpec((None, r, tc), lambda j: (p, 0, j))
    return pl.pallas_call(
        _hy_stage3_kernel,
        grid=(cols // tc,),
        in_specs=[_full(m.shape), pl.BlockSpec((n1, tc), lambda j: (0, j)), pl.BlockSpec((n1, tc), lambda j: (0, j)),
                  plane(0), plane(1), plane(0), plane(1), pl.BlockSpec((1, tc), lambda j: (0, j))],
        out_specs=pl.BlockSpec((2, r, tc), lambda j: (0, 0, j)),
        out_shape=jax.ShapeDtypeStruct((2, r, cols), F32),
        compiler_params=_params("arbitrary"),
        name="hyena_inv_stage3",
    )(m, br, bi, vg2, vg2, x02, x02, bias_cols)


def _hyena_long(vg, x0, k2, inv_norm, bias):
    B, L, C = vg.shape
    assert B == 2
    n2 = DFT_N2
    n1 = 2 * L // n2
    r = L // n2
    fa_r, fa_i = _filter_stage1(k2.reshape(n1, n2 * C), n1)
    kr, ki = _filter_stage2(fa_r, fa_i, inv_norm, n1)
    vg2 = vg.reshape(2, r, n2 * C)
    ar, ai = _hy_stage1(vg2, n1)
    br, bi = _hy_stage2(ar, ai, kr, ki, n1)
    y = _hy_stage3(br.reshape(n1, n2 * C), bi.reshape(n1, n2 * C), vg2, x0.reshape(2, r, n2 * C),
                   jnp.tile(bias.reshape(1, C), (1, n2)), n1)
    return y.reshape(2, L, C)


def _hy_ctx_kernel(vg_ref, x0_ref, kext_ref, sc_ref, bias_ref, o_ref):
    n = vg_ref.shape[0]

    def body(a, acc):
        src = vg_ref[pl.ds(pl.multiple_of(8 * a, 8), 8), :]
        win = pl.multiple_of(n - 8 * a, 8)
        for r in range(8):
            acc = acc + kext_ref[r, pl.ds(win, n), :] * src[r:r + 1]
        return acc

    acc = lax.fori_loop(0, n // 8, body, jnp.zeros(vg_ref.shape, F32))
    o_ref[...] = (acc * sc_ref[...] + vg_ref[...] * bias_ref[...]) * x0_ref[...]


def _hyena_ctx(vg, x0, k2, inv_norm, bias):
    B, n, C = vg.shape
    kext = jnp.concatenate([k2[n:], k2[:n]], axis=0)
    kext = jnp.stack([jnp.roll(kext, r, axis=0) for r in range(8)])
    blk = pl.BlockSpec((None, n, C), lambda b: (b, 0, 0))
    return pl.pallas_call(
        _hy_ctx_kernel,
        grid=(B,),
        in_specs=[blk, blk, _full((8, 2 * n, C)), _full((1, C)), _full((1, C))],
        out_specs=blk,
        out_shape=jax.ShapeDtypeStruct((B, n, C), F32),
        compiler_params=_params("arbitrary"),
        name="hyena_ctx",
    )(vg, x0, kext, inv_norm, bias.reshape(1, C))


def _fn_stage1_kernel(c_ref, s_ref, re_ref, im_ref, ar_ref, ai_ref):
    c, s, re, im = c_ref[...], s_ref[...], re_ref[...], im_ref[...]
    ar_ref[...] = _dot_hi(c, re) + _dot_hi(s, im)
    ai_ref[...] = _dot_hi(c, im) - _dot_hi(s, re)


def _fn_stage2_kernel(ar_ref, ai_ref, twr_ref, twi_ref, fr_ref, fi_ref, o_ref):
    f_re, f_im, t_re, t_im = fr_ref[...], fi_ref[...], twr_ref[...], twi_ref[...]
    g_re = f_re * t_re - f_im * t_im
    g_im = f_re * t_im + f_im * t_re
    o_ref[...] = _dot_hi(g_re, ar_ref[...]) - _dot_hi(g_im, ai_ref[...])


def _fnet_long(fre, fim, n_lat):
    B, S, C = fre.shape
    n2 = DFT_N2
    n1 = n_lat // n2
    cols = n2 * C
    c1, s1 = _dft_mats(n1, 1.0)
    tc = _col_tile(cols)
    rows = pl.BlockSpec((None, n1, tc), lambda b, j: (b, 0, j))
    a_shape = jax.ShapeDtypeStruct((B, n1, cols), F32)
    ar, ai = pl.pallas_call(
        _fn_stage1_kernel,
        grid=(B, cols // tc),
        in_specs=[_full((n1, n1)), _full((n1, n1)), rows, rows],
        out_specs=[rows, rows],
        out_shape=[a_shape, a_shape],
        compiler_params=_params("arbitrary", "arbitrary"),
        name="fnet_stage1",
    )(jnp.asarray(c1, F32), jnp.asarray(s1, F32), fre.reshape(B, S // n2, cols), fim.reshape(B, S // n2, cols))
    fr, fi = _dft_mats(n2, -1.0)
    scale = 1.0 / math.sqrt(n_lat * C)
    twr, twi = _twiddle(n1, n2, -1.0)
    blk = pl.BlockSpec((None, None, n2, C), lambda b, k: (b, k, 0, 0))
    tw = pl.BlockSpec((None, 1, n2), lambda b, k: (k, 0, 0))
    y = pl.pallas_call(
        _fn_stage2_kernel,
        grid=(B, n1),
        in_specs=[blk, blk, tw, tw, _full((n2, n2)), _full((n2, n2))],
        out_specs=pl.BlockSpec((None, n2, C), lambda b, k: (b, 0, k)),
        out_shape=jax.ShapeDtypeStruct((B, n2, n1 * C), F32),
        compiler_params=_params("arbitrary", "arbitrary"),
        name="fnet_stage2",
    )(ar.reshape(B, n1, n2, C), ai.reshape(B, n1, n2, C), jnp.asarray(twr.reshape(n1, 1, n2), F32),
      jnp.asarray(twi.reshape(n1, 1, n2), F32), jnp.asarray(fr * scale, F32), jnp.asarray(fi * scale, F32))
    return y.reshape(B, n_lat, C)


def _fn_ctx_kernel(c_ref, s_ref, re_ref, im_ref, o_ref):
    o_ref[...] = _dot_hi(c_ref[...], re_ref[...]) + _dot_hi(s_ref[...], im_ref[...])


def _fnet_ctx(fre, fim, n_lat, n_ctx):
    B, S, C = fre.shape
    c1, s1 = _dft_mats(n_ctx, 1.0)
    scale = 1.0 / math.sqrt(n_ctx * C)
    blk = pl.BlockSpec((None, n_ctx, C), lambda b: (b, n_lat // n_ctx, 0))
    return pl.pallas_call(
        _fn_ctx_kernel,
        grid=(B,),
        in_specs=[_full((n_ctx, n_ctx)), _full((n_ctx, n_ctx)), blk, blk],
        out_specs=pl.BlockSpec((None, n_ctx, C), lambda b: (b, 0, 0)),
        out_shape=jax.ShapeDtypeStruct((B, n_ctx, C), F32),
        compiler_params=_params("arbitrary"),
        name="fnet_ctx",
    )(jnp.asarray(c1 * scale, F32), jnp.asarray(s1 * scale, F32), fre, fim)


def _gla_kernel(qk_ref, v_ref, g_ref, o_ref, st_ref):
    d = pl.program_id(1)
    n = pl.program_id(2)
    tm = qk_ref.shape[0]
    ck = GLA_CHUNK

    @pl.when(n == 0)
    def _():
        st_ref[...] = jnp.zeros_like(st_ref)

    ri = lax.broadcasted_iota(jnp.int32, (ck, ck), 0)
    ci = lax.broadcasted_iota(jnp.int32, (ck, ck), 1)

    def run(reverse):
        keep = (ci >= ri) if reverse else (ci <= ri)
        tri = keep.astype(F32)
        order = range(tm // ck - 1, -1, -1) if reverse else range(tm // ck)
        for sc in order:
            r0 = sc * ck
            g = g_ref[r0:r0 + ck, :]
            b = _dot_hi(tri, g)
            b_end = b[0:1] if reverse else b[ck - 1:ck]
            q = qk_ref[r0:r0 + ck, :LANES] * jnp.exp(b)
            k = qk_ref[r0:r0 + ck, LANES:]
            k_in = k * jnp.exp(-b)
            k_out = k * jnp.exp(b_end - b)
            decay = jnp.exp(b_end)
            for hd in range(GLA_HEADS):
                ks = slice(hd * GLA_DK, (hd + 1) * GLA_DK)
                vs = slice(hd * GLA_DV, (hd + 1) * GLA_DV)
                qh, vh = q[:, ks].astype(BF16), v_ref[r0:r0 + ck, vs].astype(BF16)
                a = lax.dot_general(qh, k_in[:, ks].astype(BF16), (((1,), (1,)), ((), ())),
                                    preferred_element_type=F32)
                a = jnp.where(keep, a, 0.0)
                st = st_ref[hd]
                o = jnp.dot(a.astype(BF16), vh, preferred_element_type=F32)
                o += lax.dot_general(qh, st.astype(BF16), (((1,), (1,)), ((), ())), preferred_element_type=F32)
                o_ref[r0:r0 + ck, vs] = o
                kv = lax.dot_general(vh, k_out[:, ks].astype(BF16), (((0,), (0,)), ((), ())),
                                     preferred_element_type=F32)
                st_ref[hd] = st * decay[:, ks] + kv

    @pl.when(d == 0)
    def _():
        run(False)

    @pl.when(d == 1)
    def _():
        run(True)


def _gla(gqk, gv, gg, n_lat):
    B, S, _ = gqk.shape
    tm = TOK_TILE
    nl = n_lat // tm
    nt = S // tm
    assert nt == nl + 1

    def row_block(d, n):
        lat = jnp.where(d == 0, n - 1, nl - n)
        return jnp.where(n == 0, nl, lat)

    return pl.pallas_call(
        _gla_kernel,
        grid=(B, 2, nt),
        in_specs=[pl.BlockSpec((None, tm, 256), lambda b, d, n: (b, row_block(d, n), 0)),
                  pl.BlockSpec((None, tm, 256), lambda b, d, n: (b, row_block(d, n), 0)),
                  pl.BlockSpec((None, tm, 128), lambda b, d, n: (b, row_block(d, n), d))],
        out_specs=pl.BlockSpec((None, None, tm, 256), lambda b, d, n: (b, d, row_block(d, n), 0)),
        out_shape=jax.ShapeDtypeStruct((B, 2, S, 256), F32),
        scratch_shapes=[pltpu.VMEM((GLA_HEADS, GLA_DV, GLA_DK), F32)],
        compiler_params=_params("arbitrary", "arbitrary", "arbitrary"),
        name="gla_scan",
    )(gqk, gv, gg)


def _mix_kernel(x_ref, mod_ref, a_ref, hy_ref, of_ref, ob_ref, og_ref, fy_ref, fw_ref, mg_ref, hm_ref, wo_ref,
                n2g_ref, rw_ref, rb_ref, xo_ref, h2_ref, ti_ref, tg_ref):
    mod = mod_ref[...]
    mg = mg_ref[...]
    hm = hm_ref[...]

    def head_rms(t):
        return t * lax.rsqrt(_dot_hi(t * t, hm) + EPS)

    a = head_rms(a_ref[...]) * mg[:, 0:256]
    hy = _rms_rows(hy_ref[...]) * mg[:, 256:512]
    og = og_ref[...]
    o = head_rms(of_ref[...] + ob_ref[...]) * mg[:, 512:768] * (og / (1.0 + jnp.exp(-og)))
    fn = _rms_rows(_dot(fy_ref[...], fw_ref[...])) * mg[:, 768:1024]
    wo = wo_ref[...]
    y = _dot(a, wo[0:256]) + _dot(hy, wo[256:512]) + _dot(o, wo[512:768]) + _dot(fn, wo[768:1024])
    x = x_ref[...] + mod[2:3] * y
    xo_ref[...] = x

    h2 = _rms_rows(x) * n2g_ref[...] * (1.0 + mod[4:5]) + mod[3:4]
    h2_ref[...] = h2.astype(BF16)
    logits = _dot_hi(h2, rw_ref[...]) + rb_ref[...]
    lane = lax.broadcasted_iota(jnp.int32, logits.shape, 1).astype(F32)
    idx_out = jnp.zeros(logits.shape, F32)
    val_out = jnp.zeros(logits.shape, F32)
    top = None
    den = jnp.zeros((logits.shape[0], 1), F32)
    for kk in range(TOP_K):
        m = jnp.max(logits, axis=-1, keepdims=True)
        idx = jnp.min(jnp.where(logits == m, lane, float(LANES)), axis=-1, keepdims=True)
        if top is None:
            top = m
        e = jnp.exp(m - top)
        den = den + e
        idx_out = jnp.where(lane == kk, idx, idx_out)
        val_out = jnp.where(lane == kk, e, val_out)
        logits = jnp.where(lane == idx, -jnp.inf, logits)
    ti_ref[...] = idx_out.astype(jnp.int32)
    tg_ref[...] = val_out / den


def _mix(xs, mod, a, hy, o2, og, fy, fnet_w, mix_g, head_mean, w_out, n2g, rw, rb, n_lat_tiles):
    B, S, D = xs.shape
    tm = TOK_TILE
    tok = lambda w: pl.BlockSpec((None, tm, w), lambda b, t: (b, t, 0))
    return pl.pallas_call(
        _mix_kernel,
        grid=(B, S // tm),
        in_specs=[tok(D),
                  pl.BlockSpec((None, 6, D), lambda b, t: (jnp.where(t >= n_lat_tiles, 2, b), 0, 0)),
                  tok(256), tok(256),
                  pl.BlockSpec((None, None, tm, 256), lambda b, t: (b, 0, t, 0)),
                  pl.BlockSpec((None, None, tm, 256), lambda b, t: (b, 1, t, 0)),
                  tok(256), tok(256),
                  _full((256, 256)), _full((1, D)), _full((256, 256)), _full((D, D)), _full((1, D)),
                  _full((D, LANES)), _full((1, LANES))],
        out_specs=[tok(D), tok(D), tok(LANES), tok(LANES)],
        out_shape=[jax.ShapeDtypeStruct((B, S, D), F32), jax.ShapeDtypeStruct((B, S, D), BF16),
                   jax.ShapeDtypeStruct((B, S, LANES), jnp.int32), jax.ShapeDtypeStruct((B, S, LANES), F32)],
        compiler_params=_params("arbitrary", "arbitrary"),
        name="mix_out_router",
    )(xs, mod, a, hy, o2, o2, og, fy, fnet_w, mix_g, head_mean, w_out, n2g, rw, rb)


def _route(top_i):
    T = top_i.shape[0]
    bm = MOE_BM
    n_assign = T * TOP_K
    e_flat = top_i.reshape(-1)
    order = jnp.argsort(e_flat)
    e_s = e_flat[order]
    counts = jnp.bincount(e_flat, length=N_EXPERTS)
    start = jnp.cumsum(counts) - counts
    padded = (counts + bm - 1) // bm * bm
    pend = jnp.cumsum(padded)
    pstart = pend - padded
    dest = (pstart[e_s] + (jnp.arange(n_assign, dtype=jnp.int32) - start[e_s])).astype(jnp.int32)
    n_blocks = (n_assign + bm - 1) // bm + N_EXPERTS
    n_rows = n_blocks * bm
    row_tok = jnp.zeros((n_rows,), jnp.int32).at[dest].set((order // TOP_K).astype(jnp.int32))
    pos = jnp.zeros((n_assign,), jnp.int32).at[order].set(dest)
    blk_e = jnp.minimum(jnp.searchsorted(pend, jnp.arange(n_blocks, dtype=pend.dtype) * bm, side='right'),
                        N_EXPERTS - 1).astype(jnp.int32)
    n_used = (pend[-1] // bm).astype(jnp.int32).reshape(1)
    return row_tok, blk_e, pos, n_used


def _gather_rows_kernel(idx_ref, src_ref, o_ref, sem):
    n = o_ref.shape[0]

    def copy(r):
        return pltpu.make_async_copy(src_ref.at[pl.ds(idx_ref[0, r], 1)], o_ref.at[pl.ds(r, 1)], sem)

    def start(r, c):
        copy(r).start()
        return c

    def wait(r, c):
        copy(r).wait()
        return c

    lax.fori_loop(0, n, start, 0)
    lax.fori_loop(0, n, wait, 0)


def _gather_rows(src, idx, rows_per_step):
    n = idx.shape[0]
    d = src.shape[1]
    nb = n // rows_per_step
    return pl.pallas_call(
        _gather_rows_kernel,
        grid=(nb,),
        in_specs=[pl.BlockSpec((None, 1, rows_per_step), lambda i: (i, 0, 0), memory_space=pltpu.SMEM),
                  pl.BlockSpec(memory_space=pl.ANY)],
        out_specs=pl.BlockSpec((rows_per_step, d), lambda i: (i, 0)),
        out_shape=jax.ShapeDtypeStruct((n, d), src.dtype),
        scratch_shapes=[pltpu.SemaphoreType.DMA(())],
        compiler_params=_params("arbitrary"),
        name="moe_gather",
    )(idx.reshape(nb, 1, rows_per_step), src)


def _expert_kernel(be_ref, nu_ref, x_ref, wg_ref, wu_ref, bg_ref, bu_ref, wd_ref, bd_ref, o_ref):
    i = pl.program_id(0)

    @pl.when(i < nu_ref[0])
    def _():
        x = x_ref[...]
        gate = jnp.minimum(jnp.dot(x, wg_ref[...], preferred_element_type=F32) + bg_ref[...], SWIGLU_LIMIT)
        up = jnp.clip(jnp.dot(x, wu_ref[...], preferred_element_type=F32) + bu_ref[...], -SWIGLU_LIMIT, SWIGLU_LIMIT)
        glu = gate / (1.0 + jnp.exp(-gate * SWIGLU_ALPHA))
        o_ref[...] = _dot((up + 1.0) * glu, wd_ref[...]) + bd_ref[...]

    @pl.when(i >= nu_ref[0])
    def _():
        o_ref[...] = jnp.zeros_like(o_ref)


def _experts(xg, blk_e, n_used, wg, wu, bg, bu, wd, bd):
    n_rows, D = xg.shape
    bm = MOE_BM
    F = wg.shape[2]
    wsel = lambda r, c: pl.BlockSpec((None, r, c), lambda i, be, nu: (be[i], 0, 0))
    return pl.pallas_call(
        _expert_kernel,
        grid_spec=pltpu.PrefetchScalarGridSpec(
            num_scalar_prefetch=2,
            grid=(n_rows // bm,),
            in_specs=[pl.BlockSpec((bm, D), lambda i, be, nu: (i, 0)),
                      wsel(D, F), wsel(D, F), wsel(1, F), wsel(1, F), wsel(F, D), wsel(1, D)],
            out_specs=pl.BlockSpec((bm, D), lambda i, be, nu: (i, 0)),
        ),
        out_shape=jax.ShapeDtypeStruct((n_rows, D), F32),
        compiler_params=_params("arbitrary"),
        name="moe_experts",
    )(blk_e, n_used, xg, wg, wu, bg, bu, wd, bd)


def _combine_kernel(x_ref, mod_ref, tg_ref, y_ref, o_ref):
    tg = tg_ref[...]
    f = jnp.zeros(x_ref.shape, F32)
    d = x_ref.shape[1]
    for kk in range(TOP_K):
        f = f + tg[:, kk:kk + 1] * y_ref[:, kk * d:(kk + 1) * d]
    o_ref[...] = x_ref[...] + mod_ref[...][5:6] * f


def _combine(xs, mod, tg, yk, n_lat_tiles):
    B, S, D = xs.shape
    tm = TOK_TILE
    return pl.pallas_call(
        _combine_kernel,
        grid=(B, S // tm),
        in_specs=[pl.BlockSpec((None, tm, D), lambda b, t: (b, t, 0)),
                  pl.BlockSpec((None, 6, D), lambda b, t: (jnp.where(t >= n_lat_tiles, 2, b), 0, 0)),
                  pl.BlockSpec((None, tm, LANES), lambda b, t: (b, t, 0)),
                  pl.BlockSpec((None, tm, TOP_K * D), lambda b, t: (b, t, 0))],
        out_specs=pl.BlockSpec((None, tm, D), lambda b, t: (b, t, 0)),
        out_shape=jax.ShapeDtypeStruct((B, S, D), F32),
        compiler_params=_params("arbitrary", "arbitrary"),
        name="moe_combine",
    )(xs, mod, tg, yk)


def _final_kernel(x_ref, g_ref, o_ref):
    o_ref[...] = _rms_rows(x_ref[...]) * g_ref[...]


def _final_norm(xs, g, n_lat):
    B, S, D = xs.shape
    tm = 512
    return pl.pallas_call(
        _final_kernel,
        grid=(B, n_lat // tm),
        in_specs=[pl.BlockSpec((None, tm, D), lambda b, t: (b, t, 0)), _full((1, D))],
        out_specs=pl.BlockSpec((None, tm, D), lambda b, t: (b, t, 0)),
        out_shape=jax.ShapeDtypeStruct((B, n_lat, D), F32),
        compiler_params=_params("arbitrary", "arbitrary"),
        name="final_norm",
    )(xs, g.reshape(1, D))


def _prep_w_in(w_in):
    cq, ckv, kr = w_in[:, 0:256], w_in[:, 256:384], w_in[:, 384:416]
    hy, gq, gk = w_in[:, 416:1184], w_in[:, 1184:1312], w_in[:, 1312:1440]
    gv, glr, og, fn = w_in[:, 1440:1696], w_in[:, 1696:1728], w_in[:, 1728:1984], w_in[:, 1984:2240]
    half = MLA_ROPE // 2
    kr_sw = jnp.concatenate([-kr[:, half:], kr[:, :half]], axis=1)
    place = lambda w: jnp.tile(jnp.pad(w, ((0, 0), (MLA_NOPE, LANES - MLA_NOPE - MLA_ROPE))), (1, MLA_HEADS))
    glr_p = jnp.pad(glr, ((0, 0), (0, LANES - 2 * GLA_LR)))
    return jnp.concatenate([cq, ckv, place(kr), place(kr_sw), hy, gq, gk, gv, glr_p, og, fn], axis=1).astype(BF16)


def _prep_wq(w_uq):
    w = w_uq.reshape(MLA_Q_LORA, MLA_HEADS, MLA_NOPE + MLA_ROPE)
    nope, rope = w[..., :MLA_NOPE], w[..., MLA_NOPE:]
    half = MLA_ROPE // 2
    z_tail = jnp.zeros((MLA_Q_LORA, MLA_HEADS, LANES - MLA_NOPE - MLA_ROPE), F32)
    z_nope = jnp.zeros((MLA_Q_LORA, MLA_HEADS, MLA_NOPE), F32)
    plain = jnp.concatenate([nope, rope, z_tail], axis=-1).reshape(MLA_Q_LORA, MLA_HEADS * LANES)
    partner = jnp.concatenate([z_nope, -rope[..., half:], rope[..., :half], z_tail], axis=-1)
    return jnp.concatenate([plain, partner.reshape(MLA_Q_LORA, MLA_HEADS * LANES)], axis=1).astype(BF16)


def _prep_wkv(w_ukv):
    w = w_ukv.reshape(MLA_KV_LORA, MLA_HEADS, MLA_NOPE + MLA_V)
    k_nope, v = w[..., :MLA_NOPE], w[..., MLA_NOPE:]
    k_placed = jnp.pad(k_nope, ((0, 0), (0, 0), (0, LANES - MLA_NOPE))).reshape(MLA_KV_LORA, MLA_HEADS * LANES)
    return jnp.concatenate([k_placed, v.reshape(MLA_KV_LORA, MLA_HEADS * MLA_V)], axis=1).astype(BF16)


def _prep_gk(gk_w, gk_b):
    w = jnp.zeros((LANES, 2 * LANES), F32)
    w = w.at[0:GLA_LR, 0:LANES].set(gk_w[0]).at[GLA_LR:2 * GLA_LR, LANES:].set(gk_w[1])
    return w.astype(BF16), jnp.concatenate([gk_b[0], gk_b[1]]).reshape(1, 2 * LANES)


def _rope_tables(n_lat, n_ctx):
    rows = n_lat // GRID_W
    row = jnp.repeat(jnp.arange(rows, dtype=F32), GRID_W)
    col = jnp.tile(jnp.arange(GRID_W, dtype=F32), rows)
    n_freq = MLA_ROPE // 4
    inv = ROPE_THETA ** (-jnp.arange(n_freq, dtype=F32) / n_freq)
    ang = jnp.concatenate([row[:, None] * inv, col[:, None] * inv], axis=-1)
    cos = jnp.concatenate([jnp.cos(ang), jnp.ones((n_ctx, MLA_ROPE // 2), F32)], axis=0)
    sin = jnp.concatenate([jnp.sin(ang), jnp.zeros((n_ctx, MLA_ROPE // 2), F32)], axis=0)
    S = n_lat + n_ctx
    ones, zeros = jnp.ones((S, MLA_NOPE), F32), jnp.zeros((S, MLA_NOPE), F32)
    tail = jnp.zeros((S, LANES - MLA_NOPE - MLA_ROPE), F32)
    cq = jnp.concatenate([ones, cos, cos, tail], axis=1) * MLA_SCALE
    sq = jnp.concatenate([zeros, sin, sin, tail], axis=1) * MLA_SCALE
    ck = jnp.concatenate([zeros, cos, cos, tail], axis=1)
    sk = jnp.concatenate([zeros, sin, sin, tail], axis=1)
    return cq, sq, ck, sk


def kernel(x, c, ctx, c_ctx, ada_w, ada_b, norm1_g, norm2_g, w_in, mla_q_g, mla_w_uq, mla_kv_g, mla_w_ukv, hy_conv_w, hy_conv_b, hy_w1, hy_b1, hy_w2, hy_b2, hy_w3, hy_freq, hy_bias, gla_gk_w, gla_gk_b, fnet_w, mix_g, w_out, router_w, router_b, moe_w_gu, moe_b_gu, moe_w_down, moe_b_down, final_g):
    B, L, D = x.shape
    Lc = ctx.shape[1]
    depth = ada_w.shape[0]
    S = L + Lc
    tm = TOK_TILE
    nlt = L // tm
    assert B == 2 and D == D_MODEL and Lc == tm and L % (2 * DFT_N2 * 8) == 0

    xs = jnp.concatenate([x, ctx], axis=1)
    cc = jnp.zeros((8, D), F32).at[0:B].set(c).at[B].set(c_ctx)
    mods = _mods(cc, ada_w, ada_b)
    tabs = _rope_tables(L, Lc)
    jc = np.arange(GROUP_W)
    ang_c = 2.0 * np.pi * ((jc[:, None] * jc[None, :]) % GROUP_W) / GROUP_W
    csc = jnp.asarray(np.concatenate([np.cos(ang_c), -np.sin(ang_c)], axis=1), F32)
    head_mean = jnp.asarray(np.kron(np.eye(GROUP_W // MLA_V), np.full((MLA_V, MLA_V), 1.0 / MLA_V)), F32)

    for i in range(depth):
        mod = mods[i].reshape(8, 6, D)
        gkw, gkb = _prep_gk(gla_gk_w[i], gla_gk_b[i])
        q, k, v, hyz, gqk, gv, gg, og, fre, fim = _proj(
            xs, mod, norm1_g[i].reshape(1, D), _prep_w_in(w_in[i]), mla_q_g[i].reshape(1, -1), _prep_wq(mla_w_uq[i]),
            mla_kv_g[i].reshape(1, -1), _prep_wkv(mla_w_ukv[i]), gkw, gkb, csc, tabs, nlt)

        a = _attention(q, k, v, L, Lc)

        filt = (hy_w1[i], hy_b1[i], hy_w2[i], hy_b2[i], hy_w3[i], hy_freq[i])
        vg_l, x0_l = _hy_pre(hyz, hy_conv_w[i], hy_conv_b[i], 0, L)
        hy_l = _hyena_long(vg_l, x0_l, *_circular_filter(*_hyena_filter(L, *filt)), hy_bias[i])
        vg_c, x0_c = _hy_pre(hyz, hy_conv_w[i], hy_conv_b[i], L, Lc)
        hy_c = _hyena_ctx(vg_c, x0_c, *_circular_filter(*_hyena_filter(Lc, *filt)), hy_bias[i])
        hy = jnp.concatenate([hy_l, hy_c], axis=1)

        o2 = _gla(gqk, gv, gg, L)

        fy = jnp.concatenate([_fnet_long(fre, fim, L), _fnet_ctx(fre, fim, L, Lc)], axis=1)

        rw = jnp.pad(router_w[i], ((0, 0), (0, LANES - N_EXPERTS)))
        rb = jnp.concatenate([router_b[i], jnp.full((LANES - N_EXPERTS,), -1e30, F32)]).reshape(1, LANES)
        xs, h2, top_i, top_g = _mix(xs, mod, a, hy, o2, og, fy, fnet_w[i].astype(BF16), mix_g[i].reshape(1, D),
                                    head_mean, w_out[i].astype(BF16), norm2_g[i].reshape(1, D), rw, rb, nlt)

        row_tok, blk_e, pos, n_used = _route(top_i.reshape(B * S, LANES)[:, :TOP_K])
        h2_words = lax.bitcast_convert_type(h2.reshape(B * S, D // 2, 2), jnp.uint32)
        xg = lax.bitcast_convert_type(_gather_rows(h2_words, row_tok, MOE_BM), BF16).reshape(-1, D)
        w_gu = moe_w_gu[i]
        yb = _experts(xg, blk_e, n_used, w_gu[:, :, 0::2].astype(BF16), w_gu[:, :, 1::2].astype(BF16),
                      moe_b_gu[i][:, None, 0::2], moe_b_gu[i][:, None, 1::2],
                      moe_w_down[i].astype(BF16), moe_b_down[i][:, None, :])
        yk = _gather_rows(yb, pos, 512).reshape(B, S, TOP_K * D)
        xs = _combine(xs, mod, top_g, yk, nlt)

    return _final_norm(xs, final_g, L)
```

```python
import functools
import math

import numpy as np
import jax
import jax.numpy as jnp
from jax import lax
from jax.experimental import pallas as pl
from jax.experimental.pallas import tpu as pltpu

F32 = jnp.float32
BF16 = jnp.bfloat16
HIGHEST = lax.Precision.HIGHEST

EPS = 1e-6
D_MODEL = 1024
GROUP_W = 256
MLA_HEADS = 4
MLA_NOPE = 64
MLA_ROPE = 32
MLA_V = 64
MLA_Q_LORA = 256
MLA_KV_LORA = 128
MLA_SCALE = (MLA_NOPE + MLA_ROPE) ** -0.5
ROPE_THETA = 10000.0
GRID_W = 64
HY_EMB = 33
HY_FFN = 64
HY_TARGET = 1e-2
HY_FAST_PCT = 0.3
HY_SLOW_PCT = 1.5
GLA_HEADS = 4
GLA_DK = 32
GLA_DV = 64
GLA_LR = 16
GLA_TAU = 16.0
GLA_CHUNK = 64
N_EXPERTS = 32
TOP_K = 4
SWIGLU_ALPHA = 1.702
SWIGLU_LIMIT = 7.0

LANES = 128
TOK_TILE = 256
DFT_N2 = 128
MOE_BM = 256
VMEM_LIMIT = 56 * 1024 * 1024

_O_CQ, _O_CKV, _O_KRP, _O_KRS, _O_HY, _O_GQ, _O_GK, _O_GV, _O_GLR, _O_OG, _O_FN, _W_ALL = (
    0, 256, 384, 896, 1408, 2176, 2304, 2432, 2688, 2816, 3072, 3328)


def _dot(a, b):
    return jnp.dot(a.astype(BF16), b.astype(BF16), preferred_element_type=F32)


def _dot_hi(a, b):
    return jnp.dot(a, b, precision=HIGHEST, preferred_element_type=F32)


def _params(*sem):
    return pltpu.CompilerParams(dimension_semantics=sem, vmem_limit_bytes=VMEM_LIMIT)


def _rms_rows(x):
    return x * lax.rsqrt(jnp.mean(x * x, axis=-1, keepdims=True) + EPS)


def _full(shape):
    n = len(shape)
    return pl.BlockSpec(shape, lambda *_: (0,) * n)


def _mods_kernel(c_ref, w_ref, b_ref, o_ref):
    c = c_ref[...]
    s = c / (1.0 + jnp.exp(-c))
    o_ref[...] = _dot_hi(s, w_ref[...]) + b_ref[...]


def _mods(cc, ada_w, ada_b):
    depth, d, n = ada_w.shape
    tn = 1024
    return pl.pallas_call(
        _mods_kernel,
        grid=(depth, n // tn),
        in_specs=[pl.BlockSpec((8, d), lambda i, j: (0, 0)),
                  pl.BlockSpec((None, d, tn), lambda i, j: (i, 0, j)),
                  pl.BlockSpec((None, 1, tn), lambda i, j: (i, 0, j))],
        out_specs=pl.BlockSpec((None, 8, tn), lambda i, j: (i, 0, j)),
        out_shape=jax.ShapeDtypeStruct((depth, 8, n), F32),
        compiler_params=_params("arbitrary", "arbitrary"),
        name="ada_mods",
    )(cc, ada_w, ada_b.reshape(depth, 1, n))


def _proj_kernel(x_ref, mod_ref, g_ref, win_ref, qg_ref, wq_ref, kvg_ref, wkv_ref, gkw_ref, gkb_ref, csc_ref,
                 cq_ref, sq_ref, ck_ref, sk_ref,
                 q_out, k_out, v_out, hy_out, gqk_out, gv_out, gg_out, og_out, fre_out, fim_out):
    x = x_ref[...]
    mod = mod_ref[...]
    h = _rms_rows(x) * g_ref[...] * (1.0 + mod[1:2]) + mod[0:1]
    z = _dot(h, win_ref[...])

    nq = _rms_rows(z[:, _O_CQ:_O_CKV]) * qg_ref[...]
    qq = _dot(nq, wq_ref[...])
    nkv = _rms_rows(z[:, _O_CKV:_O_KRP]) * kvg_ref[...]
    kvu = _dot(nkv, wkv_ref[...])
    cq, sq, ck, sk = cq_ref[...], sq_ref[...], ck_ref[...], sk_ref[...]
    for hd in range(MLA_HEADS):
        a, b = hd * LANES, (hd + 1) * LANES
        q_out[:, a:b] = (qq[:, a:b] * cq + qq[:, 512 + a:512 + b] * sq).astype(BF16)
        k_out[:, a:b] = (kvu[:, a:b] + z[:, _O_KRP + a:_O_KRP + b] * ck
                         + z[:, _O_KRS + a:_O_KRS + b] * sk).astype(BF16)
    v_out[...] = kvu[:, 512:768].astype(BF16)

    hy_out[...] = z[:, _O_HY:_O_GQ]
    gqk_out[:, :LANES] = z[:, _O_GQ:_O_GK] * (GLA_DK ** -0.5)
    gqk_out[:, LANES:] = z[:, _O_GK:_O_GV]
    gv_out[...] = z[:, _O_GV:_O_GLR]
    gates = _dot(z[:, _O_GLR:_O_OG], gkw_ref[...]) + gkb_ref[...]
    gg_out[...] = (jnp.minimum(gates, 0.0) - jnp.log(1.0 + jnp.exp(-jnp.abs(gates)))) * (1.0 / GLA_TAU)
    og_out[...] = z[:, _O_OG:_O_FN]
    fcs = _dot_hi(z[:, _O_FN:_W_ALL], csc_ref[...])
    fre_out[...] = fcs[:, :GROUP_W]
    fim_out[...] = fcs[:, GROUP_W:]


def _proj(xs, mod, g1, win, qg, wq, kvg, wkv, gkw, gkb, csc, tabs, n_lat_tiles):
    B, S, D = xs.shape
    tm = TOK_TILE
    tok = lambda w: pl.BlockSpec((None, tm, w), lambda b, t: (b, t, 0))
    tab = pl.BlockSpec((tm, LANES), lambda b, t: (t, 0))
    shp = lambda w, dt: jax.ShapeDtypeStruct((B, S, w), dt)
    return pl.pallas_call(
        _proj_kernel,
        grid=(B, S // tm),
        in_specs=[tok(D),
                  pl.BlockSpec((None, 6, D), lambda b, t: (jnp.where(t >= n_lat_tiles, 2, b), 0, 0)),
                  _full((1, D)), _full(win.shape), _full((1, MLA_Q_LORA)), _full(wq.shape),
                  _full((1, MLA_KV_LORA)), _full(wkv.shape), _full(gkw.shape), _full(gkb.shape), _full(csc.shape),
                  tab, tab, tab, tab],
        out_specs=[tok(512), tok(512), tok(256), tok(768), tok(256), tok(256), tok(256), tok(256), tok(256), tok(256)],
        out_shape=[shp(512, BF16), shp(512, BF16), shp(256, BF16), shp(768, F32), shp(256, F32), shp(256, F32),
                   shp(256, F32), shp(256, F32), shp(256, F32), shp(256, F32)],
        compiler_params=_params("arbitrary", "arbitrary"),
        name="in_proj",
    )(xs, mod, g1, win, qg, wq, kvg, wkv, gkw, gkb, csc, *tabs)


def _attn_kernel(q_ref, k_ref, v_ref, o_ref, *, n_lat_tiles, tk, n_lat, n_ctx):
    qi = pl.program_id(2)
    tq = q_ref.shape[0]

    def heads(k0, n_chunks, width):
        def body(c, carry):
            off = pl.multiple_of(k0 + c * width, 256)
            out = []
            for hd in range(2):
                m, l, acc = carry[hd]
                q = q_ref[:, hd * LANES:(hd + 1) * LANES]
                kc = k_ref[pl.ds(off, width), hd * LANES:(hd + 1) * LANES]
                vc = v_ref[pl.ds(off, width), hd * MLA_V:(hd + 1) * MLA_V]
                s = lax.dot_general(q, kc, (((1,), (1,)), ((), ())), preferred_element_type=F32)
                m_new = jnp.maximum(m, jnp.max(s, axis=-1, keepdims=True))
                alpha = jnp.exp2(m - m_new)
                p = jnp.exp2(s - m_new)
                l = alpha * l + jnp.sum(p, axis=-1, keepdims=True)
                acc = alpha * acc + jnp.dot(p.astype(BF16), vc, preferred_element_type=F32)
                out.append((m_new, l, acc))
            return tuple(out)

        one = (jnp.full((tq, 1), -1e30, F32), jnp.zeros((tq, 1), F32), jnp.zeros((tq, MLA_V), F32))
        res = lax.fori_loop(0, n_chunks, body, (one, one))
        for hd in range(2):
            _, l, acc = res[hd]
            o_ref[:, hd * MLA_V:(hd + 1) * MLA_V] = acc / l

    @pl.when(qi < n_lat_tiles)
    def _():
        heads(0, (n_lat + n_ctx) // tk, tk)

    @pl.when(qi >= n_lat_tiles)
    def _():
        heads(n_lat, 1, n_ctx)


def _attention(q, k, v, n_lat, n_ctx):
    B, S, _ = q.shape
    tq = TOK_TILE
    tk = 1280 if S % 1280 == 0 else 256
    kern = functools.partial(_attn_kernel, n_lat_tiles=n_lat // tq, tk=tk, n_lat=n_lat, n_ctx=n_ctx)
    return pl.pallas_call(
        kern,
        grid=(B, 2, S // tq),
        in_specs=[pl.BlockSpec((None, tq, 256), lambda b, p, t: (b, t, p)),
                  pl.BlockSpec((None, S, 256), lambda b, p, t: (b, 0, p)),
                  pl.BlockSpec((None, S, 128), lambda b, p, t: (b, 0, p))],
        out_specs=pl.BlockSpec((None, tq, 128), lambda b, p, t: (b, t, p)),
        out_shape=jax.ShapeDtypeStruct((B, S, 256), F32),
        compiler_params=_params("arbitrary", "arbitrary", "arbitrary"),
        name="mla_attention",
    )(q, k, v)


def _hy_pre_kernel(z_ref, zp_ref, zn_ref, w_ref, b_ref, vg_ref, x0_ref, *, n_tiles):
    i = pl.program_id(1)
    z = z_ref[...]
    tm = z.shape[0]
    rows = lax.broadcasted_iota(jnp.int32, z.shape, 0)
    prev_row = jnp.where(i == 0, 0.0, zp_ref[7:8, :])
    next_row = jnp.where(i == n_tiles - 1, 0.0, zn_ref[0:1, :])
    z_m = jnp.where(rows == 0, prev_row, pltpu.roll(z, 1, 0))
    z_p = jnp.where(rows == tm - 1, next_row, pltpu.roll(z, tm - 1, 0))
    w = w_ref[...]
    u = z_m * w[0:1] + z * w[1:2] + z_p * w[2:3] + b_ref[...]
    vg_ref[...] = u[:, 2 * GROUP_W:] * u[:, GROUP_W:2 * GROUP_W]
    x0_ref[...] = u[:, :GROUP_W]


def _hy_pre(hyz, conv_w, conv_b, row0, n_rows):
    B, S, W = hyz.shape
    tm = TOK_TILE
    nt = n_rows // tm
    t0, r8, last8 = row0 // tm, row0 // 8, S // 8 - 1
    kern = functools.partial(_hy_pre_kernel, n_tiles=nt)
    out = jax.ShapeDtypeStruct((B, n_rows, GROUP_W), F32)
    return pl.pallas_call(
        kern,
        grid=(B, nt),
        in_specs=[pl.BlockSpec((None, tm, W), lambda b, i: (b, t0 + i, 0)),
                  pl.BlockSpec((None, 8, W), lambda b, i: (b, jnp.maximum(r8 + i * (tm // 8) - 1, 0), 0)),
                  pl.BlockSpec((None, 8, W), lambda b, i: (b, jnp.minimum(r8 + (i + 1) * (tm // 8), last8), 0)),
                  _full((3, W)), _full((1, W))],
        out_specs=[pl.BlockSpec((None, tm, GROUP_W), lambda b, i: (b, i, 0))] * 2,
        out_shape=[out, out],
        compiler_params=_params("arbitrary", "arbitrary"),
        name="hyena_pre",
    )(hyz, hyz, hyz, conv_w, conv_b.reshape(1, W))


def _filter_kernel(fv_ref, w1_ref, b1_ref, w2_ref, b2_ref, w3_ref, fr_ref, dl_ref, h_ref, ss_ref, *, n_pos):
    i = pl.program_id(0)
    tl = h_ref.shape[0]
    row = lax.broadcasted_iota(jnp.int32, (tl, LANES), 0) + i * tl
    pos = jnp.where(row <= n_pos, row, 2 * n_pos - row).astype(F32)
    lane = lax.broadcasted_iota(jnp.int32, (tl, LANES), 1)
    t = pos * (1.0 / (n_pos - 1))
    arg = (pos * (2.0 * math.pi / n_pos)) * fv_ref[...]
    feat = jnp.where(lane == 0, t, jnp.where(lane < 17, jnp.cos(arg), jnp.where(lane < HY_EMB, -jnp.sin(arg), 0.0)))
    fr = fr_ref[...]
    h = jnp.sin(fr * (_dot_hi(feat, w1_ref[...]) + b1_ref[...]))
    h = jnp.sin(fr * (_dot_hi(h, w2_ref[...]) + b2_ref[...]))
    h = _dot_hi(h, w3_ref[...]) * jnp.exp(-t[:, 0:1] * dl_ref[...])
    r1 = row[:, 0:1]
    h = jnp.where(r1 < n_pos, h[:, :GROUP_W], jnp.where(r1 == n_pos, 0.0, h[:, GROUP_W:]))
    h_ref[...] = h

    @pl.when(i == 0)
    def _():
        ss_ref[...] = jnp.zeros_like(ss_ref)

    ss_ref[...] += jnp.sum(h * h, axis=0, keepdims=True)


def _hyena_filter(n_pos, w1, b1, w2, b2, w3, freq):
    tl = min(2 * n_pos, 512)
    bands = (HY_EMB - 1) // 2
    f = np.linspace(1e-4, bands - 1, bands)
    fv = np.zeros((1, LANES), np.float32)
    fv[0, 1:17] = f
    fv[0, 17:33] = f
    w1p = jnp.zeros((LANES, HY_FFN), F32).at[:HY_EMB].set(w1)
    max_decay = math.log(HY_TARGET) / HY_FAST_PCT
    min_decay = math.log(HY_TARGET) / HY_SLOW_PCT
    deltas = np.abs(np.linspace(min_decay, max_decay, GROUP_W)).astype(np.float32)
    dl = jnp.asarray(np.concatenate([deltas, deltas])[None, :])
    kern = functools.partial(_filter_kernel, n_pos=n_pos)
    k2, ss = pl.pallas_call(
        kern,
        grid=(2 * n_pos // tl,),
        in_specs=[_full((1, LANES)), _full((LANES, HY_FFN)), _full((1, HY_FFN)), _full((HY_FFN, HY_FFN)),
                  _full((1, HY_FFN)), _full((HY_FFN, 2 * GROUP_W)), _full((1, HY_FFN)), _full((1, 2 * GROUP_W))],
        out_specs=[pl.BlockSpec((tl, GROUP_W), lambda i: (i, 0)), _full((1, GROUP_W))],
        out_shape=[jax.ShapeDtypeStruct((2 * n_pos, GROUP_W), F32), jax.ShapeDtypeStruct((1, GROUP_W), F32)],
        compiler_params=_params("arbitrary"),
        name="hyena_filter",
    )(jnp.asarray(fv), w1p, b1.reshape(1, -1), w2, b2.reshape(1, -1), w3, freq.reshape(1, -1), dl)
    return k2, lax.rsqrt(ss)


def _dft_mats(n, sign):
    j = np.arange(n)
    ang = 2.0 * np.pi * ((j[:, None] * j[None, :]) % n) / n
    return np.cos(ang), sign * np.sin(ang)


def _twiddle(n1, n2, sign):
    ang = 2.0 * np.pi * ((np.arange(n1)[:, None] * np.arange(n2)[None, :]) % (n1 * n2)) / (n1 * n2)
    return np.cos(ang), sign * np.sin(ang)


def _block_complex(re, im):
    return np.block([[re, -im], [im, re]])


def _stage1_kernel(m_ref, zr_ref, zi_ref, ar_ref, ai_ref):
    a = _dot_hi(m_ref[...], jnp.concatenate([zr_ref[...], zi_ref[...]], axis=0))
    half = ar_ref.shape[0]
    ar_ref[...] = a[:half]
    ai_ref[...] = a[half:]


def _stage1_real_kernel(m_ref, zr_ref, ar_ref, ai_ref):
    a = _dot_hi(m_ref[...], zr_ref[...])
    half = ar_ref.shape[0]
    ar_ref[...] = a[:half]
    ai_ref[...] = a[half:]


def _col_tile(cols):
    return 2048 if cols % 2048 == 0 else cols


def _hy_stage1(vg2, n1):
    _, r, cols = vg2.shape
    cr, ci = _dft_mats(n1, -1.0)
    m = jnp.asarray(_block_complex(cr[:, :r], ci[:, :r]), F32)
    tc = _col_tile(cols)
    out = jax.ShapeDtypeStruct((n1, cols), F32)
    return pl.pallas_call(
        _stage1_kernel,
        grid=(cols // tc,),
        in_specs=[_full(m.shape), pl.BlockSpec((None, r, tc), lambda j: (0, 0, j)),
                  pl.BlockSpec((None, r, tc), lambda j: (1, 0, j))],
        out_specs=[pl.BlockSpec((n1, tc), lambda j: (0, j))] * 2,
        out_shape=[out, out],
        compiler_params=_params("arbitrary"),
        name="hyena_fwd_stage1",
    )(m, vg2, vg2)


def _filter_stage1(k2v, n1):
    _, cols = k2v.shape
    cr, ci = _dft_mats(n1, -1.0)
    m = jnp.asarray(np.concatenate([cr, ci], axis=0), F32)
    tc = _col_tile(cols)
    out = jax.ShapeDtypeStruct((n1, cols), F32)
    return pl.pallas_call(
        _stage1_real_kernel,
        grid=(cols // tc,),
        in_specs=[_full(m.shape), pl.BlockSpec((n1, tc), lambda j: (0, j))],
        out_specs=[pl.BlockSpec((n1, tc), lambda j: (0, j))] * 2,
        out_shape=[out, out],
        compiler_params=_params("arbitrary"),
        name="hyena_filter_stage1",
    )(m, k2v)


def _twiddled(f_re, f_im, t_re, t_im):
    g_re = f_re * t_re - f_im * t_im
    g_im = f_re * t_im + f_im * t_re
    return jnp.concatenate([jnp.concatenate([g_re, -g_im], axis=1), jnp.concatenate([g_im, g_re], axis=1)], axis=0)


def _filter_stage2_kernel(ar_ref, ai_ref, twr_ref, twi_ref, fr_ref, fi_ref, sc_ref, kr_ref, ki_ref):
    g = _twiddled(fr_ref[...], fi_ref[...], twr_ref[...], twi_ref[...])
    x = _dot_hi(g, jnp.concatenate([ar_ref[...], ai_ref[...]], axis=0)) * sc_ref[...]
    n2 = kr_ref.shape[0]
    kr_ref[...] = x[:n2]
    ki_ref[...] = x[n2:]


def _filter_stage2(ar, ai, inv_norm, n1):
    n2 = DFT_N2
    c = ar.shape[1] // n2
    fr, fi = _dft_mats(n2, -1.0)
    twr, twi = _twiddle(n1, n2, -1.0)
    blk = pl.BlockSpec((None, n2, c), lambda k: (k, 0, 0))
    tw = pl.BlockSpec((None, 1, n2), lambda k: (k, 0, 0))
    out = jax.ShapeDtypeStruct((n1, n2, c), F32)
    return pl.pallas_call(
        _filter_stage2_kernel,
        grid=(n1,),
        in_specs=[blk, blk, tw, tw, _full((n2, n2)), _full((n2, n2)), _full((1, c))],
        out_specs=[blk, blk],
        out_shape=[out, out],
        compiler_params=_params("arbitrary"),
        name="hyena_filter_stage2",
    )(ar.reshape(n1, n2, c), ai.reshape(n1, n2, c), jnp.asarray(twr.reshape(n1, 1, n2), F32),
      jnp.asarray(twi.reshape(n1, 1, n2), F32), jnp.asarray(fr, F32), jnp.asarray(fi, F32), inv_norm)


def _hy_stage2_kernel(ar_ref, ai_ref, kr_ref, ki_ref, twr_ref, twi_ref, tcr_ref, tci_ref, fr_ref, fi_ref,
                      br_ref, bi_ref):
    f_re, f_im = fr_ref[...], fi_ref[...]
    n2 = f_re.shape[0]
    g = _twiddled(f_re, f_im, twr_ref[...], twi_ref[...])
    x = _dot_hi(g, jnp.concatenate([ar_ref[...], ai_ref[...]], axis=0))
    x_re, x_im = x[:n2], x[n2:]
    k_re, k_im = kr_ref[...], ki_ref[...]
    y = jnp.concatenate([x_re * k_re - x_im * k_im, x_re * k_im + x_im * k_re], axis=0)
    g_inv = _twiddled(f_re, -f_im, tcr_ref[...], -tci_ref[...])
    b = _dot_hi(g_inv, y)
    br_ref[...] = b[:n2]
    bi_ref[...] = b[n2:]


def _hy_stage2(ar, ai, kr, ki, n1):
    n2 = DFT_N2
    c = kr.shape[2]
    fr, fi = _dft_mats(n2, -1.0)
    twr, twi = _twiddle(n1, n2, -1.0)
    blk = pl.BlockSpec((None, n2, c), lambda k: (k, 0, 0))
    tw = pl.BlockSpec((None, 1, n2), lambda k: (k, 0, 0))
    twc = pl.BlockSpec((None, n2, 1), lambda k: (k, 0, 0))
    out = jax.ShapeDtypeStruct((n1, n2, c), F32)
    return pl.pallas_call(
        _hy_stage2_kernel,
        grid=(n1,),
        in_specs=[blk, blk, blk, blk, tw, tw, twc, twc, _full((n2, n2)), _full((n2, n2))],
        out_specs=[blk, blk],
        out_shape=[out, out],
        compiler_params=_params("arbitrary"),
        name="hyena_conv_stage2",
    )(ar.reshape(n1, n2, c), ai.reshape(n1, n2, c), kr, ki,
      jnp.asarray(twr.reshape(n1, 1, n2), F32), jnp.asarray(twi.reshape(n1, 1, n2), F32),
      jnp.asarray(twr.reshape(n1, n2, 1), F32), jnp.asarray(twi.reshape(n1, n2, 1), F32),
      jnp.asarray(fr, F32), jnp.asarray(fi, F32))


def _hy_stage3_kernel(m_ref, br_ref, bi_ref, vg0_ref, vg1_ref, x00_ref, x01_ref, bias_ref, o_ref):
    conv = _dot_hi(m_ref[...], jnp.concatenate([br_ref[...], bi_ref[...]], axis=0))
    r = vg0_ref.shape[0]
    bias = bias_ref[...]
    o_ref[0] = (conv[:r] + vg0_ref[...] * bias) * x00_ref[...]
    o_ref[1] = (conv[r:] + vg1_ref[...] * bias) * x01_ref[...]


def _hy_stage3(br, bi, vg2, x02, bias_cols, n1):
    _, r, cols = vg2.shape
    cr, ci = _dft_mats(n1, 1.0)
    m = jnp.asarray(_block_complex(cr[:r], ci[:r]) / (n1 * DFT_N2), F32)
    tc = _col_tile(cols)
    plane = lambda p: pl.BlockSpec((None, r, tc), lambda j: (p, 0, j))
    return pl.pallas_call(
        _hy_stage3_kernel,
        grid=(cols // tc,),
        in_specs=[_full(m.shape), pl.BlockSpec((n1, tc), lambda j: (0, j)), pl.BlockSpec((n1, tc), lambda j: (0, j)),
                  plane(0), plane(1), plane(0), plane(1), pl.BlockSpec((1, tc), lambda j: (0, j))],
        out_specs=pl.BlockSpec((2, r, tc), lambda j: (0, 0, j)),
        out_shape=jax.ShapeDtypeStruct((2, r, cols), F32),
        compiler_params=_params("arbitrary"),
        name="hyena_inv_stage3",
    )(m, br, bi, vg2, vg2, x02, x02, bias_cols)


def _hyena_long(vg, x0, k2, inv_norm, bias):
    B, L, C = vg.shape
    assert B == 2
    n2 = DFT_N2
    n1 = 2 * L // n2
    r = L // n2
    fa_r, fa_i = _filter_stage1(k2.reshape(n1, n2 * C), n1)
    kr, ki = _filter_stage2(fa_r, fa_i, inv_norm, n1)
    vg2 = vg.reshape(2, r, n2 * C)
    ar, ai = _hy_stage1(vg2, n1)
    br, bi = _hy_stage2(ar, ai, kr, ki, n1)
    y = _hy_stage3(br.reshape(n1, n2 * C), bi.reshape(n1, n2 * C), vg2, x0.reshape(2, r, n2 * C),
                   jnp.tile(bias.reshape(1, C), (1, n2)), n1)
    return y.reshape(2, L, C)


def _hy_ctx_kernel(vg_ref, x0_ref, kext_ref, sc_ref, bias_ref, o_ref):
    n = vg_ref.shape[0]

    def body(a, acc):
        src = vg_ref[pl.ds(pl.multiple_of(8 * a, 8), 8), :]
        win = pl.multiple_of(n - 8 * a, 8)
        for r in range(8):
            acc = acc + kext_ref[r, pl.ds(win, n), :] * src[r:r + 1]
        return acc

    acc = lax.fori_loop(0, n // 8, body, jnp.zeros(vg_ref.shape, F32))
    o_ref[...] = (acc * sc_ref[...] + vg_ref[...] * bias_ref[...]) * x0_ref[...]


def _hyena_ctx(vg, x0, k2, inv_norm, bias):
    B, n, C = vg.shape
    kext = jnp.concatenate([k2[n:], k2[:n]], axis=0)
    kext = jnp.stack([jnp.roll(kext, r, axis=0) for r in range(8)])
    blk = pl.BlockSpec((None, n, C), lambda b: (b, 0, 0))
    return pl.pallas_call(
        _hy_ctx_kernel,
        grid=(B,),
        in_specs=[blk, blk, _full((8, 2 * n, C)), _full((1, C)), _full((1, C))],
        out_specs=blk,
        out_shape=jax.ShapeDtypeStruct((B, n, C), F32),
        compiler_params=_params("arbitrary"),
        name="hyena_ctx",
    )(vg, x0, kext, inv_norm, bias.reshape(1, C))


def _fn_stage1_kernel(c_ref, s_ref, re_ref, im_ref, ar_ref, ai_ref):
    c, s, re, im = c_ref[...], s_ref[...], re_ref[...], im_ref[...]
    ar_ref[...] = _dot_hi(c, re) + _dot_hi(s, im)
    ai_ref[...] = _dot_hi(c, im) - _dot_hi(s, re)


def _fn_stage2_kernel(ar_ref, ai_ref, twr_ref, twi_ref, fr_ref, fi_ref, o_ref):
    f_re, f_im, t_re, t_im = fr_ref[...], fi_ref[...], twr_ref[...], twi_ref[...]
    g_re = f_re * t_re - f_im * t_im
    g_im = f_re * t_im + f_im * t_re
    o_ref[...] = _dot_hi(g_re, ar_ref[...]) - _dot_hi(g_im, ai_ref[...])


def _fnet_long(fre, fim, n_lat):
    B, S, C = fre.shape
    n2 = DFT_N2
    n1 = n_lat // n2
    cols = n2 * C
    c1, s1 = _dft_mats(n1, 1.0)
    tc = _col_tile(cols)
    rows = pl.BlockSpec((None, n1, tc), lambda b, j: (b, 0, j))
    a_shape = jax.ShapeDtypeStruct((B, n1, cols), F32)
    ar, ai = pl.pallas_call(
        _fn_stage1_kernel,
        grid=(B, cols // tc),
        in_specs=[_full((n1, n1)), _full((n1, n1)), rows, rows],
        out_specs=[rows, rows],
        out_shape=[a_shape, a_shape],
        compiler_params=_params("arbitrary", "arbitrary"),
        name="fnet_stage1",
    )(jnp.asarray(c1, F32), jnp.asarray(s1, F32), fre.reshape(B, S // n2, cols), fim.reshape(B, S // n2, cols))
    fr, fi = _dft_mats(n2, -1.0)
    scale = 1.0 / math.sqrt(n_lat * C)
    twr, twi = _twiddle(n1, n2, -1.0)
    blk = pl.BlockSpec((None, None, n2, C), lambda b, k: (b, k, 0, 0))
    tw = pl.BlockSpec((None, 1, n2), lambda b, k: (k, 0, 0))
    y = pl.pallas_call(
        _fn_stage2_kernel,
        grid=(B, n1),
        in_specs=[blk, blk, tw, tw, _full((n2, n2)), _full((n2, n2))],
        out_specs=pl.BlockSpec((None, n2, C), lambda b, k: (b, 0, k)),
        out_shape=jax.ShapeDtypeStruct((B, n2, n1 * C), F32),
        compiler_params=_params("arbitrary", "arbitrary"),
        name="fnet_stage2",
    )(ar.reshape(B, n1, n2, C), ai.reshape(B, n1, n2, C), jnp.asarray(twr.reshape(n1, 1, n2), F32),
      jnp.asarray(twi.reshape(n1, 1, n2), F32), jnp.asarray(fr * scale, F32), jnp.asarray(fi * scale, F32))
    return y.reshape(B, n_lat, C)


def _fn_ctx_kernel(c_ref, s_ref, re_ref, im_ref, o_ref):
    o_ref[...] = _dot_hi(c_ref[...], re_ref[...]) + _dot_hi(s_ref[...], im_ref[...])


def _fnet_ctx(fre, fim, n_lat, n_ctx):
    B, S, C = fre.shape
    c1, s1 = _dft_mats(n_ctx, 1.0)
    scale = 1.0 / math.sqrt(n_ctx * C)
    blk = pl.BlockSpec((None, n_ctx, C), lambda b: (b, n_lat // n_ctx, 0))
    return pl.pallas_call(
        _fn_ctx_kernel,
        grid=(B,),
        in_specs=[_full((n_ctx, n_ctx)), _full((n_ctx, n_ctx)), blk, blk],
        out_specs=pl.BlockSpec((None, n_ctx, C), lambda b: (b, 0, 0)),
        out_shape=jax.ShapeDtypeStruct((B, n_ctx, C), F32),
        compiler_params=_params("arbitrary"),
        name="fnet_ctx",
    )(jnp.asarray(c1 * scale, F32), jnp.asarray(s1 * scale, F32), fre, fim)


def _gla_kernel(qk_ref, v_ref, g_ref, o_ref, st_ref):
    d = pl.program_id(1)
    n = pl.program_id(2)
    tm = qk_ref.shape[0]
    ck = GLA_CHUNK

    @pl.when(n == 0)
    def _():
        st_ref[...] = jnp.zeros_like(st_ref)

    ri = lax.broadcasted_iota(jnp.int32, (ck, ck), 0)
    ci = lax.broadcasted_iota(jnp.int32, (ck, ck), 1)

    def run(reverse):
        keep = (ci >= ri) if reverse else (ci <= ri)
        tri = keep.astype(F32)
        order = range(tm // ck - 1, -1, -1) if reverse else range(tm // ck)
        for sc in order:
            r0 = sc * ck
            g = g_ref[r0:r0 + ck, :]
            b = _dot_hi(tri, g)
            b_end = b[0:1] if reverse else b[ck - 1:ck]
            q = qk_ref[r0:r0 + ck, :LANES] * jnp.exp(b)
            k = qk_ref[r0:r0 + ck, LANES:]
            k_in = k * jnp.exp(-b)
            k_out = k * jnp.exp(b_end - b)
            decay = jnp.exp(b_end)
            for hd in range(GLA_HEADS):
                ks = slice(hd * GLA_DK, (hd + 1) * GLA_DK)
                vs = slice(hd * GLA_DV, (hd + 1) * GLA_DV)
                qh, vh = q[:, ks].astype(BF16), v_ref[r0:r0 + ck, vs].astype(BF16)
                a = lax.dot_general(qh, k_in[:, ks].astype(BF16), (((1,), (1,)), ((), ())),
                                    preferred_element_type=F32)
                a = jnp.where(keep, a, 0.0)
                st = st_ref[hd]
                o = jnp.dot(a.astype(BF16), vh, preferred_element_type=F32)
                o += lax.dot_general(qh, st.astype(BF16), (((1,), (1,)), ((), ())), preferred_element_type=F32)
                o_ref[r0:r0 + ck, vs] = o
                kv = lax.dot_general(vh, k_out[:, ks].astype(BF16), (((0,), (0,)), ((), ())),
                                     preferred_element_type=F32)
                st_ref[hd] = st * decay[:, ks] + kv

    @pl.when(d == 0)
    def _():
        run(False)

    @pl.when(d == 1)
    def _():
        run(True)


def _gla(gqk, gv, gg, n_lat):
    B, S, _ = gqk.shape
    tm = TOK_TILE
    nl = n_lat // tm
    nt = S // tm
    assert nt == nl + 1

    def row_block(d, n):
        lat = jnp.where(d == 0, n - 1, nl - n)
        return jnp.where(n == 0, nl, lat)

    return pl.pallas_call(
        _gla_kernel,
        grid=(B, 2, nt),
        in_specs=[pl.BlockSpec((None, tm, 256), lambda b, d, n: (b, row_block(d, n), 0)),
                  pl.BlockSpec((None, tm, 256), lambda b, d, n: (b, row_block(d, n), 0)),
                  pl.BlockSpec((None, tm, 128), lambda b, d, n: (b, row_block(d, n), d))],
        out_specs=pl.BlockSpec((None, None, tm, 256), lambda b, d, n: (b, d, row_block(d, n), 0)),
        out_shape=jax.ShapeDtypeStruct((B, 2, S, 256), F32),
        scratch_shapes=[pltpu.VMEM((GLA_HEADS, GLA_DV, GLA_DK), F32)],
        compiler_params=_params("arbitrary", "arbitrary", "arbitrary"),
        name="gla_scan",
    )(gqk, gv, gg)


def _mix_kernel(x_ref, mod_ref, a_ref, hy_ref, of_ref, ob_ref, og_ref, fy_ref, fw_ref, mg_ref, hm_ref, wo_ref,
                n2g_ref, rw_ref, rb_ref, xo_ref, h2_ref, ti_ref, tg_ref, rk_ref, cnt_ref, run_ref):
    first = jnp.logical_and(pl.program_id(0) == 0, pl.program_id(1) == 0)

    @pl.when(first)
    def _():
        run_ref[...] = jnp.zeros_like(run_ref)

    mod = mod_ref[...]
    mg = mg_ref[...]
    hm = hm_ref[...]

    def head_rms(t):
        return t * lax.rsqrt(_dot_hi(t * t, hm) + EPS)

    a = head_rms(a_ref[...]) * mg[:, 0:256]
    hy = _rms_rows(hy_ref[...]) * mg[:, 256:512]
    og = og_ref[...]
    o = head_rms(of_ref[...] + ob_ref[...]) * mg[:, 512:768] * (og / (1.0 + jnp.exp(-og)))
    fn = _rms_rows(_dot(fy_ref[...], fw_ref[...])) * mg[:, 768:1024]
    wo = wo_ref[...]
    y = _dot(a, wo[0:256]) + _dot(hy, wo[256:512]) + _dot(o, wo[512:768]) + _dot(fn, wo[768:1024])
    x = x_ref[...] + mod[2:3] * y
    xo_ref[...] = x

    h2 = _rms_rows(x) * n2g_ref[...] * (1.0 + mod[4:5]) + mod[3:4]
    bits = pltpu.bitcast(h2.astype(BF16).astype(F32), jnp.uint32)
    half = bits.shape[1] // 2
    h2_ref[...] = (bits[:, :half] >> 16) | (bits[:, half:] & jnp.uint32(0xFFFF0000))

    logits = _dot_hi(h2, rw_ref[...]) + rb_ref[...]
    tm = logits.shape[0]
    lane = lax.broadcasted_iota(jnp.int32, logits.shape, 1).astype(F32)
    idx_out = jnp.zeros(logits.shape, F32)
    val_out = jnp.zeros(logits.shape, F32)
    chosen = jnp.zeros(logits.shape, F32)
    picks = []
    top = None
    den = jnp.zeros((tm, 1), F32)
    for kk in range(TOP_K):
        m = jnp.max(logits, axis=-1, keepdims=True)
        idx = jnp.min(jnp.where(logits == m, lane, float(LANES)), axis=-1, keepdims=True)
        if top is None:
            top = m
        e = jnp.exp(m - top)
        den = den + e
        hit = lane == idx
        picks.append(hit)
        chosen = jnp.where(hit, 1.0, chosen)
        idx_out = jnp.where(lane == kk, idx, idx_out)
        val_out = jnp.where(lane == kk, e, val_out)
        logits = jnp.where(hit, -jnp.inf, logits)
    ti_ref[...] = idx_out.astype(jnp.int32)
    tg_ref[...] = val_out / den

    ri = lax.broadcasted_iota(jnp.int32, (tm, tm), 0)
    ci = lax.broadcasted_iota(jnp.int32, (tm, tm), 1)
    before = _dot((ci < ri).astype(F32), chosen) + run_ref[...]
    rank = jnp.zeros(logits.shape, F32)
    for kk in range(TOP_K):
        r = jnp.sum(jnp.where(picks[kk], before, 0.0), axis=-1, keepdims=True)
        rank = jnp.where(lane == kk, r, rank)
    rk_ref[...] = rank.astype(jnp.int32)
    run_ref[...] += jnp.sum(chosen, axis=0, keepdims=True)
    cnt_ref[...] = run_ref[...]


def _mix(xs, mod, a, hy, o2, og, fy, fnet_w, mix_g, head_mean, w_out, n2g, rw, rb, n_lat_tiles):
    B, S, D = xs.shape
    tm = TOK_TILE
    tok = lambda w: pl.BlockSpec((None, tm, w), lambda b, t: (b, t, 0))
    return pl.pallas_call(
        _mix_kernel,
        grid=(B, S // tm),
        in_specs=[tok(D),
                  pl.BlockSpec((None, 6, D), lambda b, t: (jnp.where(t >= n_lat_tiles, 2, b), 0, 0)),
                  tok(256), tok(256),
                  pl.BlockSpec((None, None, tm, 256), lambda b, t: (b, 0, t, 0)),
                  pl.BlockSpec((None, None, tm, 256), lambda b, t: (b, 1, t, 0)),
                  tok(256), tok(256),
                  _full((256, 256)), _full((1, D)), _full((256, 256)), _full((D, D)), _full((1, D)),
                  _full((D, LANES)), _full((1, LANES))],
        out_specs=[tok(D), tok(D // 2), tok(LANES), tok(LANES), tok(LANES), _full((1, LANES))],
        out_shape=[jax.ShapeDtypeStruct((B, S, D), F32), jax.ShapeDtypeStruct((B, S, D // 2), jnp.uint32),
                   jax.ShapeDtypeStruct((B, S, LANES), jnp.int32), jax.ShapeDtypeStruct((B, S, LANES), F32),
                   jax.ShapeDtypeStruct((B, S, LANES), jnp.int32), jax.ShapeDtypeStruct((1, LANES), F32)],
        scratch_shapes=[pltpu.VMEM((1, LANES), F32)],
        compiler_params=_params("arbitrary", "arbitrary"),
        name="mix_out_router",
    )(xs, mod, a, hy, o2, o2, og, fy, fnet_w, mix_g, head_mean, w_out, n2g, rw, rb)


def _route(top_i, rank, counts):
    T = top_i.shape[0]
    bm = MOE_BM
    padded = (counts + bm - 1) // bm * bm
    pend = jnp.cumsum(padded)
    pstart = pend - padded
    pos = (pstart[top_i] + rank).astype(jnp.int32)
    n_blocks = (T * TOP_K + bm - 1) // bm + N_EXPERTS
    blk_e = jnp.minimum(jnp.searchsorted(pend, jnp.arange(n_blocks, dtype=pend.dtype) * bm, side='right'),
                        N_EXPERTS - 1).astype(jnp.int32)
    n_used = (pend[-1] // bm).astype(jnp.int32).reshape(1)
    return pos, blk_e, n_used, n_blocks * bm


def _dispatch_kernel(pos_ref, h_ref, zero_ref, xg_ref, sem):
    del zero_ref
    tm = h_ref.shape[0]

    def copy(t, kk):
        return pltpu.make_async_copy(h_ref.at[pl.ds(t, 1)], xg_ref.at[pl.ds(pos_ref[0, t * TOP_K + kk], 1)], sem)

    def start(t, c):
        for kk in range(TOP_K):
            copy(t, kk).start()
        return c

    def wait(t, c):
        for kk in range(TOP_K):
            copy(t, kk).wait()
        return c

    lax.fori_loop(0, tm, start, 0, unroll=8)
    lax.fori_loop(0, tm, wait, 0, unroll=8)


def _dispatch(h2w, pos, n_rows):
    T, W = h2w.shape
    tm = TOK_TILE
    nb = T // tm
    return pl.pallas_call(
        _dispatch_kernel,
        grid=(nb,),
        in_specs=[pl.BlockSpec((None, 1, tm * TOP_K), lambda i: (i, 0, 0), memory_space=pltpu.SMEM),
                  pl.BlockSpec((tm, W), lambda i: (i, 0)),
                  pl.BlockSpec(memory_space=pl.ANY)],
        out_specs=pl.BlockSpec(memory_space=pl.ANY),
        out_shape=jax.ShapeDtypeStruct((n_rows, W), h2w.dtype),
        scratch_shapes=[pltpu.SemaphoreType.DMA(())],
        input_output_aliases={2: 0},
        compiler_params=_params("arbitrary"),
        name="moe_dispatch",
    )(pos.reshape(nb, 1, tm * TOP_K), h2w, jnp.zeros((n_rows, W), h2w.dtype))


def _gu_prep_kernel(w_ref, p_ref, g_ref, u_ref):
    y = jnp.dot(w_ref[...].astype(BF16), p_ref[...], preferred_element_type=F32)
    half = y.shape[1] // 2
    g_ref[...] = y[:, :half].astype(BF16)
    u_ref[...] = y[:, half:].astype(BF16)


def _gu_prep(w_gu):
    E, D, F2 = w_gu.shape
    tn = 512
    perm = np.zeros((tn, tn), np.float32)
    perm[2 * np.arange(tn // 2), np.arange(tn // 2)] = 1.0
    perm[2 * np.arange(tn // 2) + 1, tn // 2 + np.arange(tn // 2)] = 1.0
    out = jax.ShapeDtypeStruct((E, D, F2 // 2), BF16)
    return pl.pallas_call(
        _gu_prep_kernel,
        grid=(E, F2 // tn),
        in_specs=[pl.BlockSpec((None, D, tn), lambda e, j: (e, 0, j)), _full((tn, tn))],
        out_specs=[pl.BlockSpec((None, D, tn // 2), lambda e, j: (e, 0, j))] * 2,
        out_shape=[out, out],
        compiler_params=_params("arbitrary", "arbitrary"),
        name="moe_weight_prep",
    )(w_gu, jnp.asarray(perm, BF16))


def _expert_kernel(be_ref, nu_ref, x_ref, wg_ref, wu_ref, bg_ref, bu_ref, wd_ref, bd_ref, o_ref):
    i = pl.program_id(0)

    @pl.when(i < nu_ref[0])
    def _():
        xw = x_ref[...]
        x = jnp.concatenate([pltpu.bitcast(xw << 16, F32), pltpu.bitcast(xw & jnp.uint32(0xFFFF0000), F32)],
                            axis=1).astype(BF16)
        gate = jnp.minimum(jnp.dot(x, wg_ref[...], preferred_element_type=F32) + bg_ref[...], SWIGLU_LIMIT)
        up = jnp.clip(jnp.dot(x, wu_ref[...], preferred_element_type=F32) + bu_ref[...], -SWIGLU_LIMIT, SWIGLU_LIMIT)
        glu = gate / (1.0 + jnp.exp(-gate * SWIGLU_ALPHA))
        o_ref[...] = _dot((up + 1.0) * glu, wd_ref[...]) + bd_ref[...]

    @pl.when(i >= nu_ref[0])
    def _():
        o_ref[...] = jnp.zeros_like(o_ref)


def _experts(xg, blk_e, n_used, wg, wu, bg, bu, wd, bd):
    n_rows, W = xg.shape
    bm = MOE_BM
    D, F = wg.shape[1], wg.shape[2]
    wsel = lambda r, c: pl.BlockSpec((None, r, c), lambda i, be, nu: (be[i], 0, 0))
    return pl.pallas_call(
        _expert_kernel,
        grid_spec=pltpu.PrefetchScalarGridSpec(
            num_scalar_prefetch=2,
            grid=(n_rows // bm,),
            in_specs=[pl.BlockSpec((bm, W), lambda i, be, nu: (i, 0)),
                      wsel(D, F), wsel(D, F), wsel(1, F), wsel(1, F), wsel(F, D), wsel(1, D)],
            out_specs=pl.BlockSpec((bm, D), lambda i, be, nu: (i, 0)),
        ),
        out_shape=jax.ShapeDtypeStruct((n_rows, D), F32),
        compiler_params=_params("arbitrary"),
        name="moe_experts",
    )(blk_e, n_used, xg, wg, wu, bg, bu, wd, bd)


def _combine_kernel(pos_ref, posn_ref, x_ref, mod_ref, tg_ref, y_ref, o_ref, buf, sem):
    i = pl.program_id(0)
    tm = x_ref.shape[0]
    slot = i % 2

    def copy(p_ref, s, t, kk):
        return pltpu.make_async_copy(y_ref.at[pl.ds(p_ref[0, t * TOP_K + kk], 1)], buf.at[s, kk, pl.ds(t, 1)],
                                     sem.at[s])

    def fetch(p_ref, s):
        def body(t, c):
            for kk in range(TOP_K):
                copy(p_ref, s, t, kk).start()
            return c
        lax.fori_loop(0, tm, body, 0, unroll=8)

    @pl.when(i == 0)
    def _():
        fetch(pos_ref, 0)

    @pl.when(i + 1 < pl.num_programs(0))
    def _():
        fetch(posn_ref, 1 - slot)

    def wait(t, c):
        for kk in range(TOP_K):
            copy(pos_ref, slot, t, kk).wait()
        return c

    lax.fori_loop(0, tm, wait, 0, unroll=8)
    tg = tg_ref[...]
    f = jnp.zeros(x_ref.shape, F32)
    for kk in range(TOP_K):
        f = f + tg[:, kk:kk + 1] * buf[slot, kk]
    o_ref[...] = x_ref[...] + mod_ref[...][5:6] * f


def _combine(xs, mod, tg, yb, pos, n_lat_tiles):
    B, S, D = xs.shape
    tm = TOK_TILE
    nt = S // tm
    nb = B * nt
    pos3 = pos.reshape(nb, 1, tm * TOP_K)
    row = lambda w: pl.BlockSpec((tm, w), lambda i: (i, 0))
    out = pl.pallas_call(
        _combine_kernel,
        grid=(nb,),
        in_specs=[pl.BlockSpec((None, 1, tm * TOP_K), lambda i: (i, 0, 0), memory_space=pltpu.SMEM),
                  pl.BlockSpec((None, 1, tm * TOP_K), lambda i: (jnp.minimum(i + 1, nb - 1), 0, 0),
                               memory_space=pltpu.SMEM),
                  row(D),
                  pl.BlockSpec((None, 6, D), lambda i: (jnp.where(i % nt >= n_lat_tiles, 2, i // nt), 0, 0)),
                  row(LANES),
                  pl.BlockSpec(memory_space=pl.ANY)],
        out_specs=row(D),
        out_shape=jax.ShapeDtypeStruct((B * S, D), F32),
        scratch_shapes=[pltpu.VMEM((2, TOP_K, tm, D), F32), pltpu.SemaphoreType.DMA((2,))],
        compiler_params=_params("arbitrary"),
        name="moe_combine",
    )(pos3, pos3, xs.reshape(B * S, D), mod, tg.reshape(B * S, LANES), yb)
    return out.reshape(B, S, D)


def _final_kernel(x_ref, g_ref, o_ref):
    o_ref[...] = _rms_rows(x_ref[...]) * g_ref[...]


def _final_norm(xs, g, n_lat):
    B, S, D = xs.shape
    tm = 512
    return pl.pallas_call(
        _final_kernel,
        grid=(B, n_lat // tm),
        in_specs=[pl.BlockSpec((None, tm, D), lambda b, t: (b, t, 0)), _full((1, D))],
        out_specs=pl.BlockSpec((None, tm, D), lambda b, t: (b, t, 0)),
        out_shape=jax.ShapeDtypeStruct((B, n_lat, D), F32),
        compiler_params=_params("arbitrary", "arbitrary"),
        name="final_norm",
    )(xs, g.reshape(1, D))


def _prep_w_in(w_in):
    cq, ckv, kr = w_in[:, 0:256], w_in[:, 256:384], w_in[:, 384:416]
    hy, gq, gk = w_in[:, 416:1184], w_in[:, 1184:1312], w_in[:, 1312:1440]
    gv, glr, og, fn = w_in[:, 1440:1696], w_in[:, 1696:1728], w_in[:, 1728:1984], w_in[:, 1984:2240]
    half = MLA_ROPE // 2
    kr_sw = jnp.concatenate([-kr[:, half:], kr[:, :half]], axis=1)
    place = lambda w: jnp.tile(jnp.pad(w, ((0, 0), (MLA_NOPE, LANES - MLA_NOPE - MLA_ROPE))), (1, MLA_HEADS))
    glr_p = jnp.pad(glr, ((0, 0), (0, LANES - 2 * GLA_LR)))
    return jnp.concatenate([cq, ckv, place(kr), place(kr_sw), hy, gq, gk, gv, glr_p, og, fn], axis=1).astype(BF16)


def _prep_wq(w_uq):
    w = w_uq.reshape(MLA_Q_LORA, MLA_HEADS, MLA_NOPE + MLA_ROPE)
    nope, rope = w[..., :MLA_NOPE], w[..., MLA_NOPE:]
    half = MLA_ROPE // 2
    z_tail = jnp.zeros((MLA_Q_LORA, MLA_HEADS, LANES - MLA_NOPE - MLA_ROPE), F32)
    z_nope = jnp.zeros((MLA_Q_LORA, MLA_HEADS, MLA_NOPE), F32)
    plain = jnp.concatenate([nope, rope, z_tail], axis=-1).reshape(MLA_Q_LORA, MLA_HEADS * LANES)
    partner = jnp.concatenate([z_nope, -rope[..., half:], rope[..., :half], z_tail], axis=-1)
    return jnp.concatenate([plain, partner.reshape(MLA_Q_LORA, MLA_HEADS * LANES)], axis=1).astype(BF16)


def _prep_wkv(w_ukv):
    w = w_ukv.reshape(MLA_KV_LORA, MLA_HEADS, MLA_NOPE + MLA_V)
    k_nope, v = w[..., :MLA_NOPE], w[..., MLA_NOPE:]
    k_placed = jnp.pad(k_nope, ((0, 0), (0, 0), (0, LANES - MLA_NOPE))).reshape(MLA_KV_LORA, MLA_HEADS * LANES)
    return jnp.concatenate([k_placed, v.reshape(MLA_KV_LORA, MLA_HEADS * MLA_V)], axis=1).astype(BF16)


def _prep_gk(gk_w, gk_b):
    w = jnp.zeros((LANES, 2 * LANES), F32)
    w = w.at[0:GLA_LR, 0:LANES].set(gk_w[0]).at[GLA_LR:2 * GLA_LR, LANES:].set(gk_w[1])
    return w.astype(BF16), jnp.concatenate([gk_b[0], gk_b[1]]).reshape(1, 2 * LANES)


def _rope_tables(n_lat, n_ctx):
    rows = n_lat // GRID_W
    row = jnp.repeat(jnp.arange(rows, dtype=F32), GRID_W)
    col = jnp.tile(jnp.arange(GRID_W, dtype=F32), rows)
    n_freq = MLA_ROPE // 4
    inv = ROPE_THETA ** (-jnp.arange(n_freq, dtype=F32) / n_freq)
    ang = jnp.concatenate([row[:, None] * inv, col[:, None] * inv], axis=-1)
    cos = jnp.concatenate([jnp.cos(ang), jnp.ones((n_ctx, MLA_ROPE // 2), F32)], axis=0)
    sin = jnp.concatenate([jnp.sin(ang), jnp.zeros((n_ctx, MLA_ROPE // 2), F32)], axis=0)
    S = n_lat + n_ctx
    ones, zeros = jnp.ones((S, MLA_NOPE), F32), jnp.zeros((S, MLA_NOPE), F32)
    tail = jnp.zeros((S, LANES - MLA_NOPE - MLA_ROPE), F32)
    q_scale = MLA_SCALE * math.log2(math.e)
    cq = jnp.concatenate([ones, cos, cos, tail], axis=1) * q_scale
    sq = jnp.concatenate([zeros, sin, sin, tail], axis=1) * q_scale
    ck = jnp.concatenate([zeros, cos, cos, tail], axis=1)
    sk = jnp.concatenate([zeros, sin, sin, tail], axis=1)
    return cq, sq, ck, sk


def kernel(x, c, ctx, c_ctx, ada_w, ada_b, norm1_g, norm2_g, w_in, mla_q_g, mla_w_uq, mla_kv_g, mla_w_ukv, hy_conv_w, hy_conv_b, hy_w1, hy_b1, hy_w2, hy_b2, hy_w3, hy_freq, hy_bias, gla_gk_w, gla_gk_b, fnet_w, mix_g, w_out, router_w, router_b, moe_w_gu, moe_b_gu, moe_w_down, moe_b_down, final_g):
    B, L, D = x.shape
    Lc = ctx.shape[1]
    depth = ada_w.shape[0]
    S = L + Lc
    tm = TOK_TILE
    nlt = L // tm
    assert B == 2 and D == D_MODEL and Lc == tm and L % (2 * DFT_N2 * 8) == 0

    xs = jnp.concatenate([x, ctx], axis=1)
    cc = jnp.zeros((8, D), F32).at[0:B].set(c).at[B].set(c_ctx)
    mods = _mods(cc, ada_w, ada_b)
    tabs = _rope_tables(L, Lc)
    jc = np.arange(GROUP_W)
    ang_c = 2.0 * np.pi * ((jc[:, None] * jc[None, :]) % GROUP_W) / GROUP_W
    csc = jnp.asarray(np.concatenate([np.cos(ang_c), -np.sin(ang_c)], axis=1), F32)
    head_mean = jnp.asarray(np.kron(np.eye(GROUP_W // MLA_V), np.full((MLA_V, MLA_V), 1.0 / MLA_V)), F32)

    for i in range(depth):
        mod = mods[i].reshape(8, 6, D)
        gkw, gkb = _prep_gk(gla_gk_w[i], gla_gk_b[i])
        q, k, v, hyz, gqk, gv, gg, og, fre, fim = _proj(
            xs, mod, norm1_g[i].reshape(1, D), _prep_w_in(w_in[i]), mla_q_g[i].reshape(1, -1), _prep_wq(mla_w_uq[i]),
            mla_kv_g[i].reshape(1, -1), _prep_wkv(mla_w_ukv[i]), gkw, gkb, csc, tabs, nlt)

        a = _attention(q, k, v, L, Lc)

        filt = (hy_w1[i], hy_b1[i], hy_w2[i], hy_b2[i], hy_w3[i], hy_freq[i])
        vg_l, x0_l = _hy_pre(hyz, hy_conv_w[i], hy_conv_b[i], 0, L)
        hy_l = _hyena_long(vg_l, x0_l, *_hyena_filter(L, *filt), hy_bias[i])
        vg_c, x0_c = _hy_pre(hyz, hy_conv_w[i], hy_conv_b[i], L, Lc)
        hy_c = _hyena_ctx(vg_c, x0_c, *_hyena_filter(Lc, *filt), hy_bias[i])
        hy = jnp.concatenate([hy_l, hy_c], axis=1)

        o2 = _gla(gqk, gv, gg, L)

        fy = jnp.concatenate([_fnet_long(fre, fim, L), _fnet_ctx(fre, fim, L, Lc)], axis=1)

        rw = jnp.pad(router_w[i], ((0, 0), (0, LANES - N_EXPERTS)))
        rb = jnp.concatenate([router_b[i], jnp.full((LANES - N_EXPERTS,), -1e30, F32)]).reshape(1, LANES)
        xs, h2w, top_i, top_g, rank, cnt = _mix(
            xs, mod, a, hy, o2, og, fy, fnet_w[i].astype(BF16), mix_g[i].reshape(1, D), head_mean,
            w_out[i].astype(BF16), norm2_g[i].reshape(1, D), rw, rb, nlt)

        pos, blk_e, n_used, n_rows = _route(top_i.reshape(B * S, LANES)[:, :TOP_K],
                                            rank.reshape(B * S, LANES)[:, :TOP_K],
                                            cnt[0, :N_EXPERTS].astype(jnp.int32))
        xg = _dispatch(h2w.reshape(B * S, D // 2), pos, n_rows)
        wg, wu = _gu_prep(moe_w_gu[i])
        yb = _experts(xg, blk_e, n_used, wg, wu, moe_b_gu[i][:, None, 0::2], moe_b_gu[i][:, None, 1::2],
                      moe_w_down[i].astype(BF16), moe_b_down[i][:, None, :])
        xs = _combine(xs, mod, top_g, yb, pos, nlt)

    return _final_norm(xs, final_g, L)
```

```python
import functools
import math

import numpy as np
import jax
import jax.numpy as jnp
from jax import lax
from jax.experimental import pallas as pl
from jax.experimental.pallas import tpu as pltpu

F32 = jnp.float32
BF16 = jnp.bfloat16
HIGHEST = lax.Precision.HIGHEST

EPS = 1e-6
D_MODEL = 1024
GROUP_W = 256
MLA_HEADS = 4
MLA_NOPE = 64
MLA_ROPE = 32
MLA_V = 64
MLA_Q_LORA = 256
MLA_KV_LORA = 128
MLA_SCALE = (MLA_NOPE + MLA_ROPE) ** -0.5
ROPE_THETA = 10000.0
GRID_W = 64
HY_EMB = 33
HY_FFN = 64
HY_TARGET = 1e-2
HY_FAST_PCT = 0.3
HY_SLOW_PCT = 1.5
GLA_HEADS = 4
GLA_DK = 32
GLA_DV = 64
GLA_LR = 16
GLA_TAU = 16.0
GLA_CHUNK = 64
N_EXPERTS = 32
TOP_K = 4
SWIGLU_ALPHA = 1.702
SWIGLU_LIMIT = 7.0

LANES = 128
TOK_TILE = 256
DFT_N2 = 128
MOE_BM = 512
VMEM_LIMIT = 56 * 1024 * 1024

_O_CQ, _O_CKV, _O_KRP, _O_KRS, _O_HY, _O_GQ, _O_GK, _O_GV, _O_GLR, _O_OG, _O_FN, _W_ALL = (
    0, 256, 384, 896, 1408, 2176, 2304, 2432, 2688, 2816, 3072, 3328)


def _dot(a, b):
    return jnp.dot(a.astype(BF16), b.astype(BF16), preferred_element_type=F32)


def _dot_hi(a, b):
    return jnp.dot(a, b, precision=HIGHEST, preferred_element_type=F32)


def _split(a):
    hi = a.astype(BF16)
    return hi, (a - hi.astype(F32)).astype(BF16)


def _dot_x3(a, b_hi, b_lo):
    a_hi, a_lo = _split(a)
    mm = lambda u, w: jnp.dot(u, w, preferred_element_type=F32)
    return mm(a_hi, b_hi) + (mm(a_lo, b_hi) + mm(a_hi, b_lo))


def _params(*sem):
    return pltpu.CompilerParams(dimension_semantics=sem, vmem_limit_bytes=VMEM_LIMIT)


def _rms_rows(x):
    return x * lax.rsqrt(jnp.mean(x * x, axis=-1, keepdims=True) + EPS)


def _full(shape):
    n = len(shape)
    return pl.BlockSpec(shape, lambda *_: (0,) * n)


def _mods_kernel(c_ref, w_ref, b_ref, o_ref):
    c = c_ref[...]
    s = c / (1.0 + jnp.exp(-c))
    o_ref[...] = _dot_hi(s, w_ref[...]) + b_ref[...]


def _mods(cc, ada_w, ada_b):
    depth, d, n = ada_w.shape
    tn = 1024
    return pl.pallas_call(
        _mods_kernel,
        grid=(depth, n // tn),
        in_specs=[pl.BlockSpec((8, d), lambda i, j: (0, 0)),
                  pl.BlockSpec((None, d, tn), lambda i, j: (i, 0, j)),
                  pl.BlockSpec((None, 1, tn), lambda i, j: (i, 0, j))],
        out_specs=pl.BlockSpec((None, 8, tn), lambda i, j: (i, 0, j)),
        out_shape=jax.ShapeDtypeStruct((depth, 8, n), F32),
        compiler_params=_params("arbitrary", "arbitrary"),
        name="ada_mods",
    )(cc, ada_w, ada_b.reshape(depth, 1, n))


def _proj_kernel(x_ref, mod_ref, g_ref, win_ref, qg_ref, wq_ref, kvg_ref, wkv_ref, wvt_ref, gkw_ref, gkb_ref,
                 csh_ref, csl_ref, cq_ref, sq_ref, ck_ref, sk_ref,
                 q_out, k_out, v_out, hy_out, gqk_out, gv_out, gg_out, og_out, fre_out, fim_out):
    x = x_ref[...]
    mod = mod_ref[...]
    h = _rms_rows(x) * g_ref[...] * (1.0 + mod[1:2]) + mod[0:1]
    z = _dot(h, win_ref[...])

    nq = _rms_rows(z[:, _O_CQ:_O_CKV]) * qg_ref[...]
    qq = _dot(nq, wq_ref[...])
    nkv = _rms_rows(z[:, _O_CKV:_O_KRP]) * kvg_ref[...]
    nkv = nkv.astype(BF16)
    kvu = jnp.dot(nkv, wkv_ref[...], preferred_element_type=F32)
    cq, sq, ck, sk = cq_ref[...], sq_ref[...], ck_ref[...], sk_ref[...]
    for hd in range(MLA_HEADS):
        a, b = hd * LANES, (hd + 1) * LANES
        q_out[:, a:b] = (qq[:, a:b] * cq + qq[:, 512 + a:512 + b] * sq).astype(BF16)
        k_out[:, a:b] = (kvu[:, a:b] + z[:, _O_KRP + a:_O_KRP + b] * ck
                         + z[:, _O_KRS + a:_O_KRS + b] * sk).astype(BF16)
    vt = lax.dot_general(wvt_ref[...], nkv, (((1,), (1,)), ((), ())), preferred_element_type=F32)
    vrow = lax.broadcasted_iota(jnp.int32, vt.shape, 0)
    v_out[...] = jnp.where(vrow % LANES == MLA_V, 1.0, vt).astype(BF16)

    hy_out[...] = z[:, _O_HY:_O_GQ]
    gqk_out[:, :LANES] = z[:, _O_GQ:_O_GK] * (GLA_DK ** -0.5)
    gqk_out[:, LANES:] = z[:, _O_GK:_O_GV]
    gv_out[...] = z[:, _O_GV:_O_GLR]
    gates = _dot(z[:, _O_GLR:_O_OG], gkw_ref[...]) + gkb_ref[...]
    gg_out[...] = (jnp.minimum(gates, 0.0) - jnp.log(1.0 + jnp.exp(-jnp.abs(gates)))) * (1.0 / GLA_TAU)
    og_out[...] = z[:, _O_OG:_O_FN]
    fcs = _dot_x3(z[:, _O_FN:_W_ALL], csh_ref[...], csl_ref[...])
    fre_out[...] = fcs[:, :GROUP_W]
    fim_out[...] = fcs[:, GROUP_W:]


def _proj(xs, mod, g1, win, qg, wq, kvg, wkv, wvt, gkw, gkb, csc, tabs, n_lat_tiles):
    B, S, D = xs.shape
    tm = TOK_TILE
    csh, csl = _split(csc)
    tok = lambda w: pl.BlockSpec((None, tm, w), lambda b, t: (b, t, 0))
    tab = pl.BlockSpec((tm, LANES), lambda b, t: (t, 0))
    shp = lambda w, dt: jax.ShapeDtypeStruct((B, S, w), dt)
    return pl.pallas_call(
        _proj_kernel,
        grid=(B, S // tm),
        in_specs=[tok(D),
                  pl.BlockSpec((None, 6, D), lambda b, t: (jnp.where(t >= n_lat_tiles, 2, b), 0, 0)),
                  _full((1, D)), _full(win.shape), _full((1, MLA_Q_LORA)), _full(wq.shape),
                  _full((1, MLA_KV_LORA)), _full(wkv.shape), _full(wvt.shape), _full(gkw.shape), _full(gkb.shape),
                  _full(csc.shape), _full(csc.shape), tab, tab, tab, tab],
        out_specs=[tok(512), tok(512), pl.BlockSpec((None, None, MLA_HEADS * LANES, tm), lambda b, t: (b, t, 0, 0)),
                   tok(768), tok(256), tok(256), tok(256), tok(256), tok(256), tok(256)],
        out_shape=[shp(512, BF16), shp(512, BF16), jax.ShapeDtypeStruct((B, S // tm, MLA_HEADS * LANES, tm), BF16),
                   shp(768, F32), shp(256, F32), shp(256, F32),
                   shp(256, F32), shp(256, F32), shp(256, F32), shp(256, F32)],
        compiler_params=_params("arbitrary", "arbitrary"),
        name="in_proj",
    )(xs, mod, g1, win, qg, wq, kvg, wkv, wvt, gkw, gkb, csh, csl, *tabs)


def _attn_kernel(q_ref, k_ref, vt_ref, o_ref, *, n_lat_tiles, tiles_per_chunk, n_lat, n_ctx):
    qi = pl.program_id(2)
    tq = q_ref.shape[0]
    tile = vt_ref.shape[2]

    def heads(tile0, n_chunks, n_tiles):
        width = n_tiles * tile

        def body(c, carry):
            t0 = tile0 + c * n_tiles
            off = pl.multiple_of(t0 * tile, tile)
            out = []
            for hd in range(2):
                m, acc = carry[hd]
                q = q_ref[:, hd * LANES:(hd + 1) * LANES]
                kc = k_ref[pl.ds(off, width), hd * LANES:(hd + 1) * LANES]
                s = lax.dot_general(kc, q, (((1,), (1,)), ((), ())), preferred_element_type=F32)
                m_new = jnp.maximum(m, jnp.max(s, axis=0, keepdims=True))
                alpha = jnp.exp2(m - m_new)
                p = jnp.exp2(s - m_new).astype(BF16)
                pv = jnp.zeros((LANES, tq), F32)
                for j in range(n_tiles):
                    vt = vt_ref[t0 + j, hd * LANES:(hd + 1) * LANES, :]
                    pv = pv + jnp.dot(vt, p[j * tile:(j + 1) * tile], preferred_element_type=F32)
                out.append((m_new, alpha * acc + pv))
            return tuple(out)

        one = (jnp.full((1, tq), -1e30, F32), jnp.zeros((LANES, tq), F32))
        res = lax.fori_loop(0, n_chunks, body, (one, one))
        for hd in range(2):
            acc_t = res[hd][1].T
            o_ref[:, hd * MLA_V:(hd + 1) * MLA_V] = acc_t[:, :MLA_V] / acc_t[:, MLA_V:MLA_V + 1]

    n_all = (n_lat + n_ctx) // tile

    @pl.when(qi < n_lat_tiles)
    def _():
        heads(0, n_all // tiles_per_chunk, tiles_per_chunk)

    @pl.when(qi >= n_lat_tiles)
    def _():
        heads(n_lat // tile, 1, n_ctx // tile)


def _attention(q, k, vt, n_lat, n_ctx):
    B, S, _ = q.shape
    tq = TOK_TILE
    n_tiles = vt.shape[1]
    tiles_per_chunk = 5 if n_tiles % 5 == 0 else 1
    kern = functools.partial(_attn_kernel, n_lat_tiles=n_lat // tq, tiles_per_chunk=tiles_per_chunk,
                             n_lat=n_lat, n_ctx=n_ctx)
    return pl.pallas_call(
        kern,
        grid=(B, 2, S // tq),
        in_specs=[pl.BlockSpec((None, tq, 256), lambda b, p, t: (b, t, p)),
                  pl.BlockSpec((None, S, 256), lambda b, p, t: (b, 0, p)),
                  pl.BlockSpec((None, n_tiles, 256, vt.shape[3]), lambda b, p, t: (b, 0, p, 0))],
        out_specs=pl.BlockSpec((None, tq, 128), lambda b, p, t: (b, t, p)),
        out_shape=jax.ShapeDtypeStruct((B, S, 256), F32),
        compiler_params=_params("arbitrary", "arbitrary", "arbitrary"),
        name="mla_attention",
    )(q, k, vt)


def _hy_pre_kernel(z_ref, zp_ref, zn_ref, w_ref, b_ref, vg_ref, x0_ref, *, n_tiles):
    i = pl.program_id(1)
    z = z_ref[...]
    tm = z.shape[0]
    rows = lax.broadcasted_iota(jnp.int32, z.shape, 0)
    prev_row = jnp.where(i == 0, 0.0, zp_ref[7:8, :])
    next_row = jnp.where(i == n_tiles - 1, 0.0, zn_ref[0:1, :])
    z_m = jnp.where(rows == 0, prev_row, pltpu.roll(z, 1, 0))
    z_p = jnp.where(rows == tm - 1, next_row, pltpu.roll(z, tm - 1, 0))
    w = w_ref[...]
    u = z_m * w[0:1] + z * w[1:2] + z_p * w[2:3] + b_ref[...]
    vg_ref[...] = u[:, 2 * GROUP_W:] * u[:, GROUP_W:2 * GROUP_W]
    x0_ref[...] = u[:, :GROUP_W]


def _hy_pre(hyz, conv_w, conv_b, row0, n_rows):
    B, S, W = hyz.shape
    tm = TOK_TILE
    nt = n_rows // tm
    t0, r8, last8 = row0 // tm, row0 // 8, S // 8 - 1
    kern = functools.partial(_hy_pre_kernel, n_tiles=nt)
    out = jax.ShapeDtypeStruct((B, n_rows, GROUP_W), F32)
    return pl.pallas_call(
        kern,
        grid=(B, nt),
        in_specs=[pl.BlockSpec((None, tm, W), lambda b, i: (b, t0 + i, 0)),
                  pl.BlockSpec((None, 8, W), lambda b, i: (b, jnp.maximum(r8 + i * (tm // 8) - 1, 0), 0)),
                  pl.BlockSpec((None, 8, W), lambda b, i: (b, jnp.minimum(r8 + (i + 1) * (tm // 8), last8), 0)),
                  _full((3, W)), _full((1, W))],
        out_specs=[pl.BlockSpec((None, tm, GROUP_W), lambda b, i: (b, i, 0))] * 2,
        out_shape=[out, out],
        compiler_params=_params("arbitrary", "arbitrary"),
        name="hyena_pre",
    )(hyz, hyz, hyz, conv_w, conv_b.reshape(1, W))


def _filter_kernel(fv_ref, w1_ref, b1_ref, w2_ref, b2_ref, w3_ref, fr_ref, dl_ref, h_ref, ss_ref, *, n_pos):
    i = pl.program_id(0)
    tl = h_ref.shape[0]
    row = lax.broadcasted_iota(jnp.int32, (tl, LANES), 0) + i * tl
    pos = jnp.where(row <= n_pos, row, 2 * n_pos - row).astype(F32)
    lane = lax.broadcasted_iota(jnp.int32, (tl, LANES), 1)
    t = pos * (1.0 / (n_pos - 1))
    arg = (pos * (2.0 * math.pi / n_pos)) * fv_ref[...]
    feat = jnp.where(lane == 0, t, jnp.where(lane < 17, jnp.cos(arg), jnp.where(lane < HY_EMB, -jnp.sin(arg), 0.0)))
    fr = fr_ref[...]
    h = jnp.sin(fr * (_dot_hi(feat, w1_ref[...]) + b1_ref[...]))
    h = jnp.sin(fr * (_dot_hi(h, w2_ref[...]) + b2_ref[...]))
    h = _dot_hi(h, w3_ref[...]) * jnp.exp(-t[:, 0:1] * dl_ref[...])
    r1 = row[:, 0:1]
    h = jnp.where(r1 < n_pos, h[:, :GROUP_W], jnp.where(r1 == n_pos, 0.0, h[:, GROUP_W:]))
    h_ref[...] = h

    @pl.when(i == 0)
    def _():
        ss_ref[...] = jnp.zeros_like(ss_ref)

    ss_ref[...] += jnp.sum(h * h, axis=0, keepdims=True)


def _hyena_filter(n_pos, w1, b1, w2, b2, w3, freq):
    tl = min(2 * n_pos, 512)
    bands = (HY_EMB - 1) // 2
    f = np.linspace(1e-4, bands - 1, bands)
    fv = np.zeros((1, LANES), np.float32)
    fv[0, 1:17] = f
    fv[0, 17:33] = f
    w1p = jnp.zeros((LANES, HY_FFN), F32).at[:HY_EMB].set(w1)
    max_decay = math.log(HY_TARGET) / HY_FAST_PCT
    min_decay = math.log(HY_TARGET) / HY_SLOW_PCT
    deltas = np.abs(np.linspace(min_decay, max_decay, GROUP_W)).astype(np.float32)
    dl = jnp.asarray(np.concatenate([deltas, deltas])[None, :])
    kern = functools.partial(_filter_kernel, n_pos=n_pos)
    k2, ss = pl.pallas_call(
        kern,
        grid=(2 * n_pos // tl,),
        in_specs=[_full((1, LANES)), _full((LANES, HY_FFN)), _full((1, HY_FFN)), _full((HY_FFN, HY_FFN)),
                  _full((1, HY_FFN)), _full((HY_FFN, 2 * GROUP_W)), _full((1, HY_FFN)), _full((1, 2 * GROUP_W))],
        out_specs=[pl.BlockSpec((tl, GROUP_W), lambda i: (i, 0)), _full((1, GROUP_W))],
        out_shape=[jax.ShapeDtypeStruct((2 * n_pos, GROUP_W), F32), jax.ShapeDtypeStruct((1, GROUP_W), F32)],
        compiler_params=_params("arbitrary"),
        name="hyena_filter",
    )(jnp.asarray(fv), w1p, b1.reshape(1, -1), w2, b2.reshape(1, -1), w3, freq.reshape(1, -1), dl)
    return k2, lax.rsqrt(ss)


def _dft_mats(n, sign):
    j = np.arange(n)
    ang = 2.0 * np.pi * ((j[:, None] * j[None, :]) % n) / n
    return np.cos(ang), sign * np.sin(ang)


def _twiddle(n1, n2, sign):
    ang = 2.0 * np.pi * ((np.arange(n1)[:, None] * np.arange(n2)[None, :]) % (n1 * n2)) / (n1 * n2)
    return np.cos(ang), sign * np.sin(ang)


def _block_complex(re, im):
    return np.block([[re, -im], [im, re]])


def _stage1_kernel(m_ref, zr_ref, zi_ref, ar_ref, ai_ref):
    a = _dot_hi(m_ref[...], jnp.concatenate([zr_ref[...], zi_ref[...]], axis=0))
    half = ar_ref.shape[0]
    ar_ref[...] = a[:half]
    ai_ref[...] = a[half:]


def _stage1_real_kernel(m_ref, zr_ref, ar_ref, ai_ref):
    a = _dot_hi(m_ref[...], zr_ref[...])
    half = ar_ref.shape[0]
    ar_ref[...] = a[:half]
    ai_ref[...] = a[half:]


def _col_tile(cols):
    return 2048 if cols % 2048 == 0 else cols


def _hy_stage1(vg2, n1):
    _, r, cols = vg2.shape
    cr, ci = _dft_mats(n1, -1.0)
    m = jnp.asarray(_block_complex(cr[:, :r], ci[:, :r]), F32)
    tc = _col_tile(cols)
    out = jax.ShapeDtypeStruct((n1, cols), F32)
    return pl.pallas_call(
        _stage1_kernel,
        grid=(cols // tc,),
        in_specs=[_full(m.shape), pl.BlockSpec((None, r, tc), lambda j: (0, 0, j)),
                  pl.BlockSpec((None, r, tc), lambda j: (1, 0, j))],
        out_specs=[pl.BlockSpec((n1, tc), lambda j: (0, j))] * 2,
        out_shape=[out, out],
        compiler_params=_params("arbitrary"),
        name="hyena_fwd_stage1",
    )(m, vg2, vg2)


def _filter_stage1(k2v, n1):
    _, cols = k2v.shape
    cr, ci = _dft_mats(n1, -1.0)
    m = jnp.asarray(np.concatenate([cr, ci], axis=0), F32)
    tc = _col_tile(cols)
    out = jax.ShapeDtypeStruct((n1, cols), F32)
    return pl.pallas_call(
        _stage1_real_kernel,
        grid=(cols // tc,),
        in_specs=[_full(m.shape), pl.BlockSpec((n1, tc), lambda j: (0, j))],
        out_specs=[pl.BlockSpec((n1, tc), lambda j: (0, j))] * 2,
        out_shape=[out, out],
        compiler_params=_params("arbitrary"),
        name="hyena_filter_stage1",
    )(m, k2v)


def _twiddled(f_re, f_im, t_re, t_im):
    g_re = f_re * t_re - f_im * t_im
    g_im = f_re * t_im + f_im * t_re
    return jnp.concatenate([jnp.concatenate([g_re, -g_im], axis=1), jnp.concatenate([g_im, g_re], axis=1)], axis=0)


def _filter_stage2_kernel(ar_ref, ai_ref, twr_ref, twi_ref, fr_ref, fi_ref, sc_ref, kr_ref, ki_ref):
    g = _twiddled(fr_ref[...], fi_ref[...], twr_ref[...], twi_ref[...])
    x = _dot_hi(g, jnp.concatenate([ar_ref[...], ai_ref[...]], axis=0)) * sc_ref[...]
    n2 = kr_ref.shape[0]
    kr_ref[...] = x[:n2]
    ki_ref[...] = x[n2:]


def _filter_stage2(ar, ai, inv_norm, n1):
    n2 = DFT_N2
    c = ar.shape[1] // n2
    fr, fi = _dft_mats(n2, -1.0)
    twr, twi = _twiddle(n1, n2, -1.0)
    blk = pl.BlockSpec((None, n2, c), lambda k: (k, 0, 0))
    tw = pl.BlockSpec((None, 1, n2), lambda k: (k, 0, 0))
    out = jax.ShapeDtypeStruct((n1, n2, c), F32)
    return pl.pallas_call(
        _filter_stage2_kernel,
        grid=(n1,),
        in_specs=[blk, blk, tw, tw, _full((n2, n2)), _full((n2, n2)), _full((1, c))],
        out_specs=[blk, blk],
        out_shape=[out, out],
        compiler_params=_params("arbitrary"),
        name="hyena_filter_stage2",
    )(ar.reshape(n1, n2, c), ai.reshape(n1, n2, c), jnp.asarray(twr.reshape(n1, 1, n2), F32),
      jnp.asarray(twi.reshape(n1, 1, n2), F32), jnp.asarray(fr, F32), jnp.asarray(fi, F32), inv_norm)


def _hy_stage2_kernel(ar_ref, ai_ref, kr_ref, ki_ref, twr_ref, twi_ref, tcr_ref, tci_ref, fr_ref, fi_ref,
                      br_ref, bi_ref):
    f_re, f_im = fr_ref[...], fi_ref[...]
    n2 = f_re.shape[0]
    g = _twiddled(f_re, f_im, twr_ref[...], twi_ref[...])
    x = _dot_hi(g, jnp.concatenate([ar_ref[...], ai_ref[...]], axis=0))
    x_re, x_im = x[:n2], x[n2:]
    k_re, k_im = kr_ref[...], ki_ref[...]
    y = jnp.concatenate([x_re * k_re - x_im * k_im, x_re * k_im + x_im * k_re], axis=0)
    g_inv = _twiddled(f_re, -f_im, tcr_ref[...], -tci_ref[...])
    b = _dot_hi(g_inv, y)
    br_ref[...] = b[:n2]
    bi_ref[...] = b[n2:]


def _hy_stage2(ar, ai, kr, ki, n1):
    n2 = DFT_N2
    c = kr.shape[2]
    fr, fi = _dft_mats(n2, -1.0)
    twr, twi = _twiddle(n1, n2, -1.0)
    blk = pl.BlockSpec((None, n2, c), lambda k: (k, 0, 0))
    tw = pl.BlockSpec((None, 1, n2), lambda k: (k, 0, 0))
    twc = pl.BlockSpec((None, n2, 1), lambda k: (k, 0, 0))
    out = jax.ShapeDtypeStruct((n1, n2, c), F32)
    return pl.pallas_call(
        _hy_stage2_kernel,
        grid=(n1,),
        in_specs=[blk, blk, blk, blk, tw, tw, twc, twc, _full((n2, n2)), _full((n2, n2))],
        out_specs=[blk, blk],
        out_shape=[out, out],
        compiler_params=_params("arbitrary"),
        name="hyena_conv_stage2",
    )(ar.reshape(n1, n2, c), ai.reshape(n1, n2, c), kr, ki,
      jnp.asarray(twr.reshape(n1, 1, n2), F32), jnp.asarray(twi.reshape(n1, 1, n2), F32),
      jnp.asarray(twr.reshape(n1, n2, 1), F32), jnp.asarray(twi.reshape(n1, n2, 1), F32),
      jnp.asarray(fr, F32), jnp.asarray(fi, F32))


def _hy_stage3_kernel(m_ref, br_ref, bi_ref, vg0_ref, vg1_ref, x00_ref, x01_ref, bias_ref, o_ref):
    conv = _dot_hi(m_ref[...], jnp.concatenate([br_ref[...], bi_ref[...]], axis=0))
    r = vg0_ref.shape[0]
    bias = bias_ref[...]
    o_ref[0] = (conv[:r] + vg0_ref[...] * bias) * x00_ref[...]
    o_ref[1] = (conv[r:] + vg1_ref[...] * bias) * x01_ref[...]


def _hy_stage3(br, bi, vg2, x02, bias_cols, n1, rows_total):
    _, r, cols = vg2.shape
    cr, ci = _dft_mats(n1, 1.0)
    m = jnp.asarray(_block_complex(cr[:r], ci[:r]) / (n1 * DFT_N2), F32)
    tc = _col_tile(cols)
    plane = lambda p: pl.BlockSpec((None, r, tc), lambda j: (p, 0, j))
    return pl.pallas_call(
        _hy_stage3_kernel,
        grid=(cols // tc,),
        in_specs=[_full(m.shape), pl.BlockSpec((n1, tc), lambda j: (0, j)), pl.BlockSpec((n1, tc), lambda j: (0, j)),
                  plane(0), plane(1), plane(0), plane(1), pl.BlockSpec((1, tc), lambda j: (0, j))],
        out_specs=pl.BlockSpec((2, r, tc), lambda j: (0, 0, j)),
        out_shape=jax.ShapeDtypeStruct((2, rows_total, cols), F32),
        compiler_params=_params("arbitrary"),
        name="hyena_inv_stage3",
    )(m, br, bi, vg2, vg2, x02, x02, bias_cols)


def _hyena_long(vg, x0, k2, inv_norm, bias, n_all):
    B, L, C = vg.shape
    assert B == 2
    n2 = DFT_N2
    n1 = 2 * L // n2
    r = L // n2
    fa_r, fa_i = _filter_stage1(k2.reshape(n1, n2 * C), n1)
    kr, ki = _filter_stage2(fa_r, fa_i, inv_norm, n1)
    vg2 = vg.reshape(2, r, n2 * C)
    ar, ai = _hy_stage1(vg2, n1)
    br, bi = _hy_stage2(ar, ai, kr, ki, n1)
    y = _hy_stage3(br.reshape(n1, n2 * C), bi.reshape(n1, n2 * C), vg2, x0.reshape(2, r, n2 * C),
                   jnp.tile(bias.reshape(1, C), (1, n2)), n1, n_all // n2)
    return y.reshape(2, n_all, C)


def _hy_ctx_kernel(vg_ref, x0_ref, kext_ref, sc_ref, bias_ref, buf_ref, o_ref):
    del buf_ref
    n = vg_ref.shape[0]

    def body(a, acc):
        src = vg_ref[pl.ds(pl.multiple_of(8 * a, 8), 8), :]
        win = pl.multiple_of(n - 8 * a, 8)
        for r in range(8):
            acc = acc + kext_ref[r, pl.ds(win, n), :] * src[r:r + 1]
        return acc

    acc = lax.fori_loop(0, n // 8, body, jnp.zeros(vg_ref.shape, F32))
    o_ref[...] = (acc * sc_ref[...] + vg_ref[...] * bias_ref[...]) * x0_ref[...]


def _hyena_ctx(vg, x0, k2, inv_norm, bias, buf, row0):
    B, n, C = vg.shape
    kext = jnp.concatenate([k2[n:], k2[:n]], axis=0)
    kext = jnp.stack([jnp.roll(kext, r, axis=0) for r in range(8)])
    blk = pl.BlockSpec((None, n, C), lambda b: (b, 0, 0))
    return pl.pallas_call(
        _hy_ctx_kernel,
        grid=(B,),
        in_specs=[blk, blk, _full((8, 2 * n, C)), _full((1, C)), _full((1, C)), pl.BlockSpec(memory_space=pl.ANY)],
        out_specs=pl.BlockSpec((None, n, C), lambda b: (b, row0 // n, 0)),
        out_shape=jax.ShapeDtypeStruct(buf.shape, F32),
        input_output_aliases={5: 0},
        compiler_params=_params("arbitrary"),
        name="hyena_ctx",
    )(vg, x0, kext, inv_norm, bias.reshape(1, C), buf)


def _fn_stage1_kernel(c_ref, s_ref, re_ref, im_ref, ar_ref, ai_ref):
    c, s, re, im = c_ref[...], s_ref[...], re_ref[...], im_ref[...]
    ar_ref[...] = _dot_hi(c, re) + _dot_hi(s, im)
    ai_ref[...] = _dot_hi(c, im) - _dot_hi(s, re)


def _fn_stage2_kernel(ar_ref, ai_ref, twr_ref, twi_ref, fr_ref, fi_ref, o_ref):
    f_re, f_im, t_re, t_im = fr_ref[...], fi_ref[...], twr_ref[...], twi_ref[...]
    g_re = f_re * t_re - f_im * t_im
    g_im = f_re * t_im + f_im * t_re
    o_ref[...] = _dot_hi(g_re, ar_ref[...]) - _dot_hi(g_im, ai_ref[...])


def _fnet_long(fre, fim, n_lat):
    B, S, C = fre.shape
    n2 = DFT_N2
    n1 = n_lat // n2
    cols = n2 * C
    c1, s1 = _dft_mats(n1, 1.0)
    tc = _col_tile(cols)
    rows = pl.BlockSpec((None, n1, tc), lambda b, j: (b, 0, j))
    a_shape = jax.ShapeDtypeStruct((B, n1, cols), F32)
    ar, ai = pl.pallas_call(
        _fn_stage1_kernel,
        grid=(B, cols // tc),
        in_specs=[_full((n1, n1)), _full((n1, n1)), rows, rows],
        out_specs=[rows, rows],
        out_shape=[a_shape, a_shape],
        compiler_params=_params("arbitrary", "arbitrary"),
        name="fnet_stage1",
    )(jnp.asarray(c1, F32), jnp.asarray(s1, F32), fre.reshape(B, S // n2, cols), fim.reshape(B, S // n2, cols))
    fr, fi = _dft_mats(n2, -1.0)
    scale = 1.0 / math.sqrt(n_lat * C)
    twr, twi = _twiddle(n1, n2, -1.0)
    blk = pl.BlockSpec((None, None, n2, C), lambda b, k: (b, k, 0, 0))
    tw = pl.BlockSpec((None, 1, n2), lambda b, k: (k, 0, 0))
    y = pl.pallas_call(
        _fn_stage2_kernel,
        grid=(B, n1),
        in_specs=[blk, blk, tw, tw, _full((n2, n2)), _full((n2, n2))],
        out_specs=pl.BlockSpec((None, n2, C), lambda b, k: (b, 0, k)),
        out_shape=jax.ShapeDtypeStruct((B, S // n1, n1 * C), F32),
        compiler_params=_params("arbitrary", "arbitrary"),
        name="fnet_stage2",
    )(ar.reshape(B, n1, n2, C), ai.reshape(B, n1, n2, C), jnp.asarray(twr.reshape(n1, 1, n2), F32),
      jnp.asarray(twi.reshape(n1, 1, n2), F32), jnp.asarray(fr * scale, F32), jnp.asarray(fi * scale, F32))
    return y.reshape(B, S, C)


def _fn_ctx_kernel(c_ref, s_ref, re_ref, im_ref, buf_ref, o_ref):
    del buf_ref
    o_ref[...] = _dot_hi(c_ref[...], re_ref[...]) + _dot_hi(s_ref[...], im_ref[...])


def _fnet_ctx(fre, fim, n_lat, n_ctx, buf):
    B, S, C = fre.shape
    c1, s1 = _dft_mats(n_ctx, 1.0)
    scale = 1.0 / math.sqrt(n_ctx * C)
    blk = pl.BlockSpec((None, n_ctx, C), lambda b: (b, n_lat // n_ctx, 0))
    return pl.pallas_call(
        _fn_ctx_kernel,
        grid=(B,),
        in_specs=[_full((n_ctx, n_ctx)), _full((n_ctx, n_ctx)), blk, blk, pl.BlockSpec(memory_space=pl.ANY)],
        out_specs=blk,
        out_shape=jax.ShapeDtypeStruct((B, S, C), F32),
        input_output_aliases={4: 0},
        compiler_params=_params("arbitrary"),
        name="fnet_ctx",
    )(jnp.asarray(c1 * scale, F32), jnp.asarray(s1 * scale, F32), fre, fim, buf)


def _gla_kernel(qkf_ref, vf_ref, gf_ref, qkb_ref, vb_ref, gb_ref, of_ref, ob_ref, st_ref):
    n = pl.program_id(1)
    tm = qkf_ref.shape[0]
    ck = GLA_CHUNK

    @pl.when(n == 0)
    def _():
        st_ref[...] = jnp.zeros_like(st_ref)

    ri = lax.broadcasted_iota(jnp.int32, (ck, ck), 0)
    ci = lax.broadcasted_iota(jnp.int32, (ck, ck), 1)

    def sub_chunk(qk_ref, v_ref, g_ref, o_ref, states, r0, reverse):
        keep = (ci >= ri) if reverse else (ci <= ri)
        g = g_ref[r0:r0 + ck, :]
        b = _dot_hi(keep.astype(F32), g)
        b_end = b[0:1] if reverse else b[ck - 1:ck]
        q = qk_ref[r0:r0 + ck, :LANES] * jnp.exp(b)
        k = qk_ref[r0:r0 + ck, LANES:]
        k_in = k * jnp.exp(-b)
        k_out = k * jnp.exp(b_end - b)
        decay = jnp.exp(b_end)
        new_states = []
        for hd in range(GLA_HEADS):
            ks = slice(hd * GLA_DK, (hd + 1) * GLA_DK)
            vs = slice(hd * GLA_DV, (hd + 1) * GLA_DV)
            qh, vh = q[:, ks].astype(BF16), v_ref[r0:r0 + ck, vs].astype(BF16)
            a = lax.dot_general(qh, k_in[:, ks].astype(BF16), (((1,), (1,)), ((), ())), preferred_element_type=F32)
            a = jnp.where(keep, a, 0.0)
            st = states[hd]
            o = jnp.dot(a.astype(BF16), vh, preferred_element_type=F32)
            o += lax.dot_general(qh, st.astype(BF16), (((1,), (1,)), ((), ())), preferred_element_type=F32)
            o_ref[r0:r0 + ck, vs] = o
            kv = lax.dot_general(vh, k_out[:, ks].astype(BF16), (((0,), (0,)), ((), ())), preferred_element_type=F32)
            new_states.append(st * decay[:, ks] + kv)
        return new_states

    st_f = [st_ref[0, hd] for hd in range(GLA_HEADS)]
    st_b = [st_ref[1, hd] for hd in range(GLA_HEADS)]
    n_sub = tm // ck
    for i in range(n_sub):
        st_f = sub_chunk(qkf_ref, vf_ref, gf_ref, of_ref, st_f, i * ck, False)
        st_b = sub_chunk(qkb_ref, vb_ref, gb_ref, ob_ref, st_b, (n_sub - 1 - i) * ck, True)
    for hd in range(GLA_HEADS):
        st_ref[0, hd] = st_f[hd]
        st_ref[1, hd] = st_b[hd]


def _gla(gqk, gv, gg, n_lat):
    B, S, _ = gqk.shape
    tm = TOK_TILE
    nl = n_lat // tm
    nt = S // tm
    assert nt == nl + 1
    fwd = lambda n: jnp.where(n == 0, nl, n - 1)
    bwd = lambda n: jnp.where(n == 0, nl, nl - n)
    out = jax.ShapeDtypeStruct((B, S, 256), F32)
    return pl.pallas_call(
        _gla_kernel,
        grid=(B, nt),
        in_specs=[pl.BlockSpec((None, tm, 256), lambda b, n: (b, fwd(n), 0)),
                  pl.BlockSpec((None, tm, 256), lambda b, n: (b, fwd(n), 0)),
                  pl.BlockSpec((None, tm, 128), lambda b, n: (b, fwd(n), 0)),
                  pl.BlockSpec((None, tm, 256), lambda b, n: (b, bwd(n), 0)),
                  pl.BlockSpec((None, tm, 256), lambda b, n: (b, bwd(n), 0)),
                  pl.BlockSpec((None, tm, 128), lambda b, n: (b, bwd(n), 1))],
        out_specs=[pl.BlockSpec((None, tm, 256), lambda b, n: (b, fwd(n), 0)),
                   pl.BlockSpec((None, tm, 256), lambda b, n: (b, bwd(n), 0))],
        out_shape=[out, out],
        scratch_shapes=[pltpu.VMEM((2, GLA_HEADS, GLA_DV, GLA_DK), F32)],
        compiler_params=_params("arbitrary", "arbitrary"),
        name="gla_scan",
    )(gqk, gv, gg, gqk, gv, gg)


def _mix_kernel(x_ref, mod_ref, a_ref, hy_ref, of_ref, ob_ref, og_ref, fy_ref, fw_ref, mg_ref, hm_ref, wo_ref,
                n2g_ref, rwh_ref, rwl_ref, rb_ref, xo_ref, h2_ref, ti_ref, tg_ref, rk_ref, cnt_ref, run_ref):
    first = jnp.logical_and(pl.program_id(0) == 0, pl.program_id(1) == 0)

    @pl.when(first)
    def _():
        run_ref[...] = jnp.zeros_like(run_ref)

    mod = mod_ref[...]
    mg = mg_ref[...]
    hm = hm_ref[...]

    def head_rms(t):
        sq_hi, sq_lo = _split(t * t)
        ms = jnp.dot(sq_hi, hm, preferred_element_type=F32) + jnp.dot(sq_lo, hm, preferred_element_type=F32)
        return t * lax.rsqrt(ms + EPS)

    a = head_rms(a_ref[...]) * mg[:, 0:256]
    hy = _rms_rows(hy_ref[...]) * mg[:, 256:512]
    og = og_ref[...]
    o = head_rms(of_ref[...] + ob_ref[...]) * mg[:, 512:768] * (og / (1.0 + jnp.exp(-og)))
    fn = _rms_rows(_dot(fy_ref[...], fw_ref[...])) * mg[:, 768:1024]
    wo = wo_ref[...]
    y = _dot(a, wo[0:256]) + _dot(hy, wo[256:512]) + _dot(o, wo[512:768]) + _dot(fn, wo[768:1024])
    x = x_ref[...] + mod[2:3] * y
    xo_ref[...] = x

    h2 = _rms_rows(x) * n2g_ref[...] * (1.0 + mod[4:5]) + mod[3:4]
    bits = pltpu.bitcast(h2.astype(BF16).astype(F32), jnp.uint32)
    half = bits.shape[1] // 2
    h2_ref[...] = (bits[:, :half] >> 16) | (bits[:, half:] & jnp.uint32(0xFFFF0000))

    logits = _dot_x3(h2, rwh_ref[...], rwl_ref[...]) + rb_ref[...]
    tm = logits.shape[0]
    lane = lax.broadcasted_iota(jnp.int32, logits.shape, 1).astype(F32)
    idx_out = jnp.zeros(logits.shape, F32)
    val_out = jnp.zeros(logits.shape, F32)
    chosen = jnp.zeros(logits.shape, F32)
    picks = []
    top = None
    den = jnp.zeros((tm, 1), F32)
    for kk in range(TOP_K):
        m = jnp.max(logits, axis=-1, keepdims=True)
        idx = jnp.min(jnp.where(logits == m, lane, float(LANES)), axis=-1, keepdims=True)
        if top is None:
            top = m
        e = jnp.exp(m - top)
        den = den + e
        hit = lane == idx
        picks.append(hit)
        chosen = jnp.where(hit, 1.0, chosen)
        idx_out = jnp.where(lane == kk, idx, idx_out)
        val_out = jnp.where(lane == kk, e, val_out)
        logits = jnp.where(hit, -jnp.inf, logits)
    ti_ref[...] = idx_out.astype(jnp.int32)
    tg_ref[...] = val_out / den

    ri = lax.broadcasted_iota(jnp.int32, (tm, tm), 0)
    ci = lax.broadcasted_iota(jnp.int32, (tm, tm), 1)
    before = _dot((ci < ri).astype(F32), chosen) + run_ref[...]
    rank = jnp.zeros(logits.shape, F32)
    for kk in range(TOP_K):
        r = jnp.sum(jnp.where(picks[kk], before, 0.0), axis=-1, keepdims=True)
        rank = jnp.where(lane == kk, r, rank)
    rk_ref[...] = rank.astype(jnp.int32)
    run_ref[...] += jnp.sum(chosen, axis=0, keepdims=True)
    cnt_ref[...] = run_ref[...]


def _mix(xs, mod, a, hy, o_f, o_b, og, fy, fnet_w, mix_g, head_mean, w_out, n2g, rw, rb, n_lat_tiles):
    B, S, D = xs.shape
    tm = TOK_TILE
    rwh, rwl = _split(rw)
    tok = lambda w: pl.BlockSpec((None, tm, w), lambda b, t: (b, t, 0))
    return pl.pallas_call(
        _mix_kernel,
        grid=(B, S // tm),
        in_specs=[tok(D),
                  pl.BlockSpec((None, 6, D), lambda b, t: (jnp.where(t >= n_lat_tiles, 2, b), 0, 0)),
                  tok(256), tok(256), tok(256), tok(256), tok(256), tok(256),
                  _full((256, 256)), _full((1, D)), _full((256, 256)), _full((D, D)), _full((1, D)),
                  _full((D, LANES)), _full((D, LANES)), _full((1, LANES))],
        out_specs=[tok(D), tok(D // 2), tok(LANES), tok(LANES), tok(LANES), _full((1, LANES))],
        out_shape=[jax.ShapeDtypeStruct((B, S, D), F32), jax.ShapeDtypeStruct((B, S, D // 2), jnp.uint32),
                   jax.ShapeDtypeStruct((B, S, LANES), jnp.int32), jax.ShapeDtypeStruct((B, S, LANES), F32),
                   jax.ShapeDtypeStruct((B, S, LANES), jnp.int32), jax.ShapeDtypeStruct((1, LANES), F32)],
        scratch_shapes=[pltpu.VMEM((1, LANES), F32)],
        compiler_params=_params("arbitrary", "arbitrary"),
        name="mix_out_router",
    )(xs, mod, a, hy, o_f, o_b, og, fy, fnet_w, mix_g, head_mean.astype(BF16), w_out, n2g, rwh, rwl, rb)


def _route(top_i, rank, counts):
    T = top_i.shape[0]
    bm = MOE_BM
    padded = (counts + bm - 1) // bm * bm
    pend = jnp.cumsum(padded)
    pstart = pend - padded
    pos = (pstart[top_i] + rank).astype(jnp.int32)
    n_blocks = (T * TOP_K + bm - 1) // bm + N_EXPERTS
    blk_row0 = jnp.arange(n_blocks, dtype=pend.dtype) * bm
    blk_e = jnp.minimum(jnp.sum(pend[None, :] <= blk_row0[:, None], axis=1), N_EXPERTS - 1).astype(jnp.int32)
    n_used = (pend[-1] // bm).astype(jnp.int32).reshape(1)
    return pos, blk_e, n_used, n_blocks * bm


def _dispatch_kernel(pos_ref, h_ref, zero_ref, xg_ref, sem):
    del zero_ref
    tm = h_ref.shape[0]

    def copy(t, kk):
        return pltpu.make_async_copy(h_ref.at[pl.ds(t, 1)], xg_ref.at[pl.ds(pos_ref[0, t * TOP_K + kk], 1)], sem)

    def start(t, c):
        for kk in range(TOP_K):
            copy(t, kk).start()
        return c

    def wait(t, c):
        for kk in range(TOP_K):
            copy(t, kk).wait()
        return c

    lax.fori_loop(0, tm, start, 0, unroll=8)
    lax.fori_loop(0, tm, wait, 0, unroll=8)


def _dispatch(h2w, pos, n_rows):
    T, W = h2w.shape
    tm = TOK_TILE
    nb = T // tm
    return pl.pallas_call(
        _dispatch_kernel,
        grid=(nb,),
        in_specs=[pl.BlockSpec((None, 1, tm * TOP_K), lambda i: (i, 0, 0), memory_space=pltpu.SMEM),
                  pl.BlockSpec((tm, W), lambda i: (i, 0)),
                  pl.BlockSpec(memory_space=pl.ANY)],
        out_specs=pl.BlockSpec(memory_space=pl.ANY),
        out_shape=jax.ShapeDtypeStruct((n_rows, W), h2w.dtype),
        scratch_shapes=[pltpu.SemaphoreType.DMA(())],
        input_output_aliases={2: 0},
        compiler_params=_params("arbitrary"),
        name="moe_dispatch",
    )(pos.reshape(nb, 1, tm * TOP_K), h2w, jnp.zeros((n_rows, W), h2w.dtype))


def _gu_prep_kernel(w_ref, p_ref, g_ref, u_ref):
    y = jnp.dot(w_ref[...].astype(BF16), p_ref[...], preferred_element_type=F32)
    half = y.shape[1] // 2
    g_ref[...] = y[:, :half].astype(BF16)
    u_ref[...] = y[:, half:].astype(BF16)


def _gu_prep(w_gu):
    E, D, F2 = w_gu.shape
    tn = 512
    perm = np.zeros((tn, tn), np.float32)
    perm[2 * np.arange(tn // 2), np.arange(tn // 2)] = 1.0
    perm[2 * np.arange(tn // 2) + 1, tn // 2 + np.arange(tn // 2)] = 1.0
    out = jax.ShapeDtypeStruct((E, D, F2 // 2), BF16)
    return pl.pallas_call(
        _gu_prep_kernel,
        grid=(E, F2 // tn),
        in_specs=[pl.BlockSpec((None, D, tn), lambda e, j: (e, 0, j)), _full((tn, tn))],
        out_specs=[pl.BlockSpec((None, D, tn // 2), lambda e, j: (e, 0, j))] * 2,
        out_shape=[out, out],
        compiler_params=_params("arbitrary", "arbitrary"),
        name="moe_weight_prep",
    )(w_gu, jnp.asarray(perm, BF16))


def _expert_kernel(be_ref, nu_ref, x_ref, wg_ref, wu_ref, bg_ref, bu_ref, wd_ref, bd_ref, o_ref):
    i = pl.program_id(0)

    @pl.when(i < nu_ref[0])
    def _():
        xw = x_ref[...]
        x = jnp.concatenate([pltpu.bitcast(xw << 16, F32), pltpu.bitcast(xw & jnp.uint32(0xFFFF0000), F32)],
                            axis=1).astype(BF16)
        gate = jnp.minimum(jnp.dot(x, wg_ref[...], preferred_element_type=F32) + bg_ref[...], SWIGLU_LIMIT)
        up = jnp.clip(jnp.dot(x, wu_ref[...], preferred_element_type=F32) + bu_ref[...], -SWIGLU_LIMIT, SWIGLU_LIMIT)
        glu = gate / (1.0 + jnp.exp(-gate * SWIGLU_ALPHA))
        o_ref[...] = _dot((up + 1.0) * glu, wd_ref[...]) + bd_ref[...]

    @pl.when(i >= nu_ref[0])
    def _():
        o_ref[...] = jnp.zeros_like(o_ref)


def _experts(xg, blk_e, n_used, wg, wu, bg, bu, wd, bd):
    n_rows, W = xg.shape
    bm = MOE_BM
    D, F = wg.shape[1], wg.shape[2]
    wsel = lambda r, c: pl.BlockSpec((None, r, c), lambda i, be, nu: (be[i], 0, 0))
    return pl.pallas_call(
        _expert_kernel,
        grid_spec=pltpu.PrefetchScalarGridSpec(
            num_scalar_prefetch=2,
            grid=(n_rows // bm,),
            in_specs=[pl.BlockSpec((bm, W), lambda i, be, nu: (i, 0)),
                      wsel(D, F), wsel(D, F), wsel(1, F), wsel(1, F), wsel(F, D), wsel(1, D)],
            out_specs=pl.BlockSpec((bm, D), lambda i, be, nu: (i, 0)),
        ),
        out_shape=jax.ShapeDtypeStruct((n_rows, D), F32),
        compiler_params=_params("arbitrary"),
        name="moe_experts",
    )(blk_e, n_used, xg, wg, wu, bg, bu, wd, bd)


def _combine_kernel(pos_ref, posn_ref, x_ref, mod_ref, tg_ref, y_ref, o_ref, buf, sem):
    i = pl.program_id(0)
    tm = x_ref.shape[0]
    slot = i % 2

    def copy(p_ref, s, t, kk):
        return pltpu.make_async_copy(y_ref.at[pl.ds(p_ref[0, t * TOP_K + kk], 1)], buf.at[s, kk, pl.ds(t, 1)],
                                     sem.at[s])

    def fetch(p_ref, s):
        def body(t, c):
            for kk in range(TOP_K):
                copy(p_ref, s, t, kk).start()
            return c
        lax.fori_loop(0, tm, body, 0, unroll=8)

    @pl.when(i == 0)
    def _():
        fetch(pos_ref, 0)

    @pl.when(i + 1 < pl.num_programs(0))
    def _():
        fetch(posn_ref, 1 - slot)

    def wait(t, c):
        for kk in range(TOP_K):
            copy(pos_ref, slot, t, kk).wait()
        return c

    lax.fori_loop(0, tm, wait, 0, unroll=8)
    tg = tg_ref[...]
    f = jnp.zeros(x_ref.shape, F32)
    for kk in range(TOP_K):
        f = f + tg[:, kk:kk + 1] * buf[slot, kk]
    o_ref[...] = x_ref[...] + mod_ref[...][5:6] * f


def _combine(xs, mod, tg, yb, pos, n_lat_tiles):
    B, S, D = xs.shape
    tm = TOK_TILE
    nt = S // tm
    nb = B * nt
    pos3 = pos.reshape(nb, 1, tm * TOP_K)
    row = lambda w: pl.BlockSpec((tm, w), lambda i: (i, 0))
    out = pl.pallas_call(
        _combine_kernel,
        grid=(nb,),
        in_specs=[pl.BlockSpec((None, 1, tm * TOP_K), lambda i: (i, 0, 0), memory_space=pltpu.SMEM),
                  pl.BlockSpec((None, 1, tm * TOP_K), lambda i: (jnp.minimum(i + 1, nb - 1), 0, 0),
                               memory_space=pltpu.SMEM),
                  row(D),
                  pl.BlockSpec((None, 6, D), lambda i: (jnp.where(i % nt >= n_lat_tiles, 2, i // nt), 0, 0)),
                  row(LANES),
                  pl.BlockSpec(memory_space=pl.ANY)],
        out_specs=row(D),
        out_shape=jax.ShapeDtypeStruct((B * S, D), F32),
        scratch_shapes=[pltpu.VMEM((2, TOP_K, tm, D), F32), pltpu.SemaphoreType.DMA((2,))],
        compiler_params=_params("arbitrary"),
        name="moe_combine",
    )(pos3, pos3, xs.reshape(B * S, D), mod, tg.reshape(B * S, LANES), yb)
    return out.reshape(B, S, D)


def _final_kernel(x_ref, g_ref, o_ref):
    o_ref[...] = _rms_rows(x_ref[...]) * g_ref[...]


def _final_norm(xs, g, n_lat):
    B, S, D = xs.shape
    tm = 512
    return pl.pallas_call(
        _final_kernel,
        grid=(B, n_lat // tm),
        in_specs=[pl.BlockSpec((None, tm, D), lambda b, t: (b, t, 0)), _full((1, D))],
        out_specs=pl.BlockSpec((None, tm, D), lambda b, t: (b, t, 0)),
        out_shape=jax.ShapeDtypeStruct((B, n_lat, D), F32),
        compiler_params=_params("arbitrary", "arbitrary"),
        name="final_norm",
    )(xs, g.reshape(1, D))


def _prep_w_in(w_in):
    cq, ckv, kr = w_in[:, 0:256], w_in[:, 256:384], w_in[:, 384:416]
    hy, gq, gk = w_in[:, 416:1184], w_in[:, 1184:1312], w_in[:, 1312:1440]
    gv, glr, og, fn = w_in[:, 1440:1696], w_in[:, 1696:1728], w_in[:, 1728:1984], w_in[:, 1984:2240]
    half = MLA_ROPE // 2
    kr_sw = jnp.concatenate([-kr[:, half:], kr[:, :half]], axis=1)
    place = lambda w: jnp.tile(jnp.pad(w, ((0, 0), (MLA_NOPE, LANES - MLA_NOPE - MLA_ROPE))), (1, MLA_HEADS))
    glr_p = jnp.pad(glr, ((0, 0), (0, LANES - 2 * GLA_LR)))
    return jnp.concatenate([cq, ckv, place(kr), place(kr_sw), hy, gq, gk, gv, glr_p, og, fn], axis=1).astype(BF16)


def _prep_wq(w_uq):
    w = w_uq.reshape(MLA_Q_LORA, MLA_HEADS, MLA_NOPE + MLA_ROPE)
    nope, rope = w[..., :MLA_NOPE], w[..., MLA_NOPE:]
    half = MLA_ROPE // 2
    z_tail = jnp.zeros((MLA_Q_LORA, MLA_HEADS, LANES - MLA_NOPE - MLA_ROPE), F32)
    z_nope = jnp.zeros((MLA_Q_LORA, MLA_HEADS, MLA_NOPE), F32)
    plain = jnp.concatenate([nope, rope, z_tail], axis=-1).reshape(MLA_Q_LORA, MLA_HEADS * LANES)
    partner = jnp.concatenate([z_nope, -rope[..., half:], rope[..., :half], z_tail], axis=-1)
    return jnp.concatenate([plain, partner.reshape(MLA_Q_LORA, MLA_HEADS * LANES)], axis=1).astype(BF16)


def _prep_wkv(w_ukv):
    w = w_ukv.reshape(MLA_KV_LORA, MLA_HEADS, MLA_NOPE + MLA_V)
    k_nope, v = w[..., :MLA_NOPE], w[..., MLA_NOPE:]
    k_placed = jnp.pad(k_nope, ((0, 0), (0, 0), (0, LANES - MLA_NOPE))).reshape(MLA_KV_LORA, MLA_HEADS * LANES)
    v_t = jnp.pad(v, ((0, 0), (0, 0), (0, LANES - MLA_V))).reshape(MLA_KV_LORA, MLA_HEADS * LANES).T
    return k_placed.astype(BF16), v_t.astype(BF16)


def _prep_gk(gk_w, gk_b):
    w = jnp.zeros((LANES, 2 * LANES), F32)
    w = w.at[0:GLA_LR, 0:LANES].set(gk_w[0]).at[GLA_LR:2 * GLA_LR, LANES:].set(gk_w[1])
    return w.astype(BF16), jnp.concatenate([gk_b[0], gk_b[1]]).reshape(1, 2 * LANES)


def _rope_tables(n_lat, n_ctx):
    rows = n_lat // GRID_W
    row = jnp.repeat(jnp.arange(rows, dtype=F32), GRID_W)
    col = jnp.tile(jnp.arange(GRID_W, dtype=F32), rows)
    n_freq = MLA_ROPE // 4
    inv = ROPE_THETA ** (-jnp.arange(n_freq, dtype=F32) / n_freq)
    ang = jnp.concatenate([row[:, None] * inv, col[:, None] * inv], axis=-1)
    cos = jnp.concatenate([jnp.cos(ang), jnp.ones((n_ctx, MLA_ROPE // 2), F32)], axis=0)
    sin = jnp.concatenate([jnp.sin(ang), jnp.zeros((n_ctx, MLA_ROPE // 2), F32)], axis=0)
    S = n_lat + n_ctx
    ones, zeros = jnp.ones((S, MLA_NOPE), F32), jnp.zeros((S, MLA_NOPE), F32)
    tail = jnp.zeros((S, LANES - MLA_NOPE - MLA_ROPE), F32)
    q_scale = MLA_SCALE * math.log2(math.e)
    cq = jnp.concatenate([ones, cos, cos, tail], axis=1) * q_scale
    sq = jnp.concatenate([zeros, sin, sin, tail], axis=1) * q_scale
    ck = jnp.concatenate([zeros, cos, cos, tail], axis=1)
    sk = jnp.concatenate([zeros, sin, sin, tail], axis=1)
    return cq, sq, ck, sk


def kernel(x, c, ctx, c_ctx, ada_w, ada_b, norm1_g, norm2_g, w_in, mla_q_g, mla_w_uq, mla_kv_g, mla_w_ukv, hy_conv_w, hy_conv_b, hy_w1, hy_b1, hy_w2, hy_b2, hy_w3, hy_freq, hy_bias, gla_gk_w, gla_gk_b, fnet_w, mix_g, w_out, router_w, router_b, moe_w_gu, moe_b_gu, moe_w_down, moe_b_down, final_g):
    B, L, D = x.shape
    Lc = ctx.shape[1]
    depth = ada_w.shape[0]
    S = L + Lc
    tm = TOK_TILE
    nlt = L // tm
    assert B == 2 and D == D_MODEL and Lc == tm and L % (2 * DFT_N2 * 8) == 0

    xs = jnp.concatenate([x, ctx], axis=1)
    cc = jnp.zeros((8, D), F32).at[0:B].set(c).at[B].set(c_ctx)
    mods = _mods(cc, ada_w, ada_b)
    tabs = _rope_tables(L, Lc)
    jc = np.arange(GROUP_W)
    ang_c = 2.0 * np.pi * ((jc[:, None] * jc[None, :]) % GROUP_W) / GROUP_W
    csc = jnp.asarray(np.concatenate([np.cos(ang_c), -np.sin(ang_c)], axis=1), F32)
    head_mean = jnp.asarray(np.kron(np.eye(GROUP_W // MLA_V), np.full((MLA_V, MLA_V), 1.0 / MLA_V)), F32)

    for i in range(depth):
        mod = mods[i].reshape(8, 6, D)
        gkw, gkb = _prep_gk(gla_gk_w[i], gla_gk_b[i])
        q, k, v, hyz, gqk, gv, gg, og, fre, fim = _proj(
            xs, mod, norm1_g[i].reshape(1, D), _prep_w_in(w_in[i]), mla_q_g[i].reshape(1, -1), _prep_wq(mla_w_uq[i]),
            mla_kv_g[i].reshape(1, -1), *_prep_wkv(mla_w_ukv[i]), gkw, gkb, csc, tabs, nlt)

        a = _attention(q, k, v, L, Lc)

        filt = (hy_w1[i], hy_b1[i], hy_w2[i], hy_b2[i], hy_w3[i], hy_freq[i])
        vg_l, x0_l = _hy_pre(hyz, hy_conv_w[i], hy_conv_b[i], 0, L)
        hy = _hyena_long(vg_l, x0_l, *_hyena_filter(L, *filt), hy_bias[i], S)
        vg_c, x0_c = _hy_pre(hyz, hy_conv_w[i], hy_conv_b[i], L, Lc)
        hy = _hyena_ctx(vg_c, x0_c, *_hyena_filter(Lc, *filt), hy_bias[i], hy, L)

        o_f, o_b = _gla(gqk, gv, gg, L)

        fy = _fnet_ctx(fre, fim, L, Lc, _fnet_long(fre, fim, L))

        rw = jnp.pad(router_w[i], ((0, 0), (0, LANES - N_EXPERTS)))
        rb = jnp.concatenate([router_b[i], jnp.full((LANES - N_EXPERTS,), -1e30, F32)]).reshape(1, LANES)
        xs, h2w, top_i, top_g, rank, cnt = _mix(
            xs, mod, a, hy, o_f, o_b, og, fy, fnet_w[i].astype(BF16), mix_g[i].reshape(1, D), head_mean,
            w_out[i].astype(BF16), norm2_g[i].reshape(1, D), rw, rb, nlt)

        pos, blk_e, n_used, n_rows = _route(top_i.reshape(B * S, LANES)[:, :TOP_K],
                                            rank.reshape(B * S, LANES)[:, :TOP_K],
                                            cnt[0, :N_EXPERTS].astype(jnp.int32))
        xg = _dispatch(h2w.reshape(B * S, D // 2), pos, n_rows)
        wg, wu = _gu_prep(moe_w_gu[i])
        yb = _experts(xg, blk_e, n_used, wg, wu, moe_b_gu[i][:, None, 0::2], moe_b_gu[i][:, None, 1::2],
                      moe_w_down[i].astype(BF16), moe_b_down[i][:, None, :])
        xs = _combine(xs, mod, top_g, yb, pos, nlt)

    return _final_norm(xs, final_g, L)
```

```python
import functools
import math

import numpy as np
import jax
import jax.numpy as jnp
from jax import lax
from jax.experimental import pallas as pl
from jax.experimental.pallas import tpu as pltpu

F32 = jnp.float32
BF16 = jnp.bfloat16
HIGHEST = lax.Precision.HIGHEST

EPS = 1e-6
D_MODEL = 1024
GROUP_W = 256
MLA_HEADS = 4
MLA_NOPE = 64
MLA_ROPE = 32
MLA_V = 64
MLA_Q_LORA = 256
MLA_KV_LORA = 128
MLA_SCALE = (MLA_NOPE + MLA_ROPE) ** -0.5
ROPE_THETA = 10000.0
GRID_W = 64
HY_EMB = 33
HY_FFN = 64
HY_TARGET = 1e-2
HY_FAST_PCT = 0.3
HY_SLOW_PCT = 1.5
GLA_HEADS = 4
GLA_DK = 32
GLA_DV = 64
GLA_LR = 16
GLA_TAU = 16.0
GLA_CHUNK = 64
N_EXPERTS = 32
TOP_K = 4
SWIGLU_ALPHA = 1.702
SWIGLU_LIMIT = 7.0

LANES = 128
TOK_TILE = 256
DFT_N2 = 128
MOE_BM = 512
VMEM_LIMIT = 56 * 1024 * 1024

_O_CQ, _O_CKV, _O_KRP, _O_KRS, _O_HY, _O_GQ, _O_GK, _O_GV, _O_GLR, _O_OG, _O_FN, _W_ALL = (
    0, 256, 384, 896, 1408, 2176, 2304, 2432, 2688, 2816, 3072, 3328)


def _dot(a, b):
    return jnp.dot(a.astype(BF16), b.astype(BF16), preferred_element_type=F32)


def _dot_hi(a, b):
    return jnp.dot(a, b, precision=HIGHEST, preferred_element_type=F32)


def _split(a):
    hi = a.astype(BF16)
    return hi, (a - hi.astype(F32)).astype(BF16)


def _dot_x3(a, b_hi, b_lo):
    a_hi, a_lo = _split(a)
    mm = lambda u, w: jnp.dot(u, w, preferred_element_type=F32)
    return mm(a_hi, b_hi) + (mm(a_lo, b_hi) + mm(a_hi, b_lo))


def _params(*sem):
    return pltpu.CompilerParams(dimension_semantics=sem, vmem_limit_bytes=VMEM_LIMIT)


def _rms_rows(x):
    return x * lax.rsqrt(jnp.mean(x * x, axis=-1, keepdims=True) + EPS)


def _full(shape):
    n = len(shape)
    return pl.BlockSpec(shape, lambda *_: (0,) * n)


def _mods_kernel(c_ref, w_ref, b_ref, o_ref):
    c = c_ref[...]
    s = c / (1.0 + jnp.exp(-c))
    o_ref[...] = _dot_hi(s, w_ref[...]) + b_ref[...]


def _mods(cc, ada_w, ada_b):
    depth, d, n = ada_w.shape
    tn = 1024
    return pl.pallas_call(
        _mods_kernel,
        grid=(depth, n // tn),
        in_specs=[pl.BlockSpec((8, d), lambda i, j: (0, 0)),
                  pl.BlockSpec((None, d, tn), lambda i, j: (i, 0, j)),
                  pl.BlockSpec((None, 1, tn), lambda i, j: (i, 0, j))],
        out_specs=pl.BlockSpec((None, 8, tn), lambda i, j: (i, 0, j)),
        out_shape=jax.ShapeDtypeStruct((depth, 8, n), F32),
        compiler_params=_params("arbitrary", "arbitrary"),
        name="ada_mods",
    )(cc, ada_w, ada_b.reshape(depth, 1, n))


def _proj_kernel(x_ref, mod_ref, g_ref, win_ref, qg_ref, wq_ref, kvg_ref, wkv_ref, wvt_ref, gkw_ref, gkb_ref,
                 csh_ref, csl_ref, cq_ref, sq_ref, ck_ref, sk_ref,
                 q_out, k_out, v_out, hy_out, gqk_out, gv_out, gg_out, og_out, fre_out, fim_out):
    x = x_ref[...]
    mod = mod_ref[...]
    h = _rms_rows(x) * g_ref[...] * (1.0 + mod[1:2]) + mod[0:1]
    z = _dot(h, win_ref[...])

    nq = _rms_rows(z[:, _O_CQ:_O_CKV]) * qg_ref[...]
    qq = _dot(nq, wq_ref[...])
    nkv = _rms_rows(z[:, _O_CKV:_O_KRP]) * kvg_ref[...]
    nkv = nkv.astype(BF16)
    kvu = jnp.dot(nkv, wkv_ref[...], preferred_element_type=F32)
    cq, sq, ck, sk = cq_ref[...], sq_ref[...], ck_ref[...], sk_ref[...]
    for hd in range(MLA_HEADS):
        a, b = hd * LANES, (hd + 1) * LANES
        q_out[:, a:b] = (qq[:, a:b] * cq + qq[:, 512 + a:512 + b] * sq).astype(BF16)
        k_out[:, a:b] = (kvu[:, a:b] + z[:, _O_KRP + a:_O_KRP + b] * ck
                         + z[:, _O_KRS + a:_O_KRS + b] * sk).astype(BF16)
    vt = lax.dot_general(wvt_ref[...], nkv, (((1,), (1,)), ((), ())), preferred_element_type=F32)
    vrow = lax.broadcasted_iota(jnp.int32, vt.shape, 0)
    v_out[...] = jnp.where(vrow % LANES == MLA_V, 1.0, vt).astype(BF16)

    hy_out[...] = z[:, _O_HY:_O_GQ]
    gqk_out[:, :LANES] = z[:, _O_GQ:_O_GK] * (GLA_DK ** -0.5)
    gqk_out[:, LANES:] = z[:, _O_GK:_O_GV]
    gv_out[...] = z[:, _O_GV:_O_GLR]
    gates = _dot(z[:, _O_GLR:_O_OG], gkw_ref[...]) + gkb_ref[...]
    gg_out[...] = (jnp.minimum(gates, 0.0) - jnp.log(1.0 + jnp.exp(-jnp.abs(gates)))) * (1.0 / GLA_TAU)
    og_out[...] = z[:, _O_OG:_O_FN]
    fcs = _dot_x3(z[:, _O_FN:_W_ALL], csh_ref[...], csl_ref[...])
    fre_out[...] = fcs[:, :GROUP_W]
    fim_out[...] = fcs[:, GROUP_W:]


def _proj(xs, mod, g1, win, qg, wq, kvg, wkv, wvt, gkw, gkb, csc, tabs, n_lat_tiles):
    B, S, D = xs.shape
    tm = TOK_TILE
    csh, csl = _split(csc)
    tok = lambda w: pl.BlockSpec((None, tm, w), lambda b, t: (b, t, 0))
    tab = pl.BlockSpec((tm, LANES), lambda b, t: (t, 0))
    shp = lambda w, dt: jax.ShapeDtypeStruct((B, S, w), dt)
    return pl.pallas_call(
        _proj_kernel,
        grid=(B, S // tm),
        in_specs=[tok(D),
                  pl.BlockSpec((None, 6, D), lambda b, t: (jnp.where(t >= n_lat_tiles, 2, b), 0, 0)),
                  _full((1, D)), _full(win.shape), _full((1, MLA_Q_LORA)), _full(wq.shape),
                  _full((1, MLA_KV_LORA)), _full(wkv.shape), _full(wvt.shape), _full(gkw.shape), _full(gkb.shape),
                  _full(csc.shape), _full(csc.shape), tab, tab, tab, tab],
        out_specs=[tok(512), tok(512), pl.BlockSpec((None, None, MLA_HEADS * LANES, tm), lambda b, t: (b, t, 0, 0)),
                   tok(768), tok(256), tok(256), tok(256), tok(256), tok(256), tok(256)],
        out_shape=[shp(512, BF16), shp(512, BF16), jax.ShapeDtypeStruct((B, S // tm, MLA_HEADS * LANES, tm), BF16),
                   shp(768, F32), shp(256, F32), shp(256, F32),
                   shp(256, F32), shp(256, F32), shp(256, F32), shp(256, F32)],
        compiler_params=_params("arbitrary", "arbitrary"),
        name="in_proj",
    )(xs, mod, g1, win, qg, wq, kvg, wkv, wvt, gkw, gkb, csh, csl, *tabs)


def _attn_kernel(q_ref, k_ref, vt_ref, o_ref, *, n_lat_tiles, tiles_per_chunk, n_lat, n_ctx):
    qi = pl.program_id(2)
    tq = q_ref.shape[0]
    tile = vt_ref.shape[2]

    def heads(tile0, n_chunks, n_tiles):
        sub = LANES

        def body(c, carry):
            t0 = tile0 + c * n_tiles
            off = pl.multiple_of(t0 * tile, tile)
            state = list(carry)
            for j in range(n_tiles * tile // sub):
                for hd in range(2):
                    m, acc = state[hd]
                    q = q_ref[:, hd * LANES:(hd + 1) * LANES]
                    kc = k_ref[pl.ds(off + j * sub, sub), hd * LANES:(hd + 1) * LANES]
                    s = lax.dot_general(kc, q, (((1,), (1,)), ((), ())), preferred_element_type=F32)
                    m_new = jnp.maximum(m, jnp.max(s, axis=0, keepdims=True))
                    alpha = jnp.exp2(m - m_new)
                    p = jnp.exp2(s - m_new).astype(BF16)
                    lo = (j * sub) % tile
                    vt = vt_ref[t0 + (j * sub) // tile, hd * LANES:(hd + 1) * LANES, lo:lo + sub]
                    state[hd] = (m_new, alpha * acc + jnp.dot(vt, p, preferred_element_type=F32))
            return tuple(state)

        one = (jnp.full((1, tq), -1e30, F32), jnp.zeros((LANES, tq), F32))
        res = lax.fori_loop(0, n_chunks, body, (one, one))
        for hd in range(2):
            acc_t = res[hd][1].T
            o_ref[:, hd * MLA_V:(hd + 1) * MLA_V] = acc_t[:, :MLA_V] / acc_t[:, MLA_V:MLA_V + 1]

    n_all = (n_lat + n_ctx) // tile

    @pl.when(qi < n_lat_tiles)
    def _():
        heads(0, n_all // tiles_per_chunk, tiles_per_chunk)

    @pl.when(qi >= n_lat_tiles)
    def _():
        heads(n_lat // tile, 1, n_ctx // tile)


def _attention(q, k, vt, n_lat, n_ctx):
    B, S, _ = q.shape
    tq = TOK_TILE
    n_tiles = vt.shape[1]
    tiles_per_chunk = 5 if n_tiles % 5 == 0 else 1
    kern = functools.partial(_attn_kernel, n_lat_tiles=n_lat // tq, tiles_per_chunk=tiles_per_chunk,
                             n_lat=n_lat, n_ctx=n_ctx)
    return pl.pallas_call(
        kern,
        grid=(B, 2, S // tq),
        in_specs=[pl.BlockSpec((None, tq, 256), lambda b, p, t: (b, t, p)),
                  pl.BlockSpec((None, S, 256), lambda b, p, t: (b, 0, p)),
                  pl.BlockSpec((None, n_tiles, 256, vt.shape[3]), lambda b, p, t: (b, 0, p, 0))],
        out_specs=pl.BlockSpec((None, tq, 128), lambda b, p, t: (b, t, p)),
        out_shape=jax.ShapeDtypeStruct((B, S, 256), F32),
        compiler_params=_params("arbitrary", "arbitrary", "arbitrary"),
        name="mla_attention",
    )(q, k, vt)


def _hy_pre_kernel(z_ref, zp_ref, zn_ref, w_ref, b_ref, vg_ref, x0_ref, *, n_tiles):
    i = pl.program_id(1)
    z = z_ref[...]
    tm = z.shape[0]
    rows = lax.broadcasted_iota(jnp.int32, z.shape, 0)
    prev_row = jnp.where(i == 0, 0.0, zp_ref[7:8, :])
    next_row = jnp.where(i == n_tiles - 1, 0.0, zn_ref[0:1, :])
    z_m = jnp.where(rows == 0, prev_row, pltpu.roll(z, 1, 0))
    z_p = jnp.where(rows == tm - 1, next_row, pltpu.roll(z, tm - 1, 0))
    w = w_ref[...]
    u = z_m * w[0:1] + z * w[1:2] + z_p * w[2:3] + b_ref[...]
    vg_ref[...] = u[:, 2 * GROUP_W:] * u[:, GROUP_W:2 * GROUP_W]
    x0_ref[...] = u[:, :GROUP_W]


def _hy_pre(hyz, conv_w, conv_b, row0, n_rows):
    B, S, W = hyz.shape
    tm = TOK_TILE
    nt = n_rows // tm
    t0, r8, last8 = row0 // tm, row0 // 8, S // 8 - 1
    kern = functools.partial(_hy_pre_kernel, n_tiles=nt)
    out = jax.ShapeDtypeStruct((B, n_rows, GROUP_W), F32)
    return pl.pallas_call(
        kern,
        grid=(B, nt),
        in_specs=[pl.BlockSpec((None, tm, W), lambda b, i: (b, t0 + i, 0)),
                  pl.BlockSpec((None, 8, W), lambda b, i: (b, jnp.maximum(r8 + i * (tm // 8) - 1, 0), 0)),
                  pl.BlockSpec((None, 8, W), lambda b, i: (b, jnp.minimum(r8 + (i + 1) * (tm // 8), last8), 0)),
                  _full((3, W)), _full((1, W))],
        out_specs=[pl.BlockSpec((None, tm, GROUP_W), lambda b, i: (b, i, 0))] * 2,
        out_shape=[out, out],
        compiler_params=_params("arbitrary", "arbitrary"),
        name="hyena_pre",
    )(hyz, hyz, hyz, conv_w, conv_b.reshape(1, W))


def _filter_kernel(fv_ref, w1_ref, b1_ref, w2_ref, b2_ref, w3_ref, fr_ref, dl_ref, h_ref, ss_ref, *, n_pos):
    i = pl.program_id(0)
    tl = h_ref.shape[0]
    row = lax.broadcasted_iota(jnp.int32, (tl, LANES), 0) + i * tl
    pos = jnp.where(row <= n_pos, row, 2 * n_pos - row).astype(F32)
    lane = lax.broadcasted_iota(jnp.int32, (tl, LANES), 1)
    t = pos * (1.0 / (n_pos - 1))
    arg = (pos * (2.0 * math.pi / n_pos)) * fv_ref[...]
    feat = jnp.where(lane == 0, t, jnp.where(lane < 17, jnp.cos(arg), jnp.where(lane < HY_EMB, -jnp.sin(arg), 0.0)))
    fr = fr_ref[...]
    h = jnp.sin(fr * (_dot_hi(feat, w1_ref[...]) + b1_ref[...]))
    h = jnp.sin(fr * (_dot_hi(h, w2_ref[...]) + b2_ref[...]))
    h = _dot_hi(h, w3_ref[...]) * jnp.exp(-t[:, 0:1] * dl_ref[...])
    r1 = row[:, 0:1]
    h = jnp.where(r1 < n_pos, h[:, :GROUP_W], jnp.where(r1 == n_pos, 0.0, h[:, GROUP_W:]))
    h_ref[...] = h

    @pl.when(i == 0)
    def _():
        ss_ref[...] = jnp.zeros_like(ss_ref)

    ss_ref[...] += jnp.sum(h * h, axis=0, keepdims=True)


def _hyena_filter(n_pos, w1, b1, w2, b2, w3, freq):
    tl = min(2 * n_pos, 512)
    bands = (HY_EMB - 1) // 2
    f = np.linspace(1e-4, bands - 1, bands)
    fv = np.zeros((1, LANES), np.float32)
    fv[0, 1:17] = f
    fv[0, 17:33] = f
    w1p = jnp.zeros((LANES, HY_FFN), F32).at[:HY_EMB].set(w1)
    max_decay = math.log(HY_TARGET) / HY_FAST_PCT
    min_decay = math.log(HY_TARGET) / HY_SLOW_PCT
    deltas = np.abs(np.linspace(min_decay, max_decay, GROUP_W)).astype(np.float32)
    dl = jnp.asarray(np.concatenate([deltas, deltas])[None, :])
    kern = functools.partial(_filter_kernel, n_pos=n_pos)
    k2, ss = pl.pallas_call(
        kern,
        grid=(2 * n_pos // tl,),
        in_specs=[_full((1, LANES)), _full((LANES, HY_FFN)), _full((1, HY_FFN)), _full((HY_FFN, HY_FFN)),
                  _full((1, HY_FFN)), _full((HY_FFN, 2 * GROUP_W)), _full((1, HY_FFN)), _full((1, 2 * GROUP_W))],
        out_specs=[pl.BlockSpec((tl, GROUP_W), lambda i: (i, 0)), _full((1, GROUP_W))],
        out_shape=[jax.ShapeDtypeStruct((2 * n_pos, GROUP_W), F32), jax.ShapeDtypeStruct((1, GROUP_W), F32)],
        compiler_params=_params("arbitrary"),
        name="hyena_filter",
    )(jnp.asarray(fv), w1p, b1.reshape(1, -1), w2, b2.reshape(1, -1), w3, freq.reshape(1, -1), dl)
    return k2, lax.rsqrt(ss)


def _dft_mats(n, sign):
    j = np.arange(n)
    ang = 2.0 * np.pi * ((j[:, None] * j[None, :]) % n) / n
    return np.cos(ang), sign * np.sin(ang)


def _twiddle(n1, n2, sign):
    ang = 2.0 * np.pi * ((np.arange(n1)[:, None] * np.arange(n2)[None, :]) % (n1 * n2)) / (n1 * n2)
    return np.cos(ang), sign * np.sin(ang)


def _block_complex(re, im):
    return np.block([[re, -im], [im, re]])


def _stage1_kernel(m_ref, zr_ref, zi_ref, ar_ref, ai_ref):
    a = _dot_hi(m_ref[...], jnp.concatenate([zr_ref[...], zi_ref[...]], axis=0))
    half = ar_ref.shape[0]
    ar_ref[...] = a[:half]
    ai_ref[...] = a[half:]


def _stage1_real_kernel(m_ref, zr_ref, ar_ref, ai_ref):
    a = _dot_hi(m_ref[...], zr_ref[...])
    half = ar_ref.shape[0]
    ar_ref[...] = a[:half]
    ai_ref[...] = a[half:]


def _col_tile(cols):
    return 2048 if cols % 2048 == 0 else cols


def _hy_stage1(vg2, n1):
    _, r, cols = vg2.shape
    cr, ci = _dft_mats(n1, -1.0)
    m = jnp.asarray(_block_complex(cr[:, :r], ci[:, :r]), F32)
    tc = _col_tile(cols)
    out = jax.ShapeDtypeStruct((n1, cols), F32)
    return pl.pallas_call(
        _stage1_kernel,
        grid=(cols // tc,),
        in_specs=[_full(m.shape), pl.BlockSpec((None, r, tc), lambda j: (0, 0, j)),
                  pl.BlockSpec((None, r, tc), lambda j: (1, 0, j))],
        out_specs=[pl.BlockSpec((n1, tc), lambda j: (0, j))] * 2,
        out_shape=[out, out],
        compiler_params=_params("arbitrary"),
        name="hyena_fwd_stage1",
    )(m, vg2, vg2)


def _filter_stage1(k2v, n1):
    _, cols = k2v.shape
    cr, ci = _dft_mats(n1, -1.0)
    m = jnp.asarray(np.concatenate([cr, ci], axis=0), F32)
    tc = _col_tile(cols)
    out = jax.ShapeDtypeStruct((n1, cols), F32)
    return pl.pallas_call(
        _stage1_real_kernel,
        grid=(cols // tc,),
        in_specs=[_full(m.shape), pl.BlockSpec((n1, tc), lambda j: (0, j))],
        out_specs=[pl.BlockSpec((n1, tc), lambda j: (0, j))] * 2,
        out_shape=[out, out],
        compiler_params=_params("arbitrary"),
        name="hyena_filter_stage1",
    )(m, k2v)


def _twiddled(f_re, f_im, t_re, t_im):
    g_re = f_re * t_re - f_im * t_im
    g_im = f_re * t_im + f_im * t_re
    return jnp.concatenate([jnp.concatenate([g_re, -g_im], axis=1), jnp.concatenate([g_im, g_re], axis=1)], axis=0)


def _filter_stage2_kernel(ar_ref, ai_ref, twr_ref, twi_ref, fr_ref, fi_ref, sc_ref, kr_ref, ki_ref):
    g = _twiddled(fr_ref[...], fi_ref[...], twr_ref[...], twi_ref[...])
    x = _dot_hi(g, jnp.concatenate([ar_ref[...], ai_ref[...]], axis=0)) * sc_ref[...]
    n2 = kr_ref.shape[0]
    kr_ref[...] = x[:n2]
    ki_ref[...] = x[n2:]


def _filter_stage2(ar, ai, inv_norm, n1):
    n2 = DFT_N2
    c = ar.shape[1] // n2
    fr, fi = _dft_mats(n2, -1.0)
    twr, twi = _twiddle(n1, n2, -1.0)
    blk = pl.BlockSpec((None, n2, c), lambda k: (k, 0, 0))
    tw = pl.BlockSpec((None, 1, n2), lambda k: (k, 0, 0))
    out = jax.ShapeDtypeStruct((n1, n2, c), F32)
    return pl.pallas_call(
        _filter_stage2_kernel,
        grid=(n1,),
        in_specs=[blk, blk, tw, tw, _full((n2, n2)), _full((n2, n2)), _full((1, c))],
        out_specs=[blk, blk],
        out_shape=[out, out],
        compiler_params=_params("arbitrary"),
        name="hyena_filter_stage2",
    )(ar.reshape(n1, n2, c), ai.reshape(n1, n2, c), jnp.asarray(twr.reshape(n1, 1, n2), F32),
      jnp.asarray(twi.reshape(n1, 1, n2), F32), jnp.asarray(fr, F32), jnp.asarray(fi, F32), inv_norm)


def _hy_stage2_kernel(ar_ref, ai_ref, kr_ref, ki_ref, twr_ref, twi_ref, tcr_ref, tci_ref, fr_ref, fi_ref,
                      br_ref, bi_ref):
    f_re, f_im = fr_ref[...], fi_ref[...]
    n2 = f_re.shape[0]
    g = _twiddled(f_re, f_im, twr_ref[...], twi_ref[...])
    x = _dot_hi(g, jnp.concatenate([ar_ref[...], ai_ref[...]], axis=0))
    x_re, x_im = x[:n2], x[n2:]
    k_re, k_im = kr_ref[...], ki_ref[...]
    y = jnp.concatenate([x_re * k_re - x_im * k_im, x_re * k_im + x_im * k_re], axis=0)
    g_inv = _twiddled(f_re, -f_im, tcr_ref[...], -tci_ref[...])
    b = _dot_hi(g_inv, y)
    br_ref[...] = b[:n2]
    bi_ref[...] = b[n2:]


def _hy_stage2(ar, ai, kr, ki, n1):
    n2 = DFT_N2
    c = kr.shape[2]
    fr, fi = _dft_mats(n2, -1.0)
    twr, twi = _twiddle(n1, n2, -1.0)
    blk = pl.BlockSpec((None, n2, c), lambda k: (k, 0, 0))
    tw = pl.BlockSpec((None, 1, n2), lambda k: (k, 0, 0))
    twc = pl.BlockSpec((None, n2, 1), lambda k: (k, 0, 0))
    out = jax.ShapeDtypeStruct((n1, n2, c), F32)
    return pl.pallas_call(
        _hy_stage2_kernel,
        grid=(n1,),
        in_specs=[blk, blk, blk, blk, tw, tw, twc, twc, _full((n2, n2)), _full((n2, n2))],
        out_specs=[blk, blk],
        out_shape=[out, out],
        compiler_params=_params("arbitrary"),
        name="hyena_conv_stage2",
    )(ar.reshape(n1, n2, c), ai.reshape(n1, n2, c), kr, ki,
      jnp.asarray(twr.reshape(n1, 1, n2), F32), jnp.asarray(twi.reshape(n1, 1, n2), F32),
      jnp.asarray(twr.reshape(n1, n2, 1), F32), jnp.asarray(twi.reshape(n1, n2, 1), F32),
      jnp.asarray(fr, F32), jnp.asarray(fi, F32))


def _hy_stage3_kernel(m_ref, br_ref, bi_ref, vg0_ref, vg1_ref, x00_ref, x01_ref, bias_ref, o_ref):
    conv = _dot_hi(m_ref[...], jnp.concatenate([br_ref[...], bi_ref[...]], axis=0))
    r = vg0_ref.shape[0]
    bias = bias_ref[...]
    o_ref[0, :r] = (conv[:r] + vg0_ref[...] * bias) * x00_ref[...]
    o_ref[1, :r] = (conv[r:] + vg1_ref[...] * bias) * x01_ref[...]
    if o_ref.shape[1] > r:
        o_ref[:, r:] = jnp.zeros((2, o_ref.shape[1] - r, o_ref.shape[2]), F32)


def _hy_stage3(br, bi, vg2, x02, bias_cols, n1, rows_total):
    _, r, cols = vg2.shape
    cr, ci = _dft_mats(n1, 1.0)
    m = jnp.asarray(_block_complex(cr[:r], ci[:r]) / (n1 * DFT_N2), F32)
    tc = _col_tile(cols)
    plane = lambda p: pl.BlockSpec((None, r, tc), lambda j: (p, 0, j))
    return pl.pallas_call(
        _hy_stage3_kernel,
        grid=(cols // tc,),
        in_specs=[_full(m.shape), pl.BlockSpec((n1, tc), lambda j: (0, j)), pl.BlockSpec((n1, tc), lambda j: (0, j)),
                  plane(0), plane(1), plane(0), plane(1), pl.BlockSpec((1, tc), lambda j: (0, j))],
        out_specs=pl.BlockSpec((2, rows_total, tc), lambda j: (0, 0, j)),
        out_shape=jax.ShapeDtypeStruct((2, rows_total, cols), F32),
        compiler_params=_params("arbitrary"),
        name="hyena_inv_stage3",
    )(m, br, bi, vg2, vg2, x02, x02, bias_cols)


def _hyena_long(vg, x0, k2, inv_norm, bias, n_all):
    B, L, C = vg.shape
    assert B == 2
    n2 = DFT_N2
    n1 = 2 * L // n2
    r = L // n2
    fa_r, fa_i = _filter_stage1(k2.reshape(n1, n2 * C), n1)
    kr, ki = _filter_stage2(fa_r, fa_i, inv_norm, n1)
    vg2 = vg.reshape(2, r, n2 * C)
    ar, ai = _hy_stage1(vg2, n1)
    br, bi = _hy_stage2(ar, ai, kr, ki, n1)
    y = _hy_stage3(br.reshape(n1, n2 * C), bi.reshape(n1, n2 * C), vg2, x0.reshape(2, r, n2 * C),
                   jnp.tile(bias.reshape(1, C), (1, n2)), n1, n_all // n2)
    return y.reshape(2, n_all, C)


def _hy_ctx_kernel(vg_ref, x0_ref, kext_ref, sc_ref, bias_ref, buf_ref, o_ref):
    del buf_ref
    n = vg_ref.shape[0]

    def body(a, acc):
        src = vg_ref[pl.ds(pl.multiple_of(8 * a, 8), 8), :]
        win = pl.multiple_of(n - 8 * a, 8)
        for r in range(8):
            acc = acc + kext_ref[r, pl.ds(win, n), :] * src[r:r + 1]
        return acc

    acc = lax.fori_loop(0, n // 8, body, jnp.zeros(vg_ref.shape, F32))
    o_ref[...] = (acc * sc_ref[...] + vg_ref[...] * bias_ref[...]) * x0_ref[...]


def _hyena_ctx(vg, x0, k2, inv_norm, bias, buf, row0):
    B, n, C = vg.shape
    kext = jnp.concatenate([k2[n:], k2[:n]], axis=0)
    kext = jnp.stack([jnp.roll(kext, r, axis=0) for r in range(8)])
    blk = pl.BlockSpec((None, n, C), lambda b: (b, 0, 0))
    return pl.pallas_call(
        _hy_ctx_kernel,
        grid=(B,),
        in_specs=[blk, blk, _full((8, 2 * n, C)), _full((1, C)), _full((1, C)), pl.BlockSpec(memory_space=pl.ANY)],
        out_specs=pl.BlockSpec((None, n, C), lambda b: (b, row0 // n, 0)),
        out_shape=jax.ShapeDtypeStruct(buf.shape, F32),
        input_output_aliases={5: 0},
        compiler_params=_params("arbitrary"),
        name="hyena_ctx",
    )(vg, x0, kext, inv_norm, bias.reshape(1, C), buf)


def _fn_stage1_kernel(c_ref, s_ref, re_ref, im_ref, ar_ref, ai_ref):
    c, s, re, im = c_ref[...], s_ref[...], re_ref[...], im_ref[...]
    ar_ref[...] = _dot_hi(c, re) + _dot_hi(s, im)
    ai_ref[...] = _dot_hi(c, im) - _dot_hi(s, re)


def _fn_stage2_kernel(ar_ref, ai_ref, twr_ref, twi_ref, fr_ref, fi_ref, o_ref):
    f_re, f_im, t_re, t_im = fr_ref[...], fi_ref[...], twr_ref[...], twi_ref[...]
    g_re = f_re * t_re - f_im * t_im
    g_im = f_re * t_im + f_im * t_re
    n2 = f_re.shape[0]
    o_ref[:n2] = _dot_hi(g_re, ar_ref[...]) - _dot_hi(g_im, ai_ref[...])
    if o_ref.shape[0] > n2:
        o_ref[n2:] = jnp.zeros((o_ref.shape[0] - n2, o_ref.shape[1]), F32)


def _fnet_long(fre, fim, n_lat):
    B, S, C = fre.shape
    n2 = DFT_N2
    n1 = n_lat // n2
    cols = n2 * C
    c1, s1 = _dft_mats(n1, 1.0)
    tc = _col_tile(cols)
    rows = pl.BlockSpec((None, n1, tc), lambda b, j: (b, 0, j))
    a_shape = jax.ShapeDtypeStruct((B, n1, cols), F32)
    ar, ai = pl.pallas_call(
        _fn_stage1_kernel,
        grid=(B, cols // tc),
        in_specs=[_full((n1, n1)), _full((n1, n1)), rows, rows],
        out_specs=[rows, rows],
        out_shape=[a_shape, a_shape],
        compiler_params=_params("arbitrary", "arbitrary"),
        name="fnet_stage1",
    )(jnp.asarray(c1, F32), jnp.asarray(s1, F32), fre.reshape(B, S // n2, cols), fim.reshape(B, S // n2, cols))
    fr, fi = _dft_mats(n2, -1.0)
    scale = 1.0 / math.sqrt(n_lat * C)
    twr, twi = _twiddle(n1, n2, -1.0)
    blk = pl.BlockSpec((None, None, n2, C), lambda b, k: (b, k, 0, 0))
    tw = pl.BlockSpec((None, 1, n2), lambda b, k: (k, 0, 0))
    y = pl.pallas_call(
        _fn_stage2_kernel,
        grid=(B, n1),
        in_specs=[blk, blk, tw, tw, _full((n2, n2)), _full((n2, n2))],
        out_specs=pl.BlockSpec((None, S // n1, C), lambda b, k: (b, 0, k)),
        out_shape=jax.ShapeDtypeStruct((B, S // n1, n1 * C), F32),
        compiler_params=_params("arbitrary", "arbitrary"),
        name="fnet_stage2",
    )(ar.reshape(B, n1, n2, C), ai.reshape(B, n1, n2, C), jnp.asarray(twr.reshape(n1, 1, n2), F32),
      jnp.asarray(twi.reshape(n1, 1, n2), F32), jnp.asarray(fr * scale, F32), jnp.asarray(fi * scale, F32))
    return y.reshape(B, S, C)


def _fn_ctx_kernel(c_ref, s_ref, re_ref, im_ref, buf_ref, o_ref):
    del buf_ref
    o_ref[...] = _dot_hi(c_ref[...], re_ref[...]) + _dot_hi(s_ref[...], im_ref[...])


def _fnet_ctx(fre, fim, n_lat, n_ctx, buf):
    B, S, C = fre.shape
    c1, s1 = _dft_mats(n_ctx, 1.0)
    scale = 1.0 / math.sqrt(n_ctx * C)
    blk = pl.BlockSpec((None, n_ctx, C), lambda b: (b, n_lat // n_ctx, 0))
    return pl.pallas_call(
        _fn_ctx_kernel,
        grid=(B,),
        in_specs=[_full((n_ctx, n_ctx)), _full((n_ctx, n_ctx)), blk, blk, pl.BlockSpec(memory_space=pl.ANY)],
        out_specs=blk,
        out_shape=jax.ShapeDtypeStruct((B, S, C), F32),
        input_output_aliases={4: 0},
        compiler_params=_params("arbitrary"),
        name="fnet_ctx",
    )(jnp.asarray(c1 * scale, F32), jnp.asarray(s1 * scale, F32), fre, fim, buf)


def _gla_kernel(qkf_ref, vf_ref, gf_ref, qkb_ref, vb_ref, gb_ref, of_ref, ob_ref, st_ref):
    n = pl.program_id(1)
    tm = qkf_ref.shape[0]
    ck = GLA_CHUNK

    @pl.when(n == 0)
    def _():
        st_ref[...] = jnp.zeros_like(st_ref)

    ri = lax.broadcasted_iota(jnp.int32, (ck, ck), 0)
    ci = lax.broadcasted_iota(jnp.int32, (ck, ck), 1)

    def sub_chunk(qk_ref, v_ref, g_ref, o_ref, states, r0, reverse):
        keep = (ci >= ri) if reverse else (ci <= ri)
        g = g_ref[r0:r0 + ck, :]
        b = _dot_hi(keep.astype(F32), g)
        b_end = b[0:1] if reverse else b[ck - 1:ck]
        q = qk_ref[r0:r0 + ck, :LANES] * jnp.exp(b)
        k = qk_ref[r0:r0 + ck, LANES:]
        k_in = k * jnp.exp(-b)
        k_out = k * jnp.exp(b_end - b)
        decay = jnp.exp(b_end)
        new_states = []
        for hd in range(GLA_HEADS):
            ks = slice(hd * GLA_DK, (hd + 1) * GLA_DK)
            vs = slice(hd * GLA_DV, (hd + 1) * GLA_DV)
            qh, vh = q[:, ks].astype(BF16), v_ref[r0:r0 + ck, vs].astype(BF16)
            a = lax.dot_general(qh, k_in[:, ks].astype(BF16), (((1,), (1,)), ((), ())), preferred_element_type=F32)
            a = jnp.where(keep, a, 0.0)
            st = states[hd]
            o = jnp.dot(a.astype(BF16), vh, preferred_element_type=F32)
            o += lax.dot_general(qh, st.astype(BF16), (((1,), (1,)), ((), ())), preferred_element_type=F32)
            o_ref[r0:r0 + ck, vs] = o
            kv = lax.dot_general(vh, k_out[:, ks].astype(BF16), (((0,), (0,)), ((), ())), preferred_element_type=F32)
            new_states.append(st * decay[:, ks] + kv)
        return new_states

    st_f = [st_ref[0, hd] for hd in range(GLA_HEADS)]
    st_b = [st_ref[1, hd] for hd in range(GLA_HEADS)]
    n_sub = tm // ck
    for i in range(n_sub):
        st_f = sub_chunk(qkf_ref, vf_ref, gf_ref, of_ref, st_f, i * ck, False)
        st_b = sub_chunk(qkb_ref, vb_ref, gb_ref, ob_ref, st_b, (n_sub - 1 - i) * ck, True)
    for hd in range(GLA_HEADS):
        st_ref[0, hd] = st_f[hd]
        st_ref[1, hd] = st_b[hd]


def _gla(gqk, gv, gg, n_lat):
    B, S, _ = gqk.shape
    tm = TOK_TILE
    nl = n_lat // tm
    nt = S // tm
    assert nt == nl + 1
    fwd = lambda n: jnp.where(n == 0, nl, n - 1)
    bwd = lambda n: jnp.where(n == 0, nl, nl - n)
    out = jax.ShapeDtypeStruct((B, S, 256), F32)
    return pl.pallas_call(
        _gla_kernel,
        grid=(B, nt),
        in_specs=[pl.BlockSpec((None, tm, 256), lambda b, n: (b, fwd(n), 0)),
                  pl.BlockSpec((None, tm, 256), lambda b, n: (b, fwd(n), 0)),
                  pl.BlockSpec((None, tm, 128), lambda b, n: (b, fwd(n), 0)),
                  pl.BlockSpec((None, tm, 256), lambda b, n: (b, bwd(n), 0)),
                  pl.BlockSpec((None, tm, 256), lambda b, n: (b, bwd(n), 0)),
                  pl.BlockSpec((None, tm, 128), lambda b, n: (b, bwd(n), 1))],
        out_specs=[pl.BlockSpec((None, tm, 256), lambda b, n: (b, fwd(n), 0)),
                   pl.BlockSpec((None, tm, 256), lambda b, n: (b, bwd(n), 0))],
        out_shape=[out, out],
        scratch_shapes=[pltpu.VMEM((2, GLA_HEADS, GLA_DV, GLA_DK), F32)],
        compiler_params=_params("arbitrary", "arbitrary"),
        name="gla_scan",
    )(gqk, gv, gg, gqk, gv, gg)


def _mix_kernel(x_ref, mod_ref, a_ref, hy_ref, of_ref, ob_ref, og_ref, fy_ref, fw_ref, mg_ref, hm_ref, wo_ref,
                n2g_ref, rwh_ref, rwl_ref, rb_ref, xo_ref, h2_ref, ti_ref, tg_ref, rk_ref, cnt_ref, run_ref):
    first = jnp.logical_and(pl.program_id(0) == 0, pl.program_id(1) == 0)

    @pl.when(first)
    def _():
        run_ref[...] = jnp.zeros_like(run_ref)

    mod = mod_ref[...]
    mg = mg_ref[...]
    hm = hm_ref[...]

    def head_rms(t):
        sq_hi, sq_lo = _split(t * t)
        ms = jnp.dot(sq_hi, hm, preferred_element_type=F32) + jnp.dot(sq_lo, hm, preferred_element_type=F32)
        return t * lax.rsqrt(ms + EPS)

    a = head_rms(a_ref[...]) * mg[:, 0:256]
    hy = _rms_rows(hy_ref[...]) * mg[:, 256:512]
    og = og_ref[...]
    o = head_rms(of_ref[...] + ob_ref[...]) * mg[:, 512:768] * (og / (1.0 + jnp.exp(-og)))
    fn = _rms_rows(_dot(fy_ref[...], fw_ref[...])) * mg[:, 768:1024]
    wo = wo_ref[...]
    y = _dot(a, wo[0:256]) + _dot(hy, wo[256:512]) + _dot(o, wo[512:768]) + _dot(fn, wo[768:1024])
    x = x_ref[...] + mod[2:3] * y
    xo_ref[...] = x

    h2 = _rms_rows(x) * n2g_ref[...] * (1.0 + mod[4:5]) + mod[3:4]
    bits = pltpu.bitcast(h2.astype(BF16).astype(F32), jnp.uint32)
    half = bits.shape[1] // 2
    h2_ref[...] = (bits[:, :half] >> 16) | (bits[:, half:] & jnp.uint32(0xFFFF0000))

    logits = _dot_x3(h2, rwh_ref[...], rwl_ref[...]) + rb_ref[...]
    tm = logits.shape[0]
    lane = lax.broadcasted_iota(jnp.int32, logits.shape, 1).astype(F32)
    idx_out = jnp.zeros(logits.shape, F32)
    val_out = jnp.zeros(logits.shape, F32)
    chosen = jnp.zeros(logits.shape, F32)
    picks = []
    top = None
    den = jnp.zeros((tm, 1), F32)
    for kk in range(TOP_K):
        m = jnp.max(logits, axis=-1, keepdims=True)
        idx = jnp.min(jnp.where(logits == m, lane, float(LANES)), axis=-1, keepdims=True)
        if top is None:
            top = m
        e = jnp.exp(m - top)
        den = den + e
        hit = lane == idx
        picks.append(hit)
        chosen = jnp.where(hit, 1.0, chosen)
        idx_out = jnp.where(lane == kk, idx, idx_out)
        val_out = jnp.where(lane == kk, e, val_out)
        logits = jnp.where(hit, -jnp.inf, logits)
    ti_ref[...] = idx_out.astype(jnp.int32)
    tg_ref[...] = val_out / den

    ri = lax.broadcasted_iota(jnp.int32, (tm, tm), 0)
    ci = lax.broadcasted_iota(jnp.int32, (tm, tm), 1)
    before = _dot((ci < ri).astype(F32), chosen) + run_ref[...]
    rank = jnp.zeros(logits.shape, F32)
    for kk in range(TOP_K):
        r = jnp.sum(jnp.where(picks[kk], before, 0.0), axis=-1, keepdims=True)
        rank = jnp.where(lane == kk, r, rank)
    rk_ref[...] = rank.astype(jnp.int32)
    run_ref[...] += jnp.sum(chosen, axis=0, keepdims=True)
    cnt_ref[...] = run_ref[...]


def _mix(xs, mod, a, hy, o_f, o_b, og, fy, fnet_w, mix_g, head_mean, w_out, n2g, rw, rb, n_lat_tiles):
    B, S, D = xs.shape
    tm = TOK_TILE
    rwh, rwl = _split(rw)
    tok = lambda w: pl.BlockSpec((None, tm, w), lambda b, t: (b, t, 0))
    return pl.pallas_call(
        _mix_kernel,
        grid=(B, S // tm),
        in_specs=[tok(D),
                  pl.BlockSpec((None, 6, D), lambda b, t: (jnp.where(t >= n_lat_tiles, 2, b), 0, 0)),
                  tok(256), tok(256), tok(256), tok(256), tok(256), tok(256),
                  _full((256, 256)), _full((1, D)), _full((256, 256)), _full((D, D)), _full((1, D)),
                  _full((D, LANES)), _full((D, LANES)), _full((1, LANES))],
        out_specs=[tok(D), tok(D // 2), tok(LANES), tok(LANES), tok(LANES), _full((1, LANES))],
        out_shape=[jax.ShapeDtypeStruct((B, S, D), F32), jax.ShapeDtypeStruct((B, S, D // 2), jnp.uint32),
                   jax.ShapeDtypeStruct((B, S, LANES), jnp.int32), jax.ShapeDtypeStruct((B, S, LANES), F32),
                   jax.ShapeDtypeStruct((B, S, LANES), jnp.int32), jax.ShapeDtypeStruct((1, LANES), F32)],
        scratch_shapes=[pltpu.VMEM((1, LANES), F32)],
        compiler_params=_params("arbitrary", "arbitrary"),
        name="mix_out_router",
    )(xs, mod, a, hy, o_f, o_b, og, fy, fnet_w, mix_g, head_mean.astype(BF16), w_out, n2g, rwh, rwl, rb)


def _route(top_i, rank, counts):
    T = top_i.shape[0]
    bm = MOE_BM
    padded = (counts + bm - 1) // bm * bm
    pend = jnp.cumsum(padded)
    pstart = pend - padded
    pos = (pstart[top_i] + rank).astype(jnp.int32)
    n_blocks = (T * TOP_K + bm - 1) // bm + N_EXPERTS
    blk_row0 = jnp.arange(n_blocks, dtype=pend.dtype) * bm
    blk_e = jnp.minimum(jnp.sum(pend[None, :] <= blk_row0[:, None], axis=1), N_EXPERTS - 1).astype(jnp.int32)
    n_used = (pend[-1] // bm).astype(jnp.int32).reshape(1)
    return pos, blk_e, n_used, n_blocks * bm


def _dispatch_kernel(pos_ref, h_ref, zero_ref, xg_ref, sem):
    del zero_ref
    tm = h_ref.shape[0]

    def copy(t, kk):
        return pltpu.make_async_copy(h_ref.at[pl.ds(t, 1)], xg_ref.at[pl.ds(pos_ref[0, t * TOP_K + kk], 1)], sem)

    def start(t, c):
        for kk in range(TOP_K):
            copy(t, kk).start()
        return c

    def wait(t, c):
        for kk in range(TOP_K):
            copy(t, kk).wait()
        return c

    lax.fori_loop(0, tm, start, 0, unroll=8)
    lax.fori_loop(0, tm, wait, 0, unroll=8)


def _dispatch(h2w, pos, n_rows):
    T, W = h2w.shape
    tm = TOK_TILE
    nb = T // tm
    return pl.pallas_call(
        _dispatch_kernel,
        grid=(nb,),
        in_specs=[pl.BlockSpec((None, 1, tm * TOP_K), lambda i: (i, 0, 0), memory_space=pltpu.SMEM),
                  pl.BlockSpec((tm, W), lambda i: (i, 0)),
                  pl.BlockSpec(memory_space=pl.ANY)],
        out_specs=pl.BlockSpec(memory_space=pl.ANY),
        out_shape=jax.ShapeDtypeStruct((n_rows, W), h2w.dtype),
        scratch_shapes=[pltpu.SemaphoreType.DMA(())],
        input_output_aliases={2: 0},
        compiler_params=_params("arbitrary"),
        name="moe_dispatch",
    )(pos.reshape(nb, 1, tm * TOP_K), h2w, jnp.zeros((n_rows, W), h2w.dtype))


def _gu_prep_kernel(w_ref, p_ref, g_ref, u_ref):
    y = jnp.dot(w_ref[...].astype(BF16), p_ref[...], preferred_element_type=F32)
    half = y.shape[1] // 2
    g_ref[...] = y[:, :half].astype(BF16)
    u_ref[...] = y[:, half:].astype(BF16)


def _gu_prep(w_gu):
    E, D, F2 = w_gu.shape
    tn = 512
    perm = np.zeros((tn, tn), np.float32)
    perm[2 * np.arange(tn // 2), np.arange(tn // 2)] = 1.0
    perm[2 * np.arange(tn // 2) + 1, tn // 2 + np.arange(tn // 2)] = 1.0
    out = jax.ShapeDtypeStruct((E, D, F2 // 2), BF16)
    return pl.pallas_call(
        _gu_prep_kernel,
        grid=(E, F2 // tn),
        in_specs=[pl.BlockSpec((None, D, tn), lambda e, j: (e, 0, j)), _full((tn, tn))],
        out_specs=[pl.BlockSpec((None, D, tn // 2), lambda e, j: (e, 0, j))] * 2,
        out_shape=[out, out],
        compiler_params=_params("arbitrary", "arbitrary"),
        name="moe_weight_prep",
    )(w_gu, jnp.asarray(perm, BF16))


def _expert_kernel(be_ref, nu_ref, x_ref, wg_ref, wu_ref, bg_ref, bu_ref, wd_ref, bd_ref, o_ref):
    i = pl.program_id(0)

    @pl.when(i < nu_ref[0])
    def _():
        xw = x_ref[...]
        x = jnp.concatenate([pltpu.bitcast(xw << 16, F32), pltpu.bitcast(xw & jnp.uint32(0xFFFF0000), F32)],
                            axis=1).astype(BF16)
        gate = jnp.minimum(jnp.dot(x, wg_ref[...], preferred_element_type=F32) + bg_ref[...], SWIGLU_LIMIT)
        up = jnp.clip(jnp.dot(x, wu_ref[...], preferred_element_type=F32) + bu_ref[...], -SWIGLU_LIMIT, SWIGLU_LIMIT)
        glu = gate / (1.0 + jnp.exp(-gate * SWIGLU_ALPHA))
        o_ref[...] = _dot((up + 1.0) * glu, wd_ref[...]) + bd_ref[...]

    @pl.when(i >= nu_ref[0])
    def _():
        o_ref[...] = jnp.zeros_like(o_ref)


def _experts(xg, blk_e, n_used, wg, wu, bg, bu, wd, bd):
    n_rows, W = xg.shape
    bm = MOE_BM
    D, F = wg.shape[1], wg.shape[2]
    wsel = lambda r, c: pl.BlockSpec((None, r, c), lambda i, be, nu: (be[i], 0, 0))
    return pl.pallas_call(
        _expert_kernel,
        grid_spec=pltpu.PrefetchScalarGridSpec(
            num_scalar_prefetch=2,
            grid=(n_rows // bm,),
            in_specs=[pl.BlockSpec((bm, W), lambda i, be, nu: (i, 0)),
                      wsel(D, F), wsel(D, F), wsel(1, F), wsel(1, F), wsel(F, D), wsel(1, D)],
            out_specs=pl.BlockSpec((bm, D), lambda i, be, nu: (i, 0)),
        ),
        out_shape=jax.ShapeDtypeStruct((n_rows, D), F32),
        compiler_params=_params("arbitrary"),
        name="moe_experts",
    )(blk_e, n_used, xg, wg, wu, bg, bu, wd, bd)


def _combine_kernel(pos_ref, posn_ref, x_ref, mod_ref, tg_ref, y_ref, o_ref, buf, sem):
    i = pl.program_id(0)
    tm = x_ref.shape[0]
    slot = i % 2

    def copy(p_ref, s, t, kk):
        return pltpu.make_async_copy(y_ref.at[pl.ds(p_ref[0, t * TOP_K + kk], 1)], buf.at[s, kk, pl.ds(t, 1)],
                                     sem.at[s])

    def fetch(p_ref, s):
        def body(t, c):
            for kk in range(TOP_K):
                copy(p_ref, s, t, kk).start()
            return c
        lax.fori_loop(0, tm, body, 0, unroll=8)

    @pl.when(i == 0)
    def _():
        fetch(pos_ref, 0)

    @pl.when(i + 1 < pl.num_programs(0))
    def _():
        fetch(posn_ref, 1 - slot)

    def wait(t, c):
        for kk in range(TOP_K):
            copy(pos_ref, slot, t, kk).wait()
        return c

    lax.fori_loop(0, tm, wait, 0, unroll=8)
    tg = tg_ref[...]
    f = jnp.zeros(x_ref.shape, F32)
    for kk in range(TOP_K):
        f = f + tg[:, kk:kk + 1] * buf[slot, kk]
    o_ref[...] = x_ref[...] + mod_ref[...][5:6] * f


def _combine(xs, mod, tg, yb, pos, n_lat_tiles):
    B, S, D = xs.shape
    tm = TOK_TILE
    nt = S // tm
    nb = B * nt
    pos3 = pos.reshape(nb, 1, tm * TOP_K)
    row = lambda w: pl.BlockSpec((tm, w), lambda i: (i, 0))
    out = pl.pallas_call(
        _combine_kernel,
        grid=(nb,),
        in_specs=[pl.BlockSpec((None, 1, tm * TOP_K), lambda i: (i, 0, 0), memory_space=pltpu.SMEM),
                  pl.BlockSpec((None, 1, tm * TOP_K), lambda i: (jnp.minimum(i + 1, nb - 1), 0, 0),
                               memory_space=pltpu.SMEM),
                  row(D),
                  pl.BlockSpec((None, 6, D), lambda i: (jnp.where(i % nt >= n_lat_tiles, 2, i // nt), 0, 0)),
                  row(LANES),
                  pl.BlockSpec(memory_space=pl.ANY)],
        out_specs=row(D),
        out_shape=jax.ShapeDtypeStruct((B * S, D), F32),
        scratch_shapes=[pltpu.VMEM((2, TOP_K, tm, D), F32), pltpu.SemaphoreType.DMA((2,))],
        compiler_params=_params("arbitrary"),
        name="moe_combine",
    )(pos3, pos3, xs.reshape(B * S, D), mod, tg.reshape(B * S, LANES), yb)
    return out.reshape(B, S, D)


def _final_kernel(x_ref, g_ref, o_ref):
    o_ref[...] = _rms_rows(x_ref[...]) * g_ref[...]


def _final_norm(xs, g, n_lat):
    B, S, D = xs.shape
    tm = 512
    return pl.pallas_call(
        _final_kernel,
        grid=(B, n_lat // tm),
        in_specs=[pl.BlockSpec((None, tm, D), lambda b, t: (b, t, 0)), _full((1, D))],
        out_specs=pl.BlockSpec((None, tm, D), lambda b, t: (b, t, 0)),
        out_shape=jax.ShapeDtypeStruct((B, n_lat, D), F32),
        compiler_params=_params("arbitrary", "arbitrary"),
        name="final_norm",
    )(xs, g.reshape(1, D))


def _prep_w_in(w_in):
    cq, ckv, kr = w_in[:, 0:256], w_in[:, 256:384], w_in[:, 384:416]
    hy, gq, gk = w_in[:, 416:1184], w_in[:, 1184:1312], w_in[:, 1312:1440]
    gv, glr, og, fn = w_in[:, 1440:1696], w_in[:, 1696:1728], w_in[:, 1728:1984], w_in[:, 1984:2240]
    half = MLA_ROPE // 2
    kr_sw = jnp.concatenate([-kr[:, half:], kr[:, :half]], axis=1)
    place = lambda w: jnp.tile(jnp.pad(w, ((0, 0), (MLA_NOPE, LANES - MLA_NOPE - MLA_ROPE))), (1, MLA_HEADS))
    glr_p = jnp.pad(glr, ((0, 0), (0, LANES - 2 * GLA_LR)))
    return jnp.concatenate([cq, ckv, place(kr), place(kr_sw), hy, gq, gk, gv, glr_p, og, fn], axis=1).astype(BF16)


def _prep_wq(w_uq):
    w = w_uq.reshape(MLA_Q_LORA, MLA_HEADS, MLA_NOPE + MLA_ROPE)
    nope, rope = w[..., :MLA_NOPE], w[..., MLA_NOPE:]
    half = MLA_ROPE // 2
    z_tail = jnp.zeros((MLA_Q_LORA, MLA_HEADS, LANES - MLA_NOPE - MLA_ROPE), F32)
    z_nope = jnp.zeros((MLA_Q_LORA, MLA_HEADS, MLA_NOPE), F32)
    plain = jnp.concatenate([nope, rope, z_tail], axis=-1).reshape(MLA_Q_LORA, MLA_HEADS * LANES)
    partner = jnp.concatenate([z_nope, -rope[..., half:], rope[..., :half], z_tail], axis=-1)
    return jnp.concatenate([plain, partner.reshape(MLA_Q_LORA, MLA_HEADS * LANES)], axis=1).astype(BF16)


def _prep_wkv(w_ukv):
    w = w_ukv.reshape(MLA_KV_LORA, MLA_HEADS, MLA_NOPE + MLA_V)
    k_nope, v = w[..., :MLA_NOPE], w[..., MLA_NOPE:]
    k_placed = jnp.pad(k_nope, ((0, 0), (0, 0), (0, LANES - MLA_NOPE))).reshape(MLA_KV_LORA, MLA_HEADS * LANES)
    v_t = jnp.pad(v, ((0, 0), (0, 0), (0, LANES - MLA_V))).reshape(MLA_KV_LORA, MLA_HEADS * LANES).T
    return k_placed.astype(BF16), v_t.astype(BF16)


def _prep_gk(gk_w, gk_b):
    w = jnp.zeros((LANES, 2 * LANES), F32)
    w = w.at[0:GLA_LR, 0:LANES].set(gk_w[0]).at[GLA_LR:2 * GLA_LR, LANES:].set(gk_w[1])
    return w.astype(BF16), jnp.concatenate([gk_b[0], gk_b[1]]).reshape(1, 2 * LANES)


def _rope_tables(n_lat, n_ctx):
    rows = n_lat // GRID_W
    row = jnp.repeat(jnp.arange(rows, dtype=F32), GRID_W)
    col = jnp.tile(jnp.arange(GRID_W, dtype=F32), rows)
    n_freq = MLA_ROPE // 4
    inv = ROPE_THETA ** (-jnp.arange(n_freq, dtype=F32) / n_freq)
    ang = jnp.concatenate([row[:, None] * inv, col[:, None] * inv], axis=-1)
    cos = jnp.concatenate([jnp.cos(ang), jnp.ones((n_ctx, MLA_ROPE // 2), F32)], axis=0)
    sin = jnp.concatenate([jnp.sin(ang), jnp.zeros((n_ctx, MLA_ROPE // 2), F32)], axis=0)
    S = n_lat + n_ctx
    ones, zeros = jnp.ones((S, MLA_NOPE), F32), jnp.zeros((S, MLA_NOPE), F32)
    tail = jnp.zeros((S, LANES - MLA_NOPE - MLA_ROPE), F32)
    q_scale = MLA_SCALE * math.log2(math.e)
    cq = jnp.concatenate([ones, cos, cos, tail], axis=1) * q_scale
    sq = jnp.concatenate([zeros, sin, sin, tail], axis=1) * q_scale
    ck = jnp.concatenate([zeros, cos, cos, tail], axis=1)
    sk = jnp.concatenate([zeros, sin, sin, tail], axis=1)
    return cq, sq, ck, sk


def kernel(x, c, ctx, c_ctx, ada_w, ada_b, norm1_g, norm2_g, w_in, mla_q_g, mla_w_uq, mla_kv_g, mla_w_ukv, hy_conv_w, hy_conv_b, hy_w1, hy_b1, hy_w2, hy_b2, hy_w3, hy_freq, hy_bias, gla_gk_w, gla_gk_b, fnet_w, mix_g, w_out, router_w, router_b, moe_w_gu, moe_b_gu, moe_w_down, moe_b_down, final_g):
    B, L, D = x.shape
    Lc = ctx.shape[1]
    depth = ada_w.shape[0]
    S = L + Lc
    tm = TOK_TILE
    nlt = L // tm
    assert B == 2 and D == D_MODEL and Lc == tm and L % (2 * DFT_N2 * 8) == 0

    xs = jnp.concatenate([x, ctx], axis=1)
    cc = jnp.zeros((8, D), F32).at[0:B].set(c).at[B].set(c_ctx)
    mods = _mods(cc, ada_w, ada_b)
    tabs = _rope_tables(L, Lc)
    jc = np.arange(GROUP_W)
    ang_c = 2.0 * np.pi * ((jc[:, None] * jc[None, :]) % GROUP_W) / GROUP_W
    csc = jnp.asarray(np.concatenate([np.cos(ang_c), -np.sin(ang_c)], axis=1), F32)
    head_mean = jnp.asarray(np.kron(np.eye(GROUP_W // MLA_V), np.full((MLA_V, MLA_V), 1.0 / MLA_V)), F32)

    for i in range(depth):
        mod = mods[i].reshape(8, 6, D)
        gkw, gkb = _prep_gk(gla_gk_w[i], gla_gk_b[i])
        q, k, v, hyz, gqk, gv, gg, og, fre, fim = _proj(
            xs, mod, norm1_g[i].reshape(1, D), _prep_w_in(w_in[i]), mla_q_g[i].reshape(1, -1), _prep_wq(mla_w_uq[i]),
            mla_kv_g[i].reshape(1, -1), *_prep_wkv(mla_w_ukv[i]), gkw, gkb, csc, tabs, nlt)

        a = _attention(q, k, v, L, Lc)

        filt = (hy_w1[i], hy_b1[i], hy_w2[i], hy_b2[i], hy_w3[i], hy_freq[i])
        vg_l, x0_l = _hy_pre(hyz, hy_conv_w[i], hy_conv_b[i], 0, L)
        hy = _hyena_long(vg_l, x0_l, *_hyena_filter(L, *filt), hy_bias[i], S)
        vg_c, x0_c = _hy_pre(hyz, hy_conv_w[i], hy_conv_b[i], L, Lc)
        hy = _hyena_ctx(vg_c, x0_c, *_hyena_filter(Lc, *filt), hy_bias[i], hy, L)

        o_f, o_b = _gla(gqk, gv, gg, L)

        fy = _fnet_ctx(fre, fim, L, Lc, _fnet_long(fre, fim, L))

        rw = jnp.pad(router_w[i], ((0, 0), (0, LANES - N_EXPERTS)))
        rb = jnp.concatenate([router_b[i], jnp.full((LANES - N_EXPERTS,), -1e30, F32)]).reshape(1, LANES)
        xs, h2w, top_i, top_g, rank, cnt = _mix(
            xs, mod, a, hy, o_f, o_b, og, fy, fnet_w[i].astype(BF16), mix_g[i].reshape(1, D), head_mean,
            w_out[i].astype(BF16), norm2_g[i].reshape(1, D), rw, rb, nlt)

        pos, blk_e, n_used, n_rows = _route(top_i.reshape(B * S, LANES)[:, :TOP_K],
                                            rank.reshape(B * S, LANES)[:, :TOP_K],
                                            cnt[0, :N_EXPERTS].astype(jnp.int32))
        xg = _dispatch(h2w.reshape(B * S, D // 2), pos, n_rows)
        wg, wu = _gu_prep(moe_w_gu[i])
        yb = _experts(xg, blk_e, n_used, wg, wu, moe_b_gu[i][:, None, 0::2], moe_b_gu[i][:, None, 1::2],
                      moe_w_down[i].astype(BF16), moe_b_down[i][:, None, :])
        xs = _combine(xs, mod, top_g, yb, pos, nlt)

    return _final_norm(xs, final_g, L)
```

```python
import functools
import math

import numpy as np
import jax
import jax.numpy as jnp
from jax import lax
from jax.experimental import pallas as pl
from jax.experimental.pallas import tpu as pltpu

F32 = jnp.float32
BF16 = jnp.bfloat16
HIGHEST = lax.Precision.HIGHEST

EPS = 1e-6
D_MODEL = 1024
GROUP_W = 256
MLA_HEADS = 4
MLA_NOPE = 64
MLA_ROPE = 32
MLA_V = 64
MLA_Q_LORA = 256
MLA_KV_LORA = 128
MLA_SCALE = (MLA_NOPE + MLA_ROPE) ** -0.5
ROPE_THETA = 10000.0
GRID_W = 64
HY_EMB = 33
HY_FFN = 64
HY_TARGET = 1e-2
HY_FAST_PCT = 0.3
HY_SLOW_PCT = 1.5
GLA_HEADS = 4
GLA_DK = 32
GLA_DV = 64
GLA_LR = 16
GLA_TAU = 16.0
GLA_CHUNK = 64
N_EXPERTS = 32
TOP_K = 4
SWIGLU_ALPHA = 1.702
SWIGLU_LIMIT = 7.0

LANES = 128
TOK_TILE = 256
DFT_N2 = 128
MOE_BM = 512
VMEM_LIMIT = 56 * 1024 * 1024

_O_CQ, _O_CKV, _O_KRP, _O_KRS, _O_HY, _O_GQ, _O_GK, _O_GV, _O_GLR, _O_OG, _O_FN, _W_ALL = (
    0, 256, 384, 896, 1408, 2176, 2304, 2432, 2688, 2816, 3072, 3328)


def _dot(a, b):
    return jnp.dot(a.astype(BF16), b.astype(BF16), preferred_element_type=F32)


def _dot_hi(a, b):
    return jnp.dot(a, b, precision=HIGHEST, preferred_element_type=F32)


def _split(a):
    hi = a.astype(BF16)
    return hi, (a - hi.astype(F32)).astype(BF16)


def _dot_x3(a, b_hi, b_lo):
    a_hi, a_lo = _split(a)
    mm = lambda u, w: jnp.dot(u, w, preferred_element_type=F32)
    return mm(a_hi, b_hi) + (mm(a_lo, b_hi) + mm(a_hi, b_lo))


def _params(*sem):
    return pltpu.CompilerParams(dimension_semantics=sem, vmem_limit_bytes=VMEM_LIMIT)


def _rms_rows(x):
    return x * lax.rsqrt(jnp.mean(x * x, axis=-1, keepdims=True) + EPS)


def _full(shape):
    n = len(shape)
    return pl.BlockSpec(shape, lambda *_: (0,) * n)


def _mods_kernel(c_ref, w_ref, b_ref, o_ref):
    c = c_ref[...]
    s = c / (1.0 + jnp.exp(-c))
    o_ref[...] = _dot_hi(s, w_ref[...]) + b_ref[...]


def _mods(cc, ada_w, ada_b):
    depth, d, n = ada_w.shape
    tn = 1024
    return pl.pallas_call(
        _mods_kernel,
        grid=(depth, n // tn),
        in_specs=[pl.BlockSpec((8, d), lambda i, j: (0, 0)),
                  pl.BlockSpec((None, d, tn), lambda i, j: (i, 0, j)),
                  pl.BlockSpec((None, 1, tn), lambda i, j: (i, 0, j))],
        out_specs=pl.BlockSpec((None, 8, tn), lambda i, j: (i, 0, j)),
        out_shape=jax.ShapeDtypeStruct((depth, 8, n), F32),
        compiler_params=_params("arbitrary", "arbitrary"),
        name="ada_mods",
    )(cc, ada_w, ada_b.reshape(depth, 1, n))


def _proj_kernel(x_ref, mod_ref, g_ref, win_ref, qg_ref, wq_ref, kvg_ref, wkv_ref, wvt_ref, gkw_ref, gkb_ref,
                 csh_ref, csl_ref, cq_ref, sq_ref, ck_ref, sk_ref,
                 q_out, k_out, v_out, hy_out, gqk_out, gv_out, gg_out, og_out, fre_out, fim_out):
    x = x_ref[...]
    mod = mod_ref[...]
    h = _rms_rows(x) * g_ref[...] * (1.0 + mod[1:2]) + mod[0:1]
    z = _dot(h, win_ref[...])

    nq = _rms_rows(z[:, _O_CQ:_O_CKV]) * qg_ref[...]
    qq = _dot(nq, wq_ref[...])
    nkv = _rms_rows(z[:, _O_CKV:_O_KRP]) * kvg_ref[...]
    nkv = nkv.astype(BF16)
    kvu = jnp.dot(nkv, wkv_ref[...], preferred_element_type=F32)
    cq, sq, ck, sk = cq_ref[...], sq_ref[...], ck_ref[...], sk_ref[...]
    for hd in range(MLA_HEADS):
        a, b = hd * LANES, (hd + 1) * LANES
        q_out[:, a:b] = (qq[:, a:b] * cq + qq[:, 512 + a:512 + b] * sq).astype(BF16)
        k_out[:, a:b] = (kvu[:, a:b] + z[:, _O_KRP + a:_O_KRP + b] * ck
                         + z[:, _O_KRS + a:_O_KRS + b] * sk).astype(BF16)
    vt = lax.dot_general(wvt_ref[...], nkv, (((1,), (1,)), ((), ())), preferred_element_type=F32)
    vrow = lax.broadcasted_iota(jnp.int32, vt.shape, 0)
    v_out[...] = jnp.where(vrow % LANES == MLA_V, 1.0, vt).astype(BF16)

    hy_out[...] = z[:, _O_HY:_O_GQ]
    gqk_out[:, :LANES] = z[:, _O_GQ:_O_GK] * (GLA_DK ** -0.5)
    gqk_out[:, LANES:] = z[:, _O_GK:_O_GV]
    gv_out[...] = z[:, _O_GV:_O_GLR]
    gates = _dot(z[:, _O_GLR:_O_OG], gkw_ref[...]) + gkb_ref[...]
    gg_out[...] = (jnp.minimum(gates, 0.0) - jnp.log(1.0 + jnp.exp(-jnp.abs(gates)))) * (1.0 / GLA_TAU)
    og_out[...] = z[:, _O_OG:_O_FN]
    fcs = _dot_x3(z[:, _O_FN:_W_ALL], csh_ref[...], csl_ref[...])
    fre_out[...] = fcs[:, :GROUP_W]
    fim_out[...] = fcs[:, GROUP_W:]


def _proj(xs, mod, g1, win, qg, wq, kvg, wkv, wvt, gkw, gkb, csc, tabs, n_lat_tiles):
    B, S, D = xs.shape
    tm = TOK_TILE
    csh, csl = _split(csc)
    tok = lambda w: pl.BlockSpec((None, tm, w), lambda b, t: (b, t, 0))
    tab = pl.BlockSpec((tm, LANES), lambda b, t: (t, 0))
    shp = lambda w, dt: jax.ShapeDtypeStruct((B, S, w), dt)
    return pl.pallas_call(
        _proj_kernel,
        grid=(B, S // tm),
        in_specs=[tok(D),
                  pl.BlockSpec((None, 6, D), lambda b, t: (jnp.where(t >= n_lat_tiles, 2, b), 0, 0)),
                  _full((1, D)), _full(win.shape), _full((1, MLA_Q_LORA)), _full(wq.shape),
                  _full((1, MLA_KV_LORA)), _full(wkv.shape), _full(wvt.shape), _full(gkw.shape), _full(gkb.shape),
                  _full(csc.shape), _full(csc.shape), tab, tab, tab, tab],
        out_specs=[tok(512), tok(512), pl.BlockSpec((None, None, MLA_HEADS * LANES, tm), lambda b, t: (b, t, 0, 0)),
                   tok(768), tok(256), tok(256), tok(256), tok(256), tok(256), tok(256)],
        out_shape=[shp(512, BF16), shp(512, BF16), jax.ShapeDtypeStruct((B, S // tm, MLA_HEADS * LANES, tm), BF16),
                   shp(768, F32), shp(256, F32), shp(256, F32),
                   shp(256, F32), shp(256, F32), shp(256, F32), shp(256, F32)],
        compiler_params=_params("arbitrary", "arbitrary"),
        name="in_proj",
    )(xs, mod, g1, win, qg, wq, kvg, wkv, wvt, gkw, gkb, csh, csl, *tabs)


def _attn_kernel(q_ref, k_ref, vt_ref, o_ref, *, n_lat_tiles, tiles_per_chunk, n_lat, n_ctx):
    qi = pl.program_id(1)
    tq = q_ref.shape[0]
    tile = vt_ref.shape[2]
    n_hd = q_ref.shape[1] // LANES

    def heads(tile0, n_chunks, n_tiles):
        sub = LANES

        def body(c, carry):
            t0 = tile0 + c * n_tiles
            off = pl.multiple_of(t0 * tile, tile)
            state = list(carry)
            for j in range(n_tiles * tile // sub):
                for hd in range(n_hd):
                    m, acc = state[hd]
                    q = q_ref[:, hd * LANES:(hd + 1) * LANES]
                    kc = k_ref[pl.ds(off + j * sub, sub), hd * LANES:(hd + 1) * LANES]
                    s = lax.dot_general(kc, q, (((1,), (1,)), ((), ())), preferred_element_type=F32)
                    m_new = jnp.maximum(m, jnp.max(s, axis=0, keepdims=True))
                    alpha = jnp.exp2(m - m_new)
                    p = jnp.exp2(s - m_new).astype(BF16)
                    lo = (j * sub) % tile
                    vt = vt_ref[t0 + (j * sub) // tile, hd * LANES:(hd + 1) * LANES, lo:lo + sub]
                    state[hd] = (m_new, alpha * acc + jnp.dot(vt, p, preferred_element_type=F32))
            return tuple(state)

        one = (jnp.full((1, tq), -1e30, F32), jnp.zeros((LANES, tq), F32))
        res = lax.fori_loop(0, n_chunks, body, (one,) * n_hd)
        for hd in range(n_hd):
            acc_t = res[hd][1].T
            o_ref[:, hd * MLA_V:(hd + 1) * MLA_V] = acc_t[:, :MLA_V] / acc_t[:, MLA_V:MLA_V + 1]

    n_all = (n_lat + n_ctx) // tile

    @pl.when(qi < n_lat_tiles)
    def _():
        heads(0, n_all // tiles_per_chunk, tiles_per_chunk)

    @pl.when(qi >= n_lat_tiles)
    def _():
        heads(n_lat // tile, 1, n_ctx // tile)


def _attention(q, k, vt, n_lat, n_ctx):
    B, S, _ = q.shape
    tq = TOK_TILE
    n_tiles = vt.shape[1]
    tiles_per_chunk = 5 if n_tiles % 5 == 0 else 1
    kern = functools.partial(_attn_kernel, n_lat_tiles=n_lat // tq, tiles_per_chunk=tiles_per_chunk,
                             n_lat=n_lat, n_ctx=n_ctx)
    return pl.pallas_call(
        kern,
        grid=(B, S // tq),
        in_specs=[pl.BlockSpec((None, tq, 512), lambda b, t: (b, t, 0)),
                  pl.BlockSpec((None, S, 512), lambda b, t: (b, 0, 0), pipeline_mode=pl.Buffered(1)),
                  pl.BlockSpec((None, n_tiles, 512, vt.shape[3]), lambda b, t: (b, 0, 0, 0),
                               pipeline_mode=pl.Buffered(1))],
        out_specs=pl.BlockSpec((None, tq, 256), lambda b, t: (b, t, 0)),
        out_shape=jax.ShapeDtypeStruct((B, S, 256), F32),
        compiler_params=_params("arbitrary", "arbitrary"),
        name="mla_attention",
    )(q, k, vt)


def _hy_pre_kernel(z_ref, zp_ref, zn_ref, w_ref, b_ref, vg_ref, x0_ref, *, n_tiles):
    i = pl.program_id(1)
    z = z_ref[...]
    tm = z.shape[0]
    rows = lax.broadcasted_iota(jnp.int32, z.shape, 0)
    prev_row = jnp.where(i == 0, 0.0, zp_ref[7:8, :])
    next_row = jnp.where(i == n_tiles - 1, 0.0, zn_ref[0:1, :])
    z_m = jnp.where(rows == 0, prev_row, pltpu.roll(z, 1, 0))
    z_p = jnp.where(rows == tm - 1, next_row, pltpu.roll(z, tm - 1, 0))
    w = w_ref[...]
    u = z_m * w[0:1] + z * w[1:2] + z_p * w[2:3] + b_ref[...]
    vg_ref[...] = u[:, 2 * GROUP_W:] * u[:, GROUP_W:2 * GROUP_W]
    x0_ref[...] = u[:, :GROUP_W]


def _hy_pre(hyz, conv_w, conv_b, row0, n_rows):
    B, S, W = hyz.shape
    tm = TOK_TILE
    nt = n_rows // tm
    t0, r8, last8 = row0 // tm, row0 // 8, S // 8 - 1
    kern = functools.partial(_hy_pre_kernel, n_tiles=nt)
    out = jax.ShapeDtypeStruct((B, n_rows, GROUP_W), F32)
    return pl.pallas_call(
        kern,
        grid=(B, nt),
        in_specs=[pl.BlockSpec((None, tm, W), lambda b, i: (b, t0 + i, 0)),
                  pl.BlockSpec((None, 8, W), lambda b, i: (b, jnp.maximum(r8 + i * (tm // 8) - 1, 0), 0)),
                  pl.BlockSpec((None, 8, W), lambda b, i: (b, jnp.minimum(r8 + (i + 1) * (tm // 8), last8), 0)),
                  _full((3, W)), _full((1, W))],
        out_specs=[pl.BlockSpec((None, tm, GROUP_W), lambda b, i: (b, i, 0))] * 2,
        out_shape=[out, out],
        compiler_params=_params("arbitrary", "arbitrary"),
        name="hyena_pre",
    )(hyz, hyz, hyz, conv_w, conv_b.reshape(1, W))


def _filter_kernel(fv_ref, w1_ref, b1_ref, w2_ref, b2_ref, w3_ref, fr_ref, dl_ref, h_ref, ss_ref, *, n_pos):
    i = pl.program_id(0)
    tl = h_ref.shape[0]
    row = lax.broadcasted_iota(jnp.int32, (tl, LANES), 0) + i * tl
    pos = jnp.where(row <= n_pos, row, 2 * n_pos - row).astype(F32)
    lane = lax.broadcasted_iota(jnp.int32, (tl, LANES), 1)
    t = pos * (1.0 / (n_pos - 1))
    arg = (pos * (2.0 * math.pi / n_pos)) * fv_ref[...]
    feat = jnp.where(lane == 0, t, jnp.where(lane < 17, jnp.cos(arg), jnp.where(lane < HY_EMB, -jnp.sin(arg), 0.0)))
    fr = fr_ref[...]
    h = jnp.sin(fr * (_dot_hi(feat, w1_ref[...]) + b1_ref[...]))
    h = jnp.sin(fr * (_dot_hi(h, w2_ref[...]) + b2_ref[...]))
    h = _dot_hi(h, w3_ref[...]) * jnp.exp(-t[:, 0:1] * dl_ref[...])
    r1 = row[:, 0:1]
    h = jnp.where(r1 < n_pos, h[:, :GROUP_W], jnp.where(r1 == n_pos, 0.0, h[:, GROUP_W:]))
    h_ref[...] = h

    @pl.when(i == 0)
    def _():
        ss_ref[...] = jnp.zeros_like(ss_ref)

    ss_ref[...] += jnp.sum(h * h, axis=0, keepdims=True)


def _hyena_filter(n_pos, w1, b1, w2, b2, w3, freq):
    tl = min(2 * n_pos, 512)
    bands = (HY_EMB - 1) // 2
    f = np.linspace(1e-4, bands - 1, bands)
    fv = np.zeros((1, LANES), np.float32)
    fv[0, 1:17] = f
    fv[0, 17:33] = f
    w1p = jnp.zeros((LANES, HY_FFN), F32).at[:HY_EMB].set(w1)
    max_decay = math.log(HY_TARGET) / HY_FAST_PCT
    min_decay = math.log(HY_TARGET) / HY_SLOW_PCT
    deltas = np.abs(np.linspace(min_decay, max_decay, GROUP_W)).astype(np.float32)
    dl = jnp.asarray(np.concatenate([deltas, deltas])[None, :])
    kern = functools.partial(_filter_kernel, n_pos=n_pos)
    k2, ss = pl.pallas_call(
        kern,
        grid=(2 * n_pos // tl,),
        in_specs=[_full((1, LANES)), _full((LANES, HY_FFN)), _full((1, HY_FFN)), _full((HY_FFN, HY_FFN)),
                  _full((1, HY_FFN)), _full((HY_FFN, 2 * GROUP_W)), _full((1, HY_FFN)), _full((1, 2 * GROUP_W))],
        out_specs=[pl.BlockSpec((tl, GROUP_W), lambda i: (i, 0)), _full((1, GROUP_W))],
        out_shape=[jax.ShapeDtypeStruct((2 * n_pos, GROUP_W), F32), jax.ShapeDtypeStruct((1, GROUP_W), F32)],
        compiler_params=_params("arbitrary"),
        name="hyena_filter",
    )(jnp.asarray(fv), w1p, b1.reshape(1, -1), w2, b2.reshape(1, -1), w3, freq.reshape(1, -1), dl)
    return k2, lax.rsqrt(ss)


def _dft_mats(n, sign):
    j = np.arange(n)
    ang = 2.0 * np.pi * ((j[:, None] * j[None, :]) % n) / n
    return np.cos(ang), sign * np.sin(ang)


def _twiddle(n1, n2, sign):
    ang = 2.0 * np.pi * ((np.arange(n1)[:, None] * np.arange(n2)[None, :]) % (n1 * n2)) / (n1 * n2)
    return np.cos(ang), sign * np.sin(ang)


def _block_complex(re, im):
    return np.block([[re, -im], [im, re]])


def _stage1_kernel(m_ref, zr_ref, zi_ref, ar_ref, ai_ref):
    a = _dot_hi(m_ref[...], jnp.concatenate([zr_ref[...], zi_ref[...]], axis=0))
    half = ar_ref.shape[0]
    ar_ref[...] = a[:half]
    ai_ref[...] = a[half:]


def _stage1_real_kernel(m_ref, zr_ref, ar_ref, ai_ref):
    a = _dot_hi(m_ref[...], zr_ref[...])
    half = ar_ref.shape[0]
    ar_ref[...] = a[:half]
    ai_ref[...] = a[half:]


def _col_tile(cols):
    return 2048 if cols % 2048 == 0 else cols


def _hy_stage1(vg2, n1):
    _, r, cols = vg2.shape
    cr, ci = _dft_mats(n1, -1.0)
    m = jnp.asarray(_block_complex(cr[:, :r], ci[:, :r]), F32)
    tc = _col_tile(cols)
    out = jax.ShapeDtypeStruct((n1, cols), F32)
    return pl.pallas_call(
        _stage1_kernel,
        grid=(cols // tc,),
        in_specs=[_full(m.shape), pl.BlockSpec((None, r, tc), lambda j: (0, 0, j)),
                  pl.BlockSpec((None, r, tc), lambda j: (1, 0, j))],
        out_specs=[pl.BlockSpec((n1, tc), lambda j: (0, j))] * 2,
        out_shape=[out, out],
        compiler_params=_params("arbitrary"),
        name="hyena_fwd_stage1",
    )(m, vg2, vg2)


def _filter_stage1(k2v, n1):
    _, cols = k2v.shape
    cr, ci = _dft_mats(n1, -1.0)
    m = jnp.asarray(np.concatenate([cr, ci], axis=0), F32)
    tc = _col_tile(cols)
    out = jax.ShapeDtypeStruct((n1, cols), F32)
    return pl.pallas_call(
        _stage1_real_kernel,
        grid=(cols // tc,),
        in_specs=[_full(m.shape), pl.BlockSpec((n1, tc), lambda j: (0, j))],
        out_specs=[pl.BlockSpec((n1, tc), lambda j: (0, j))] * 2,
        out_shape=[out, out],
        compiler_params=_params("arbitrary"),
        name="hyena_filter_stage1",
    )(m, k2v)


def _twiddled(f_re, f_im, t_re, t_im):
    g_re = f_re * t_re - f_im * t_im
    g_im = f_re * t_im + f_im * t_re
    return jnp.concatenate([jnp.concatenate([g_re, -g_im], axis=1), jnp.concatenate([g_im, g_re], axis=1)], axis=0)


def _filter_stage2_kernel(ar_ref, ai_ref, twr_ref, twi_ref, fr_ref, fi_ref, sc_ref, kr_ref, ki_ref):
    g = _twiddled(fr_ref[...], fi_ref[...], twr_ref[...], twi_ref[...])
    x = _dot_hi(g, jnp.concatenate([ar_ref[...], ai_ref[...]], axis=0)) * sc_ref[...]
    n2 = kr_ref.shape[0]
    kr_ref[...] = x[:n2]
    ki_ref[...] = x[n2:]


def _filter_stage2(ar, ai, inv_norm, n1):
    n2 = DFT_N2
    c = ar.shape[1] // n2
    fr, fi = _dft_mats(n2, -1.0)
    twr, twi = _twiddle(n1, n2, -1.0)
    blk = pl.BlockSpec((None, n2, c), lambda k: (k, 0, 0))
    tw = pl.BlockSpec((None, 1, n2), lambda k: (k, 0, 0))
    out = jax.ShapeDtypeStruct((n1, n2, c), F32)
    return pl.pallas_call(
        _filter_stage2_kernel,
        grid=(n1,),
        in_specs=[blk, blk, tw, tw, _full((n2, n2)), _full((n2, n2)), _full((1, c))],
        out_specs=[blk, blk],
        out_shape=[out, out],
        compiler_params=_params("arbitrary"),
        name="hyena_filter_stage2",
    )(ar.reshape(n1, n2, c), ai.reshape(n1, n2, c), jnp.asarray(twr.reshape(n1, 1, n2), F32),
      jnp.asarray(twi.reshape(n1, 1, n2), F32), jnp.asarray(fr, F32), jnp.asarray(fi, F32), inv_norm)


def _hy_stage2_kernel(ar_ref, ai_ref, kr_ref, ki_ref, twr_ref, twi_ref, tcr_ref, tci_ref, fr_ref, fi_ref,
                      br_ref, bi_ref):
    f_re, f_im = fr_ref[...], fi_ref[...]
    n2 = f_re.shape[0]
    g = _twiddled(f_re, f_im, twr_ref[...], twi_ref[...])
    x = _dot_hi(g, jnp.concatenate([ar_ref[...], ai_ref[...]], axis=0))
    x_re, x_im = x[:n2], x[n2:]
    k_re, k_im = kr_ref[...], ki_ref[...]
    y = jnp.concatenate([x_re * k_re - x_im * k_im, x_re * k_im + x_im * k_re], axis=0)
    g_inv = _twiddled(f_re, -f_im, tcr_ref[...], -tci_ref[...])
    b = _dot_hi(g_inv, y)
    br_ref[...] = b[:n2]
    bi_ref[...] = b[n2:]


def _hy_stage2(ar, ai, kr, ki, n1):
    n2 = DFT_N2
    c = kr.shape[2]
    fr, fi = _dft_mats(n2, -1.0)
    twr, twi = _twiddle(n1, n2, -1.0)
    blk = pl.BlockSpec((None, n2, c), lambda k: (k, 0, 0))
    tw = pl.BlockSpec((None, 1, n2), lambda k: (k, 0, 0))
    twc = pl.BlockSpec((None, n2, 1), lambda k: (k, 0, 0))
    out = jax.ShapeDtypeStruct((n1, n2, c), F32)
    return pl.pallas_call(
        _hy_stage2_kernel,
        grid=(n1,),
        in_specs=[blk, blk, blk, blk, tw, tw, twc, twc, _full((n2, n2)), _full((n2, n2))],
        out_specs=[blk, blk],
        out_shape=[out, out],
        compiler_params=_params("arbitrary"),
        name="hyena_conv_stage2",
    )(ar.reshape(n1, n2, c), ai.reshape(n1, n2, c), kr, ki,
      jnp.asarray(twr.reshape(n1, 1, n2), F32), jnp.asarray(twi.reshape(n1, 1, n2), F32),
      jnp.asarray(twr.reshape(n1, n2, 1), F32), jnp.asarray(twi.reshape(n1, n2, 1), F32),
      jnp.asarray(fr, F32), jnp.asarray(fi, F32))


def _hy_stage3_kernel(m_ref, br_ref, bi_ref, vg0_ref, vg1_ref, x00_ref, x01_ref, bias_ref, o_ref):
    conv = _dot_hi(m_ref[...], jnp.concatenate([br_ref[...], bi_ref[...]], axis=0))
    r = vg0_ref.shape[0]
    bias = bias_ref[...]
    o_ref[0, :r] = (conv[:r] + vg0_ref[...] * bias) * x00_ref[...]
    o_ref[1, :r] = (conv[r:] + vg1_ref[...] * bias) * x01_ref[...]
    if o_ref.shape[1] > r:
        o_ref[:, r:] = jnp.zeros((2, o_ref.shape[1] - r, o_ref.shape[2]), F32)


def _hy_stage3(br, bi, vg2, x02, bias_cols, n1, rows_total):
    _, r, cols = vg2.shape
    cr, ci = _dft_mats(n1, 1.0)
    m = jnp.asarray(_block_complex(cr[:r], ci[:r]) / (n1 * DFT_N2), F32)
    tc = _col_tile(cols)
    plane = lambda p: pl.BlockSpec((None, r, tc), lambda j: (p, 0, j))
    return pl.pallas_call(
        _hy_stage3_kernel,
        grid=(cols // tc,),
        in_specs=[_full(m.shape), pl.BlockSpec((n1, tc), lambda j: (0, j)), pl.BlockSpec((n1, tc), lambda j: (0, j)),
                  plane(0), plane(1), plane(0), plane(1), pl.BlockSpec((1, tc), lambda j: (0, j))],
        out_specs=pl.BlockSpec((2, rows_total, tc), lambda j: (0, 0, j)),
        out_shape=jax.ShapeDtypeStruct((2, rows_total, cols), F32),
        compiler_params=_params("arbitrary"),
        name="hyena_inv_stage3",
    )(m, br, bi, vg2, vg2, x02, x02, bias_cols)


def _hyena_long(vg, x0, k2, inv_norm, bias, n_all):
    B, L, C = vg.shape
    assert B == 2
    n2 = DFT_N2
    n1 = 2 * L // n2
    r = L // n2
    fa_r, fa_i = _filter_stage1(k2.reshape(n1, n2 * C), n1)
    kr, ki = _filter_stage2(fa_r, fa_i, inv_norm, n1)
    vg2 = vg.reshape(2, r, n2 * C)
    ar, ai = _hy_stage1(vg2, n1)
    br, bi = _hy_stage2(ar, ai, kr, ki, n1)
    y = _hy_stage3(br.reshape(n1, n2 * C), bi.reshape(n1, n2 * C), vg2, x0.reshape(2, r, n2 * C),
                   jnp.tile(bias.reshape(1, C), (1, n2)), n1, n_all // n2)
    return y.reshape(2, n_all, C)


def _hy_ctx_kernel(vg_ref, x0_ref, kext_ref, sc_ref, bias_ref, buf_ref, o_ref):
    del buf_ref
    n = vg_ref.shape[0]

    def body(a, acc):
        src = vg_ref[pl.ds(pl.multiple_of(8 * a, 8), 8), :]
        win = pl.multiple_of(n - 8 * a, 8)
        for r in range(8):
            acc = acc + kext_ref[r, pl.ds(win, n), :] * src[r:r + 1]
        return acc

    acc = lax.fori_loop(0, n // 8, body, jnp.zeros(vg_ref.shape, F32))
    o_ref[...] = (acc * sc_ref[...] + vg_ref[...] * bias_ref[...]) * x0_ref[...]


def _hyena_ctx(vg, x0, k2, inv_norm, bias, buf, row0):
    B, n, C = vg.shape
    kext = jnp.concatenate([k2[n:], k2[:n]], axis=0)
    kext = jnp.stack([jnp.roll(kext, r, axis=0) for r in range(8)])
    blk = pl.BlockSpec((None, n, C), lambda b: (b, 0, 0))
    return pl.pallas_call(
        _hy_ctx_kernel,
        grid=(B,),
        in_specs=[blk, blk, _full((8, 2 * n, C)), _full((1, C)), _full((1, C)), pl.BlockSpec(memory_space=pl.ANY)],
        out_specs=pl.BlockSpec((None, n, C), lambda b: (b, row0 // n, 0)),
        out_shape=jax.ShapeDtypeStruct(buf.shape, F32),
        input_output_aliases={5: 0},
        compiler_params=_params("arbitrary"),
        name="hyena_ctx",
    )(vg, x0, kext, inv_norm, bias.reshape(1, C), buf)


def _fn_stage1_kernel(c_ref, s_ref, re_ref, im_ref, ar_ref, ai_ref):
    c, s, re, im = c_ref[...], s_ref[...], re_ref[...], im_ref[...]
    ar_ref[...] = _dot_hi(c, re) + _dot_hi(s, im)
    ai_ref[...] = _dot_hi(c, im) - _dot_hi(s, re)


def _fn_stage2_kernel(ar_ref, ai_ref, twr_ref, twi_ref, fr_ref, fi_ref, o_ref):
    f_re, f_im, t_re, t_im = fr_ref[...], fi_ref[...], twr_ref[...], twi_ref[...]
    g_re = f_re * t_re - f_im * t_im
    g_im = f_re * t_im + f_im * t_re
    n2 = f_re.shape[0]
    o_ref[:n2] = _dot_hi(g_re, ar_ref[...]) - _dot_hi(g_im, ai_ref[...])
    if o_ref.shape[0] > n2:
        o_ref[n2:] = jnp.zeros((o_ref.shape[0] - n2, o_ref.shape[1]), F32)


def _fnet_long(fre, fim, n_lat):
    B, S, C = fre.shape
    n2 = DFT_N2
    n1 = n_lat // n2
    cols = n2 * C
    c1, s1 = _dft_mats(n1, 1.0)
    tc = _col_tile(cols)
    rows = pl.BlockSpec((None, n1, tc), lambda b, j: (b, 0, j))
    a_shape = jax.ShapeDtypeStruct((B, n1, cols), F32)
    ar, ai = pl.pallas_call(
        _fn_stage1_kernel,
        grid=(B, cols // tc),
        in_specs=[_full((n1, n1)), _full((n1, n1)), rows, rows],
        out_specs=[rows, rows],
        out_shape=[a_shape, a_shape],
        compiler_params=_params("arbitrary", "arbitrary"),
        name="fnet_stage1",
    )(jnp.asarray(c1, F32), jnp.asarray(s1, F32), fre.reshape(B, S // n2, cols), fim.reshape(B, S // n2, cols))
    fr, fi = _dft_mats(n2, -1.0)
    scale = 1.0 / math.sqrt(n_lat * C)
    twr, twi = _twiddle(n1, n2, -1.0)
    blk = pl.BlockSpec((None, None, n2, C), lambda b, k: (b, k, 0, 0))
    tw = pl.BlockSpec((None, 1, n2), lambda b, k: (k, 0, 0))
    y = pl.pallas_call(
        _fn_stage2_kernel,
        grid=(B, n1),
        in_specs=[blk, blk, tw, tw, _full((n2, n2)), _full((n2, n2))],
        out_specs=pl.BlockSpec((None, S // n1, C), lambda b, k: (b, 0, k)),
        out_shape=jax.ShapeDtypeStruct((B, S // n1, n1 * C), F32),
        compiler_params=_params("arbitrary", "arbitrary"),
        name="fnet_stage2",
    )(ar.reshape(B, n1, n2, C), ai.reshape(B, n1, n2, C), jnp.asarray(twr.reshape(n1, 1, n2), F32),
      jnp.asarray(twi.reshape(n1, 1, n2), F32), jnp.asarray(fr * scale, F32), jnp.asarray(fi * scale, F32))
    return y.reshape(B, S, C)


def _fn_ctx_kernel(c_ref, s_ref, re_ref, im_ref, buf_ref, o_ref):
    del buf_ref
    o_ref[...] = _dot_hi(c_ref[...], re_ref[...]) + _dot_hi(s_ref[...], im_ref[...])


def _fnet_ctx(fre, fim, n_lat, n_ctx, buf):
    B, S, C = fre.shape
    c1, s1 = _dft_mats(n_ctx, 1.0)
    scale = 1.0 / math.sqrt(n_ctx * C)
    blk = pl.BlockSpec((None, n_ctx, C), lambda b: (b, n_lat // n_ctx, 0))
    return pl.pallas_call(
        _fn_ctx_kernel,
        grid=(B,),
        in_specs=[_full((n_ctx, n_ctx)), _full((n_ctx, n_ctx)), blk, blk, pl.BlockSpec(memory_space=pl.ANY)],
        out_specs=blk,
        out_shape=jax.ShapeDtypeStruct((B, S, C), F32),
        input_output_aliases={4: 0},
        compiler_params=_params("arbitrary"),
        name="fnet_ctx",
    )(jnp.asarray(c1 * scale, F32), jnp.asarray(s1 * scale, F32), fre, fim, buf)


def _gla_kernel(qkf_ref, vf_ref, gf_ref, qkb_ref, vb_ref, gb_ref, of_ref, ob_ref, st_ref):
    n = pl.program_id(1)
    tm = qkf_ref.shape[0]
    ck = GLA_CHUNK

    @pl.when(n == 0)
    def _():
        st_ref[...] = jnp.zeros_like(st_ref)

    ri = lax.broadcasted_iota(jnp.int32, (ck, ck), 0)
    ci = lax.broadcasted_iota(jnp.int32, (ck, ck), 1)

    def sub_chunk(qk_ref, v_ref, g_ref, o_ref, states, r0, reverse):
        keep = (ci >= ri) if reverse else (ci <= ri)
        g = g_ref[r0:r0 + ck, :]
        b = _dot_hi(keep.astype(F32), g)
        b_end = b[0:1] if reverse else b[ck - 1:ck]
        q = qk_ref[r0:r0 + ck, :LANES] * jnp.exp(b)
        k = qk_ref[r0:r0 + ck, LANES:]
        k_in = k * jnp.exp(-b)
        k_out = k * jnp.exp(b_end - b)
        decay = jnp.exp(b_end)
        new_states = []
        for hd in range(GLA_HEADS):
            ks = slice(hd * GLA_DK, (hd + 1) * GLA_DK)
            vs = slice(hd * GLA_DV, (hd + 1) * GLA_DV)
            qh, vh = q[:, ks].astype(BF16), v_ref[r0:r0 + ck, vs].astype(BF16)
            a = lax.dot_general(qh, k_in[:, ks].astype(BF16), (((1,), (1,)), ((), ())), preferred_element_type=F32)
            a = jnp.where(keep, a, 0.0)
            st = states[hd]
            o = jnp.dot(a.astype(BF16), vh, preferred_element_type=F32)
            o += lax.dot_general(qh, st.astype(BF16), (((1,), (1,)), ((), ())), preferred_element_type=F32)
            o_ref[r0:r0 + ck, vs] = o
            kv = lax.dot_general(vh, k_out[:, ks].astype(BF16), (((0,), (0,)), ((), ())), preferred_element_type=F32)
            new_states.append(st * decay[:, ks] + kv)
        return new_states

    st_f = [st_ref[0, hd] for hd in range(GLA_HEADS)]
    st_b = [st_ref[1, hd] for hd in range(GLA_HEADS)]
    n_sub = tm // ck
    for i in range(n_sub):
        st_f = sub_chunk(qkf_ref, vf_ref, gf_ref, of_ref, st_f, i * ck, False)
        st_b = sub_chunk(qkb_ref, vb_ref, gb_ref, ob_ref, st_b, (n_sub - 1 - i) * ck, True)
    for hd in range(GLA_HEADS):
        st_ref[0, hd] = st_f[hd]
        st_ref[1, hd] = st_b[hd]


def _gla(gqk, gv, gg, n_lat):
    B, S, _ = gqk.shape
    tm = TOK_TILE
    nl = n_lat // tm
    nt = S // tm
    assert nt == nl + 1
    fwd = lambda n: jnp.where(n == 0, nl, n - 1)
    bwd = lambda n: jnp.where(n == 0, nl, nl - n)
    out = jax.ShapeDtypeStruct((B, S, 256), F32)
    return pl.pallas_call(
        _gla_kernel,
        grid=(B, nt),
        in_specs=[pl.BlockSpec((None, tm, 256), lambda b, n: (b, fwd(n), 0)),
                  pl.BlockSpec((None, tm, 256), lambda b, n: (b, fwd(n), 0)),
                  pl.BlockSpec((None, tm, 128), lambda b, n: (b, fwd(n), 0)),
                  pl.BlockSpec((None, tm, 256), lambda b, n: (b, bwd(n), 0)),
                  pl.BlockSpec((None, tm, 256), lambda b, n: (b, bwd(n), 0)),
                  pl.BlockSpec((None, tm, 128), lambda b, n: (b, bwd(n), 1))],
        out_specs=[pl.BlockSpec((None, tm, 256), lambda b, n: (b, fwd(n), 0)),
                   pl.BlockSpec((None, tm, 256), lambda b, n: (b, bwd(n), 0))],
        out_shape=[out, out],
        scratch_shapes=[pltpu.VMEM((2, GLA_HEADS, GLA_DV, GLA_DK), F32)],
        compiler_params=_params("arbitrary", "arbitrary"),
        name="gla_scan",
    )(gqk, gv, gg, gqk, gv, gg)


def _mix_kernel(x_ref, mod_ref, a_ref, hy_ref, of_ref, ob_ref, og_ref, fy_ref, fw_ref, mg_ref, hm_ref, wo_ref,
                n2g_ref, rwh_ref, rwl_ref, rb_ref, xo_ref, h2_ref, ti_ref, tg_ref, rk_ref, cnt_ref, run_ref):
    first = jnp.logical_and(pl.program_id(0) == 0, pl.program_id(1) == 0)

    @pl.when(first)
    def _():
        run_ref[...] = jnp.zeros_like(run_ref)

    mod = mod_ref[...]
    mg = mg_ref[...]
    hm = hm_ref[...]

    def head_rms(t):
        sq_hi, sq_lo = _split(t * t)
        ms = jnp.dot(sq_hi, hm, preferred_element_type=F32) + jnp.dot(sq_lo, hm, preferred_element_type=F32)
        return t * lax.rsqrt(ms + EPS)

    a = head_rms(a_ref[...]) * mg[:, 0:256]
    hy = _rms_rows(hy_ref[...]) * mg[:, 256:512]
    og = og_ref[...]
    o = head_rms(of_ref[...] + ob_ref[...]) * mg[:, 512:768] * (og / (1.0 + jnp.exp(-og)))
    fn = _rms_rows(_dot(fy_ref[...], fw_ref[...])) * mg[:, 768:1024]
    wo = wo_ref[...]
    y = _dot(a, wo[0:256]) + _dot(hy, wo[256:512]) + _dot(o, wo[512:768]) + _dot(fn, wo[768:1024])
    x = x_ref[...] + mod[2:3] * y
    xo_ref[...] = x

    h2 = _rms_rows(x) * n2g_ref[...] * (1.0 + mod[4:5]) + mod[3:4]
    bits = pltpu.bitcast(h2.astype(BF16).astype(F32), jnp.uint32)
    half = bits.shape[1] // 2
    h2_ref[...] = (bits[:, :half] >> 16) | (bits[:, half:] & jnp.uint32(0xFFFF0000))

    logits = _dot_x3(h2, rwh_ref[...], rwl_ref[...]) + rb_ref[...]
    tm = logits.shape[0]
    lane = lax.broadcasted_iota(jnp.int32, logits.shape, 1).astype(F32)
    idx_out = jnp.zeros(logits.shape, F32)
    val_out = jnp.zeros(logits.shape, F32)
    chosen = jnp.zeros(logits.shape, F32)
    picks = []
    top = None
    den = jnp.zeros((tm, 1), F32)
    for kk in range(TOP_K):
        m = jnp.max(logits, axis=-1, keepdims=True)
        idx = jnp.min(jnp.where(logits == m, lane, float(LANES)), axis=-1, keepdims=True)
        if top is None:
            top = m
        e = jnp.exp(m - top)
        den = den + e
        hit = lane == idx
        picks.append(hit)
        chosen = jnp.where(hit, 1.0, chosen)
        idx_out = jnp.where(lane == kk, idx, idx_out)
        val_out = jnp.where(lane == kk, e, val_out)
        logits = jnp.where(hit, -jnp.inf, logits)
    ti_ref[...] = idx_out.astype(jnp.int32)
    tg_ref[...] = val_out / den

    ri = lax.broadcasted_iota(jnp.int32, (tm, tm), 0)
    ci = lax.broadcasted_iota(jnp.int32, (tm, tm), 1)
    before = _dot((ci < ri).astype(F32), chosen) + run_ref[...]
    rank = jnp.zeros(logits.shape, F32)
    for kk in range(TOP_K):
        r = jnp.sum(jnp.where(picks[kk], before, 0.0), axis=-1, keepdims=True)
        rank = jnp.where(lane == kk, r, rank)
    rk_ref[...] = rank.astype(jnp.int32)
    run_ref[...] += jnp.sum(chosen, axis=0, keepdims=True)
    cnt_ref[...] = run_ref[...]


def _mix(xs, mod, a, hy, o_f, o_b, og, fy, fnet_w, mix_g, head_mean, w_out, n2g, rw, rb, n_lat_tiles):
    B, S, D = xs.shape
    tm = TOK_TILE
    rwh, rwl = _split(rw)
    tok = lambda w: pl.BlockSpec((None, tm, w), lambda b, t: (b, t, 0))
    return pl.pallas_call(
        _mix_kernel,
        grid=(B, S // tm),
        in_specs=[tok(D),
                  pl.BlockSpec((None, 6, D), lambda b, t: (jnp.where(t >= n_lat_tiles, 2, b), 0, 0)),
                  tok(256), tok(256), tok(256), tok(256), tok(256), tok(256),
                  _full((256, 256)), _full((1, D)), _full((256, 256)), _full((D, D)), _full((1, D)),
                  _full((D, LANES)), _full((D, LANES)), _full((1, LANES))],
        out_specs=[tok(D), tok(D // 2), tok(LANES), tok(LANES), tok(LANES), _full((1, LANES))],
        out_shape=[jax.ShapeDtypeStruct((B, S, D), F32), jax.ShapeDtypeStruct((B, S, D // 2), jnp.uint32),
                   jax.ShapeDtypeStruct((B, S, LANES), jnp.int32), jax.ShapeDtypeStruct((B, S, LANES), F32),
                   jax.ShapeDtypeStruct((B, S, LANES), jnp.int32), jax.ShapeDtypeStruct((1, LANES), F32)],
        scratch_shapes=[pltpu.VMEM((1, LANES), F32)],
        compiler_params=_params("arbitrary", "arbitrary"),
        name="mix_out_router",
    )(xs, mod, a, hy, o_f, o_b, og, fy, fnet_w, mix_g, head_mean.astype(BF16), w_out, n2g, rwh, rwl, rb)


def _route(top_i, rank, counts):
    T = top_i.shape[0]
    bm = MOE_BM
    padded = (counts + bm - 1) // bm * bm
    pend = jnp.cumsum(padded)
    pstart = pend - padded
    pos = (pstart[top_i] + rank).astype(jnp.int32)
    n_blocks = (T * TOP_K + bm - 1) // bm + N_EXPERTS
    blk_row0 = jnp.arange(n_blocks, dtype=pend.dtype) * bm
    blk_e = jnp.minimum(jnp.sum(pend[None, :] <= blk_row0[:, None], axis=1), N_EXPERTS - 1).astype(jnp.int32)
    n_used = (pend[-1] // bm).astype(jnp.int32).reshape(1)
    return pos, blk_e, n_used, n_blocks * bm


def _dispatch_kernel(pos_ref, h_ref, zero_ref, xg_ref, sem):
    del zero_ref
    tm = h_ref.shape[0]

    def copy(t, kk):
        return pltpu.make_async_copy(h_ref.at[pl.ds(t, 1)], xg_ref.at[pl.ds(pos_ref[0, t * TOP_K + kk], 1)], sem)

    def start(t, c):
        for kk in range(TOP_K):
            copy(t, kk).start()
        return c

    def wait(t, c):
        for kk in range(TOP_K):
            copy(t, kk).wait()
        return c

    lax.fori_loop(0, tm, start, 0, unroll=8)
    lax.fori_loop(0, tm, wait, 0, unroll=8)


def _dispatch(h2w, pos, n_rows):
    T, W = h2w.shape
    tm = TOK_TILE
    nb = T // tm
    return pl.pallas_call(
        _dispatch_kernel,
        grid=(nb,),
        in_specs=[pl.BlockSpec((None, 1, tm * TOP_K), lambda i: (i, 0, 0), memory_space=pltpu.SMEM),
                  pl.BlockSpec((tm, W), lambda i: (i, 0)),
                  pl.BlockSpec(memory_space=pl.ANY)],
        out_specs=pl.BlockSpec(memory_space=pl.ANY),
        out_shape=jax.ShapeDtypeStruct((n_rows, W), h2w.dtype),
        scratch_shapes=[pltpu.SemaphoreType.DMA(())],
        input_output_aliases={2: 0},
        compiler_params=_params("arbitrary"),
        name="moe_dispatch",
    )(pos.reshape(nb, 1, tm * TOP_K), h2w, jnp.zeros((n_rows, W), h2w.dtype))


def _gu_prep_kernel(w_ref, p_ref, g_ref, u_ref):
    y = jnp.dot(w_ref[...].astype(BF16), p_ref[...], preferred_element_type=F32)
    half = y.shape[1] // 2
    g_ref[...] = y[:, :half].astype(BF16)
    u_ref[...] = y[:, half:].astype(BF16)


def _cast_kernel(w_ref, o_ref):
    o_ref[...] = w_ref[...].astype(BF16)


def _down_prep(w_down, layer):
    _, E, F, D = w_down.shape
    return pl.pallas_call(
        _cast_kernel,
        grid=(E,),
        in_specs=[pl.BlockSpec((None, None, F, D), lambda e: (layer, e, 0, 0))],
        out_specs=pl.BlockSpec((None, F, D), lambda e: (e, 0, 0)),
        out_shape=jax.ShapeDtypeStruct((E, F, D), BF16),
        compiler_params=_params("arbitrary"),
        name="moe_down_prep",
    )(w_down)


def _gu_prep(w_gu, layer):
    _, E, D, F2 = w_gu.shape
    tn = 512
    perm = np.zeros((tn, tn), np.float32)
    perm[2 * np.arange(tn // 2), np.arange(tn // 2)] = 1.0
    perm[2 * np.arange(tn // 2) + 1, tn // 2 + np.arange(tn // 2)] = 1.0
    out = jax.ShapeDtypeStruct((E, D, F2 // 2), BF16)
    return pl.pallas_call(
        _gu_prep_kernel,
        grid=(E, F2 // tn),
        in_specs=[pl.BlockSpec((None, None, D, tn), lambda e, j: (layer, e, 0, j)), _full((tn, tn))],
        out_specs=[pl.BlockSpec((None, D, tn // 2), lambda e, j: (e, 0, j))] * 2,
        out_shape=[out, out],
        compiler_params=_params("arbitrary", "arbitrary"),
        name="moe_weight_prep",
    )(w_gu, jnp.asarray(perm, BF16))


def _expert_kernel(be_ref, nu_ref, x_ref, wg_ref, wu_ref, bg_ref, bu_ref, wd_ref, bd_ref, o_ref):
    i = pl.program_id(0)

    @pl.when(i < nu_ref[0])
    def _():
        xw = x_ref[...]
        x = jnp.concatenate([pltpu.bitcast(xw << 16, F32), pltpu.bitcast(xw & jnp.uint32(0xFFFF0000), F32)],
                            axis=1).astype(BF16)
        gate = jnp.minimum(jnp.dot(x, wg_ref[...], preferred_element_type=F32) + bg_ref[...], SWIGLU_LIMIT)
        up = jnp.clip(jnp.dot(x, wu_ref[...], preferred_element_type=F32) + bu_ref[...], -SWIGLU_LIMIT, SWIGLU_LIMIT)
        glu = gate / (1.0 + jnp.exp(-gate * SWIGLU_ALPHA))
        o_ref[...] = _dot((up + 1.0) * glu, wd_ref[...]) + bd_ref[...]

    @pl.when(i >= nu_ref[0])
    def _():
        o_ref[...] = jnp.zeros_like(o_ref)


def _experts(xg, blk_e, n_used, wg, wu, bg, bu, wd, bd):
    n_rows, W = xg.shape
    bm = MOE_BM
    D, F = wg.shape[1], wg.shape[2]
    wsel = lambda r, c: pl.BlockSpec((None, r, c), lambda i, be, nu: (be[i], 0, 0))
    return pl.pallas_call(
        _expert_kernel,
        grid_spec=pltpu.PrefetchScalarGridSpec(
            num_scalar_prefetch=2,
            grid=(n_rows // bm,),
            in_specs=[pl.BlockSpec((bm, W), lambda i, be, nu: (i, 0)),
                      wsel(D, F), wsel(D, F), wsel(1, F), wsel(1, F), wsel(F, D), wsel(1, D)],
            out_specs=pl.BlockSpec((bm, D), lambda i, be, nu: (i, 0)),
        ),
        out_shape=jax.ShapeDtypeStruct((n_rows, D), F32),
        compiler_params=_params("arbitrary"),
        name="moe_experts",
    )(blk_e, n_used, xg, wg, wu, bg, bu, wd, bd)


def _combine_kernel(pos_ref, posn_ref, x_ref, mod_ref, tg_ref, fg_ref, y_ref, o_ref, buf, sem, *, final):
    i = pl.program_id(0)
    tm = x_ref.shape[0]
    slot = i % 2

    def copy(p_ref, s, t, kk):
        return pltpu.make_async_copy(y_ref.at[pl.ds(p_ref[0, t * TOP_K + kk], 1)], buf.at[s, kk, pl.ds(t, 1)],
                                     sem.at[s])

    def fetch(p_ref, s):
        def body(t, c):
            for kk in range(TOP_K):
                copy(p_ref, s, t, kk).start()
            return c
        lax.fori_loop(0, tm, body, 0, unroll=8)

    @pl.when(i == 0)
    def _():
        fetch(pos_ref, 0)

    @pl.when(i + 1 < pl.num_programs(0))
    def _():
        fetch(posn_ref, 1 - slot)

    def wait(t, c):
        for kk in range(TOP_K):
            copy(pos_ref, slot, t, kk).wait()
        return c

    lax.fori_loop(0, tm, wait, 0, unroll=8)
    tg = tg_ref[...]
    f = jnp.zeros(x_ref.shape, F32)
    for kk in range(TOP_K):
        f = f + tg[:, kk:kk + 1] * buf[slot, kk]
    x = x_ref[...] + mod_ref[...][5:6] * f
    o_ref[...] = _rms_rows(x) * fg_ref[...] if final else x


def _combine(xs, mod, tg, yb, pos, n_lat_tiles, final_g=None):
    B, S, D = xs.shape
    tm = TOK_TILE
    nt = S // tm
    final = final_g is not None
    per_b = n_lat_tiles if final else nt
    nb = B * per_b
    tile = lambda i: (i // per_b) * nt + i % per_b
    pos3 = pos.reshape(B * nt, 1, tm * TOP_K)
    row = lambda w: pl.BlockSpec((tm, w), lambda i: (tile(i), 0))
    fg = final_g.reshape(1, D) if final else jnp.ones((1, D), F32)
    out = pl.pallas_call(
        functools.partial(_combine_kernel, final=final),
        grid=(nb,),
        in_specs=[pl.BlockSpec((None, 1, tm * TOP_K), lambda i: (tile(i), 0, 0), memory_space=pltpu.SMEM),
                  pl.BlockSpec((None, 1, tm * TOP_K), lambda i: (tile(jnp.minimum(i + 1, nb - 1)), 0, 0),
                               memory_space=pltpu.SMEM),
                  row(D),
                  pl.BlockSpec((None, 6, D),
                               lambda i: (jnp.where(i % per_b >= n_lat_tiles, 2, i // per_b), 0, 0)),
                  row(LANES),
                  _full((1, D)),
                  pl.BlockSpec(memory_space=pl.ANY)],
        out_specs=pl.BlockSpec((tm, D), lambda i: (i, 0)),
        out_shape=jax.ShapeDtypeStruct((nb * tm, D), F32),
        scratch_shapes=[pltpu.VMEM((2, TOP_K, tm, D), F32), pltpu.SemaphoreType.DMA((2,))],
        compiler_params=_params("arbitrary"),
        name="moe_combine",
    )(pos3, pos3, xs.reshape(B * S, D), mod, tg.reshape(B * S, LANES), fg, yb)
    return out.reshape(B, nb // B * tm, D)


def _prep_w_in(w_in):
    cq, ckv, kr = w_in[:, 0:256], w_in[:, 256:384], w_in[:, 384:416]
    hy, gq, gk = w_in[:, 416:1184], w_in[:, 1184:1312], w_in[:, 1312:1440]
    gv, glr, og, fn = w_in[:, 1440:1696], w_in[:, 1696:1728], w_in[:, 1728:1984], w_in[:, 1984:2240]
    half = MLA_ROPE // 2
    kr_sw = jnp.concatenate([-kr[:, half:], kr[:, :half]], axis=1)
    place = lambda w: jnp.tile(jnp.pad(w, ((0, 0), (MLA_NOPE, LANES - MLA_NOPE - MLA_ROPE))), (1, MLA_HEADS))
    glr_p = jnp.pad(glr, ((0, 0), (0, LANES - 2 * GLA_LR)))
    return jnp.concatenate([cq, ckv, place(kr), place(kr_sw), hy, gq, gk, gv, glr_p, og, fn], axis=1).astype(BF16)


def _prep_wq(w_uq):
    w = w_uq.reshape(MLA_Q_LORA, MLA_HEADS, MLA_NOPE + MLA_ROPE)
    nope, rope = w[..., :MLA_NOPE], w[..., MLA_NOPE:]
    half = MLA_ROPE // 2
    z_tail = jnp.zeros((MLA_Q_LORA, MLA_HEADS, LANES - MLA_NOPE - MLA_ROPE), F32)
    z_nope = jnp.zeros((MLA_Q_LORA, MLA_HEADS, MLA_NOPE), F32)
    plain = jnp.concatenate([nope, rope, z_tail], axis=-1).reshape(MLA_Q_LORA, MLA_HEADS * LANES)
    partner = jnp.concatenate([z_nope, -rope[..., half:], rope[..., :half], z_tail], axis=-1)
    return jnp.concatenate([plain, partner.reshape(MLA_Q_LORA, MLA_HEADS * LANES)], axis=1).astype(BF16)


def _prep_wkv(w_ukv):
    w = w_ukv.reshape(MLA_KV_LORA, MLA_HEADS, MLA_NOPE + MLA_V)
    k_nope, v = w[..., :MLA_NOPE], w[..., MLA_NOPE:]
    k_placed = jnp.pad(k_nope, ((0, 0), (0, 0), (0, LANES - MLA_NOPE))).reshape(MLA_KV_LORA, MLA_HEADS * LANES)
    v_t = jnp.pad(v, ((0, 0), (0, 0), (0, LANES - MLA_V))).reshape(MLA_KV_LORA, MLA_HEADS * LANES).T
    return k_placed.astype(BF16), v_t.astype(BF16)


def _prep_gk(gk_w, gk_b):
    w = jnp.zeros((LANES, 2 * LANES), F32)
    w = w.at[0:GLA_LR, 0:LANES].set(gk_w[0]).at[GLA_LR:2 * GLA_LR, LANES:].set(gk_w[1])
    return w.astype(BF16), jnp.concatenate([gk_b[0], gk_b[1]]).reshape(1, 2 * LANES)


def _rope_tables(n_lat, n_ctx):
    rows = n_lat // GRID_W
    row = jnp.repeat(jnp.arange(rows, dtype=F32), GRID_W)
    col = jnp.tile(jnp.arange(GRID_W, dtype=F32), rows)
    n_freq = MLA_ROPE // 4
    inv = ROPE_THETA ** (-jnp.arange(n_freq, dtype=F32) / n_freq)
    ang = jnp.concatenate([row[:, None] * inv, col[:, None] * inv], axis=-1)
    cos = jnp.concatenate([jnp.cos(ang), jnp.ones((n_ctx, MLA_ROPE // 2), F32)], axis=0)
    sin = jnp.concatenate([jnp.sin(ang), jnp.zeros((n_ctx, MLA_ROPE // 2), F32)], axis=0)
    S = n_lat + n_ctx
    ones, zeros = jnp.ones((S, MLA_NOPE), F32), jnp.zeros((S, MLA_NOPE), F32)
    tail = jnp.zeros((S, LANES - MLA_NOPE - MLA_ROPE), F32)
    q_scale = MLA_SCALE * math.log2(math.e)
    cq = jnp.concatenate([ones, cos, cos, tail], axis=1) * q_scale
    sq = jnp.concatenate([zeros, sin, sin, tail], axis=1) * q_scale
    ck = jnp.concatenate([zeros, cos, cos, tail], axis=1)
    sk = jnp.concatenate([zeros, sin, sin, tail], axis=1)
    return cq, sq, ck, sk


def kernel(x, c, ctx, c_ctx, ada_w, ada_b, norm1_g, norm2_g, w_in, mla_q_g, mla_w_uq, mla_kv_g, mla_w_ukv, hy_conv_w, hy_conv_b, hy_w1, hy_b1, hy_w2, hy_b2, hy_w3, hy_freq, hy_bias, gla_gk_w, gla_gk_b, fnet_w, mix_g, w_out, router_w, router_b, moe_w_gu, moe_b_gu, moe_w_down, moe_b_down, final_g):
    B, L, D = x.shape
    Lc = ctx.shape[1]
    depth = ada_w.shape[0]
    S = L + Lc
    tm = TOK_TILE
    nlt = L // tm
    assert B == 2 and D == D_MODEL and Lc == tm and L % (2 * DFT_N2 * 8) == 0

    xs = jnp.concatenate([x, ctx], axis=1)
    cc = jnp.zeros((8, D), F32).at[0:B].set(c).at[B].set(c_ctx)
    mods = _mods(cc, ada_w, ada_b)
    tabs = _rope_tables(L, Lc)
    jc = np.arange(GROUP_W)
    ang_c = 2.0 * np.pi * ((jc[:, None] * jc[None, :]) % GROUP_W) / GROUP_W
    csc = jnp.asarray(np.concatenate([np.cos(ang_c), -np.sin(ang_c)], axis=1), F32)
    head_mean = jnp.asarray(np.kron(np.eye(GROUP_W // MLA_V), np.full((MLA_V, MLA_V), 1.0 / MLA_V)), F32)

    for i in range(depth):
        mod = mods[i].reshape(8, 6, D)
        gkw, gkb = _prep_gk(gla_gk_w[i], gla_gk_b[i])
        q, k, v, hyz, gqk, gv, gg, og, fre, fim = _proj(
            xs, mod, norm1_g[i].reshape(1, D), _prep_w_in(w_in[i]), mla_q_g[i].reshape(1, -1), _prep_wq(mla_w_uq[i]),
            mla_kv_g[i].reshape(1, -1), *_prep_wkv(mla_w_ukv[i]), gkw, gkb, csc, tabs, nlt)

        a = _attention(q, k, v, L, Lc)

        filt = (hy_w1[i], hy_b1[i], hy_w2[i], hy_b2[i], hy_w3[i], hy_freq[i])
        vg_l, x0_l = _hy_pre(hyz, hy_conv_w[i], hy_conv_b[i], 0, L)
        hy = _hyena_long(vg_l, x0_l, *_hyena_filter(L, *filt), hy_bias[i], S)
        vg_c, x0_c = _hy_pre(hyz, hy_conv_w[i], hy_conv_b[i], L, Lc)
        hy = _hyena_ctx(vg_c, x0_c, *_hyena_filter(Lc, *filt), hy_bias[i], hy, L)

        o_f, o_b = _gla(gqk, gv, gg, L)

        fy = _fnet_ctx(fre, fim, L, Lc, _fnet_long(fre, fim, L))

        rw = jnp.pad(router_w[i], ((0, 0), (0, LANES - N_EXPERTS)))
        rb = jnp.concatenate([router_b[i], jnp.full((LANES - N_EXPERTS,), -1e30, F32)]).reshape(1, LANES)
        xs, h2w, top_i, top_g, rank, cnt = _mix(
            xs, mod, a, hy, o_f, o_b, og, fy, fnet_w[i].astype(BF16), mix_g[i].reshape(1, D), head_mean,
            w_out[i].astype(BF16), norm2_g[i].reshape(1, D), rw, rb, nlt)

        pos, blk_e, n_used, n_rows = _route(top_i.reshape(B * S, LANES)[:, :TOP_K],
                                            rank.reshape(B * S, LANES)[:, :TOP_K],
                                            cnt[0, :N_EXPERTS].astype(jnp.int32))
        xg = _dispatch(h2w.reshape(B * S, D // 2), pos, n_rows)
        wg, wu = _gu_prep(moe_w_gu, i)
        yb = _experts(xg, blk_e, n_used, wg, wu, moe_b_gu[i][:, None, 0::2], moe_b_gu[i][:, None, 1::2],
                      _down_prep(moe_w_down, i), moe_b_down[i][:, None, :])
        xs = _combine(xs, mod, top_g, yb, pos, nlt, final_g if i == depth - 1 else None)

    return xs
```

```python
import functools
import math

import numpy as np
import jax
import jax.numpy as jnp
from jax import lax
from jax.experimental import pallas as pl
from jax.experimental.pallas import tpu as pltpu

F32 = jnp.float32
BF16 = jnp.bfloat16
HIGHEST = lax.Precision.HIGHEST

EPS = 1e-6
D_MODEL = 1024
GROUP_W = 256
MLA_HEADS = 4
MLA_NOPE = 64
MLA_ROPE = 32
MLA_V = 64
MLA_Q_LORA = 256
MLA_KV_LORA = 128
MLA_SCALE = (MLA_NOPE + MLA_ROPE) ** -0.5
ROPE_THETA = 10000.0
GRID_W = 64
HY_EMB = 33
HY_FFN = 64
HY_TARGET = 1e-2
HY_FAST_PCT = 0.3
HY_SLOW_PCT = 1.5
GLA_HEADS = 4
GLA_DK = 32
GLA_DV = 64
GLA_LR = 16
GLA_TAU = 16.0
GLA_CHUNK = 64
N_EXPERTS = 32
TOP_K = 4
SWIGLU_ALPHA = 1.702
SWIGLU_LIMIT = 7.0

LANES = 128
TOK_TILE = 256
DFT_N2 = 128
MOE_BM = 512
VMEM_LIMIT = 56 * 1024 * 1024

_O_CQ, _O_CKV, _O_KRP, _O_KRS, _O_HY, _O_GQ, _O_GK, _O_GV, _O_GLR, _O_OG, _O_FN, _W_ALL = (
    0, 256, 384, 896, 1408, 2176, 2304, 2432, 2688, 2816, 3072, 3328)


def _dot(a, b):
    return jnp.dot(a.astype(BF16), b.astype(BF16), preferred_element_type=F32)


def _dot_hi(a, b):
    return jnp.dot(a, b, precision=HIGHEST, preferred_element_type=F32)


def _split(a):
    hi = a.astype(BF16)
    return hi, (a - hi.astype(F32)).astype(BF16)


def _dot_x3(a, b_hi, b_lo):
    a_hi, a_lo = _split(a)
    mm = lambda u, w: jnp.dot(u, w, preferred_element_type=F32)
    return mm(a_hi, b_hi) + (mm(a_lo, b_hi) + mm(a_hi, b_lo))


def _params(*sem):
    return pltpu.CompilerParams(dimension_semantics=sem, vmem_limit_bytes=VMEM_LIMIT)


def _rms_rows(x):
    return x * lax.rsqrt(jnp.mean(x * x, axis=-1, keepdims=True) + EPS)


def _full(shape):
    n = len(shape)
    return pl.BlockSpec(shape, lambda *_: (0,) * n)


def _mods_kernel(c_ref, w_ref, b_ref, o_ref):
    c = c_ref[...]
    s = c / (1.0 + jnp.exp(-c))
    o_ref[...] = _dot_hi(s, w_ref[...]) + b_ref[...]


def _mods(cc, ada_w, ada_b):
    depth, d, n = ada_w.shape
    tn = 1024
    return pl.pallas_call(
        _mods_kernel,
        grid=(depth, n // tn),
        in_specs=[pl.BlockSpec((8, d), lambda i, j: (0, 0)),
                  pl.BlockSpec((None, d, tn), lambda i, j: (i, 0, j)),
                  pl.BlockSpec((None, 1, tn), lambda i, j: (i, 0, j))],
        out_specs=pl.BlockSpec((None, 8, tn), lambda i, j: (i, 0, j)),
        out_shape=jax.ShapeDtypeStruct((depth, 8, n), F32),
        compiler_params=_params("arbitrary", "arbitrary"),
        name="ada_mods",
    )(cc, ada_w, ada_b.reshape(depth, 1, n))


def _proj_kernel(x_ref, mod_ref, g_ref, win_ref, qg_ref, wq_ref, kvg_ref, wkv_ref, wvt_ref, gkw_ref, gkb_ref,
                 csh_ref, csl_ref, cq_ref, sq_ref, ck_ref, sk_ref,
                 q_out, k_out, v_out, hy_out, gqk_out, gv_out, gg_out, og_out, fre_out, fim_out):
    x = x_ref[...]
    mod = mod_ref[...]
    h = _rms_rows(x) * g_ref[...] * (1.0 + mod[1:2]) + mod[0:1]
    z = _dot(h, win_ref[...])

    nq = _rms_rows(z[:, _O_CQ:_O_CKV]) * qg_ref[...]
    qq = _dot(nq, wq_ref[...])
    nkv = _rms_rows(z[:, _O_CKV:_O_KRP]) * kvg_ref[...]
    nkv = nkv.astype(BF16)
    kvu = jnp.dot(nkv, wkv_ref[...], preferred_element_type=F32)
    cq, sq, ck, sk = cq_ref[...], sq_ref[...], ck_ref[...], sk_ref[...]
    for hd in range(MLA_HEADS):
        a, b = hd * LANES, (hd + 1) * LANES
        q_out[:, a:b] = (qq[:, a:b] * cq + qq[:, 512 + a:512 + b] * sq).astype(BF16)
        k_out[:, a:b] = (kvu[:, a:b] + z[:, _O_KRP + a:_O_KRP + b] * ck
                         + z[:, _O_KRS + a:_O_KRS + b] * sk).astype(BF16)
    vt = lax.dot_general(wvt_ref[...], nkv, (((1,), (1,)), ((), ())), preferred_element_type=F32)
    vrow = lax.broadcasted_iota(jnp.int32, vt.shape, 0)
    v_out[...] = jnp.where(vrow % LANES == MLA_V, 1.0, vt).astype(BF16)

    hy_out[...] = z[:, _O_HY:_O_GQ]
    gqk_out[:, :LANES] = z[:, _O_GQ:_O_GK] * (GLA_DK ** -0.5)
    gqk_out[:, LANES:] = z[:, _O_GK:_O_GV]
    gv_out[...] = z[:, _O_GV:_O_GLR]
    gates = _dot(z[:, _O_GLR:_O_OG], gkw_ref[...]) + gkb_ref[...]
    gg_out[...] = (jnp.minimum(gates, 0.0) - jnp.log(1.0 + jnp.exp(-jnp.abs(gates)))) * (1.0 / GLA_TAU)
    og_out[...] = z[:, _O_OG:_O_FN]
    fcs = _dot_x3(z[:, _O_FN:_W_ALL], csh_ref[...], csl_ref[...])
    fre_out[...] = fcs[:, :GROUP_W]
    fim_out[...] = fcs[:, GROUP_W:]


def _proj(xs, mod, g1, win, qg, wq, kvg, wkv, wvt, gkw, gkb, csc, tabs, n_lat_tiles):
    B, S, D = xs.shape
    tm = TOK_TILE
    csh, csl = _split(csc)
    tok = lambda w: pl.BlockSpec((None, tm, w), lambda b, t: (b, t, 0))
    tab = pl.BlockSpec((tm, LANES), lambda b, t: (t, 0))
    shp = lambda w, dt: jax.ShapeDtypeStruct((B, S, w), dt)
    return pl.pallas_call(
        _proj_kernel,
        grid=(B, S // tm),
        in_specs=[tok(D),
                  pl.BlockSpec((None, 6, D), lambda b, t: (jnp.where(t >= n_lat_tiles, 2, b), 0, 0)),
                  _full((1, D)), _full(win.shape), _full((1, MLA_Q_LORA)), _full(wq.shape),
                  _full((1, MLA_KV_LORA)), _full(wkv.shape), _full(wvt.shape), _full(gkw.shape), _full(gkb.shape),
                  _full(csc.shape), _full(csc.shape), tab, tab, tab, tab],
        out_specs=[tok(512), tok(512), pl.BlockSpec((None, None, MLA_HEADS * LANES, tm), lambda b, t: (b, t, 0, 0)),
                   tok(768), tok(256), tok(256), tok(256), tok(256), tok(256), tok(256)],
        out_shape=[shp(512, BF16), shp(512, BF16), jax.ShapeDtypeStruct((B, S // tm, MLA_HEADS * LANES, tm), BF16),
                   shp(768, F32), shp(256, F32), shp(256, F32),
                   shp(256, F32), shp(256, F32), shp(256, F32), shp(256, F32)],
        compiler_params=_params("arbitrary", "arbitrary"),
        name="in_proj",
    )(xs, mod, g1, win, qg, wq, kvg, wkv, wvt, gkw, gkb, csh, csl, *tabs)


def _attn_kernel(q_ref, k_ref, vt_ref, o_ref, *, n_lat_tiles, tiles_per_chunk, n_lat, n_ctx):
    qi = pl.program_id(1)
    tq = q_ref.shape[0]
    tile = vt_ref.shape[2]
    n_hd = q_ref.shape[1] // LANES

    def heads(tile0, n_chunks, n_tiles):
        sub = LANES

        def body(c, carry):
            t0 = tile0 + c * n_tiles
            off = pl.multiple_of(t0 * tile, tile)
            state = list(carry)
            for j in range(n_tiles * tile // sub):
                for hd in range(n_hd):
                    m, acc = state[hd]
                    q = q_ref[:, hd * LANES:(hd + 1) * LANES]
                    kc = k_ref[pl.ds(off + j * sub, sub), hd * LANES:(hd + 1) * LANES]
                    s = lax.dot_general(kc, q, (((1,), (1,)), ((), ())), preferred_element_type=F32)
                    m_new = jnp.maximum(m, jnp.max(s, axis=0, keepdims=True))
                    alpha = jnp.exp2(m - m_new)
                    p = jnp.exp2(s - m_new).astype(BF16)
                    lo = (j * sub) % tile
                    vt = vt_ref[t0 + (j * sub) // tile, hd * LANES:(hd + 1) * LANES, lo:lo + sub]
                    state[hd] = (m_new, alpha * acc + jnp.dot(vt, p, preferred_element_type=F32))
            return tuple(state)

        one = (jnp.full((1, tq), -1e30, F32), jnp.zeros((LANES, tq), F32))
        res = lax.fori_loop(0, n_chunks, body, (one,) * n_hd)
        for hd in range(n_hd):
            acc_t = res[hd][1].T
            o_ref[:, hd * MLA_V:(hd + 1) * MLA_V] = acc_t[:, :MLA_V] / acc_t[:, MLA_V:MLA_V + 1]

    n_all = (n_lat + n_ctx) // tile

    @pl.when(qi < n_lat_tiles)
    def _():
        heads(0, n_all // tiles_per_chunk, tiles_per_chunk)

    @pl.when(qi >= n_lat_tiles)
    def _():
        heads(n_lat // tile, 1, n_ctx // tile)


def _attention(q, k, vt, n_lat, n_ctx):
    B, S, _ = q.shape
    tq = TOK_TILE
    n_tiles = vt.shape[1]
    tiles_per_chunk = 13 if n_tiles % 13 == 0 else 1
    kern = functools.partial(_attn_kernel, n_lat_tiles=n_lat // tq, tiles_per_chunk=tiles_per_chunk,
                             n_lat=n_lat, n_ctx=n_ctx)
    return pl.pallas_call(
        kern,
        grid=(B, S // tq),
        in_specs=[pl.BlockSpec((None, tq, 512), lambda b, t: (b, t, 0)),
                  pl.BlockSpec((None, S, 512), lambda b, t: (b, 0, 0), pipeline_mode=pl.Buffered(1)),
                  pl.BlockSpec((None, n_tiles, 512, vt.shape[3]), lambda b, t: (b, 0, 0, 0),
                               pipeline_mode=pl.Buffered(1))],
        out_specs=pl.BlockSpec((None, tq, 256), lambda b, t: (b, t, 0)),
        out_shape=jax.ShapeDtypeStruct((B, S, 256), F32),
        compiler_params=_params("arbitrary", "arbitrary"),
        name="mla_attention",
    )(q, k, vt)


def _hy_pre_kernel(z_ref, zp_ref, zn_ref, w_ref, b_ref, vg_ref, x0_ref, *, n_tiles):
    i = pl.program_id(1)
    z = z_ref[...]
    tm = z.shape[0]
    rows = lax.broadcasted_iota(jnp.int32, z.shape, 0)
    prev_row = jnp.where(i == 0, 0.0, zp_ref[7:8, :])
    next_row = jnp.where(i == n_tiles - 1, 0.0, zn_ref[0:1, :])
    z_m = jnp.where(rows == 0, prev_row, pltpu.roll(z, 1, 0))
    z_p = jnp.where(rows == tm - 1, next_row, pltpu.roll(z, tm - 1, 0))
    w = w_ref[...]
    u = z_m * w[0:1] + z * w[1:2] + z_p * w[2:3] + b_ref[...]
    vg_ref[...] = u[:, 2 * GROUP_W:] * u[:, GROUP_W:2 * GROUP_W]
    x0_ref[...] = u[:, :GROUP_W]


def _hy_pre(hyz, conv_w, conv_b, row0, n_rows):
    B, S, W = hyz.shape
    tm = TOK_TILE
    nt = n_rows // tm
    t0, r8, last8 = row0 // tm, row0 // 8, S // 8 - 1
    kern = functools.partial(_hy_pre_kernel, n_tiles=nt)
    out = jax.ShapeDtypeStruct((B, n_rows, GROUP_W), F32)
    return pl.pallas_call(
        kern,
        grid=(B, nt),
        in_specs=[pl.BlockSpec((None, tm, W), lambda b, i: (b, t0 + i, 0)),
                  pl.BlockSpec((None, 8, W), lambda b, i: (b, jnp.maximum(r8 + i * (tm // 8) - 1, 0), 0)),
                  pl.BlockSpec((None, 8, W), lambda b, i: (b, jnp.minimum(r8 + (i + 1) * (tm // 8), last8), 0)),
                  _full((3, W)), _full((1, W))],
        out_specs=[pl.BlockSpec((None, tm, GROUP_W), lambda b, i: (b, i, 0))] * 2,
        out_shape=[out, out],
        compiler_params=_params("arbitrary", "arbitrary"),
        name="hyena_pre",
    )(hyz, hyz, hyz, conv_w, conv_b.reshape(1, W))


def _filter_kernel(fv_ref, w1_ref, b1_ref, w2_ref, b2_ref, w3_ref, fr_ref, dl_ref, h_ref, ss_ref, *, n_pos):
    i = pl.program_id(0)
    tl = h_ref.shape[0]
    row = lax.broadcasted_iota(jnp.int32, (tl, LANES), 0) + i * tl
    pos = jnp.where(row <= n_pos, row, 2 * n_pos - row).astype(F32)
    lane = lax.broadcasted_iota(jnp.int32, (tl, LANES), 1)
    t = pos * (1.0 / (n_pos - 1))
    arg = (pos * (2.0 * math.pi / n_pos)) * fv_ref[...]
    feat = jnp.where(lane == 0, t, jnp.where(lane < 17, jnp.cos(arg), jnp.where(lane < HY_EMB, -jnp.sin(arg), 0.0)))
    fr = fr_ref[...]
    h = jnp.sin(fr * (_dot_hi(feat, w1_ref[...]) + b1_ref[...]))
    h = jnp.sin(fr * (_dot_hi(h, w2_ref[...]) + b2_ref[...]))
    h = _dot_hi(h, w3_ref[...]) * jnp.exp(-t[:, 0:1] * dl_ref[...])
    r1 = row[:, 0:1]
    h = jnp.where(r1 < n_pos, h[:, :GROUP_W], jnp.where(r1 == n_pos, 0.0, h[:, GROUP_W:]))
    h_ref[...] = h

    @pl.when(i == 0)
    def _():
        ss_ref[...] = jnp.zeros_like(ss_ref)

    ss_ref[...] += jnp.sum(h * h, axis=0, keepdims=True)


def _hyena_filter(n_pos, w1, b1, w2, b2, w3, freq):
    tl = min(2 * n_pos, 512)
    bands = (HY_EMB - 1) // 2
    f = np.linspace(1e-4, bands - 1, bands)
    fv = np.zeros((1, LANES), np.float32)
    fv[0, 1:17] = f
    fv[0, 17:33] = f
    w1p = jnp.zeros((LANES, HY_FFN), F32).at[:HY_EMB].set(w1)
    max_decay = math.log(HY_TARGET) / HY_FAST_PCT
    min_decay = math.log(HY_TARGET) / HY_SLOW_PCT
    deltas = np.abs(np.linspace(min_decay, max_decay, GROUP_W)).astype(np.float32)
    dl = jnp.asarray(np.concatenate([deltas, deltas])[None, :])
    kern = functools.partial(_filter_kernel, n_pos=n_pos)
    k2, ss = pl.pallas_call(
        kern,
        grid=(2 * n_pos // tl,),
        in_specs=[_full((1, LANES)), _full((LANES, HY_FFN)), _full((1, HY_FFN)), _full((HY_FFN, HY_FFN)),
                  _full((1, HY_FFN)), _full((HY_FFN, 2 * GROUP_W)), _full((1, HY_FFN)), _full((1, 2 * GROUP_W))],
        out_specs=[pl.BlockSpec((tl, GROUP_W), lambda i: (i, 0)), _full((1, GROUP_W))],
        out_shape=[jax.ShapeDtypeStruct((2 * n_pos, GROUP_W), F32), jax.ShapeDtypeStruct((1, GROUP_W), F32)],
        compiler_params=_params("arbitrary"),
        name="hyena_filter",
    )(jnp.asarray(fv), w1p, b1.reshape(1, -1), w2, b2.reshape(1, -1), w3, freq.reshape(1, -1), dl)
    return k2, lax.rsqrt(ss)


def _dft_mats(n, sign):
    j = np.arange(n)
    ang = 2.0 * np.pi * ((j[:, None] * j[None, :]) % n) / n
    return np.cos(ang), sign * np.sin(ang)


def _twiddle(n1, n2, sign):
    ang = 2.0 * np.pi * ((np.arange(n1)[:, None] * np.arange(n2)[None, :]) % (n1 * n2)) / (n1 * n2)
    return np.cos(ang), sign * np.sin(ang)


def _block_complex(re, im):
    return np.block([[re, -im], [im, re]])


def _stage1_kernel(m_ref, zr_ref, zi_ref, ar_ref, ai_ref):
    a = _dot_hi(m_ref[...], jnp.concatenate([zr_ref[...], zi_ref[...]], axis=0))
    half = ar_ref.shape[0]
    ar_ref[...] = a[:half]
    ai_ref[...] = a[half:]


def _stage1_real_kernel(m_ref, zr_ref, ar_ref, ai_ref):
    a = _dot_hi(m_ref[...], zr_ref[...])
    half = ar_ref.shape[0]
    ar_ref[...] = a[:half]
    ai_ref[...] = a[half:]


def _col_tile(cols):
    return 2048 if cols % 2048 == 0 else cols


def _hy_stage1(vg2, n1):
    _, r, cols = vg2.shape
    cr, ci = _dft_mats(n1, -1.0)
    m = jnp.asarray(_block_complex(cr[:, :r], ci[:, :r]), F32)
    tc = _col_tile(cols)
    out = jax.ShapeDtypeStruct((n1, cols), F32)
    return pl.pallas_call(
        _stage1_kernel,
        grid=(cols // tc,),
        in_specs=[_full(m.shape), pl.BlockSpec((None, r, tc), lambda j: (0, 0, j)),
                  pl.BlockSpec((None, r, tc), lambda j: (1, 0, j))],
        out_specs=[pl.BlockSpec((n1, tc), lambda j: (0, j))] * 2,
        out_shape=[out, out],
        compiler_params=_params("arbitrary"),
        name="hyena_fwd_stage1",
    )(m, vg2, vg2)


def _filter_stage1(k2v, n1):
    _, cols = k2v.shape
    cr, ci = _dft_mats(n1, -1.0)
    m = jnp.asarray(np.concatenate([cr, ci], axis=0), F32)
    tc = _col_tile(cols)
    out = jax.ShapeDtypeStruct((n1, cols), F32)
    return pl.pallas_call(
        _stage1_real_kernel,
        grid=(cols // tc,),
        in_specs=[_full(m.shape), pl.BlockSpec((n1, tc), lambda j: (0, j))],
        out_specs=[pl.BlockSpec((n1, tc), lambda j: (0, j))] * 2,
        out_shape=[out, out],
        compiler_params=_params("arbitrary"),
        name="hyena_filter_stage1",
    )(m, k2v)


def _twiddled(f_re, f_im, t_re, t_im):
    g_re = f_re * t_re - f_im * t_im
    g_im = f_re * t_im + f_im * t_re
    return jnp.concatenate([jnp.concatenate([g_re, -g_im], axis=1), jnp.concatenate([g_im, g_re], axis=1)], axis=0)


def _filter_stage2_kernel(ar_ref, ai_ref, twr_ref, twi_ref, fr_ref, fi_ref, sc_ref, kr_ref, ki_ref):
    g = _twiddled(fr_ref[...], fi_ref[...], twr_ref[...], twi_ref[...])
    x = _dot_hi(g, jnp.concatenate([ar_ref[...], ai_ref[...]], axis=0)) * sc_ref[...]
    n2 = kr_ref.shape[0]
    kr_ref[...] = x[:n2]
    ki_ref[...] = x[n2:]


def _filter_stage2(ar, ai, inv_norm, n1):
    n2 = DFT_N2
    c = ar.shape[1] // n2
    fr, fi = _dft_mats(n2, -1.0)
    twr, twi = _twiddle(n1, n2, -1.0)
    blk = pl.BlockSpec((None, n2, c), lambda k: (k, 0, 0))
    tw = pl.BlockSpec((None, 1, n2), lambda k: (k, 0, 0))
    out = jax.ShapeDtypeStruct((n1, n2, c), F32)
    return pl.pallas_call(
        _filter_stage2_kernel,
        grid=(n1,),
        in_specs=[blk, blk, tw, tw, _full((n2, n2)), _full((n2, n2)), _full((1, c))],
        out_specs=[blk, blk],
        out_shape=[out, out],
        compiler_params=_params("arbitrary"),
        name="hyena_filter_stage2",
    )(ar.reshape(n1, n2, c), ai.reshape(n1, n2, c), jnp.asarray(twr.reshape(n1, 1, n2), F32),
      jnp.asarray(twi.reshape(n1, 1, n2), F32), jnp.asarray(fr, F32), jnp.asarray(fi, F32), inv_norm)


def _hy_stage2_kernel(ar_ref, ai_ref, kr_ref, ki_ref, twr_ref, twi_ref, tcr_ref, tci_ref, fr_ref, fi_ref,
                      br_ref, bi_ref):
    f_re, f_im = fr_ref[...], fi_ref[...]
    n2 = f_re.shape[0]
    g = _twiddled(f_re, f_im, twr_ref[...], twi_ref[...])
    x = _dot_hi(g, jnp.concatenate([ar_ref[...], ai_ref[...]], axis=0))
    x_re, x_im = x[:n2], x[n2:]
    k_re, k_im = kr_ref[...], ki_ref[...]
    y = jnp.concatenate([x_re * k_re - x_im * k_im, x_re * k_im + x_im * k_re], axis=0)
    g_inv = _twiddled(f_re, -f_im, tcr_ref[...], -tci_ref[...])
    b = _dot_hi(g_inv, y)
    br_ref[...] = b[:n2]
    bi_ref[...] = b[n2:]


def _hy_stage2(ar, ai, kr, ki, n1):
    n2 = DFT_N2
    c = kr.shape[2]
    fr, fi = _dft_mats(n2, -1.0)
    twr, twi = _twiddle(n1, n2, -1.0)
    blk = pl.BlockSpec((None, n2, c), lambda k: (k, 0, 0))
    tw = pl.BlockSpec((None, 1, n2), lambda k: (k, 0, 0))
    twc = pl.BlockSpec((None, n2, 1), lambda k: (k, 0, 0))
    out = jax.ShapeDtypeStruct((n1, n2, c), F32)
    return pl.pallas_call(
        _hy_stage2_kernel,
        grid=(n1,),
        in_specs=[blk, blk, blk, blk, tw, tw, twc, twc, _full((n2, n2)), _full((n2, n2))],
        out_specs=[blk, blk],
        out_shape=[out, out],
        compiler_params=_params("arbitrary"),
        name="hyena_conv_stage2",
    )(ar.reshape(n1, n2, c), ai.reshape(n1, n2, c), kr, ki,
      jnp.asarray(twr.reshape(n1, 1, n2), F32), jnp.asarray(twi.reshape(n1, 1, n2), F32),
      jnp.asarray(twr.reshape(n1, n2, 1), F32), jnp.asarray(twi.reshape(n1, n2, 1), F32),
      jnp.asarray(fr, F32), jnp.asarray(fi, F32))


def _hy_stage3_kernel(m_ref, br_ref, bi_ref, vg0_ref, vg1_ref, x00_ref, x01_ref, bias_ref, o_ref):
    conv = _dot_hi(m_ref[...], jnp.concatenate([br_ref[...], bi_ref[...]], axis=0))
    r = vg0_ref.shape[0]
    bias = bias_ref[...]
    o_ref[0, :r] = (conv[:r] + vg0_ref[...] * bias) * x00_ref[...]
    o_ref[1, :r] = (conv[r:] + vg1_ref[...] * bias) * x01_ref[...]
    if o_ref.shape[1] > r:
        o_ref[:, r:] = jnp.zeros((2, o_ref.shape[1] - r, o_ref.shape[2]), F32)


def _hy_stage3(br, bi, vg2, x02, bias_cols, n1, rows_total):
    _, r, cols = vg2.shape
    cr, ci = _dft_mats(n1, 1.0)
    m = jnp.asarray(_block_complex(cr[:r], ci[:r]) / (n1 * DFT_N2), F32)
    tc = _col_tile(cols)
    plane = lambda p: pl.BlockSpec((None, r, tc), lambda j: (p, 0, j))
    return pl.pallas_call(
        _hy_stage3_kernel,
        grid=(cols // tc,),
        in_specs=[_full(m.shape), pl.BlockSpec((n1, tc), lambda j: (0, j)), pl.BlockSpec((n1, tc), lambda j: (0, j)),
                  plane(0), plane(1), plane(0), plane(1), pl.BlockSpec((1, tc), lambda j: (0, j))],
        out_specs=pl.BlockSpec((2, rows_total, tc), lambda j: (0, 0, j)),
        out_shape=jax.ShapeDtypeStruct((2, rows_total, cols), F32),
        compiler_params=_params("arbitrary"),
        name="hyena_inv_stage3",
    )(m, br, bi, vg2, vg2, x02, x02, bias_cols)


def _hyena_long(vg, x0, k2, inv_norm, bias, n_all):
    B, L, C = vg.shape
    assert B == 2
    n2 = DFT_N2
    n1 = 2 * L // n2
    r = L // n2
    fa_r, fa_i = _filter_stage1(k2.reshape(n1, n2 * C), n1)
    kr, ki = _filter_stage2(fa_r, fa_i, inv_norm, n1)
    vg2 = vg.reshape(2, r, n2 * C)
    ar, ai = _hy_stage1(vg2, n1)
    br, bi = _hy_stage2(ar, ai, kr, ki, n1)
    y = _hy_stage3(br.reshape(n1, n2 * C), bi.reshape(n1, n2 * C), vg2, x0.reshape(2, r, n2 * C),
                   jnp.tile(bias.reshape(1, C), (1, n2)), n1, n_all // n2)
    return y.reshape(2, n_all, C)


def _hy_ctx_kernel(vg_ref, x0_ref, kext_ref, sc_ref, bias_ref, buf_ref, o_ref):
    del buf_ref
    n = vg_ref.shape[0]

    def body(a, acc):
        src = vg_ref[pl.ds(pl.multiple_of(8 * a, 8), 8), :]
        win = pl.multiple_of(n - 8 * a, 8)
        for r in range(8):
            acc = acc + kext_ref[r, pl.ds(win, n), :] * src[r:r + 1]
        return acc

    acc = lax.fori_loop(0, n // 8, body, jnp.zeros(vg_ref.shape, F32))
    o_ref[...] = (acc * sc_ref[...] + vg_ref[...] * bias_ref[...]) * x0_ref[...]


def _hyena_ctx(vg, x0, k2, inv_norm, bias, buf, row0):
    B, n, C = vg.shape
    kext = jnp.concatenate([k2[n:], k2[:n]], axis=0)
    kext = jnp.stack([jnp.roll(kext, r, axis=0) for r in range(8)])
    blk = pl.BlockSpec((None, n, C), lambda b: (b, 0, 0))
    return pl.pallas_call(
        _hy_ctx_kernel,
        grid=(B,),
        in_specs=[blk, blk, _full((8, 2 * n, C)), _full((1, C)), _full((1, C)), pl.BlockSpec(memory_space=pl.ANY)],
        out_specs=pl.BlockSpec((None, n, C), lambda b: (b, row0 // n, 0)),
        out_shape=jax.ShapeDtypeStruct(buf.shape, F32),
        input_output_aliases={5: 0},
        compiler_params=_params("arbitrary"),
        name="hyena_ctx",
    )(vg, x0, kext, inv_norm, bias.reshape(1, C), buf)


def _fn_stage1_kernel(c_ref, s_ref, re_ref, im_ref, ar_ref, ai_ref):
    c, s, re, im = c_ref[...], s_ref[...], re_ref[...], im_ref[...]
    ar_ref[...] = _dot_hi(c, re) + _dot_hi(s, im)
    ai_ref[...] = _dot_hi(c, im) - _dot_hi(s, re)


def _fn_stage2_kernel(ar_ref, ai_ref, twr_ref, twi_ref, fr_ref, fi_ref, o_ref):
    f_re, f_im, t_re, t_im = fr_ref[...], fi_ref[...], twr_ref[...], twi_ref[...]
    g_re = f_re * t_re - f_im * t_im
    g_im = f_re * t_im + f_im * t_re
    n2 = f_re.shape[0]
    o_ref[:n2] = _dot_hi(g_re, ar_ref[...]) - _dot_hi(g_im, ai_ref[...])
    if o_ref.shape[0] > n2:
        o_ref[n2:] = jnp.zeros((o_ref.shape[0] - n2, o_ref.shape[1]), F32)


def _fnet_long(fre, fim, n_lat):
    B, S, C = fre.shape
    n2 = DFT_N2
    n1 = n_lat // n2
    cols = n2 * C
    c1, s1 = _dft_mats(n1, 1.0)
    tc = _col_tile(cols)
    rows = pl.BlockSpec((None, n1, tc), lambda b, j: (b, 0, j))
    a_shape = jax.ShapeDtypeStruct((B, n1, cols), F32)
    ar, ai = pl.pallas_call(
        _fn_stage1_kernel,
        grid=(B, cols // tc),
        in_specs=[_full((n1, n1)), _full((n1, n1)), rows, rows],
        out_specs=[rows, rows],
        out_shape=[a_shape, a_shape],
        compiler_params=_params("arbitrary", "arbitrary"),
        name="fnet_stage1",
    )(jnp.asarray(c1, F32), jnp.asarray(s1, F32), fre.reshape(B, S // n2, cols), fim.reshape(B, S // n2, cols))
    fr, fi = _dft_mats(n2, -1.0)
    scale = 1.0 / math.sqrt(n_lat * C)
    twr, twi = _twiddle(n1, n2, -1.0)
    blk = pl.BlockSpec((None, None, n2, C), lambda b, k: (b, k, 0, 0))
    tw = pl.BlockSpec((None, 1, n2), lambda b, k: (k, 0, 0))
    y = pl.pallas_call(
        _fn_stage2_kernel,
        grid=(B, n1),
        in_specs=[blk, blk, tw, tw, _full((n2, n2)), _full((n2, n2))],
        out_specs=pl.BlockSpec((None, S // n1, C), lambda b, k: (b, 0, k)),
        out_shape=jax.ShapeDtypeStruct((B, S // n1, n1 * C), F32),
        compiler_params=_params("arbitrary", "arbitrary"),
        name="fnet_stage2",
    )(ar.reshape(B, n1, n2, C), ai.reshape(B, n1, n2, C), jnp.asarray(twr.reshape(n1, 1, n2), F32),
      jnp.asarray(twi.reshape(n1, 1, n2), F32), jnp.asarray(fr * scale, F32), jnp.asarray(fi * scale, F32))
    return y.reshape(B, S, C)


def _fn_ctx_kernel(c_ref, s_ref, re_ref, im_ref, buf_ref, o_ref):
    del buf_ref
    o_ref[...] = _dot_hi(c_ref[...], re_ref[...]) + _dot_hi(s_ref[...], im_ref[...])


def _fnet_ctx(fre, fim, n_lat, n_ctx, buf):
    B, S, C = fre.shape
    c1, s1 = _dft_mats(n_ctx, 1.0)
    scale = 1.0 / math.sqrt(n_ctx * C)
    blk = pl.BlockSpec((None, n_ctx, C), lambda b: (b, n_lat // n_ctx, 0))
    return pl.pallas_call(
        _fn_ctx_kernel,
        grid=(B,),
        in_specs=[_full((n_ctx, n_ctx)), _full((n_ctx, n_ctx)), blk, blk, pl.BlockSpec(memory_space=pl.ANY)],
        out_specs=blk,
        out_shape=jax.ShapeDtypeStruct((B, S, C), F32),
        input_output_aliases={4: 0},
        compiler_params=_params("arbitrary"),
        name="fnet_ctx",
    )(jnp.asarray(c1 * scale, F32), jnp.asarray(s1 * scale, F32), fre, fim, buf)


def _gla_kernel(qkf_ref, vf_ref, gf_ref, qkb_ref, vb_ref, gb_ref, of_ref, ob_ref, st_ref):
    n = pl.program_id(0)
    n_b, tm = qkf_ref.shape[0], qkf_ref.shape[1]
    ck = GLA_CHUNK

    @pl.when(n == 0)
    def _():
        st_ref[...] = jnp.zeros_like(st_ref)

    ri = lax.broadcasted_iota(jnp.int32, (ck, ck), 0)
    ci = lax.broadcasted_iota(jnp.int32, (ck, ck), 1)

    def sub_chunk(qk_ref, v_ref, g_ref, o_ref, states, r0, reverse):
        keep = (ci >= ri) if reverse else (ci <= ri)
        g = g_ref[r0:r0 + ck, :]
        b = _dot_hi(keep.astype(F32), g)
        b_end = b[0:1] if reverse else b[ck - 1:ck]
        q = qk_ref[r0:r0 + ck, :LANES] * jnp.exp(b)
        k = qk_ref[r0:r0 + ck, LANES:]
        k_in = k * jnp.exp(-b)
        k_out = k * jnp.exp(b_end - b)
        decay = jnp.exp(b_end)
        new_states = []
        for hd in range(GLA_HEADS):
            ks = slice(hd * GLA_DK, (hd + 1) * GLA_DK)
            vs = slice(hd * GLA_DV, (hd + 1) * GLA_DV)
            qh, vh = q[:, ks].astype(BF16), v_ref[r0:r0 + ck, vs].astype(BF16)
            a = lax.dot_general(qh, k_in[:, ks].astype(BF16), (((1,), (1,)), ((), ())), preferred_element_type=F32)
            a = jnp.where(keep, a, 0.0)
            st = states[hd]
            o = jnp.dot(a.astype(BF16), vh, preferred_element_type=F32)
            o += lax.dot_general(qh, st.astype(BF16), (((1,), (1,)), ((), ())), preferred_element_type=F32)
            o_ref[r0:r0 + ck, vs] = o
            kv = lax.dot_general(vh, k_out[:, ks].astype(BF16), (((0,), (0,)), ((), ())), preferred_element_type=F32)
            new_states.append(st * decay[:, ks] + kv)
        return new_states

    st_f = [[st_ref[0, b, hd] for hd in range(GLA_HEADS)] for b in range(n_b)]
    st_b = [[st_ref[1, b, hd] for hd in range(GLA_HEADS)] for b in range(n_b)]
    n_sub = tm // ck
    for i in range(n_sub):
        for b in range(n_b):
            st_f[b] = sub_chunk(qkf_ref.at[b], vf_ref.at[b], gf_ref.at[b], of_ref.at[b], st_f[b], i * ck, False)
            st_b[b] = sub_chunk(qkb_ref.at[b], vb_ref.at[b], gb_ref.at[b], ob_ref.at[b], st_b[b],
                                (n_sub - 1 - i) * ck, True)
    for b in range(n_b):
        for hd in range(GLA_HEADS):
            st_ref[0, b, hd] = st_f[b][hd]
            st_ref[1, b, hd] = st_b[b][hd]


def _gla(gqk, gv, gg, n_lat):
    B, S, _ = gqk.shape
    tm = TOK_TILE
    nl = n_lat // tm
    nt = S // tm
    assert nt == nl + 1
    fwd = lambda n: jnp.where(n == 0, nl, n - 1)
    bwd = lambda n: jnp.where(n == 0, nl, nl - n)
    out = jax.ShapeDtypeStruct((B, S, 256), F32)
    return pl.pallas_call(
        _gla_kernel,
        grid=(nt,),
        in_specs=[pl.BlockSpec((B, tm, 256), lambda n: (0, fwd(n), 0)),
                  pl.BlockSpec((B, tm, 256), lambda n: (0, fwd(n), 0)),
                  pl.BlockSpec((B, tm, 128), lambda n: (0, fwd(n), 0)),
                  pl.BlockSpec((B, tm, 256), lambda n: (0, bwd(n), 0)),
                  pl.BlockSpec((B, tm, 256), lambda n: (0, bwd(n), 0)),
                  pl.BlockSpec((B, tm, 128), lambda n: (0, bwd(n), 1))],
        out_specs=[pl.BlockSpec((B, tm, 256), lambda n: (0, fwd(n), 0)),
                   pl.BlockSpec((B, tm, 256), lambda n: (0, bwd(n), 0))],
        out_shape=[out, out],
        scratch_shapes=[pltpu.VMEM((2, B, GLA_HEADS, GLA_DV, GLA_DK), F32)],
        compiler_params=_params("arbitrary"),
        name="gla_scan",
    )(gqk, gv, gg, gqk, gv, gg)


def _mix_kernel(x_ref, mod_ref, a_ref, hy_ref, of_ref, ob_ref, og_ref, fy_ref, fw_ref, mg_ref, hm_ref, wo_ref,
                n2g_ref, rwh_ref, rwl_ref, rb_ref, xo_ref, h2_ref, ti_ref, tg_ref, rk_ref, cnt_ref, run_ref):
    first = jnp.logical_and(pl.program_id(0) == 0, pl.program_id(1) == 0)

    @pl.when(first)
    def _():
        run_ref[...] = jnp.zeros_like(run_ref)

    mod = mod_ref[...]
    mg = mg_ref[...]
    hm = hm_ref[...]

    def head_rms(t):
        sq_hi, sq_lo = _split(t * t)
        ms = jnp.dot(sq_hi, hm, preferred_element_type=F32) + jnp.dot(sq_lo, hm, preferred_element_type=F32)
        return t * lax.rsqrt(ms + EPS)

    a = head_rms(a_ref[...]) * mg[:, 0:256]
    hy = _rms_rows(hy_ref[...]) * mg[:, 256:512]
    og = og_ref[...]
    o = head_rms(of_ref[...] + ob_ref[...]) * mg[:, 512:768] * (og / (1.0 + jnp.exp(-og)))
    fn = _rms_rows(_dot(fy_ref[...], fw_ref[...])) * mg[:, 768:1024]
    wo = wo_ref[...]
    y = _dot(a, wo[0:256]) + _dot(hy, wo[256:512]) + _dot(o, wo[512:768]) + _dot(fn, wo[768:1024])
    x = x_ref[...] + mod[2:3] * y
    xo_ref[...] = x

    h2 = _rms_rows(x) * n2g_ref[...] * (1.0 + mod[4:5]) + mod[3:4]
    bits = pltpu.bitcast(h2.astype(BF16).astype(F32), jnp.uint32)
    half = bits.shape[1] // 2
    h2_ref[...] = (bits[:, :half] >> 16) | (bits[:, half:] & jnp.uint32(0xFFFF0000))

    logits = _dot_x3(h2, rwh_ref[...], rwl_ref[...]) + rb_ref[...]
    tm = logits.shape[0]
    lane = lax.broadcasted_iota(jnp.int32, logits.shape, 1).astype(F32)
    idx_out = jnp.zeros(logits.shape, F32)
    val_out = jnp.zeros(logits.shape, F32)
    chosen = jnp.zeros(logits.shape, F32)
    picks = []
    top = None
    den = jnp.zeros((tm, 1), F32)
    for kk in range(TOP_K):
        m = jnp.max(logits, axis=-1, keepdims=True)
        idx = jnp.min(jnp.where(logits == m, lane, float(LANES)), axis=-1, keepdims=True)
        if top is None:
            top = m
        e = jnp.exp(m - top)
        den = den + e
        hit = lane == idx
        picks.append(hit)
        chosen = jnp.where(hit, 1.0, chosen)
        idx_out = jnp.where(lane == kk, idx, idx_out)
        val_out = jnp.where(lane == kk, e, val_out)
        logits = jnp.where(hit, -jnp.inf, logits)
    ti_ref[...] = idx_out.astype(jnp.int32)
    tg_ref[...] = val_out / den

    ri = lax.broadcasted_iota(jnp.int32, (tm, tm), 0)
    ci = lax.broadcasted_iota(jnp.int32, (tm, tm), 1)
    before = _dot((ci < ri).astype(F32), chosen) + run_ref[...]
    rank = jnp.zeros(logits.shape, F32)
    for kk in range(TOP_K):
        r = jnp.sum(jnp.where(picks[kk], before, 0.0), axis=-1, keepdims=True)
        rank = jnp.where(lane == kk, r, rank)
    rk_ref[...] = rank.astype(jnp.int32)
    run_ref[...] += jnp.sum(chosen, axis=0, keepdims=True)
    cnt_ref[...] = run_ref[...]


def _mix(xs, mod, a, hy, o_f, o_b, og, fy, fnet_w, mix_g, head_mean, w_out, n2g, rw, rb, n_lat_tiles):
    B, S, D = xs.shape
    tm = TOK_TILE
    rwh, rwl = _split(rw)
    tok = lambda w: pl.BlockSpec((None, tm, w), lambda b, t: (b, t, 0))
    return pl.pallas_call(
        _mix_kernel,
        grid=(B, S // tm),
        in_specs=[tok(D),
                  pl.BlockSpec((None, 6, D), lambda b, t: (jnp.where(t >= n_lat_tiles, 2, b), 0, 0)),
                  tok(256), tok(256), tok(256), tok(256), tok(256), tok(256),
                  _full((256, 256)), _full((1, D)), _full((256, 256)), _full((D, D)), _full((1, D)),
                  _full((D, LANES)), _full((D, LANES)), _full((1, LANES))],
        out_specs=[tok(D), tok(D // 2), tok(LANES), tok(LANES), tok(LANES), _full((1, LANES))],
        out_shape=[jax.ShapeDtypeStruct((B, S, D), F32), jax.ShapeDtypeStruct((B, S, D // 2), jnp.uint32),
                   jax.ShapeDtypeStruct((B, S, LANES), jnp.int32), jax.ShapeDtypeStruct((B, S, LANES), F32),
                   jax.ShapeDtypeStruct((B, S, LANES), jnp.int32), jax.ShapeDtypeStruct((1, LANES), F32)],
        scratch_shapes=[pltpu.VMEM((1, LANES), F32)],
        compiler_params=_params("arbitrary", "arbitrary"),
        name="mix_out_router",
    )(xs, mod, a, hy, o_f, o_b, og, fy, fnet_w, mix_g, head_mean.astype(BF16), w_out, n2g, rwh, rwl, rb)


def _route(top_i, rank, counts):
    T = top_i.shape[0]
    bm = MOE_BM
    padded = (counts + bm - 1) // bm * bm
    pend = jnp.cumsum(padded)
    pstart = pend - padded
    pos = (pstart[top_i] + rank).astype(jnp.int32)
    n_blocks = (T * TOP_K + bm - 1) // bm + N_EXPERTS
    blk_row0 = jnp.arange(n_blocks, dtype=pend.dtype) * bm
    blk_e = jnp.minimum(jnp.sum(pend[None, :] <= blk_row0[:, None], axis=1), N_EXPERTS - 1).astype(jnp.int32)
    n_used = (pend[-1] // bm).astype(jnp.int32).reshape(1)
    return pos, blk_e, n_used, n_blocks * bm


def _dispatch_kernel(pos_ref, h_ref, zero_ref, xg_ref, sem):
    del zero_ref
    tm = h_ref.shape[0]

    def copy(t, kk):
        return pltpu.make_async_copy(h_ref.at[pl.ds(t, 1)], xg_ref.at[pl.ds(pos_ref[0, t * TOP_K + kk], 1)], sem)

    def start(t, c):
        for kk in range(TOP_K):
            copy(t, kk).start()
        return c

    def wait(t, c):
        for kk in range(TOP_K):
            copy(t, kk).wait()
        return c

    lax.fori_loop(0, tm, start, 0, unroll=8)
    lax.fori_loop(0, tm, wait, 0, unroll=8)


def _dispatch(h2w, pos, n_rows):
    T, W = h2w.shape
    tm = TOK_TILE
    nb = T // tm
    return pl.pallas_call(
        _dispatch_kernel,
        grid=(nb,),
        in_specs=[pl.BlockSpec((None, 1, tm * TOP_K), lambda i: (i, 0, 0), memory_space=pltpu.SMEM),
                  pl.BlockSpec((tm, W), lambda i: (i, 0)),
                  pl.BlockSpec(memory_space=pl.ANY)],
        out_specs=pl.BlockSpec(memory_space=pl.ANY),
        out_shape=jax.ShapeDtypeStruct((n_rows, W), h2w.dtype),
        scratch_shapes=[pltpu.SemaphoreType.DMA(())],
        input_output_aliases={2: 0},
        compiler_params=_params("arbitrary"),
        name="moe_dispatch",
    )(pos.reshape(nb, 1, tm * TOP_K), h2w, jnp.zeros((n_rows, W), h2w.dtype))


def _gu_prep_kernel(w_ref, p_ref, g_ref, u_ref):
    y = jnp.dot(w_ref[...].astype(BF16), p_ref[...], preferred_element_type=F32)
    half = y.shape[1] // 2
    g_ref[...] = y[:, :half].astype(BF16)
    u_ref[...] = y[:, half:].astype(BF16)


def _cast_kernel(w_ref, o_ref):
    o_ref[...] = w_ref[...].astype(BF16)


def _down_prep(w_down, layer):
    _, E, F, D = w_down.shape
    return pl.pallas_call(
        _cast_kernel,
        grid=(E,),
        in_specs=[pl.BlockSpec((None, None, F, D), lambda e: (layer, e, 0, 0))],
        out_specs=pl.BlockSpec((None, F, D), lambda e: (e, 0, 0)),
        out_shape=jax.ShapeDtypeStruct((E, F, D), BF16),
        compiler_params=_params("arbitrary"),
        name="moe_down_prep",
    )(w_down)


def _gu_prep(w_gu, layer):
    _, E, D, F2 = w_gu.shape
    tn = 512
    perm = np.zeros((tn, tn), np.float32)
    perm[2 * np.arange(tn // 2), np.arange(tn // 2)] = 1.0
    perm[2 * np.arange(tn // 2) + 1, tn // 2 + np.arange(tn // 2)] = 1.0
    out = jax.ShapeDtypeStruct((E, D, F2 // 2), BF16)
    return pl.pallas_call(
        _gu_prep_kernel,
        grid=(E, F2 // tn),
        in_specs=[pl.BlockSpec((None, None, D, tn), lambda e, j: (layer, e, 0, j)), _full((tn, tn))],
        out_specs=[pl.BlockSpec((None, D, tn // 2), lambda e, j: (e, 0, j))] * 2,
        out_shape=[out, out],
        compiler_params=_params("arbitrary", "arbitrary"),
        name="moe_weight_prep",
    )(w_gu, jnp.asarray(perm, BF16))


def _expert_kernel(be_ref, nu_ref, x_ref, wg_ref, wu_ref, bg_ref, bu_ref, wd_ref, bd_ref, o_ref):
    i = pl.program_id(0)

    @pl.when(i < nu_ref[0])
    def _():
        xw = x_ref[...]
        x = jnp.concatenate([pltpu.bitcast(xw << 16, F32), pltpu.bitcast(xw & jnp.uint32(0xFFFF0000), F32)],
                            axis=1).astype(BF16)
        gate = jnp.minimum(jnp.dot(x, wg_ref[...], preferred_element_type=F32) + bg_ref[...], SWIGLU_LIMIT)
        up = jnp.clip(jnp.dot(x, wu_ref[...], preferred_element_type=F32) + bu_ref[...], -SWIGLU_LIMIT, SWIGLU_LIMIT)
        glu = gate / (1.0 + jnp.exp(-gate * SWIGLU_ALPHA))
        o_ref[...] = _dot((up + 1.0) * glu, wd_ref[...]) + bd_ref[...]

    @pl.when(i >= nu_ref[0])
    def _():
        o_ref[...] = jnp.zeros_like(o_ref)


def _experts(xg, blk_e, n_used, wg, wu, bg, bu, wd, bd):
    n_rows, W = xg.shape
    bm = MOE_BM
    D, F = wg.shape[1], wg.shape[2]
    wsel = lambda r, c: pl.BlockSpec((None, r, c), lambda i, be, nu: (be[i], 0, 0))
    return pl.pallas_call(
        _expert_kernel,
        grid_spec=pltpu.PrefetchScalarGridSpec(
            num_scalar_prefetch=2,
            grid=(n_rows // bm,),
            in_specs=[pl.BlockSpec((bm, W), lambda i, be, nu: (i, 0)),
                      wsel(D, F), wsel(D, F), wsel(1, F), wsel(1, F), wsel(F, D), wsel(1, D)],
            out_specs=pl.BlockSpec((bm, D), lambda i, be, nu: (i, 0)),
        ),
        out_shape=jax.ShapeDtypeStruct((n_rows, D), F32),
        compiler_params=_params("arbitrary"),
        name="moe_experts",
    )(blk_e, n_used, xg, wg, wu, bg, bu, wd, bd)


def _combine_kernel(pos_ref, posn_ref, x_ref, mod_ref, tg_ref, fg_ref, y_ref, o_ref, buf, sem, *, final):
    i = pl.program_id(0)
    tm = x_ref.shape[0]
    slot = i % 2

    def copy(p_ref, s, t, kk):
        return pltpu.make_async_copy(y_ref.at[pl.ds(p_ref[0, t * TOP_K + kk], 1)], buf.at[s, kk, pl.ds(t, 1)],
                                     sem.at[s])

    def fetch(p_ref, s):
        def body(t, c):
            for kk in range(TOP_K):
                copy(p_ref, s, t, kk).start()
            return c
        lax.fori_loop(0, tm, body, 0, unroll=8)

    @pl.when(i == 0)
    def _():
        fetch(pos_ref, 0)

    @pl.when(i + 1 < pl.num_programs(0))
    def _():
        fetch(posn_ref, 1 - slot)

    def wait(t, c):
        for kk in range(TOP_K):
            copy(pos_ref, slot, t, kk).wait()
        return c

    lax.fori_loop(0, tm, wait, 0, unroll=8)
    tg = tg_ref[...]
    f = jnp.zeros(x_ref.shape, F32)
    for kk in range(TOP_K):
        f = f + tg[:, kk:kk + 1] * buf[slot, kk]
    x = x_ref[...] + mod_ref[...][5:6] * f
    o_ref[...] = _rms_rows(x) * fg_ref[...] if final else x


def _combine(xs, mod, tg, yb, pos, n_lat_tiles, final_g=None):
    B, S, D = xs.shape
    tm = TOK_TILE
    nt = S // tm
    final = final_g is not None
    per_b = n_lat_tiles if final else nt
    nb = B * per_b
    tile = lambda i: (i // per_b) * nt + i % per_b
    pos3 = pos.reshape(B * nt, 1, tm * TOP_K)
    row = lambda w: pl.BlockSpec((tm, w), lambda i: (tile(i), 0))
    fg = final_g.reshape(1, D) if final else jnp.ones((1, D), F32)
    out = pl.pallas_call(
        functools.partial(_combine_kernel, final=final),
        grid=(nb,),
        in_specs=[pl.BlockSpec((None, 1, tm * TOP_K), lambda i: (tile(i), 0, 0), memory_space=pltpu.SMEM),
                  pl.BlockSpec((None, 1, tm * TOP_K), lambda i: (tile(jnp.minimum(i + 1, nb - 1)), 0, 0),
                               memory_space=pltpu.SMEM),
                  row(D),
                  pl.BlockSpec((None, 6, D),
                               lambda i: (jnp.where(i % per_b >= n_lat_tiles, 2, i // per_b), 0, 0)),
                  row(LANES),
                  _full((1, D)),
                  pl.BlockSpec(memory_space=pl.ANY)],
        out_specs=pl.BlockSpec((tm, D), lambda i: (i, 0)),
        out_shape=jax.ShapeDtypeStruct((nb * tm, D), F32),
        scratch_shapes=[pltpu.VMEM((2, TOP_K, tm, D), F32), pltpu.SemaphoreType.DMA((2,))],
        compiler_params=_params("arbitrary"),
        name="moe_combine",
    )(pos3, pos3, xs.reshape(B * S, D), mod, tg.reshape(B * S, LANES), fg, yb)
    return out.reshape(B, nb // B * tm, D)


def _prep_w_in(w_in):
    cq, ckv, kr = w_in[:, 0:256], w_in[:, 256:384], w_in[:, 384:416]
    hy, gq, gk = w_in[:, 416:1184], w_in[:, 1184:1312], w_in[:, 1312:1440]
    gv, glr, og, fn = w_in[:, 1440:1696], w_in[:, 1696:1728], w_in[:, 1728:1984], w_in[:, 1984:2240]
    half = MLA_ROPE // 2
    kr_sw = jnp.concatenate([-kr[:, half:], kr[:, :half]], axis=1)
    place = lambda w: jnp.tile(jnp.pad(w, ((0, 0), (MLA_NOPE, LANES - MLA_NOPE - MLA_ROPE))), (1, MLA_HEADS))
    glr_p = jnp.pad(glr, ((0, 0), (0, LANES - 2 * GLA_LR)))
    return jnp.concatenate([cq, ckv, place(kr), place(kr_sw), hy, gq, gk, gv, glr_p, og, fn], axis=1).astype(BF16)


def _prep_wq(w_uq):
    w = w_uq.reshape(MLA_Q_LORA, MLA_HEADS, MLA_NOPE + MLA_ROPE)
    nope, rope = w[..., :MLA_NOPE], w[..., MLA_NOPE:]
    half = MLA_ROPE // 2
    z_tail = jnp.zeros((MLA_Q_LORA, MLA_HEADS, LANES - MLA_NOPE - MLA_ROPE), F32)
    z_nope = jnp.zeros((MLA_Q_LORA, MLA_HEADS, MLA_NOPE), F32)
    plain = jnp.concatenate([nope, rope, z_tail], axis=-1).reshape(MLA_Q_LORA, MLA_HEADS * LANES)
    partner = jnp.concatenate([z_nope, -rope[..., half:], rope[..., :half], z_tail], axis=-1)
    return jnp.concatenate([plain, partner.reshape(MLA_Q_LORA, MLA_HEADS * LANES)], axis=1).astype(BF16)


def _prep_wkv(w_ukv):
    w = w_ukv.reshape(MLA_KV_LORA, MLA_HEADS, MLA_NOPE + MLA_V)
    k_nope, v = w[..., :MLA_NOPE], w[..., MLA_NOPE:]
    k_placed = jnp.pad(k_nope, ((0, 0), (0, 0), (0, LANES - MLA_NOPE))).reshape(MLA_KV_LORA, MLA_HEADS * LANES)
    v_t = jnp.pad(v, ((0, 0), (0, 0), (0, LANES - MLA_V))).reshape(MLA_KV_LORA, MLA_HEADS * LANES).T
    return k_placed.astype(BF16), v_t.astype(BF16)


def _prep_gk(gk_w, gk_b):
    w = jnp.zeros((LANES, 2 * LANES), F32)
    w = w.at[0:GLA_LR, 0:LANES].set(gk_w[0]).at[GLA_LR:2 * GLA_LR, LANES:].set(gk_w[1])
    return w.astype(BF16), jnp.concatenate([gk_b[0], gk_b[1]]).reshape(1, 2 * LANES)


def _rope_tables(n_lat, n_ctx):
    rows = n_lat // GRID_W
    row = jnp.repeat(jnp.arange(rows, dtype=F32), GRID_W)
    col = jnp.tile(jnp.arange(GRID_W, dtype=F32), rows)
    n_freq = MLA_ROPE // 4
    inv = ROPE_THETA ** (-jnp.arange(n_freq, dtype=F32) / n_freq)
    ang = jnp.concatenate([row[:, None] * inv, col[:, None] * inv], axis=-1)
    cos = jnp.concatenate([jnp.cos(ang), jnp.ones((n_ctx, MLA_ROPE // 2), F32)], axis=0)
    sin = jnp.concatenate([jnp.sin(ang), jnp.zeros((n_ctx, MLA_ROPE // 2), F32)], axis=0)
    S = n_lat + n_ctx
    ones, zeros = jnp.ones((S, MLA_NOPE), F32), jnp.zeros((S, MLA_NOPE), F32)
    tail = jnp.zeros((S, LANES - MLA_NOPE - MLA_ROPE), F32)
    q_scale = MLA_SCALE * math.log2(math.e)
    cq = jnp.concatenate([ones, cos, cos, tail], axis=1) * q_scale
    sq = jnp.concatenate([zeros, sin, sin, tail], axis=1) * q_scale
    ck = jnp.concatenate([zeros, cos, cos, tail], axis=1)
    sk = jnp.concatenate([zeros, sin, sin, tail], axis=1)
    return cq, sq, ck, sk


def kernel(x, c, ctx, c_ctx, ada_w, ada_b, norm1_g, norm2_g, w_in, mla_q_g, mla_w_uq, mla_kv_g, mla_w_ukv, hy_conv_w, hy_conv_b, hy_w1, hy_b1, hy_w2, hy_b2, hy_w3, hy_freq, hy_bias, gla_gk_w, gla_gk_b, fnet_w, mix_g, w_out, router_w, router_b, moe_w_gu, moe_b_gu, moe_w_down, moe_b_down, final_g):
    B, L, D = x.shape
    Lc = ctx.shape[1]
    depth = ada_w.shape[0]
    S = L + Lc
    tm = TOK_TILE
    nlt = L // tm
    assert B == 2 and D == D_MODEL and Lc == tm and L % (2 * DFT_N2 * 8) == 0

    xs = jnp.concatenate([x, ctx], axis=1)
    cc = jnp.zeros((8, D), F32).at[0:B].set(c).at[B].set(c_ctx)
    mods = _mods(cc, ada_w, ada_b)
    tabs = _rope_tables(L, Lc)
    jc = np.arange(GROUP_W)
    ang_c = 2.0 * np.pi * ((jc[:, None] * jc[None, :]) % GROUP_W) / GROUP_W
    csc = jnp.asarray(np.concatenate([np.cos(ang_c), -np.sin(ang_c)], axis=1), F32)
    head_mean = jnp.asarray(np.kron(np.eye(GROUP_W // MLA_V), np.full((MLA_V, MLA_V), 1.0 / MLA_V)), F32)

    for i in range(depth):
        mod = mods[i].reshape(8, 6, D)
        gkw, gkb = _prep_gk(gla_gk_w[i], gla_gk_b[i])
        q, k, v, hyz, gqk, gv, gg, og, fre, fim = _proj(
            xs, mod, norm1_g[i].reshape(1, D), _prep_w_in(w_in[i]), mla_q_g[i].reshape(1, -1), _prep_wq(mla_w_uq[i]),
            mla_kv_g[i].reshape(1, -1), *_prep_wkv(mla_w_ukv[i]), gkw, gkb, csc, tabs, nlt)

        a = _attention(q, k, v, L, Lc)

        filt = (hy_w1[i], hy_b1[i], hy_w2[i], hy_b2[i], hy_w3[i], hy_freq[i])
        vg_l, x0_l = _hy_pre(hyz, hy_conv_w[i], hy_conv_b[i], 0, L)
        hy = _hyena_long(vg_l, x0_l, *_hyena_filter(L, *filt), hy_bias[i], S)
        vg_c, x0_c = _hy_pre(hyz, hy_conv_w[i], hy_conv_b[i], L, Lc)
        hy = _hyena_ctx(vg_c, x0_c, *_hyena_filter(Lc, *filt), hy_bias[i], hy, L)

        o_f, o_b = _gla(gqk, gv, gg, L)

        fy = _fnet_ctx(fre, fim, L, Lc, _fnet_long(fre, fim, L))

        rw = jnp.pad(router_w[i], ((0, 0), (0, LANES - N_EXPERTS)))
        rb = jnp.concatenate([router_b[i], jnp.full((LANES - N_EXPERTS,), -1e30, F32)]).reshape(1, LANES)
        xs, h2w, top_i, top_g, rank, cnt = _mix(
            xs, mod, a, hy, o_f, o_b, og, fy, fnet_w[i].astype(BF16), mix_g[i].reshape(1, D), head_mean,
            w_out[i].astype(BF16), norm2_g[i].reshape(1, D), rw, rb, nlt)

        pos, blk_e, n_used, n_rows = _route(top_i.reshape(B * S, LANES)[:, :TOP_K],
                                            rank.reshape(B * S, LANES)[:, :TOP_K],
                                            cnt[0, :N_EXPERTS].astype(jnp.int32))
        xg = _dispatch(h2w.reshape(B * S, D // 2), pos, n_rows)
        wg, wu = _gu_prep(moe_w_gu, i)
        yb = _experts(xg, blk_e, n_used, wg, wu, moe_b_gu[i][:, None, 0::2], moe_b_gu[i][:, None, 1::2],
                      _down_prep(moe_w_down, i), moe_b_down[i][:, None, :])
        xs = _combine(xs, mod, top_g, yb, pos, nlt, final_g if i == depth - 1 else None)

    return xs
```

```python
import functools
import math

import numpy as np
import jax
import jax.numpy as jnp
from jax import lax
from jax.experimental import pallas as pl
from jax.experimental.pallas import tpu as pltpu

F32 = jnp.float32
BF16 = jnp.bfloat16
HIGHEST = lax.Precision.HIGHEST

EPS = 1e-6
D_MODEL = 1024
GROUP_W = 256
MLA_HEADS = 4
MLA_NOPE = 64
MLA_ROPE = 32
MLA_V = 64
MLA_Q_LORA = 256
MLA_KV_LORA = 128
MLA_SCALE = (MLA_NOPE + MLA_ROPE) ** -0.5
ROPE_THETA = 10000.0
GRID_W = 64
HY_EMB = 33
HY_FFN = 64
HY_TARGET = 1e-2
HY_FAST_PCT = 0.3
HY_SLOW_PCT = 1.5
GLA_HEADS = 4
GLA_DK = 32
GLA_DV = 64
GLA_LR = 16
GLA_TAU = 16.0
GLA_CHUNK = 64
N_EXPERTS = 32
TOP_K = 4
SWIGLU_ALPHA = 1.702
SWIGLU_LIMIT = 7.0

LANES = 128
TOK_TILE = 256
DFT_N2 = 128
MOE_BM = 512
VMEM_LIMIT = 56 * 1024 * 1024

_O_CQ, _O_CKV, _O_KRP, _O_KRS, _O_HY, _O_GQ, _O_GK, _O_GV, _O_GLR, _O_OG, _O_FN, _W_ALL = (
    0, 256, 384, 896, 1408, 2176, 2304, 2432, 2688, 2816, 3072, 3328)


def _dot(a, b):
    return jnp.dot(a.astype(BF16), b.astype(BF16), preferred_element_type=F32)


def _dot_hi(a, b):
    return jnp.dot(a, b, precision=HIGHEST, preferred_element_type=F32)


def _split(a):
    hi = a.astype(BF16)
    return hi, (a - hi.astype(F32)).astype(BF16)


def _dot_x3(a, b_hi, b_lo):
    a_hi, a_lo = _split(a)
    mm = lambda u, w: jnp.dot(u, w, preferred_element_type=F32)
    return mm(a_hi, b_hi) + (mm(a_lo, b_hi) + mm(a_hi, b_lo))


def _pack_pairs(x):
    bits = pltpu.bitcast(x.astype(BF16).astype(F32), jnp.uint32)
    w = bits.shape[1] // 2
    return (bits[:, :w] >> 16) | (bits[:, w:] & jnp.uint32(0xFFFF0000))


def _unpack_pairs(words):
    return pltpu.bitcast(words << 16, F32), pltpu.bitcast(words & jnp.uint32(0xFFFF0000), F32)


def _params(*sem):
    return pltpu.CompilerParams(dimension_semantics=sem, vmem_limit_bytes=VMEM_LIMIT)


def _rms_rows(x):
    return x * lax.rsqrt(jnp.mean(x * x, axis=-1, keepdims=True) + EPS)


def _full(shape):
    n = len(shape)
    return pl.BlockSpec(shape, lambda *_: (0,) * n)


def _mods_kernel(c_ref, w_ref, b_ref, o_ref):
    c = c_ref[...]
    s = c / (1.0 + jnp.exp(-c))
    o_ref[...] = _dot_hi(s, w_ref[...]) + b_ref[...]


def _mods(cc, ada_w, ada_b):
    depth, d, n = ada_w.shape
    tn = 1024
    return pl.pallas_call(
        _mods_kernel,
        grid=(depth, n // tn),
        in_specs=[pl.BlockSpec((8, d), lambda i, j: (0, 0)),
                  pl.BlockSpec((None, d, tn), lambda i, j: (i, 0, j)),
                  pl.BlockSpec((None, 1, tn), lambda i, j: (i, 0, j))],
        out_specs=pl.BlockSpec((None, 8, tn), lambda i, j: (i, 0, j)),
        out_shape=jax.ShapeDtypeStruct((depth, 8, n), F32),
        compiler_params=_params("arbitrary", "arbitrary"),
        name="ada_mods",
    )(cc, ada_w, ada_b.reshape(depth, 1, n))


def _proj_kernel(x_ref, mod_ref, g_ref, win_ref, qg_ref, wq_ref, kvg_ref, wkv_ref, wvt_ref, gkw_ref, gkb_ref,
                 csh_ref, csl_ref, cq_ref, sq_ref, ck_ref, sk_ref,
                 q_out, k_out, v_out, hy_out, gqk_out, gv_out, gg_out, og_out, fre_out, fim_out):
    x = x_ref[...]
    mod = mod_ref[...]
    h = _rms_rows(x) * g_ref[...] * (1.0 + mod[1:2]) + mod[0:1]
    z = _dot(h, win_ref[...])

    nq = _rms_rows(z[:, _O_CQ:_O_CKV]) * qg_ref[...]
    qq = _dot(nq, wq_ref[...])
    nkv = _rms_rows(z[:, _O_CKV:_O_KRP]) * kvg_ref[...]
    nkv = nkv.astype(BF16)
    kvu = jnp.dot(nkv, wkv_ref[...], preferred_element_type=F32)
    cq, sq, ck, sk = cq_ref[...], sq_ref[...], ck_ref[...], sk_ref[...]
    for hd in range(MLA_HEADS):
        a, b = hd * LANES, (hd + 1) * LANES
        q_out[:, a:b] = (qq[:, a:b] * cq + qq[:, 512 + a:512 + b] * sq).astype(BF16)
        k_out[:, a:b] = (kvu[:, a:b] + z[:, _O_KRP + a:_O_KRP + b] * ck
                         + z[:, _O_KRS + a:_O_KRS + b] * sk).astype(BF16)
    vt = lax.dot_general(wvt_ref[...], nkv, (((1,), (1,)), ((), ())), preferred_element_type=F32)
    vrow = lax.broadcasted_iota(jnp.int32, vt.shape, 0)
    v_out[...] = jnp.where(vrow % LANES == MLA_V, 1.0, vt).astype(BF16)

    hy_out[...] = z[:, _O_HY:_O_GQ]
    gqk_out[:, :LANES] = z[:, _O_GQ:_O_GK] * (GLA_DK ** -0.5)
    gqk_out[:, LANES:] = z[:, _O_GK:_O_GV]
    gv_out[...] = z[:, _O_GV:_O_GLR]
    gates = _dot(z[:, _O_GLR:_O_OG], gkw_ref[...]) + gkb_ref[...]
    gg_out[...] = (jnp.minimum(gates, 0.0) - jnp.log(1.0 + jnp.exp(-jnp.abs(gates)))) * (1.0 / GLA_TAU)
    og_out[...] = z[:, _O_OG:_O_FN]
    fcs = _dot_x3(z[:, _O_FN:_W_ALL], csh_ref[...], csl_ref[...])
    fre_out[...] = fcs[:, :GROUP_W]
    fim_out[...] = fcs[:, GROUP_W:]


def _proj(xs, mod, g1, win, qg, wq, kvg, wkv, wvt, gkw, gkb, csc, tabs, n_lat_tiles):
    B, S, D = xs.shape
    tm = TOK_TILE
    csh, csl = _split(csc)
    tok = lambda w: pl.BlockSpec((None, tm, w), lambda b, t: (b, t, 0))
    tab = pl.BlockSpec((tm, LANES), lambda b, t: (t, 0))
    shp = lambda w, dt: jax.ShapeDtypeStruct((B, S, w), dt)
    return pl.pallas_call(
        _proj_kernel,
        grid=(B, S // tm),
        in_specs=[tok(D),
                  pl.BlockSpec((None, 6, D), lambda b, t: (jnp.where(t >= n_lat_tiles, 2, b), 0, 0)),
                  _full((1, D)), _full(win.shape), _full((1, MLA_Q_LORA)), _full(wq.shape),
                  _full((1, MLA_KV_LORA)), _full(wkv.shape), _full(wvt.shape), _full(gkw.shape), _full(gkb.shape),
                  _full(csc.shape), _full(csc.shape), tab, tab, tab, tab],
        out_specs=[tok(512), tok(512), pl.BlockSpec((None, None, MLA_HEADS * LANES, tm), lambda b, t: (b, t, 0, 0)),
                   tok(768), tok(256), tok(256), tok(256), tok(256), tok(256), tok(256)],
        out_shape=[shp(512, BF16), shp(512, BF16), jax.ShapeDtypeStruct((B, S // tm, MLA_HEADS * LANES, tm), BF16),
                   shp(768, F32), shp(256, F32), shp(256, F32),
                   shp(256, F32), shp(256, F32), shp(256, F32), shp(256, F32)],
        compiler_params=_params("arbitrary", "arbitrary"),
        name="in_proj",
    )(xs, mod, g1, win, qg, wq, kvg, wkv, wvt, gkw, gkb, csh, csl, *tabs)


def _attn_kernel(q_ref, k_ref, vt_ref, o_ref, *, n_lat_tiles, tiles_per_chunk, n_lat, n_ctx):
    qi = pl.program_id(1)
    tq = q_ref.shape[0]
    tile = vt_ref.shape[2]
    n_hd = q_ref.shape[1] // LANES

    def heads(tile0, n_chunks, n_tiles):
        sub = LANES

        def body(c, carry):
            t0 = tile0 + c * n_tiles
            off = pl.multiple_of(t0 * tile, tile)
            state = list(carry)
            for j in range(n_tiles * tile // sub):
                for hd in range(n_hd):
                    m, acc = state[hd]
                    q = q_ref[:, hd * LANES:(hd + 1) * LANES]
                    kc = k_ref[pl.ds(off + j * sub, sub), hd * LANES:(hd + 1) * LANES]
                    s = lax.dot_general(kc, q, (((1,), (1,)), ((), ())), preferred_element_type=F32)
                    m_new = jnp.maximum(m, jnp.max(s, axis=0, keepdims=True))
                    alpha = jnp.exp2(m - m_new)
                    p = jnp.exp2(s - m_new).astype(BF16)
                    lo = (j * sub) % tile
                    vt = vt_ref[t0 + (j * sub) // tile, hd * LANES:(hd + 1) * LANES, lo:lo + sub]
                    state[hd] = (m_new, alpha * acc + jnp.dot(vt, p, preferred_element_type=F32))
            return tuple(state)

        one = (jnp.full((1, tq), -1e30, F32), jnp.zeros((LANES, tq), F32))
        res = lax.fori_loop(0, n_chunks, body, (one,) * n_hd)
        for hd in range(n_hd):
            acc_t = res[hd][1].T
            o_ref[:, hd * MLA_V:(hd + 1) * MLA_V] = acc_t[:, :MLA_V] / acc_t[:, MLA_V:MLA_V + 1]

    n_all = (n_lat + n_ctx) // tile

    @pl.when(qi < n_lat_tiles)
    def _():
        heads(0, n_all // tiles_per_chunk, tiles_per_chunk)

    @pl.when(qi >= n_lat_tiles)
    def _():
        heads(n_lat // tile, 1, n_ctx // tile)


def _attention(q, k, vt, n_lat, n_ctx):
    B, S, _ = q.shape
    tq = TOK_TILE
    n_tiles = vt.shape[1]
    tiles_per_chunk = 13 if n_tiles % 13 == 0 else 1
    kern = functools.partial(_attn_kernel, n_lat_tiles=n_lat // tq, tiles_per_chunk=tiles_per_chunk,
                             n_lat=n_lat, n_ctx=n_ctx)
    return pl.pallas_call(
        kern,
        grid=(B, S // tq),
        in_specs=[pl.BlockSpec((None, tq, 512), lambda b, t: (b, t, 0)),
                  pl.BlockSpec((None, S, 512), lambda b, t: (b, 0, 0), pipeline_mode=pl.Buffered(1)),
                  pl.BlockSpec((None, n_tiles, 512, vt.shape[3]), lambda b, t: (b, 0, 0, 0),
                               pipeline_mode=pl.Buffered(1))],
        out_specs=pl.BlockSpec((None, tq, 256), lambda b, t: (b, t, 0)),
        out_shape=jax.ShapeDtypeStruct((B, S, 256), F32),
        compiler_params=_params("arbitrary", "arbitrary"),
        name="mla_attention",
    )(q, k, vt)


def _hy_pre_kernel(z_ref, zp_ref, zn_ref, w_ref, b_ref, vg_ref, x0_ref, *, n_tiles):
    i = pl.program_id(1)
    z = z_ref[...]
    tm = z.shape[0]
    rows = lax.broadcasted_iota(jnp.int32, z.shape, 0)
    prev_row = jnp.where(i == 0, 0.0, zp_ref[7:8, :])
    next_row = jnp.where(i == n_tiles - 1, 0.0, zn_ref[0:1, :])
    z_m = jnp.where(rows == 0, prev_row, pltpu.roll(z, 1, 0))
    z_p = jnp.where(rows == tm - 1, next_row, pltpu.roll(z, tm - 1, 0))
    w = w_ref[...]
    u = z_m * w[0:1] + z * w[1:2] + z_p * w[2:3] + b_ref[...]
    vg_ref[...] = u[:, 2 * GROUP_W:] * u[:, GROUP_W:2 * GROUP_W]
    x0_ref[...] = u[:, :GROUP_W]


def _hy_pre(hyz, conv_w, conv_b, row0, n_rows):
    B, S, W = hyz.shape
    tm = TOK_TILE
    nt = n_rows // tm
    t0, r8, last8 = row0 // tm, row0 // 8, S // 8 - 1
    kern = functools.partial(_hy_pre_kernel, n_tiles=nt)
    out = jax.ShapeDtypeStruct((B, n_rows, GROUP_W), F32)
    return pl.pallas_call(
        kern,
        grid=(B, nt),
        in_specs=[pl.BlockSpec((None, tm, W), lambda b, i: (b, t0 + i, 0)),
                  pl.BlockSpec((None, 8, W), lambda b, i: (b, jnp.maximum(r8 + i * (tm // 8) - 1, 0), 0)),
                  pl.BlockSpec((None, 8, W), lambda b, i: (b, jnp.minimum(r8 + (i + 1) * (tm // 8), last8), 0)),
                  _full((3, W)), _full((1, W))],
        out_specs=[pl.BlockSpec((None, tm, GROUP_W), lambda b, i: (b, i, 0))] * 2,
        out_shape=[out, out],
        compiler_params=_params("arbitrary", "arbitrary"),
        name="hyena_pre",
    )(hyz, hyz, hyz, conv_w, conv_b.reshape(1, W))


def _filter_kernel(step_ref, cb_ref, sb_ref, w1_ref, b1_ref, w2_ref, b2_ref, w3_ref, fr_ref, dl_ref, h_ref, ss_ref, *,
                   n_pos):
    i = pl.program_id(0)
    tl = h_ref.shape[0]
    row = lax.broadcasted_iota(jnp.int32, (tl, LANES), 0) + i * tl
    pos = jnp.where(row <= n_pos, row, 2 * n_pos - row).astype(F32)
    lane = lax.broadcasted_iota(jnp.int32, (tl, LANES), 1)
    t = pos * (1.0 / (n_pos - 1))
    st = step_ref[...]
    cb, sb = cb_ref[...], sb_ref[...]
    cos_a = st[0:1] * cb - st[3:4] * sb
    sin_a = st[1:2] * cb + st[2:3] * sb
    feat = jnp.where(lane == 0, t, jnp.where(lane < 17, cos_a, jnp.where(lane < HY_EMB, -sin_a, 0.0)))
    fr = fr_ref[...]
    h = jnp.sin(fr * (_dot_hi(feat, w1_ref[...]) + b1_ref[...]))
    h = jnp.sin(fr * (_dot_hi(h, w2_ref[...]) + b2_ref[...]))
    h = _dot_hi(h, w3_ref[...]) * jnp.exp(-t[:, 0:1] * dl_ref[...])
    r1 = row[:, 0:1]
    h = jnp.where(r1 < n_pos, h[:, :GROUP_W], jnp.where(r1 == n_pos, 0.0, h[:, GROUP_W:]))
    h_ref[...] = h

    @pl.when(i == 0)
    def _():
        ss_ref[...] = jnp.zeros_like(ss_ref)

    ss_ref[...] += jnp.sum(h * h, axis=0, keepdims=True)


def _hyena_filter(n_pos, w1, b1, w2, b2, w3, freq):
    tl = min(n_pos, 512)
    n_steps = 2 * n_pos // tl
    bands = (HY_EMB - 1) // 2
    f = np.linspace(1e-4, bands - 1, bands)
    fv = np.zeros((LANES,))
    fv[1:17] = f
    fv[17:33] = f
    step0 = np.arange(n_steps) * tl
    mirrored = step0 >= n_pos
    base = 2.0 * np.pi * np.where(mirrored, 2 * n_pos - step0, step0)[:, None] * fv[None, :] / n_pos
    sign = np.where(mirrored, -1.0, 1.0)[:, None]
    steps = np.zeros((n_steps, 8, LANES))
    steps[:, 0], steps[:, 1] = np.cos(base), np.sin(base)
    steps[:, 2], steps[:, 3] = sign * np.cos(base), sign * np.sin(base)
    inner = 2.0 * np.pi * np.arange(tl)[:, None] * fv[None, :] / n_pos
    w1p = jnp.zeros((LANES, HY_FFN), F32).at[:HY_EMB].set(w1)
    max_decay = math.log(HY_TARGET) / HY_FAST_PCT
    min_decay = math.log(HY_TARGET) / HY_SLOW_PCT
    deltas = np.abs(np.linspace(min_decay, max_decay, GROUP_W)).astype(np.float32)
    dl = jnp.asarray(np.concatenate([deltas, deltas])[None, :])
    kern = functools.partial(_filter_kernel, n_pos=n_pos)
    k2, ss = pl.pallas_call(
        kern,
        grid=(n_steps,),
        in_specs=[pl.BlockSpec((None, 8, LANES), lambda i: (i, 0, 0)), _full((tl, LANES)), _full((tl, LANES)),
                  _full((LANES, HY_FFN)), _full((1, HY_FFN)), _full((HY_FFN, HY_FFN)),
                  _full((1, HY_FFN)), _full((HY_FFN, 2 * GROUP_W)), _full((1, HY_FFN)), _full((1, 2 * GROUP_W))],
        out_specs=[pl.BlockSpec((tl, GROUP_W), lambda i: (i, 0)), _full((1, GROUP_W))],
        out_shape=[jax.ShapeDtypeStruct((2 * n_pos, GROUP_W), F32), jax.ShapeDtypeStruct((1, GROUP_W), F32)],
        compiler_params=_params("arbitrary"),
        name="hyena_filter",
    )(jnp.asarray(steps, F32), jnp.asarray(np.cos(inner), F32), jnp.asarray(np.sin(inner), F32),
      w1p, b1.reshape(1, -1), w2, b2.reshape(1, -1), w3, freq.reshape(1, -1), dl)
    return k2, lax.rsqrt(ss)


def _dft_mats(n, sign):
    j = np.arange(n)
    ang = 2.0 * np.pi * ((j[:, None] * j[None, :]) % n) / n
    return np.cos(ang), sign * np.sin(ang)


def _twiddle(n1, n2, sign):
    ang = 2.0 * np.pi * ((np.arange(n1)[:, None] * np.arange(n2)[None, :]) % (n1 * n2)) / (n1 * n2)
    return np.cos(ang), sign * np.sin(ang)


def _block_complex(re, im):
    return np.block([[re, -im], [im, re]])


def _stage1_kernel(m_ref, zr_ref, zi_ref, ar_ref, ai_ref):
    a = _dot_hi(m_ref[...], jnp.concatenate([zr_ref[...], zi_ref[...]], axis=0))
    half = ar_ref.shape[0]
    ar_ref[...] = a[:half]
    ai_ref[...] = a[half:]


def _stage1_real_kernel(m_ref, zr_ref, ar_ref, ai_ref):
    a = _dot_hi(m_ref[...], zr_ref[...])
    half = ar_ref.shape[0]
    ar_ref[...] = a[:half]
    ai_ref[...] = a[half:]


def _col_tile(cols):
    return 2048 if cols % 2048 == 0 else cols


def _hy_stage1(vg2, n1):
    _, r, cols = vg2.shape
    cr, ci = _dft_mats(n1, -1.0)
    m = jnp.asarray(_block_complex(cr[:, :r], ci[:, :r]), F32)
    tc = _col_tile(cols)
    out = jax.ShapeDtypeStruct((n1, cols), F32)
    return pl.pallas_call(
        _stage1_kernel,
        grid=(cols // tc,),
        in_specs=[_full(m.shape), pl.BlockSpec((None, r, tc), lambda j: (0, 0, j)),
                  pl.BlockSpec((None, r, tc), lambda j: (1, 0, j))],
        out_specs=[pl.BlockSpec((n1, tc), lambda j: (0, j))] * 2,
        out_shape=[out, out],
        compiler_params=_params("arbitrary"),
        name="hyena_fwd_stage1",
    )(m, vg2, vg2)


def _filter_stage1(k2v, n1):
    _, cols = k2v.shape
    cr, ci = _dft_mats(n1, -1.0)
    m = jnp.asarray(np.concatenate([cr, ci], axis=0), F32)
    tc = _col_tile(cols)
    out = jax.ShapeDtypeStruct((n1, cols), F32)
    return pl.pallas_call(
        _stage1_real_kernel,
        grid=(cols // tc,),
        in_specs=[_full(m.shape), pl.BlockSpec((n1, tc), lambda j: (0, j))],
        out_specs=[pl.BlockSpec((n1, tc), lambda j: (0, j))] * 2,
        out_shape=[out, out],
        compiler_params=_params("arbitrary"),
        name="hyena_filter_stage1",
    )(m, k2v)


def _twiddled(f_re, f_im, t_re, t_im):
    g_re = f_re * t_re - f_im * t_im
    g_im = f_re * t_im + f_im * t_re
    return jnp.concatenate([jnp.concatenate([g_re, -g_im], axis=1), jnp.concatenate([g_im, g_re], axis=1)], axis=0)


def _filter_stage2_kernel(ar_ref, ai_ref, twr_ref, twi_ref, fr_ref, fi_ref, sc_ref, kr_ref, ki_ref):
    g = _twiddled(fr_ref[...], fi_ref[...], twr_ref[...], twi_ref[...])
    x = _dot_hi(g, jnp.concatenate([ar_ref[...], ai_ref[...]], axis=0)) * sc_ref[...]
    n2 = kr_ref.shape[0]
    kr_ref[...] = x[:n2]
    ki_ref[...] = x[n2:]


def _filter_stage2(ar, ai, inv_norm, n1):
    n2 = DFT_N2
    c = ar.shape[1] // n2
    fr, fi = _dft_mats(n2, -1.0)
    twr, twi = _twiddle(n1, n2, -1.0)
    blk = pl.BlockSpec((None, n2, c), lambda k: (k, 0, 0))
    tw = pl.BlockSpec((None, 1, n2), lambda k: (k, 0, 0))
    out = jax.ShapeDtypeStruct((n1, n2, c), F32)
    return pl.pallas_call(
        _filter_stage2_kernel,
        grid=(n1,),
        in_specs=[blk, blk, tw, tw, _full((n2, n2)), _full((n2, n2)), _full((1, c))],
        out_specs=[blk, blk],
        out_shape=[out, out],
        compiler_params=_params("arbitrary"),
        name="hyena_filter_stage2",
    )(ar.reshape(n1, n2, c), ai.reshape(n1, n2, c), jnp.asarray(twr.reshape(n1, 1, n2), F32),
      jnp.asarray(twi.reshape(n1, 1, n2), F32), jnp.asarray(fr, F32), jnp.asarray(fi, F32), inv_norm)


def _hy_stage2_kernel(ar_ref, ai_ref, kr_ref, ki_ref, twr_ref, twi_ref, tcr_ref, tci_ref, fr_ref, fi_ref,
                      br_ref, bi_ref):
    f_re, f_im = fr_ref[...], fi_ref[...]
    n2 = f_re.shape[0]
    g = _twiddled(f_re, f_im, twr_ref[...], twi_ref[...])
    x = _dot_hi(g, jnp.concatenate([ar_ref[...], ai_ref[...]], axis=0))
    x_re, x_im = x[:n2], x[n2:]
    k_re, k_im = kr_ref[...], ki_ref[...]
    y = jnp.concatenate([x_re * k_re - x_im * k_im, x_re * k_im + x_im * k_re], axis=0)
    g_inv = _twiddled(f_re, -f_im, tcr_ref[...], -tci_ref[...])
    b = _dot_hi(g_inv, y)
    br_ref[...] = b[:n2]
    bi_ref[...] = b[n2:]


def _hy_stage2(ar, ai, kr, ki, n1):
    n2 = DFT_N2
    c = kr.shape[2]
    fr, fi = _dft_mats(n2, -1.0)
    twr, twi = _twiddle(n1, n2, -1.0)
    blk = pl.BlockSpec((None, n2, c), lambda k: (k, 0, 0))
    tw = pl.BlockSpec((None, 1, n2), lambda k: (k, 0, 0))
    twc = pl.BlockSpec((None, n2, 1), lambda k: (k, 0, 0))
    out = jax.ShapeDtypeStruct((n1, n2, c), F32)
    return pl.pallas_call(
        _hy_stage2_kernel,
        grid=(n1,),
        in_specs=[blk, blk, blk, blk, tw, tw, twc, twc, _full((n2, n2)), _full((n2, n2))],
        out_specs=[blk, blk],
        out_shape=[out, out],
        compiler_params=_params("arbitrary"),
        name="hyena_conv_stage2",
    )(ar.reshape(n1, n2, c), ai.reshape(n1, n2, c), kr, ki,
      jnp.asarray(twr.reshape(n1, 1, n2), F32), jnp.asarray(twi.reshape(n1, 1, n2), F32),
      jnp.asarray(twr.reshape(n1, n2, 1), F32), jnp.asarray(twi.reshape(n1, n2, 1), F32),
      jnp.asarray(fr, F32), jnp.asarray(fi, F32))


def _hy_stage3_kernel(m_ref, br_ref, bi_ref, vg0_ref, vg1_ref, x00_ref, x01_ref, bias_ref, o_ref):
    conv = _dot_hi(m_ref[...], jnp.concatenate([br_ref[...], bi_ref[...]], axis=0))
    r = vg0_ref.shape[0]
    bias = bias_ref[...]
    o_ref[0, :r] = (conv[:r] + vg0_ref[...] * bias) * x00_ref[...]
    o_ref[1, :r] = (conv[r:] + vg1_ref[...] * bias) * x01_ref[...]
    if o_ref.shape[1] > r:
        o_ref[:, r:] = jnp.zeros((2, o_ref.shape[1] - r, o_ref.shape[2]), F32)


def _hy_stage3(br, bi, vg2, x02, bias_cols, n1, rows_total):
    _, r, cols = vg2.shape
    cr, ci = _dft_mats(n1, 1.0)
    m = jnp.asarray(_block_complex(cr[:r], ci[:r]) / (n1 * DFT_N2), F32)
    tc = _col_tile(cols)
    plane = lambda p: pl.BlockSpec((None, r, tc), lambda j: (p, 0, j))
    return pl.pallas_call(
        _hy_stage3_kernel,
        grid=(cols // tc,),
        in_specs=[_full(m.shape), pl.BlockSpec((n1, tc), lambda j: (0, j)), pl.BlockSpec((n1, tc), lambda j: (0, j)),
                  plane(0), plane(1), plane(0), plane(1), pl.BlockSpec((1, tc), lambda j: (0, j))],
        out_specs=pl.BlockSpec((2, rows_total, tc), lambda j: (0, 0, j)),
        out_shape=jax.ShapeDtypeStruct((2, rows_total, cols), F32),
        compiler_params=_params("arbitrary"),
        name="hyena_inv_stage3",
    )(m, br, bi, vg2, vg2, x02, x02, bias_cols)


def _hyena_long(vg, x0, k2, inv_norm, bias, n_all):
    B, L, C = vg.shape
    assert B == 2
    n2 = DFT_N2
    n1 = 2 * L // n2
    r = L // n2
    fa_r, fa_i = _filter_stage1(k2.reshape(n1, n2 * C), n1)
    kr, ki = _filter_stage2(fa_r, fa_i, inv_norm, n1)
    vg2 = vg.reshape(2, r, n2 * C)
    ar, ai = _hy_stage1(vg2, n1)
    br, bi = _hy_stage2(ar, ai, kr, ki, n1)
    y = _hy_stage3(br.reshape(n1, n2 * C), bi.reshape(n1, n2 * C), vg2, x0.reshape(2, r, n2 * C),
                   jnp.tile(bias.reshape(1, C), (1, n2)), n1, n_all // n2)
    return y.reshape(2, n_all, C)


def _hy_ctx_kernel(vg_ref, x0_ref, kext_ref, sc_ref, bias_ref, buf_ref, o_ref):
    del buf_ref
    n = vg_ref.shape[0]

    def body(a, acc):
        src = vg_ref[pl.ds(pl.multiple_of(8 * a, 8), 8), :]
        win = pl.multiple_of(n - 8 * a, 8)
        for r in range(8):
            acc = acc + kext_ref[r, pl.ds(win, n), :] * src[r:r + 1]
        return acc

    acc = lax.fori_loop(0, n // 8, body, jnp.zeros(vg_ref.shape, F32))
    o_ref[...] = (acc * sc_ref[...] + vg_ref[...] * bias_ref[...]) * x0_ref[...]


def _hyena_ctx(vg, x0, k2, inv_norm, bias, buf, row0):
    B, n, C = vg.shape
    kext = jnp.concatenate([k2[n:], k2[:n]], axis=0)
    kext = jnp.stack([jnp.roll(kext, r, axis=0) for r in range(8)])
    blk = pl.BlockSpec((None, n, C), lambda b: (b, 0, 0))
    return pl.pallas_call(
        _hy_ctx_kernel,
        grid=(B,),
        in_specs=[blk, blk, _full((8, 2 * n, C)), _full((1, C)), _full((1, C)), pl.BlockSpec(memory_space=pl.ANY)],
        out_specs=pl.BlockSpec((None, n, C), lambda b: (b, row0 // n, 0)),
        out_shape=jax.ShapeDtypeStruct(buf.shape, F32),
        input_output_aliases={5: 0},
        compiler_params=_params("arbitrary"),
        name="hyena_ctx",
    )(vg, x0, kext, inv_norm, bias.reshape(1, C), buf)


def _fn_stage1_kernel(c_ref, s_ref, re_ref, im_ref, ar_ref, ai_ref):
    c, s, re, im = c_ref[...], s_ref[...], re_ref[...], im_ref[...]
    ar_ref[...] = _dot_hi(c, re) + _dot_hi(s, im)
    ai_ref[...] = _dot_hi(c, im) - _dot_hi(s, re)


def _fn_stage2_kernel(ar_ref, ai_ref, twr_ref, twi_ref, fr_ref, fi_ref, o_ref):
    f_re, f_im, t_re, t_im = fr_ref[...], fi_ref[...], twr_ref[...], twi_ref[...]
    g_re = f_re * t_re - f_im * t_im
    g_im = f_re * t_im + f_im * t_re
    n2 = f_re.shape[0]
    o_ref[:n2] = _dot_hi(g_re, ar_ref[...]) - _dot_hi(g_im, ai_ref[...])
    if o_ref.shape[0] > n2:
        o_ref[n2:] = jnp.zeros((o_ref.shape[0] - n2, o_ref.shape[1]), F32)


def _fnet_long(fre, fim, n_lat):
    B, S, C = fre.shape
    n2 = DFT_N2
    n1 = n_lat // n2
    cols = n2 * C
    c1, s1 = _dft_mats(n1, 1.0)
    tc = _col_tile(cols)
    rows = pl.BlockSpec((None, n1, tc), lambda b, j: (b, 0, j))
    a_shape = jax.ShapeDtypeStruct((B, n1, cols), F32)
    ar, ai = pl.pallas_call(
        _fn_stage1_kernel,
        grid=(B, cols // tc),
        in_specs=[_full((n1, n1)), _full((n1, n1)), rows, rows],
        out_specs=[rows, rows],
        out_shape=[a_shape, a_shape],
        compiler_params=_params("arbitrary", "arbitrary"),
        name="fnet_stage1",
    )(jnp.asarray(c1, F32), jnp.asarray(s1, F32), fre.reshape(B, S // n2, cols), fim.reshape(B, S // n2, cols))
    fr, fi = _dft_mats(n2, -1.0)
    scale = 1.0 / math.sqrt(n_lat * C)
    twr, twi = _twiddle(n1, n2, -1.0)
    blk = pl.BlockSpec((None, None, n2, C), lambda b, k: (b, k, 0, 0))
    tw = pl.BlockSpec((None, 1, n2), lambda b, k: (k, 0, 0))
    y = pl.pallas_call(
        _fn_stage2_kernel,
        grid=(B, n1),
        in_specs=[blk, blk, tw, tw, _full((n2, n2)), _full((n2, n2))],
        out_specs=pl.BlockSpec((None, S // n1, C), lambda b, k: (b, 0, k)),
        out_shape=jax.ShapeDtypeStruct((B, S // n1, n1 * C), F32),
        compiler_params=_params("arbitrary", "arbitrary"),
        name="fnet_stage2",
    )(ar.reshape(B, n1, n2, C), ai.reshape(B, n1, n2, C), jnp.asarray(twr.reshape(n1, 1, n2), F32),
      jnp.asarray(twi.reshape(n1, 1, n2), F32), jnp.asarray(fr * scale, F32), jnp.asarray(fi * scale, F32))
    return y.reshape(B, S, C)


def _fn_ctx_kernel(c_ref, s_ref, re_ref, im_ref, buf_ref, o_ref):
    del buf_ref
    o_ref[...] = _dot_hi(c_ref[...], re_ref[...]) + _dot_hi(s_ref[...], im_ref[...])


def _fnet_ctx(fre, fim, n_lat, n_ctx, buf):
    B, S, C = fre.shape
    c1, s1 = _dft_mats(n_ctx, 1.0)
    scale = 1.0 / math.sqrt(n_ctx * C)
    blk = pl.BlockSpec((None, n_ctx, C), lambda b: (b, n_lat // n_ctx, 0))
    return pl.pallas_call(
        _fn_ctx_kernel,
        grid=(B,),
        in_specs=[_full((n_ctx, n_ctx)), _full((n_ctx, n_ctx)), blk, blk, pl.BlockSpec(memory_space=pl.ANY)],
        out_specs=blk,
        out_shape=jax.ShapeDtypeStruct((B, S, C), F32),
        input_output_aliases={4: 0},
        compiler_params=_params("arbitrary"),
        name="fnet_ctx",
    )(jnp.asarray(c1 * scale, F32), jnp.asarray(s1 * scale, F32), fre, fim, buf)


def _gla_kernel(qkf_ref, vf_ref, gf_ref, qkb_ref, vb_ref, gb_ref, of_ref, ob_ref, st_ref):
    n = pl.program_id(0)
    n_b, tm = qkf_ref.shape[0], qkf_ref.shape[1]
    ck = GLA_CHUNK

    @pl.when(n == 0)
    def _():
        st_ref[...] = jnp.zeros_like(st_ref)

    ri = lax.broadcasted_iota(jnp.int32, (ck, ck), 0)
    ci = lax.broadcasted_iota(jnp.int32, (ck, ck), 1)

    def sub_chunk(qk_ref, v_ref, g_ref, o_ref, states, r0, reverse):
        keep = (ci >= ri) if reverse else (ci <= ri)
        g = g_ref[r0:r0 + ck, :]
        b = _dot_hi(keep.astype(F32), g)
        b_end = b[0:1] if reverse else b[ck - 1:ck]
        q = qk_ref[r0:r0 + ck, :LANES] * jnp.exp(b)
        k = qk_ref[r0:r0 + ck, LANES:]
        k_in = k * jnp.exp(-b)
        k_out = k * jnp.exp(b_end - b)
        decay = jnp.exp(b_end)
        new_states = []
        for hd in range(GLA_HEADS):
            ks = slice(hd * GLA_DK, (hd + 1) * GLA_DK)
            vs = slice(hd * GLA_DV, (hd + 1) * GLA_DV)
            qh, vh = q[:, ks].astype(BF16), v_ref[r0:r0 + ck, vs].astype(BF16)
            a = lax.dot_general(qh, k_in[:, ks].astype(BF16), (((1,), (1,)), ((), ())), preferred_element_type=F32)
            a = jnp.where(keep, a, 0.0)
            st = states[hd]
            o = jnp.dot(a.astype(BF16), vh, preferred_element_type=F32)
            o += lax.dot_general(qh, st.astype(BF16), (((1,), (1,)), ((), ())), preferred_element_type=F32)
            o_ref[r0:r0 + ck, vs] = o
            kv = lax.dot_general(vh, k_out[:, ks].astype(BF16), (((0,), (0,)), ((), ())), preferred_element_type=F32)
            new_states.append(st * decay[:, ks] + kv)
        return new_states

    st_f = [[st_ref[0, b, hd] for hd in range(GLA_HEADS)] for b in range(n_b)]
    st_b = [[st_ref[1, b, hd] for hd in range(GLA_HEADS)] for b in range(n_b)]
    n_sub = tm // ck
    for i in range(n_sub):
        for b in range(n_b):
            st_f[b] = sub_chunk(qkf_ref.at[b], vf_ref.at[b], gf_ref.at[b], of_ref.at[b], st_f[b], i * ck, False)
            st_b[b] = sub_chunk(qkb_ref.at[b], vb_ref.at[b], gb_ref.at[b], ob_ref.at[b], st_b[b],
                                (n_sub - 1 - i) * ck, True)
    for b in range(n_b):
        for hd in range(GLA_HEADS):
            st_ref[0, b, hd] = st_f[b][hd]
            st_ref[1, b, hd] = st_b[b][hd]


def _gla(gqk, gv, gg, n_lat):
    B, S, _ = gqk.shape
    tm = TOK_TILE
    nl = n_lat // tm
    nt = S // tm
    assert nt == nl + 1
    fwd = lambda n: jnp.where(n == 0, nl, n - 1)
    bwd = lambda n: jnp.where(n == 0, nl, nl - n)
    out = jax.ShapeDtypeStruct((B, S, 256), F32)
    return pl.pallas_call(
        _gla_kernel,
        grid=(nt,),
        in_specs=[pl.BlockSpec((B, tm, 256), lambda n: (0, fwd(n), 0)),
                  pl.BlockSpec((B, tm, 256), lambda n: (0, fwd(n), 0)),
                  pl.BlockSpec((B, tm, 128), lambda n: (0, fwd(n), 0)),
                  pl.BlockSpec((B, tm, 256), lambda n: (0, bwd(n), 0)),
                  pl.BlockSpec((B, tm, 256), lambda n: (0, bwd(n), 0)),
                  pl.BlockSpec((B, tm, 128), lambda n: (0, bwd(n), 1))],
        out_specs=[pl.BlockSpec((B, tm, 256), lambda n: (0, fwd(n), 0)),
                   pl.BlockSpec((B, tm, 256), lambda n: (0, bwd(n), 0))],
        out_shape=[out, out],
        scratch_shapes=[pltpu.VMEM((2, B, GLA_HEADS, GLA_DV, GLA_DK), F32)],
        compiler_params=_params("arbitrary"),
        name="gla_scan",
    )(gqk, gv, gg, gqk, gv, gg)


def _mix_kernel(x_ref, mod_ref, a_ref, hy_ref, of_ref, ob_ref, og_ref, fy_ref, fw_ref, mg_ref, hm_ref, wo_ref,
                n2g_ref, rwh_ref, rwl_ref, rb_ref, xo_ref, h2_ref, ti_ref, tg_ref, rk_ref, cnt_ref, run_ref):
    first = jnp.logical_and(pl.program_id(0) == 0, pl.program_id(1) == 0)

    @pl.when(first)
    def _():
        run_ref[...] = jnp.zeros_like(run_ref)

    mod = mod_ref[...]
    mg = mg_ref[...]
    hm = hm_ref[...]

    def head_rms(t):
        sq_hi, sq_lo = _split(t * t)
        ms = jnp.dot(sq_hi, hm, preferred_element_type=F32) + jnp.dot(sq_lo, hm, preferred_element_type=F32)
        return t * lax.rsqrt(ms + EPS)

    a = head_rms(a_ref[...]) * mg[:, 0:256]
    hy = _rms_rows(hy_ref[...]) * mg[:, 256:512]
    og = og_ref[...]
    o = head_rms(of_ref[...] + ob_ref[...]) * mg[:, 512:768] * (og / (1.0 + jnp.exp(-og)))
    fn = _rms_rows(_dot(fy_ref[...], fw_ref[...])) * mg[:, 768:1024]
    wo = wo_ref[...]
    y = _dot(a, wo[0:256]) + _dot(hy, wo[256:512]) + _dot(o, wo[512:768]) + _dot(fn, wo[768:1024])
    x = x_ref[...] + mod[2:3] * y
    xo_ref[...] = x

    h2 = _rms_rows(x) * n2g_ref[...] * (1.0 + mod[4:5]) + mod[3:4]
    h2_ref[...] = _pack_pairs(h2)

    logits = _dot_x3(h2, rwh_ref[...], rwl_ref[...]) + rb_ref[...]
    tm = logits.shape[0]
    lane = lax.broadcasted_iota(jnp.int32, logits.shape, 1).astype(F32)
    idx_out = jnp.zeros(logits.shape, F32)
    val_out = jnp.zeros(logits.shape, F32)
    chosen = jnp.zeros(logits.shape, F32)
    picks = []
    top = None
    den = jnp.zeros((tm, 1), F32)
    for kk in range(TOP_K):
        m = jnp.max(logits, axis=-1, keepdims=True)
        idx = jnp.min(jnp.where(logits == m, lane, float(LANES)), axis=-1, keepdims=True)
        if top is None:
            top = m
        e = jnp.exp(m - top)
        den = den + e
        hit = lane == idx
        picks.append(hit)
        chosen = jnp.where(hit, 1.0, chosen)
        idx_out = jnp.where(lane == kk, idx, idx_out)
        val_out = jnp.where(lane == kk, e, val_out)
        logits = jnp.where(hit, -jnp.inf, logits)
    ti_ref[...] = idx_out.astype(jnp.int32)
    tg_ref[...] = val_out / den

    ri = lax.broadcasted_iota(jnp.int32, (tm, tm), 0)
    ci = lax.broadcasted_iota(jnp.int32, (tm, tm), 1)
    before = _dot((ci < ri).astype(F32), chosen) + run_ref[...]
    rank = jnp.zeros(logits.shape, F32)
    for kk in range(TOP_K):
        r = jnp.sum(jnp.where(picks[kk], before, 0.0), axis=-1, keepdims=True)
        rank = jnp.where(lane == kk, r, rank)
    rk_ref[...] = rank.astype(jnp.int32)
    run_ref[...] += jnp.sum(chosen, axis=0, keepdims=True)
    cnt_ref[...] = run_ref[...]


def _mix(xs, mod, a, hy, o_f, o_b, og, fy, fnet_w, mix_g, head_mean, w_out, n2g, rw, rb, n_lat_tiles):
    B, S, D = xs.shape
    tm = TOK_TILE
    rwh, rwl = _split(rw)
    tok = lambda w: pl.BlockSpec((None, tm, w), lambda b, t: (b, t, 0))
    return pl.pallas_call(
        _mix_kernel,
        grid=(B, S // tm),
        in_specs=[tok(D),
                  pl.BlockSpec((None, 6, D), lambda b, t: (jnp.where(t >= n_lat_tiles, 2, b), 0, 0)),
                  tok(256), tok(256), tok(256), tok(256), tok(256), tok(256),
                  _full((256, 256)), _full((1, D)), _full((256, 256)), _full((D, D)), _full((1, D)),
                  _full((D, LANES)), _full((D, LANES)), _full((1, LANES))],
        out_specs=[tok(D), tok(D // 2), tok(LANES), tok(LANES), tok(LANES), _full((1, LANES))],
        out_shape=[jax.ShapeDtypeStruct((B, S, D), F32), jax.ShapeDtypeStruct((B, S, D // 2), jnp.uint32),
                   jax.ShapeDtypeStruct((B, S, LANES), jnp.int32), jax.ShapeDtypeStruct((B, S, LANES), F32),
                   jax.ShapeDtypeStruct((B, S, LANES), jnp.int32), jax.ShapeDtypeStruct((1, LANES), F32)],
        scratch_shapes=[pltpu.VMEM((1, LANES), F32)],
        compiler_params=_params("arbitrary", "arbitrary"),
        name="mix_out_router",
    )(xs, mod, a, hy, o_f, o_b, og, fy, fnet_w, mix_g, head_mean.astype(BF16), w_out, n2g, rwh, rwl, rb)


def _route(top_i, rank, counts):
    T = top_i.shape[0]
    bm = MOE_BM
    padded = (counts + bm - 1) // bm * bm
    pend = jnp.cumsum(padded)
    pstart = pend - padded
    pos = (pstart[top_i] + rank).astype(jnp.int32)
    n_blocks = (T * TOP_K + bm - 1) // bm + N_EXPERTS
    blk_row0 = jnp.arange(n_blocks, dtype=pend.dtype) * bm
    blk_e = jnp.minimum(jnp.sum(pend[None, :] <= blk_row0[:, None], axis=1), N_EXPERTS - 1).astype(jnp.int32)
    n_used = (pend[-1] // bm).astype(jnp.int32).reshape(1)
    return pos, blk_e, n_used, n_blocks * bm


def _dispatch_kernel(pos_ref, h_ref, zero_ref, xg_ref, sem):
    del zero_ref
    tm = h_ref.shape[0]

    def copy(t, kk):
        return pltpu.make_async_copy(h_ref.at[pl.ds(t, 1)], xg_ref.at[pl.ds(pos_ref[0, t * TOP_K + kk], 1)], sem)

    def start(t, c):
        for kk in range(TOP_K):
            copy(t, kk).start()
        return c

    def wait(t, c):
        for kk in range(TOP_K):
            copy(t, kk).wait()
        return c

    lax.fori_loop(0, tm, start, 0, unroll=8)
    lax.fori_loop(0, tm, wait, 0, unroll=8)


def _dispatch(h2w, pos, n_rows):
    T, W = h2w.shape
    tm = TOK_TILE
    nb = T // tm
    return pl.pallas_call(
        _dispatch_kernel,
        grid=(nb,),
        in_specs=[pl.BlockSpec((None, 1, tm * TOP_K), lambda i: (i, 0, 0), memory_space=pltpu.SMEM),
                  pl.BlockSpec((tm, W), lambda i: (i, 0)),
                  pl.BlockSpec(memory_space=pl.ANY)],
        out_specs=pl.BlockSpec(memory_space=pl.ANY),
        out_shape=jax.ShapeDtypeStruct((n_rows, W), h2w.dtype),
        scratch_shapes=[pltpu.SemaphoreType.DMA(())],
        input_output_aliases={2: 0},
        compiler_params=_params("arbitrary"),
        name="moe_dispatch",
    )(pos.reshape(nb, 1, tm * TOP_K), h2w, jnp.zeros((n_rows, W), h2w.dtype))


def _gu_prep_kernel(w_ref, p_ref, g_ref, u_ref):
    y = jnp.dot(w_ref[...].astype(BF16), p_ref[...], preferred_element_type=F32)
    half = y.shape[1] // 2
    g_ref[...] = y[:, :half].astype(BF16)
    u_ref[...] = y[:, half:].astype(BF16)


def _cast_kernel(w_ref, o_ref):
    o_ref[...] = w_ref[...].astype(BF16)


def _down_prep(w_down, layer):
    _, E, F, D = w_down.shape
    return pl.pallas_call(
        _cast_kernel,
        grid=(E,),
        in_specs=[pl.BlockSpec((None, None, F, D), lambda e: (layer, e, 0, 0))],
        out_specs=pl.BlockSpec((None, F, D), lambda e: (e, 0, 0)),
        out_shape=jax.ShapeDtypeStruct((E, F, D), BF16),
        compiler_params=_params("arbitrary"),
        name="moe_down_prep",
    )(w_down)


def _gu_prep(w_gu, layer):
    _, E, D, F2 = w_gu.shape
    tn = 512
    perm = np.zeros((tn, tn), np.float32)
    perm[2 * np.arange(tn // 2), np.arange(tn // 2)] = 1.0
    perm[2 * np.arange(tn // 2) + 1, tn // 2 + np.arange(tn // 2)] = 1.0
    out = jax.ShapeDtypeStruct((E, D, F2 // 2), BF16)
    return pl.pallas_call(
        _gu_prep_kernel,
        grid=(E, F2 // tn),
        in_specs=[pl.BlockSpec((None, None, D, tn), lambda e, j: (layer, e, 0, j)), _full((tn, tn))],
        out_specs=[pl.BlockSpec((None, D, tn // 2), lambda e, j: (e, 0, j))] * 2,
        out_shape=[out, out],
        compiler_params=_params("arbitrary", "arbitrary"),
        name="moe_weight_prep",
    )(w_gu, jnp.asarray(perm, BF16))


def _expert_kernel(be_ref, nu_ref, x_ref, wg_ref, wu_ref, bg_ref, bu_ref, wd_ref, bd_ref, o_ref):
    i = pl.program_id(0)

    @pl.when(i < nu_ref[0])
    def _():
        x = jnp.concatenate(_unpack_pairs(x_ref[...]), axis=1).astype(BF16)
        gate = jnp.minimum(jnp.dot(x, wg_ref[...], preferred_element_type=F32) + bg_ref[...], SWIGLU_LIMIT)
        up = jnp.clip(jnp.dot(x, wu_ref[...], preferred_element_type=F32) + bu_ref[...], -SWIGLU_LIMIT, SWIGLU_LIMIT)
        glu = gate / (1.0 + jnp.exp(-gate * SWIGLU_ALPHA))
        o_ref[...] = _pack_pairs(_dot((up + 1.0) * glu, wd_ref[...]) + bd_ref[...])

    @pl.when(i >= nu_ref[0])
    def _():
        o_ref[...] = jnp.zeros_like(o_ref)


def _experts(xg, blk_e, n_used, wg, wu, bg, bu, wd, bd):
    n_rows, W = xg.shape
    bm = MOE_BM
    D, F = wg.shape[1], wg.shape[2]
    wsel = lambda r, c: pl.BlockSpec((None, r, c), lambda i, be, nu: (be[i], 0, 0))
    return pl.pallas_call(
        _expert_kernel,
        grid_spec=pltpu.PrefetchScalarGridSpec(
            num_scalar_prefetch=2,
            grid=(n_rows // bm,),
            in_specs=[pl.BlockSpec((bm, W), lambda i, be, nu: (i, 0)),
                      wsel(D, F), wsel(D, F), wsel(1, F), wsel(1, F), wsel(F, D), wsel(1, D)],
            out_specs=pl.BlockSpec((bm, W), lambda i, be, nu: (i, 0)),
        ),
        out_shape=jax.ShapeDtypeStruct((n_rows, W), jnp.uint32),
        compiler_params=_params("arbitrary"),
        name="moe_experts",
    )(blk_e, n_used, xg, wg, wu, bg, bu, wd, bd)


def _combine_kernel(pos_ref, posn_ref, x_ref, mod_ref, tg_ref, fg_ref, y_ref, o_ref, buf, sem, *, final):
    i = pl.program_id(0)
    tm = x_ref.shape[0]
    slot = i % 2

    def copy(p_ref, s, t, kk):
        return pltpu.make_async_copy(y_ref.at[pl.ds(p_ref[0, t * TOP_K + kk], 1)], buf.at[s, kk, pl.ds(t, 1)],
                                     sem.at[s])

    def fetch(p_ref, s):
        def body(t, c):
            for kk in range(TOP_K):
                copy(p_ref, s, t, kk).start()
            return c
        lax.fori_loop(0, tm, body, 0, unroll=8)

    @pl.when(i == 0)
    def _():
        fetch(pos_ref, 0)

    @pl.when(i + 1 < pl.num_programs(0))
    def _():
        fetch(posn_ref, 1 - slot)

    def wait(t, c):
        for kk in range(TOP_K):
            copy(pos_ref, slot, t, kk).wait()
        return c

    lax.fori_loop(0, tm, wait, 0, unroll=8)
    tg = tg_ref[...]
    f_lo = jnp.zeros((tm, x_ref.shape[1] // 2), F32)
    f_hi = f_lo
    for kk in range(TOP_K):
        y_lo, y_hi = _unpack_pairs(buf[slot, kk])
        f_lo = f_lo + tg[:, kk:kk + 1] * y_lo
        f_hi = f_hi + tg[:, kk:kk + 1] * y_hi
    x = x_ref[...] + mod_ref[...][5:6] * jnp.concatenate([f_lo, f_hi], axis=1)
    o_ref[...] = _rms_rows(x) * fg_ref[...] if final else x


def _combine(xs, mod, tg, yb, pos, n_lat_tiles, final_g=None):
    B, S, D = xs.shape
    tm = TOK_TILE
    nt = S // tm
    final = final_g is not None
    per_b = n_lat_tiles if final else nt
    nb = B * per_b
    tile = lambda i: (i // per_b) * nt + i % per_b
    pos3 = pos.reshape(B * nt, 1, tm * TOP_K)
    row = lambda w: pl.BlockSpec((tm, w), lambda i: (tile(i), 0))
    fg = final_g.reshape(1, D) if final else jnp.ones((1, D), F32)
    out = pl.pallas_call(
        functools.partial(_combine_kernel, final=final),
        grid=(nb,),
        in_specs=[pl.BlockSpec((None, 1, tm * TOP_K), lambda i: (tile(i), 0, 0), memory_space=pltpu.SMEM),
                  pl.BlockSpec((None, 1, tm * TOP_K), lambda i: (tile(jnp.minimum(i + 1, nb - 1)), 0, 0),
                               memory_space=pltpu.SMEM),
                  row(D),
                  pl.BlockSpec((None, 6, D),
                               lambda i: (jnp.where(i % per_b >= n_lat_tiles, 2, i // per_b), 0, 0)),
                  row(LANES),
                  _full((1, D)),
                  pl.BlockSpec(memory_space=pl.ANY)],
        out_specs=pl.BlockSpec((tm, D), lambda i: (i, 0)),
        out_shape=jax.ShapeDtypeStruct((nb * tm, D), F32),
        scratch_shapes=[pltpu.VMEM((2, TOP_K, tm, D // 2), jnp.uint32), pltpu.SemaphoreType.DMA((2,))],
        compiler_params=_params("arbitrary"),
        name="moe_combine",
    )(pos3, pos3, xs.reshape(B * S, D), mod, tg.reshape(B * S, LANES), fg, yb)
    return out.reshape(B, nb // B * tm, D)


def _prep_w_in(w_in):
    cq, ckv, kr = w_in[:, 0:256], w_in[:, 256:384], w_in[:, 384:416]
    hy, gq, gk = w_in[:, 416:1184], w_in[:, 1184:1312], w_in[:, 1312:1440]
    gv, glr, og, fn = w_in[:, 1440:1696], w_in[:, 1696:1728], w_in[:, 1728:1984], w_in[:, 1984:2240]
    half = MLA_ROPE // 2
    kr_sw = jnp.concatenate([-kr[:, half:], kr[:, :half]], axis=1)
    place = lambda w: jnp.tile(jnp.pad(w, ((0, 0), (MLA_NOPE, LANES - MLA_NOPE - MLA_ROPE))), (1, MLA_HEADS))
    glr_p = jnp.pad(glr, ((0, 0), (0, LANES - 2 * GLA_LR)))
    return jnp.concatenate([cq, ckv, place(kr), place(kr_sw), hy, gq, gk, gv, glr_p, og, fn], axis=1).astype(BF16)


def _prep_wq(w_uq):
    w = w_uq.reshape(MLA_Q_LORA, MLA_HEADS, MLA_NOPE + MLA_ROPE)
    nope, rope = w[..., :MLA_NOPE], w[..., MLA_NOPE:]
    half = MLA_ROPE // 2
    z_tail = jnp.zeros((MLA_Q_LORA, MLA_HEADS, LANES - MLA_NOPE - MLA_ROPE), F32)
    z_nope = jnp.zeros((MLA_Q_LORA, MLA_HEADS, MLA_NOPE), F32)
    plain = jnp.concatenate([nope, rope, z_tail], axis=-1).reshape(MLA_Q_LORA, MLA_HEADS * LANES)
    partner = jnp.concatenate([z_nope, -rope[..., half:], rope[..., :half], z_tail], axis=-1)
    return jnp.concatenate([plain, partner.reshape(MLA_Q_LORA, MLA_HEADS * LANES)], axis=1).astype(BF16)


def _prep_wkv(w_ukv):
    w = w_ukv.reshape(MLA_KV_LORA, MLA_HEADS, MLA_NOPE + MLA_V)
    k_nope, v = w[..., :MLA_NOPE], w[..., MLA_NOPE:]
    k_placed = jnp.pad(k_nope, ((0, 0), (0, 0), (0, LANES - MLA_NOPE))).reshape(MLA_KV_LORA, MLA_HEADS * LANES)
    v_t = jnp.pad(v, ((0, 0), (0, 0), (0, LANES - MLA_V))).reshape(MLA_KV_LORA, MLA_HEADS * LANES).T
    return k_placed.astype(BF16), v_t.astype(BF16)


def _prep_gk(gk_w, gk_b):
    w = jnp.zeros((LANES, 2 * LANES), F32)
    w = w.at[0:GLA_LR, 0:LANES].set(gk_w[0]).at[GLA_LR:2 * GLA_LR, LANES:].set(gk_w[1])
    return w.astype(BF16), jnp.concatenate([gk_b[0], gk_b[1]]).reshape(1, 2 * LANES)


def _rope_tables(n_lat, n_ctx):
    rows = n_lat // GRID_W
    row = jnp.repeat(jnp.arange(rows, dtype=F32), GRID_W)
    col = jnp.tile(jnp.arange(GRID_W, dtype=F32), rows)
    n_freq = MLA_ROPE // 4
    inv = ROPE_THETA ** (-jnp.arange(n_freq, dtype=F32) / n_freq)
    ang = jnp.concatenate([row[:, None] * inv, col[:, None] * inv], axis=-1)
    cos = jnp.concatenate([jnp.cos(ang), jnp.ones((n_ctx, MLA_ROPE // 2), F32)], axis=0)
    sin = jnp.concatenate([jnp.sin(ang), jnp.zeros((n_ctx, MLA_ROPE // 2), F32)], axis=0)
    S = n_lat + n_ctx
    ones, zeros = jnp.ones((S, MLA_NOPE), F32), jnp.zeros((S, MLA_NOPE), F32)
    tail = jnp.zeros((S, LANES - MLA_NOPE - MLA_ROPE), F32)
    q_scale = MLA_SCALE * math.log2(math.e)
    cq = jnp.concatenate([ones, cos, cos, tail], axis=1) * q_scale
    sq = jnp.concatenate([zeros, sin, sin, tail], axis=1) * q_scale
    ck = jnp.concatenate([zeros, cos, cos, tail], axis=1)
    sk = jnp.concatenate([zeros, sin, sin, tail], axis=1)
    return cq, sq, ck, sk


def kernel(x, c, ctx, c_ctx, ada_w, ada_b, norm1_g, norm2_g, w_in, mla_q_g, mla_w_uq, mla_kv_g, mla_w_ukv, hy_conv_w, hy_conv_b, hy_w1, hy_b1, hy_w2, hy_b2, hy_w3, hy_freq, hy_bias, gla_gk_w, gla_gk_b, fnet_w, mix_g, w_out, router_w, router_b, moe_w_gu, moe_b_gu, moe_w_down, moe_b_down, final_g):
    B, L, D = x.shape
    Lc = ctx.shape[1]
    depth = ada_w.shape[0]
    S = L + Lc
    tm = TOK_TILE
    nlt = L // tm
    assert B == 2 and D == D_MODEL and Lc == tm and L % (2 * DFT_N2 * 8) == 0

    xs = jnp.concatenate([x, ctx], axis=1)
    cc = jnp.zeros((8, D), F32).at[0:B].set(c).at[B].set(c_ctx)
    mods = _mods(cc, ada_w, ada_b)
    tabs = _rope_tables(L, Lc)
    jc = np.arange(GROUP_W)
    ang_c = 2.0 * np.pi * ((jc[:, None] * jc[None, :]) % GROUP_W) / GROUP_W
    csc = jnp.asarray(np.concatenate([np.cos(ang_c), -np.sin(ang_c)], axis=1), F32)
    head_mean = jnp.asarray(np.kron(np.eye(GROUP_W // MLA_V), np.full((MLA_V, MLA_V), 1.0 / MLA_V)), F32)

    for i in range(depth):
        mod = mods[i].reshape(8, 6, D)
        gkw, gkb = _prep_gk(gla_gk_w[i], gla_gk_b[i])
        q, k, v, hyz, gqk, gv, gg, og, fre, fim = _proj(
            xs, mod, norm1_g[i].reshape(1, D), _prep_w_in(w_in[i]), mla_q_g[i].reshape(1, -1), _prep_wq(mla_w_uq[i]),
            mla_kv_g[i].reshape(1, -1), *_prep_wkv(mla_w_ukv[i]), gkw, gkb, csc, tabs, nlt)

        a = _attention(q, k, v, L, Lc)

        filt = (hy_w1[i], hy_b1[i], hy_w2[i], hy_b2[i], hy_w3[i], hy_freq[i])
        vg_l, x0_l = _hy_pre(hyz, hy_conv_w[i], hy_conv_b[i], 0, L)
        hy = _hyena_long(vg_l, x0_l, *_hyena_filter(L, *filt), hy_bias[i], S)
        vg_c, x0_c = _hy_pre(hyz, hy_conv_w[i], hy_conv_b[i], L, Lc)
        hy = _hyena_ctx(vg_c, x0_c, *_hyena_filter(Lc, *filt), hy_bias[i], hy, L)

        o_f, o_b = _gla(gqk, gv, gg, L)

        fy = _fnet_ctx(fre, fim, L, Lc, _fnet_long(fre, fim, L))

        rw = jnp.pad(router_w[i], ((0, 0), (0, LANES - N_EXPERTS)))
        rb = jnp.concatenate([router_b[i], jnp.full((LANES - N_EXPERTS,), -1e30, F32)]).reshape(1, LANES)
        xs, h2w, top_i, top_g, rank, cnt = _mix(
            xs, mod, a, hy, o_f, o_b, og, fy, fnet_w[i].astype(BF16), mix_g[i].reshape(1, D), head_mean,
            w_out[i].astype(BF16), norm2_g[i].reshape(1, D), rw, rb, nlt)

        pos, blk_e, n_used, n_rows = _route(top_i.reshape(B * S, LANES)[:, :TOP_K],
                                            rank.reshape(B * S, LANES)[:, :TOP_K],
                                            cnt[0, :N_EXPERTS].astype(jnp.int32))
        xg = _dispatch(h2w.reshape(B * S, D // 2), pos, n_rows)
        wg, wu = _gu_prep(moe_w_gu, i)
        yb = _experts(xg, blk_e, n_used, wg, wu, moe_b_gu[i][:, None, 0::2], moe_b_gu[i][:, None, 1::2],
                      _down_prep(moe_w_down, i), moe_b_down[i][:, None, :])
        xs = _combine(xs, mod, top_g, yb, pos, nlt, final_g if i == depth - 1 else None)

    return xs
```

```python
import functools
import math

import numpy as np
import jax
import jax.numpy as jnp
from jax import lax
from jax.experimental import pallas as pl
from jax.experimental.pallas import tpu as pltpu

F32 = jnp.float32
BF16 = jnp.bfloat16
HIGHEST = lax.Precision.HIGHEST

EPS = 1e-6
D_MODEL = 1024
GROUP_W = 256
MLA_HEADS = 4
MLA_NOPE = 64
MLA_ROPE = 32
MLA_V = 64
MLA_Q_LORA = 256
MLA_KV_LORA = 128
MLA_SCALE = (MLA_NOPE + MLA_ROPE) ** -0.5
ROPE_THETA = 10000.0
GRID_W = 64
HY_EMB = 33
HY_FFN = 64
HY_TARGET = 1e-2
HY_FAST_PCT = 0.3
HY_SLOW_PCT = 1.5
GLA_HEADS = 4
GLA_DK = 32
GLA_DV = 64
GLA_LR = 16
GLA_TAU = 16.0
GLA_CHUNK = 64
N_EXPERTS = 32
TOP_K = 4
SWIGLU_ALPHA = 1.702
SWIGLU_LIMIT = 7.0

LANES = 128
TOK_TILE = 256
DFT_N2 = 128
MOE_BM = 512
VMEM_LIMIT = 56 * 1024 * 1024

_O_CQ, _O_CKV, _O_KRP, _O_KRS, _O_HY, _O_GQ, _O_GK, _O_GV, _O_GLR, _O_OG, _O_FN, _W_ALL = (
    0, 256, 384, 896, 1408, 2176, 2304, 2432, 2688, 2816, 3072, 3328)


def _dot(a, b):
    return jnp.dot(a.astype(BF16), b.astype(BF16), preferred_element_type=F32)


def _dot_hi(a, b):
    return jnp.dot(a, b, precision=HIGHEST, preferred_element_type=F32)


def _split(a):
    hi = a.astype(BF16)
    return hi, (a - hi.astype(F32)).astype(BF16)


def _dot_x3(a, b_hi, b_lo):
    a_hi, a_lo = _split(a)
    mm = lambda u, w: jnp.dot(u, w, preferred_element_type=F32)
    return mm(a_hi, b_hi) + (mm(a_lo, b_hi) + mm(a_hi, b_lo))


def _pack_pairs(x):
    bits = pltpu.bitcast(x.astype(BF16).astype(F32), jnp.uint32)
    w = bits.shape[1] // 2
    return (bits[:, :w] >> 16) | (bits[:, w:] & jnp.uint32(0xFFFF0000))


def _unpack_pairs(words):
    return pltpu.bitcast(words << 16, F32), pltpu.bitcast(words & jnp.uint32(0xFFFF0000), F32)


def _params(*sem):
    return pltpu.CompilerParams(dimension_semantics=sem, vmem_limit_bytes=VMEM_LIMIT)


def _rms_rows(x):
    return x * lax.rsqrt(jnp.mean(x * x, axis=-1, keepdims=True) + EPS)


def _full(shape):
    n = len(shape)
    return pl.BlockSpec(shape, lambda *_: (0,) * n)


def _mods_kernel(c_ref, w_ref, b_ref, o_ref):
    c = c_ref[...]
    s = c / (1.0 + jnp.exp(-c))
    o_ref[...] = _dot_hi(s, w_ref[...]) + b_ref[...]


def _mods(cc, ada_w, ada_b):
    depth, d, n = ada_w.shape
    tn = 1024
    return pl.pallas_call(
        _mods_kernel,
        grid=(depth, n // tn),
        in_specs=[pl.BlockSpec((8, d), lambda i, j: (0, 0)),
                  pl.BlockSpec((None, d, tn), lambda i, j: (i, 0, j)),
                  pl.BlockSpec((None, 1, tn), lambda i, j: (i, 0, j))],
        out_specs=pl.BlockSpec((None, 8, tn), lambda i, j: (i, 0, j)),
        out_shape=jax.ShapeDtypeStruct((depth, 8, n), F32),
        compiler_params=_params("arbitrary", "arbitrary"),
        name="ada_mods",
    )(cc, ada_w, ada_b.reshape(depth, 1, n))


def _proj_kernel(x_ref, mod_ref, g_ref, win_ref, qg_ref, wq_ref, kvg_ref, wkv_ref, wvt_ref, gkw_ref, gkb_ref,
                 csh_ref, csl_ref, cq_ref, sq_ref, ck_ref, sk_ref,
                 q_out, k_out, v_out, hy_out, gqk_out, gv_out, gg_out, og_out, fre_out, fim_out):
    x = x_ref[...]
    mod = mod_ref[...]
    h = _rms_rows(x) * g_ref[...] * (1.0 + mod[1:2]) + mod[0:1]
    z = _dot(h, win_ref[...])

    nq = _rms_rows(z[:, _O_CQ:_O_CKV]) * qg_ref[...]
    qq = _dot(nq, wq_ref[...])
    nkv = _rms_rows(z[:, _O_CKV:_O_KRP]) * kvg_ref[...]
    nkv = nkv.astype(BF16)
    kvu = jnp.dot(nkv, wkv_ref[...], preferred_element_type=F32)
    cq, sq, ck, sk = cq_ref[...], sq_ref[...], ck_ref[...], sk_ref[...]
    for hd in range(MLA_HEADS):
        a, b = hd * LANES, (hd + 1) * LANES
        q_out[:, a:b] = (qq[:, a:b] * cq + qq[:, 512 + a:512 + b] * sq).astype(BF16)
        k_out[:, a:b] = (kvu[:, a:b] + z[:, _O_KRP + a:_O_KRP + b] * ck
                         + z[:, _O_KRS + a:_O_KRS + b] * sk).astype(BF16)
    vt = lax.dot_general(wvt_ref[...], nkv, (((1,), (1,)), ((), ())), preferred_element_type=F32)
    vrow = lax.broadcasted_iota(jnp.int32, vt.shape, 0)
    v_out[...] = jnp.where(vrow % LANES == MLA_V, 1.0, vt).astype(BF16)

    hy_out[...] = z[:, _O_HY:_O_GQ]
    gqk_out[:, :LANES] = z[:, _O_GQ:_O_GK] * (GLA_DK ** -0.5)
    gqk_out[:, LANES:] = z[:, _O_GK:_O_GV]
    gv_out[...] = z[:, _O_GV:_O_GLR]
    gates = _dot(z[:, _O_GLR:_O_OG], gkw_ref[...]) + gkb_ref[...]
    gg_out[...] = (jnp.minimum(gates, 0.0) - jnp.log(1.0 + jnp.exp(-jnp.abs(gates)))) * (1.0 / GLA_TAU)
    og_out[...] = z[:, _O_OG:_O_FN]
    fcs = _dot_x3(z[:, _O_FN:_W_ALL], csh_ref[...], csl_ref[...])
    fre_out[...] = fcs[:, :GROUP_W]
    fim_out[...] = fcs[:, GROUP_W:]


def _proj(xs, mod, g1, win, qg, wq, kvg, wkv, wvt, gkw, gkb, csc, tabs, n_lat_tiles):
    B, S, D = xs.shape
    tm = TOK_TILE
    csh, csl = _split(csc)
    tok = lambda w: pl.BlockSpec((None, tm, w), lambda b, t: (b, t, 0))
    tab = pl.BlockSpec((tm, LANES), lambda b, t: (t, 0))
    shp = lambda w, dt: jax.ShapeDtypeStruct((B, S, w), dt)
    return pl.pallas_call(
        _proj_kernel,
        grid=(B, S // tm),
        in_specs=[tok(D),
                  pl.BlockSpec((None, 6, D), lambda b, t: (jnp.where(t >= n_lat_tiles, 2, b), 0, 0)),
                  _full((1, D)), _full(win.shape), _full((1, MLA_Q_LORA)), _full(wq.shape),
                  _full((1, MLA_KV_LORA)), _full(wkv.shape), _full(wvt.shape), _full(gkw.shape), _full(gkb.shape),
                  _full(csc.shape), _full(csc.shape), tab, tab, tab, tab],
        out_specs=[tok(512), tok(512), pl.BlockSpec((None, None, MLA_HEADS * LANES, tm), lambda b, t: (b, t, 0, 0)),
                   tok(768), tok(256), tok(256), tok(256), tok(256), tok(256), tok(256)],
        out_shape=[shp(512, BF16), shp(512, BF16), jax.ShapeDtypeStruct((B, S // tm, MLA_HEADS * LANES, tm), BF16),
                   shp(768, F32), shp(256, F32), shp(256, F32),
                   shp(256, F32), shp(256, F32), shp(256, F32), shp(256, F32)],
        compiler_params=_params("arbitrary", "arbitrary"),
        name="in_proj",
    )(xs, mod, g1, win, qg, wq, kvg, wkv, wvt, gkw, gkb, csh, csl, *tabs)


def _attn_kernel(q_ref, k_ref, vt_ref, o_ref, *, n_lat_tiles, tiles_per_chunk, n_lat, n_ctx):
    qi = pl.program_id(1)
    tq = q_ref.shape[0]
    tile = vt_ref.shape[2]
    n_hd = q_ref.shape[1] // LANES

    def heads(tile0, n_chunks, n_tiles):
        sub = LANES

        def body(c, carry):
            t0 = tile0 + c * n_tiles
            off = pl.multiple_of(t0 * tile, tile)
            state = list(carry)
            for j in range(n_tiles * tile // sub):
                for hd in range(n_hd):
                    m, acc = state[hd]
                    q = q_ref[:, hd * LANES:(hd + 1) * LANES]
                    kc = k_ref[pl.ds(off + j * sub, sub), hd * LANES:(hd + 1) * LANES]
                    s = lax.dot_general(kc, q, (((1,), (1,)), ((), ())), preferred_element_type=F32)
                    m_new = jnp.maximum(m, jnp.max(s, axis=0, keepdims=True))
                    alpha = jnp.exp2(m - m_new)
                    p = jnp.exp2(s - m_new).astype(BF16)
                    lo = (j * sub) % tile
                    vt = vt_ref[t0 + (j * sub) // tile, hd * LANES:(hd + 1) * LANES, lo:lo + sub]
                    state[hd] = (m_new, alpha * acc + jnp.dot(vt, p, preferred_element_type=F32))
            return tuple(state)

        one = (jnp.full((1, tq), -1e30, F32), jnp.zeros((LANES, tq), F32))
        res = lax.fori_loop(0, n_chunks, body, (one,) * n_hd)
        for hd in range(n_hd):
            acc_t = res[hd][1].T
            o_ref[:, hd * MLA_V:(hd + 1) * MLA_V] = acc_t[:, :MLA_V] / acc_t[:, MLA_V:MLA_V + 1]

    n_all = (n_lat + n_ctx) // tile

    @pl.when(qi < n_lat_tiles)
    def _():
        heads(0, n_all // tiles_per_chunk, tiles_per_chunk)

    @pl.when(qi >= n_lat_tiles)
    def _():
        heads(n_lat // tile, 1, n_ctx // tile)


def _attention(q, k, vt, n_lat, n_ctx):
    B, S, _ = q.shape
    tq = TOK_TILE
    n_tiles = vt.shape[1]
    tiles_per_chunk = 13 if n_tiles % 13 == 0 else 1
    kern = functools.partial(_attn_kernel, n_lat_tiles=n_lat // tq, tiles_per_chunk=tiles_per_chunk,
                             n_lat=n_lat, n_ctx=n_ctx)
    return pl.pallas_call(
        kern,
        grid=(B, S // tq),
        in_specs=[pl.BlockSpec((None, tq, 512), lambda b, t: (b, t, 0)),
                  pl.BlockSpec((None, S, 512), lambda b, t: (b, 0, 0), pipeline_mode=pl.Buffered(1)),
                  pl.BlockSpec((None, n_tiles, 512, vt.shape[3]), lambda b, t: (b, 0, 0, 0),
                               pipeline_mode=pl.Buffered(1))],
        out_specs=pl.BlockSpec((None, tq, 256), lambda b, t: (b, t, 0)),
        out_shape=jax.ShapeDtypeStruct((B, S, 256), F32),
        compiler_params=_params("arbitrary", "arbitrary"),
        name="mla_attention",
    )(q, k, vt)


def _hy_pre_kernel(z_ref, zp_ref, zn_ref, w_ref, b_ref, vg_ref, x0_ref, *, n_tiles):
    i = pl.program_id(1)
    z = z_ref[...]
    tm = z.shape[0]
    rows = lax.broadcasted_iota(jnp.int32, z.shape, 0)
    prev_row = jnp.where(i == 0, 0.0, zp_ref[7:8, :])
    next_row = jnp.where(i == n_tiles - 1, 0.0, zn_ref[0:1, :])
    z_m = jnp.where(rows == 0, prev_row, pltpu.roll(z, 1, 0))
    z_p = jnp.where(rows == tm - 1, next_row, pltpu.roll(z, tm - 1, 0))
    w = w_ref[...]
    u = z_m * w[0:1] + z * w[1:2] + z_p * w[2:3] + b_ref[...]
    vg_ref[...] = u[:, 2 * GROUP_W:] * u[:, GROUP_W:2 * GROUP_W]
    x0_ref[...] = u[:, :GROUP_W]


def _hy_pre(hyz, conv_w, conv_b, row0, n_rows):
    B, S, W = hyz.shape
    tm = TOK_TILE
    nt = n_rows // tm
    t0, r8, last8 = row0 // tm, row0 // 8, S // 8 - 1
    kern = functools.partial(_hy_pre_kernel, n_tiles=nt)
    out = jax.ShapeDtypeStruct((B, n_rows, GROUP_W), F32)
    return pl.pallas_call(
        kern,
        grid=(B, nt),
        in_specs=[pl.BlockSpec((None, tm, W), lambda b, i: (b, t0 + i, 0)),
                  pl.BlockSpec((None, 8, W), lambda b, i: (b, jnp.maximum(r8 + i * (tm // 8) - 1, 0), 0)),
                  pl.BlockSpec((None, 8, W), lambda b, i: (b, jnp.minimum(r8 + (i + 1) * (tm // 8), last8), 0)),
                  _full((3, W)), _full((1, W))],
        out_specs=[pl.BlockSpec((None, tm, GROUP_W), lambda b, i: (b, i, 0))] * 2,
        out_shape=[out, out],
        compiler_params=_params("arbitrary", "arbitrary"),
        name="hyena_pre",
    )(hyz, hyz, hyz, conv_w, conv_b.reshape(1, W))


def _filter_kernel(step_ref, cb_ref, sb_ref, w1_ref, b1_ref, w2_ref, b2_ref, w3_ref, fr_ref, dl_ref, h_ref, ss_ref, *,
                   n_pos):
    i = pl.program_id(0)
    tl = h_ref.shape[0]
    row = lax.broadcasted_iota(jnp.int32, (tl, LANES), 0) + i * tl
    pos = jnp.where(row <= n_pos, row, 2 * n_pos - row).astype(F32)
    lane = lax.broadcasted_iota(jnp.int32, (tl, LANES), 1)
    t = pos * (1.0 / (n_pos - 1))
    st = step_ref[...]
    cb, sb = cb_ref[...], sb_ref[...]
    cos_a = st[0:1] * cb - st[3:4] * sb
    sin_a = st[1:2] * cb + st[2:3] * sb
    feat = jnp.where(lane == 0, t, jnp.where(lane < 17, cos_a, jnp.where(lane < HY_EMB, -sin_a, 0.0)))
    fr = fr_ref[...]
    h = jnp.sin(fr * (_dot_hi(feat, w1_ref[...]) + b1_ref[...]))
    h = jnp.sin(fr * (_dot_hi(h, w2_ref[...]) + b2_ref[...]))
    h = _dot_hi(h, w3_ref[...]) * jnp.exp(-t[:, 0:1] * dl_ref[...])
    r1 = row[:, 0:1]
    h = jnp.where(r1 < n_pos, h[:, :GROUP_W], jnp.where(r1 == n_pos, 0.0, h[:, GROUP_W:]))
    h_ref[...] = h

    @pl.when(i == 0)
    def _():
        ss_ref[...] = jnp.zeros_like(ss_ref)

    ss_ref[...] += jnp.sum(h * h, axis=0, keepdims=True)


def _hyena_filter(n_pos, w1, b1, w2, b2, w3, freq):
    tl = min(n_pos, 512)
    n_steps = 2 * n_pos // tl
    bands = (HY_EMB - 1) // 2
    f = np.linspace(1e-4, bands - 1, bands)
    fv = np.zeros((LANES,))
    fv[1:17] = f
    fv[17:33] = f
    step0 = np.arange(n_steps) * tl
    mirrored = step0 >= n_pos
    base = 2.0 * np.pi * np.where(mirrored, 2 * n_pos - step0, step0)[:, None] * fv[None, :] / n_pos
    sign = np.where(mirrored, -1.0, 1.0)[:, None]
    steps = np.zeros((n_steps, 8, LANES))
    steps[:, 0], steps[:, 1] = np.cos(base), np.sin(base)
    steps[:, 2], steps[:, 3] = sign * np.cos(base), sign * np.sin(base)
    inner = 2.0 * np.pi * np.arange(tl)[:, None] * fv[None, :] / n_pos
    w1p = jnp.zeros((LANES, HY_FFN), F32).at[:HY_EMB].set(w1)
    max_decay = math.log(HY_TARGET) / HY_FAST_PCT
    min_decay = math.log(HY_TARGET) / HY_SLOW_PCT
    deltas = np.abs(np.linspace(min_decay, max_decay, GROUP_W)).astype(np.float32)
    dl = jnp.asarray(np.concatenate([deltas, deltas])[None, :])
    kern = functools.partial(_filter_kernel, n_pos=n_pos)
    k2, ss = pl.pallas_call(
        kern,
        grid=(n_steps,),
        in_specs=[pl.BlockSpec((None, 8, LANES), lambda i: (i, 0, 0)), _full((tl, LANES)), _full((tl, LANES)),
                  _full((LANES, HY_FFN)), _full((1, HY_FFN)), _full((HY_FFN, HY_FFN)),
                  _full((1, HY_FFN)), _full((HY_FFN, 2 * GROUP_W)), _full((1, HY_FFN)), _full((1, 2 * GROUP_W))],
        out_specs=[pl.BlockSpec((tl, GROUP_W), lambda i: (i, 0)), _full((1, GROUP_W))],
        out_shape=[jax.ShapeDtypeStruct((2 * n_pos, GROUP_W), F32), jax.ShapeDtypeStruct((1, GROUP_W), F32)],
        compiler_params=_params("arbitrary"),
        name="hyena_filter",
    )(jnp.asarray(steps, F32), jnp.asarray(np.cos(inner), F32), jnp.asarray(np.sin(inner), F32),
      w1p, b1.reshape(1, -1), w2, b2.reshape(1, -1), w3, freq.reshape(1, -1), dl)
    return k2, lax.rsqrt(ss)


def _dft_mats(n, sign):
    j = np.arange(n)
    ang = 2.0 * np.pi * ((j[:, None] * j[None, :]) % n) / n
    return np.cos(ang), sign * np.sin(ang)


def _twiddle(n1, n2, sign):
    ang = 2.0 * np.pi * ((np.arange(n1)[:, None] * np.arange(n2)[None, :]) % (n1 * n2)) / (n1 * n2)
    return np.cos(ang), sign * np.sin(ang)


def _block_complex(re, im):
    return np.block([[re, -im], [im, re]])


N2_BLK = 8


def _stage1_kernel(m_ref, zr_ref, zi_ref, ar_ref, ai_ref):
    half = ar_ref.shape[0]
    for j in range(N2_BLK):
        a = _dot_hi(m_ref[...], jnp.concatenate([zr_ref[:, j, :], zi_ref[:, j, :]], axis=0))
        ar_ref[:, j, :] = a[:half]
        ai_ref[:, j, :] = a[half:]


def _stage1_real_kernel(m_ref, zr_ref, ar_ref, ai_ref):
    half = ar_ref.shape[0]
    for j in range(N2_BLK):
        a = _dot_hi(m_ref[...], zr_ref[:, j, :])
        ar_ref[:, j, :] = a[:half]
        ai_ref[:, j, :] = a[half:]


def _hy_stage1(vg4, n1):
    _, r, n2, c = vg4.shape
    cr, ci = _dft_mats(n1, -1.0)
    m = jnp.asarray(_block_complex(cr[:, :r], ci[:, :r]), F32)
    out = jax.ShapeDtypeStruct((n1, n2, c), F32)
    return pl.pallas_call(
        _stage1_kernel,
        grid=(n2 // N2_BLK,),
        in_specs=[_full(m.shape), pl.BlockSpec((None, r, N2_BLK, c), lambda j: (0, 0, j, 0)),
                  pl.BlockSpec((None, r, N2_BLK, c), lambda j: (1, 0, j, 0))],
        out_specs=[pl.BlockSpec((n1, N2_BLK, c), lambda j: (0, j, 0))] * 2,
        out_shape=[out, out],
        compiler_params=_params("arbitrary"),
        name="hyena_fwd_stage1",
    )(m, vg4, vg4)


def _filter_stage1(k3, n1):
    _, n2, c = k3.shape
    cr, ci = _dft_mats(n1, -1.0)
    m = jnp.asarray(np.concatenate([cr, ci], axis=0), F32)
    out = jax.ShapeDtypeStruct((n1, n2, c), F32)
    blk = pl.BlockSpec((n1, N2_BLK, c), lambda j: (0, j, 0))
    return pl.pallas_call(
        _stage1_real_kernel,
        grid=(n2 // N2_BLK,),
        in_specs=[_full(m.shape), blk],
        out_specs=[blk, blk],
        out_shape=[out, out],
        compiler_params=_params("arbitrary"),
        name="hyena_filter_stage1",
    )(m, k3)


def _twiddled(f_re, f_im, t_re, t_im):
    g_re = f_re * t_re - f_im * t_im
    g_im = f_re * t_im + f_im * t_re
    return jnp.concatenate([jnp.concatenate([g_re, -g_im], axis=1), jnp.concatenate([g_im, g_re], axis=1)], axis=0)


def _filter_stage2_kernel(ar_ref, ai_ref, twr_ref, twi_ref, fr_ref, fi_ref, sc_ref, kr_ref, ki_ref):
    g = _twiddled(fr_ref[...], fi_ref[...], twr_ref[...], twi_ref[...])
    x = _dot_hi(g, jnp.concatenate([ar_ref[...], ai_ref[...]], axis=0)) * sc_ref[...]
    n2 = kr_ref.shape[0]
    kr_ref[...] = x[:n2]
    ki_ref[...] = x[n2:]


def _filter_stage2(ar, ai, inv_norm, n1):
    _, n2, c = ar.shape
    fr, fi = _dft_mats(n2, -1.0)
    twr, twi = _twiddle(n1, n2, -1.0)
    blk = pl.BlockSpec((None, n2, c), lambda k: (k, 0, 0))
    tw = pl.BlockSpec((None, 1, n2), lambda k: (k, 0, 0))
    out = jax.ShapeDtypeStruct((n1, n2, c), F32)
    return pl.pallas_call(
        _filter_stage2_kernel,
        grid=(n1,),
        in_specs=[blk, blk, tw, tw, _full((n2, n2)), _full((n2, n2)), _full((1, c))],
        out_specs=[blk, blk],
        out_shape=[out, out],
        compiler_params=_params("arbitrary"),
        name="hyena_filter_stage2",
    )(ar, ai, jnp.asarray(twr.reshape(n1, 1, n2), F32),
      jnp.asarray(twi.reshape(n1, 1, n2), F32), jnp.asarray(fr, F32), jnp.asarray(fi, F32), inv_norm)


def _hy_stage2_kernel(ar_ref, ai_ref, kr_ref, ki_ref, twr_ref, twi_ref, tcr_ref, tci_ref, fr_ref, fi_ref,
                      br_ref, bi_ref):
    f_re, f_im = fr_ref[...], fi_ref[...]
    n2 = f_re.shape[0]
    g = _twiddled(f_re, f_im, twr_ref[...], twi_ref[...])
    x = _dot_hi(g, jnp.concatenate([ar_ref[...], ai_ref[...]], axis=0))
    x_re, x_im = x[:n2], x[n2:]
    k_re, k_im = kr_ref[...], ki_ref[...]
    y = jnp.concatenate([x_re * k_re - x_im * k_im, x_re * k_im + x_im * k_re], axis=0)
    g_inv = _twiddled(f_re, -f_im, tcr_ref[...], -tci_ref[...])
    b = _dot_hi(g_inv, y)
    br_ref[...] = b[:n2]
    bi_ref[...] = b[n2:]


def _hy_stage2(ar, ai, kr, ki, n1):
    n2 = DFT_N2
    c = kr.shape[2]
    fr, fi = _dft_mats(n2, -1.0)
    twr, twi = _twiddle(n1, n2, -1.0)
    blk = pl.BlockSpec((None, n2, c), lambda k: (k, 0, 0))
    tw = pl.BlockSpec((None, 1, n2), lambda k: (k, 0, 0))
    twc = pl.BlockSpec((None, n2, 1), lambda k: (k, 0, 0))
    out = jax.ShapeDtypeStruct((n1, n2, c), F32)
    return pl.pallas_call(
        _hy_stage2_kernel,
        grid=(n1,),
        in_specs=[blk, blk, blk, blk, tw, tw, twc, twc, _full((n2, n2)), _full((n2, n2))],
        out_specs=[blk, blk],
        out_shape=[out, out],
        compiler_params=_params("arbitrary"),
        name="hyena_conv_stage2",
    )(ar, ai, kr, ki,
      jnp.asarray(twr.reshape(n1, 1, n2), F32), jnp.asarray(twi.reshape(n1, 1, n2), F32),
      jnp.asarray(twr.reshape(n1, n2, 1), F32), jnp.asarray(twi.reshape(n1, n2, 1), F32),
      jnp.asarray(fr, F32), jnp.asarray(fi, F32))


def _hy_stage3_kernel(m_ref, br_ref, bi_ref, vg0_ref, vg1_ref, x00_ref, x01_ref, bias_ref, o_ref):
    r = vg0_ref.shape[0]
    bias = bias_ref[...]
    for j in range(N2_BLK):
        conv = _dot_hi(m_ref[...], jnp.concatenate([br_ref[:, j, :], bi_ref[:, j, :]], axis=0))
        o_ref[0, :r, j, :] = (conv[:r] + vg0_ref[:, j, :] * bias) * x00_ref[:, j, :]
        o_ref[1, :r, j, :] = (conv[r:] + vg1_ref[:, j, :] * bias) * x01_ref[:, j, :]
    if o_ref.shape[1] > r:
        o_ref[:, r:] = jnp.zeros((2, o_ref.shape[1] - r) + o_ref.shape[2:], F32)


def _hy_stage3(br, bi, vg4, x04, bias, n1, rows_total):
    _, r, n2, c = vg4.shape
    cr, ci = _dft_mats(n1, 1.0)
    m = jnp.asarray(_block_complex(cr[:r], ci[:r]) / (n1 * n2), F32)
    plane = lambda p: pl.BlockSpec((None, r, N2_BLK, c), lambda j: (p, 0, j, 0))
    spec = pl.BlockSpec((n1, N2_BLK, c), lambda j: (0, j, 0))
    return pl.pallas_call(
        _hy_stage3_kernel,
        grid=(n2 // N2_BLK,),
        in_specs=[_full(m.shape), spec, spec, plane(0), plane(1), plane(0), plane(1), _full((1, c))],
        out_specs=pl.BlockSpec((2, rows_total, N2_BLK, c), lambda j: (0, 0, j, 0)),
        out_shape=jax.ShapeDtypeStruct((2, rows_total, n2, c), F32),
        compiler_params=_params("arbitrary"),
        name="hyena_inv_stage3",
    )(m, br, bi, vg4, vg4, x04, x04, bias)


def _hyena_long(vg, x0, k2, inv_norm, bias, n_all):
    B, L, C = vg.shape
    assert B == 2
    n2 = DFT_N2
    n1 = 2 * L // n2
    r = L // n2
    fa_r, fa_i = _filter_stage1(k2.reshape(n1, n2, C), n1)
    kr, ki = _filter_stage2(fa_r, fa_i, inv_norm, n1)
    vg4 = vg.reshape(2, r, n2, C)
    ar, ai = _hy_stage1(vg4, n1)
    br, bi = _hy_stage2(ar, ai, kr, ki, n1)
    y = _hy_stage3(br, bi, vg4, x0.reshape(2, r, n2, C), bias.reshape(1, C), n1, n_all // n2)
    return y.reshape(2, n_all, C)


def _hy_ctx_kernel(vg_ref, x0_ref, kext_ref, sc_ref, bias_ref, buf_ref, o_ref):
    del buf_ref
    n = vg_ref.shape[0]

    def body(a, acc):
        src = vg_ref[pl.ds(pl.multiple_of(8 * a, 8), 8), :]
        win = pl.multiple_of(n - 8 * a, 8)
        for r in range(8):
            acc = acc + kext_ref[r, pl.ds(win, n), :] * src[r:r + 1]
        return acc

    acc = lax.fori_loop(0, n // 8, body, jnp.zeros(vg_ref.shape, F32))
    o_ref[...] = (acc * sc_ref[...] + vg_ref[...] * bias_ref[...]) * x0_ref[...]


def _hyena_ctx(vg, x0, k2, inv_norm, bias, buf, row0):
    B, n, C = vg.shape
    kext = jnp.concatenate([k2[n:], k2[:n]], axis=0)
    kext = jnp.stack([jnp.roll(kext, r, axis=0) for r in range(8)])
    blk = pl.BlockSpec((None, n, C), lambda b: (b, 0, 0))
    return pl.pallas_call(
        _hy_ctx_kernel,
        grid=(B,),
        in_specs=[blk, blk, _full((8, 2 * n, C)), _full((1, C)), _full((1, C)), pl.BlockSpec(memory_space=pl.ANY)],
        out_specs=pl.BlockSpec((None, n, C), lambda b: (b, row0 // n, 0)),
        out_shape=jax.ShapeDtypeStruct(buf.shape, F32),
        input_output_aliases={5: 0},
        compiler_params=_params("arbitrary"),
        name="hyena_ctx",
    )(vg, x0, kext, inv_norm, bias.reshape(1, C), buf)


def _fn_stage1_kernel(c_ref, s_ref, re_ref, im_ref, ar_ref, ai_ref):
    c, s = c_ref[...], s_ref[...]
    for j in range(N2_BLK):
        re, im = re_ref[:, j, :], im_ref[:, j, :]
        ar_ref[:, j, :] = _dot_hi(c, re) + _dot_hi(s, im)
        ai_ref[:, j, :] = _dot_hi(c, im) - _dot_hi(s, re)


def _fn_stage2_kernel(ar_ref, ai_ref, twr_ref, twi_ref, fr_ref, fi_ref, o_ref):
    f_re, f_im = fr_ref[...], fi_ref[...]
    n2 = f_re.shape[0]
    for j in range(N2_BLK):
        t_re, t_im = twr_ref[j], twi_ref[j]
        g_re = f_re * t_re - f_im * t_im
        g_im = f_re * t_im + f_im * t_re
        o_ref[:n2, j, :] = _dot_hi(g_re, ar_ref[j]) - _dot_hi(g_im, ai_ref[j])
    if o_ref.shape[0] > n2:
        o_ref[n2:] = jnp.zeros((o_ref.shape[0] - n2,) + o_ref.shape[1:], F32)


def _fnet_long(fre, fim, n_lat):
    B, S, C = fre.shape
    n2 = DFT_N2
    n1 = n_lat // n2
    c1, s1 = _dft_mats(n1, 1.0)
    rows = pl.BlockSpec((None, n1, N2_BLK, C), lambda b, j: (b, 0, j, 0))
    a_shape = jax.ShapeDtypeStruct((B, n1, n2, C), F32)
    ar, ai = pl.pallas_call(
        _fn_stage1_kernel,
        grid=(B, n2 // N2_BLK),
        in_specs=[_full((n1, n1)), _full((n1, n1)), rows, rows],
        out_specs=[rows, rows],
        out_shape=[a_shape, a_shape],
        compiler_params=_params("arbitrary", "arbitrary"),
        name="fnet_stage1",
    )(jnp.asarray(c1, F32), jnp.asarray(s1, F32), fre.reshape(B, S // n2, n2, C), fim.reshape(B, S // n2, n2, C))
    fr, fi = _dft_mats(n2, -1.0)
    scale = 1.0 / math.sqrt(n_lat * C)
    twr, twi = _twiddle(n1, n2, -1.0)
    blk = pl.BlockSpec((None, N2_BLK, n2, C), lambda b, k: (b, k, 0, 0))
    tw = pl.BlockSpec((N2_BLK, 1, n2), lambda b, k: (k, 0, 0))
    y = pl.pallas_call(
        _fn_stage2_kernel,
        grid=(B, n1 // N2_BLK),
        in_specs=[blk, blk, tw, tw, _full((n2, n2)), _full((n2, n2))],
        out_specs=pl.BlockSpec((None, S // n1, N2_BLK, C), lambda b, k: (b, 0, k, 0)),
        out_shape=jax.ShapeDtypeStruct((B, S // n1, n1, C), F32),
        compiler_params=_params("arbitrary", "arbitrary"),
        name="fnet_stage2",
    )(ar, ai, jnp.asarray(twr.reshape(n1, 1, n2), F32),
      jnp.asarray(twi.reshape(n1, 1, n2), F32), jnp.asarray(fr * scale, F32), jnp.asarray(fi * scale, F32))
    return y.reshape(B, S, C)


def _fn_ctx_kernel(c_ref, s_ref, re_ref, im_ref, buf_ref, o_ref):
    del buf_ref
    o_ref[...] = _dot_hi(c_ref[...], re_ref[...]) + _dot_hi(s_ref[...], im_ref[...])


def _fnet_ctx(fre, fim, n_lat, n_ctx, buf):
    B, S, C = fre.shape
    c1, s1 = _dft_mats(n_ctx, 1.0)
    scale = 1.0 / math.sqrt(n_ctx * C)
    blk = pl.BlockSpec((None, n_ctx, C), lambda b: (b, n_lat // n_ctx, 0))
    return pl.pallas_call(
        _fn_ctx_kernel,
        grid=(B,),
        in_specs=[_full((n_ctx, n_ctx)), _full((n_ctx, n_ctx)), blk, blk, pl.BlockSpec(memory_space=pl.ANY)],
        out_specs=blk,
        out_shape=jax.ShapeDtypeStruct((B, S, C), F32),
        input_output_aliases={4: 0},
        compiler_params=_params("arbitrary"),
        name="fnet_ctx",
    )(jnp.asarray(c1 * scale, F32), jnp.asarray(s1 * scale, F32), fre, fim, buf)


def _gla_kernel(qkf_ref, vf_ref, gf_ref, qkb_ref, vb_ref, gb_ref, of_ref, ob_ref, st_ref):
    n = pl.program_id(0)
    n_b, tm = qkf_ref.shape[0], qkf_ref.shape[1]
    ck = GLA_CHUNK

    @pl.when(n == 0)
    def _():
        st_ref[...] = jnp.zeros_like(st_ref)

    ri = lax.broadcasted_iota(jnp.int32, (ck, ck), 0)
    ci = lax.broadcasted_iota(jnp.int32, (ck, ck), 1)

    def sub_chunk(qk_ref, v_ref, g_ref, o_ref, states, r0, reverse):
        keep = (ci >= ri) if reverse else (ci <= ri)
        g = g_ref[r0:r0 + ck, :]
        b = _dot_hi(keep.astype(F32), g)
        b_end = b[0:1] if reverse else b[ck - 1:ck]
        q = qk_ref[r0:r0 + ck, :LANES] * jnp.exp(b)
        k = qk_ref[r0:r0 + ck, LANES:]
        k_in = k * jnp.exp(-b)
        k_out = k * jnp.exp(b_end - b)
        decay = jnp.exp(b_end)
        new_states = []
        for hd in range(GLA_HEADS):
            ks = slice(hd * GLA_DK, (hd + 1) * GLA_DK)
            vs = slice(hd * GLA_DV, (hd + 1) * GLA_DV)
            qh, vh = q[:, ks].astype(BF16), v_ref[r0:r0 + ck, vs].astype(BF16)
            a = lax.dot_general(qh, k_in[:, ks].astype(BF16), (((1,), (1,)), ((), ())), preferred_element_type=F32)
            a = jnp.where(keep, a, 0.0)
            st = states[hd]
            o = jnp.dot(a.astype(BF16), vh, preferred_element_type=F32)
            o += lax.dot_general(qh, st.astype(BF16), (((1,), (1,)), ((), ())), preferred_element_type=F32)
            o_ref[r0:r0 + ck, vs] = o
            kv = lax.dot_general(vh, k_out[:, ks].astype(BF16), (((0,), (0,)), ((), ())), preferred_element_type=F32)
            new_states.append(st * decay[:, ks] + kv)
        return new_states

    st_f = [[st_ref[0, b, hd] for hd in range(GLA_HEADS)] for b in range(n_b)]
    st_b = [[st_ref[1, b, hd] for hd in range(GLA_HEADS)] for b in range(n_b)]
    n_sub = tm // ck
    for i in range(n_sub):
        for b in range(n_b):
            st_f[b] = sub_chunk(qkf_ref.at[b], vf_ref.at[b], gf_ref.at[b], of_ref.at[b], st_f[b], i * ck, False)
            st_b[b] = sub_chunk(qkb_ref.at[b], vb_ref.at[b], gb_ref.at[b], ob_ref.at[b], st_b[b],
                                (n_sub - 1 - i) * ck, True)
    for b in range(n_b):
        for hd in range(GLA_HEADS):
            st_ref[0, b, hd] = st_f[b][hd]
            st_ref[1, b, hd] = st_b[b][hd]


def _gla(gqk, gv, gg, n_lat):
    B, S, _ = gqk.shape
    tm = TOK_TILE
    nl = n_lat // tm
    nt = S // tm
    assert nt == nl + 1
    fwd = lambda n: jnp.where(n == 0, nl, n - 1)
    bwd = lambda n: jnp.where(n == 0, nl, nl - n)
    out = jax.ShapeDtypeStruct((B, S, 256), F32)
    return pl.pallas_call(
        _gla_kernel,
        grid=(nt,),
        in_specs=[pl.BlockSpec((B, tm, 256), lambda n: (0, fwd(n), 0)),
                  pl.BlockSpec((B, tm, 256), lambda n: (0, fwd(n), 0)),
                  pl.BlockSpec((B, tm, 128), lambda n: (0, fwd(n), 0)),
                  pl.BlockSpec((B, tm, 256), lambda n: (0, bwd(n), 0)),
                  pl.BlockSpec((B, tm, 256), lambda n: (0, bwd(n), 0)),
                  pl.BlockSpec((B, tm, 128), lambda n: (0, bwd(n), 1))],
        out_specs=[pl.BlockSpec((B, tm, 256), lambda n: (0, fwd(n), 0)),
                   pl.BlockSpec((B, tm, 256), lambda n: (0, bwd(n), 0))],
        out_shape=[out, out],
        scratch_shapes=[pltpu.VMEM((2, B, GLA_HEADS, GLA_DV, GLA_DK), F32)],
        compiler_params=_params("arbitrary"),
        name="gla_scan",
    )(gqk, gv, gg, gqk, gv, gg)


def _mix_kernel(x_ref, mod_ref, a_ref, hy_ref, of_ref, ob_ref, og_ref, fy_ref, fw_ref, mg_ref, hm_ref, wo_ref,
                n2g_ref, rwh_ref, rwl_ref, rb_ref, xo_ref, h2_ref, ti_ref, tg_ref, rk_ref, cnt_ref, run_ref):
    first = jnp.logical_and(pl.program_id(0) == 0, pl.program_id(1) == 0)

    @pl.when(first)
    def _():
        run_ref[...] = jnp.zeros_like(run_ref)

    mod = mod_ref[...]
    mg = mg_ref[...]
    hm = hm_ref[...]

    def head_rms(t):
        sq_hi, sq_lo = _split(t * t)
        ms = jnp.dot(sq_hi, hm, preferred_element_type=F32) + jnp.dot(sq_lo, hm, preferred_element_type=F32)
        return t * lax.rsqrt(ms + EPS)

    a = head_rms(a_ref[...]) * mg[:, 0:256]
    hy = _rms_rows(hy_ref[...]) * mg[:, 256:512]
    og = og_ref[...]
    o = head_rms(of_ref[...] + ob_ref[...]) * mg[:, 512:768] * (og / (1.0 + jnp.exp(-og)))
    fn = _rms_rows(_dot(fy_ref[...], fw_ref[...])) * mg[:, 768:1024]
    wo = wo_ref[...]
    y = _dot(a, wo[0:256]) + _dot(hy, wo[256:512]) + _dot(o, wo[512:768]) + _dot(fn, wo[768:1024])
    x = x_ref[...] + mod[2:3] * y
    xo_ref[...] = x

    h2 = _rms_rows(x) * n2g_ref[...] * (1.0 + mod[4:5]) + mod[3:4]
    h2_ref[...] = _pack_pairs(h2)

    logits = _dot_x3(h2, rwh_ref[...], rwl_ref[...]) + rb_ref[...]
    tm = logits.shape[0]
    lane = lax.broadcasted_iota(jnp.int32, logits.shape, 1).astype(F32)
    idx_out = jnp.zeros(logits.shape, F32)
    val_out = jnp.zeros(logits.shape, F32)
    chosen = jnp.zeros(logits.shape, F32)
    picks = []
    top = None
    den = jnp.zeros((tm, 1), F32)
    for kk in range(TOP_K):
        m = jnp.max(logits, axis=-1, keepdims=True)
        idx = jnp.min(jnp.where(logits == m, lane, float(LANES)), axis=-1, keepdims=True)
        if top is None:
            top = m
        e = jnp.exp(m - top)
        den = den + e
        hit = lane == idx
        picks.append(hit)
        chosen = jnp.where(hit, 1.0, chosen)
        idx_out = jnp.where(lane == kk, idx, idx_out)
        val_out = jnp.where(lane == kk, e, val_out)
        logits = jnp.where(hit, -jnp.inf, logits)
    ti_ref[...] = idx_out.astype(jnp.int32)
    tg_ref[...] = val_out / den

    ri = lax.broadcasted_iota(jnp.int32, (tm, tm), 0)
    ci = lax.broadcasted_iota(jnp.int32, (tm, tm), 1)
    before = _dot((ci < ri).astype(F32), chosen) + run_ref[...]
    rank = jnp.zeros(logits.shape, F32)
    for kk in range(TOP_K):
        r = jnp.sum(jnp.where(picks[kk], before, 0.0), axis=-1, keepdims=True)
        rank = jnp.where(lane == kk, r, rank)
    rk_ref[...] = rank.astype(jnp.int32)
    run_ref[...] += jnp.sum(chosen, axis=0, keepdims=True)
    cnt_ref[...] = run_ref[...]


def _mix(xs, mod, a, hy, o_f, o_b, og, fy, fnet_w, mix_g, head_mean, w_out, n2g, rw, rb, n_lat_tiles):
    B, S, D = xs.shape
    tm = TOK_TILE
    rwh, rwl = _split(rw)
    tok = lambda w: pl.BlockSpec((None, tm, w), lambda b, t: (b, t, 0))
    return pl.pallas_call(
        _mix_kernel,
        grid=(B, S // tm),
        in_specs=[tok(D),
                  pl.BlockSpec((None, 6, D), lambda b, t: (jnp.where(t >= n_lat_tiles, 2, b), 0, 0)),
                  tok(256), tok(256), tok(256), tok(256), tok(256), tok(256),
                  _full((256, 256)), _full((1, D)), _full((256, 256)), _full((D, D)), _full((1, D)),
                  _full((D, LANES)), _full((D, LANES)), _full((1, LANES))],
        out_specs=[tok(D), tok(D // 2), tok(LANES), tok(LANES), tok(LANES), _full((1, LANES))],
        out_shape=[jax.ShapeDtypeStruct((B, S, D), F32), jax.ShapeDtypeStruct((B, S, D // 2), jnp.uint32),
                   jax.ShapeDtypeStruct((B, S, LANES), jnp.int32), jax.ShapeDtypeStruct((B, S, LANES), F32),
                   jax.ShapeDtypeStruct((B, S, LANES), jnp.int32), jax.ShapeDtypeStruct((1, LANES), F32)],
        scratch_shapes=[pltpu.VMEM((1, LANES), F32)],
        compiler_params=_params("arbitrary", "arbitrary"),
        name="mix_out_router",
    )(xs, mod, a, hy, o_f, o_b, og, fy, fnet_w, mix_g, head_mean.astype(BF16), w_out, n2g, rwh, rwl, rb)


def _route(top_i, rank, counts):
    T = top_i.shape[0]
    bm = MOE_BM
    padded = (counts + bm - 1) // bm * bm
    pend = jnp.cumsum(padded)
    pstart = pend - padded
    pos = (pstart[top_i] + rank).astype(jnp.int32)
    n_blocks = (T * TOP_K + bm - 1) // bm + N_EXPERTS
    blk_row0 = jnp.arange(n_blocks, dtype=pend.dtype) * bm
    blk_e = jnp.minimum(jnp.sum(pend[None, :] <= blk_row0[:, None], axis=1), N_EXPERTS - 1).astype(jnp.int32)
    n_used = (pend[-1] // bm).astype(jnp.int32).reshape(1)
    return pos, blk_e, n_used, n_blocks * bm


def _dispatch_kernel(pos_ref, h_ref, zero_ref, xg_ref, sem):
    del zero_ref
    tm = h_ref.shape[0]

    def copy(t, kk):
        return pltpu.make_async_copy(h_ref.at[pl.ds(t, 1)], xg_ref.at[pl.ds(pos_ref[0, t * TOP_K + kk], 1)], sem)

    def start(t, c):
        for kk in range(TOP_K):
            copy(t, kk).start()
        return c

    def wait(t, c):
        for kk in range(TOP_K):
            copy(t, kk).wait()
        return c

    lax.fori_loop(0, tm, start, 0, unroll=8)
    lax.fori_loop(0, tm, wait, 0, unroll=8)


def _dispatch(h2w, pos, n_rows):
    T, W = h2w.shape
    tm = TOK_TILE
    nb = T // tm
    return pl.pallas_call(
        _dispatch_kernel,
        grid=(nb,),
        in_specs=[pl.BlockSpec((None, 1, tm * TOP_K), lambda i: (i, 0, 0), memory_space=pltpu.SMEM),
                  pl.BlockSpec((tm, W), lambda i: (i, 0)),
                  pl.BlockSpec(memory_space=pl.ANY)],
        out_specs=pl.BlockSpec(memory_space=pl.ANY),
        out_shape=jax.ShapeDtypeStruct((n_rows, W), h2w.dtype),
        scratch_shapes=[pltpu.SemaphoreType.DMA(())],
        input_output_aliases={2: 0},
        compiler_params=_params("arbitrary"),
        name="moe_dispatch",
    )(pos.reshape(nb, 1, tm * TOP_K), h2w, jnp.zeros((n_rows, W), h2w.dtype))


def _gu_prep_kernel(w_ref, p_ref, g_ref, u_ref):
    y = jnp.dot(w_ref[...].astype(BF16), p_ref[...], preferred_element_type=F32)
    half = y.shape[1] // 2
    g_ref[...] = y[:, :half].astype(BF16)
    u_ref[...] = y[:, half:].astype(BF16)


def _cast_kernel(w_ref, o_ref):
    o_ref[...] = w_ref[...].astype(BF16)


def _down_prep(w_down, layer):
    _, E, F, D = w_down.shape
    return pl.pallas_call(
        _cast_kernel,
        grid=(E,),
        in_specs=[pl.BlockSpec((None, None, F, D), lambda e: (layer, e, 0, 0))],
        out_specs=pl.BlockSpec((None, F, D), lambda e: (e, 0, 0)),
        out_shape=jax.ShapeDtypeStruct((E, F, D), BF16),
        compiler_params=_params("arbitrary"),
        name="moe_down_prep",
    )(w_down)


def _gu_prep(w_gu, layer):
    _, E, D, F2 = w_gu.shape
    tn = 512
    perm = np.zeros((tn, tn), np.float32)
    perm[2 * np.arange(tn // 2), np.arange(tn // 2)] = 1.0
    perm[2 * np.arange(tn // 2) + 1, tn // 2 + np.arange(tn // 2)] = 1.0
    out = jax.ShapeDtypeStruct((E, D, F2 // 2), BF16)
    return pl.pallas_call(
        _gu_prep_kernel,
        grid=(E, F2 // tn),
        in_specs=[pl.BlockSpec((None, None, D, tn), lambda e, j: (layer, e, 0, j)), _full((tn, tn))],
        out_specs=[pl.BlockSpec((None, D, tn // 2), lambda e, j: (e, 0, j))] * 2,
        out_shape=[out, out],
        compiler_params=_params("arbitrary", "arbitrary"),
        name="moe_weight_prep",
    )(w_gu, jnp.asarray(perm, BF16))


def _expert_kernel(be_ref, nu_ref, x_ref, wg_ref, wu_ref, bg_ref, bu_ref, wd_ref, bd_ref, o_ref):
    i = pl.program_id(0)

    @pl.when(i < nu_ref[0])
    def _():
        x = jnp.concatenate(_unpack_pairs(x_ref[...]), axis=1).astype(BF16)
        gate = jnp.minimum(jnp.dot(x, wg_ref[...], preferred_element_type=F32) + bg_ref[...], SWIGLU_LIMIT)
        up = jnp.clip(jnp.dot(x, wu_ref[...], preferred_element_type=F32) + bu_ref[...], -SWIGLU_LIMIT, SWIGLU_LIMIT)
        glu = gate / (1.0 + jnp.exp(-gate * SWIGLU_ALPHA))
        o_ref[...] = _pack_pairs(_dot((up + 1.0) * glu, wd_ref[...]) + bd_ref[...])

    @pl.when(i >= nu_ref[0])
    def _():
        o_ref[...] = jnp.zeros_like(o_ref)


def _experts(xg, blk_e, n_used, wg, wu, bg, bu, wd, bd):
    n_rows, W = xg.shape
    bm = MOE_BM
    D, F = wg.shape[1], wg.shape[2]
    wsel = lambda r, c: pl.BlockSpec((None, r, c), lambda i, be, nu: (be[i], 0, 0))
    return pl.pallas_call(
        _expert_kernel,
        grid_spec=pltpu.PrefetchScalarGridSpec(
            num_scalar_prefetch=2,
            grid=(n_rows // bm,),
            in_specs=[pl.BlockSpec((bm, W), lambda i, be, nu: (i, 0)),
                      wsel(D, F), wsel(D, F), wsel(1, F), wsel(1, F), wsel(F, D), wsel(1, D)],
            out_specs=pl.BlockSpec((bm, W), lambda i, be, nu: (i, 0)),
        ),
        out_shape=jax.ShapeDtypeStruct((n_rows, W), jnp.uint32),
        compiler_params=_params("arbitrary"),
        name="moe_experts",
    )(blk_e, n_used, xg, wg, wu, bg, bu, wd, bd)


def _combine_kernel(pos_ref, posn_ref, x_ref, mod_ref, tg_ref, fg_ref, y_ref, o_ref, buf, sem, *, final):
    i = pl.program_id(0)
    tm = x_ref.shape[0]
    slot = i % 2

    def copy(p_ref, s, t, kk):
        return pltpu.make_async_copy(y_ref.at[pl.ds(p_ref[0, t * TOP_K + kk], 1)], buf.at[s, kk, pl.ds(t, 1)],
                                     sem.at[s])

    def fetch(p_ref, s):
        def body(t, c):
            for kk in range(TOP_K):
                copy(p_ref, s, t, kk).start()
            return c
        lax.fori_loop(0, tm, body, 0, unroll=8)

    @pl.when(i == 0)
    def _():
        fetch(pos_ref, 0)

    @pl.when(i + 1 < pl.num_programs(0))
    def _():
        fetch(posn_ref, 1 - slot)

    def wait(t, c):
        for kk in range(TOP_K):
            copy(pos_ref, slot, t, kk).wait()
        return c

    lax.fori_loop(0, tm, wait, 0, unroll=8)
    tg = tg_ref[...]
    f_lo = jnp.zeros((tm, x_ref.shape[1] // 2), F32)
    f_hi = f_lo
    for kk in range(TOP_K):
        y_lo, y_hi = _unpack_pairs(buf[slot, kk])
        f_lo = f_lo + tg[:, kk:kk + 1] * y_lo
        f_hi = f_hi + tg[:, kk:kk + 1] * y_hi
    x = x_ref[...] + mod_ref[...][5:6] * jnp.concatenate([f_lo, f_hi], axis=1)
    o_ref[...] = _rms_rows(x) * fg_ref[...] if final else x


def _combine(xs, mod, tg, yb, pos, n_lat_tiles, final_g=None):
    B, S, D = xs.shape
    tm = TOK_TILE
    nt = S // tm
    final = final_g is not None
    per_b = n_lat_tiles if final else nt
    nb = B * per_b
    tile = lambda i: (i // per_b) * nt + i % per_b
    pos3 = pos.reshape(B * nt, 1, tm * TOP_K)
    row = lambda w: pl.BlockSpec((tm, w), lambda i: (tile(i), 0))
    fg = final_g.reshape(1, D) if final else jnp.ones((1, D), F32)
    out = pl.pallas_call(
        functools.partial(_combine_kernel, final=final),
        grid=(nb,),
        in_specs=[pl.BlockSpec((None, 1, tm * TOP_K), lambda i: (tile(i), 0, 0), memory_space=pltpu.SMEM),
                  pl.BlockSpec((None, 1, tm * TOP_K), lambda i: (tile(jnp.minimum(i + 1, nb - 1)), 0, 0),
                               memory_space=pltpu.SMEM),
                  row(D),
                  pl.BlockSpec((None, 6, D),
                               lambda i: (jnp.where(i % per_b >= n_lat_tiles, 2, i // per_b), 0, 0)),
                  row(LANES),
                  _full((1, D)),
                  pl.BlockSpec(memory_space=pl.ANY)],
        out_specs=pl.BlockSpec((tm, D), lambda i: (i, 0)),
        out_shape=jax.ShapeDtypeStruct((nb * tm, D), F32),
        scratch_shapes=[pltpu.VMEM((2, TOP_K, tm, D // 2), jnp.uint32), pltpu.SemaphoreType.DMA((2,))],
        compiler_params=_params("arbitrary"),
        name="moe_combine",
    )(pos3, pos3, xs.reshape(B * S, D), mod, tg.reshape(B * S, LANES), fg, yb)
    return out.reshape(B, nb // B * tm, D)


def _prep_w_in(w_in):
    cq, ckv, kr = w_in[:, 0:256], w_in[:, 256:384], w_in[:, 384:416]
    hy, gq, gk = w_in[:, 416:1184], w_in[:, 1184:1312], w_in[:, 1312:1440]
    gv, glr, og, fn = w_in[:, 1440:1696], w_in[:, 1696:1728], w_in[:, 1728:1984], w_in[:, 1984:2240]
    half = MLA_ROPE // 2
    kr_sw = jnp.concatenate([-kr[:, half:], kr[:, :half]], axis=1)
    place = lambda w: jnp.tile(jnp.pad(w, ((0, 0), (MLA_NOPE, LANES - MLA_NOPE - MLA_ROPE))), (1, MLA_HEADS))
    glr_p = jnp.pad(glr, ((0, 0), (0, LANES - 2 * GLA_LR)))
    return jnp.concatenate([cq, ckv, place(kr), place(kr_sw), hy, gq, gk, gv, glr_p, og, fn], axis=1).astype(BF16)


def _prep_wq(w_uq):
    w = w_uq.reshape(MLA_Q_LORA, MLA_HEADS, MLA_NOPE + MLA_ROPE)
    nope, rope = w[..., :MLA_NOPE], w[..., MLA_NOPE:]
    half = MLA_ROPE // 2
    z_tail = jnp.zeros((MLA_Q_LORA, MLA_HEADS, LANES - MLA_NOPE - MLA_ROPE), F32)
    z_nope = jnp.zeros((MLA_Q_LORA, MLA_HEADS, MLA_NOPE), F32)
    plain = jnp.concatenate([nope, rope, z_tail], axis=-1).reshape(MLA_Q_LORA, MLA_HEADS * LANES)
    partner = jnp.concatenate([z_nope, -rope[..., half:], rope[..., :half], z_tail], axis=-1)
    return jnp.concatenate([plain, partner.reshape(MLA_Q_LORA, MLA_HEADS * LANES)], axis=1).astype(BF16)


def _prep_wkv(w_ukv):
    w = w_ukv.reshape(MLA_KV_LORA, MLA_HEADS, MLA_NOPE + MLA_V)
    k_nope, v = w[..., :MLA_NOPE], w[..., MLA_NOPE:]
    k_placed = jnp.pad(k_nope, ((0, 0), (0, 0), (0, LANES - MLA_NOPE))).reshape(MLA_KV_LORA, MLA_HEADS * LANES)
    v_t = jnp.pad(v, ((0, 0), (0, 0), (0, LANES - MLA_V))).reshape(MLA_KV_LORA, MLA_HEADS * LANES).T
    return k_placed.astype(BF16), v_t.astype(BF16)


def _prep_gk(gk_w, gk_b):
    w = jnp.zeros((LANES, 2 * LANES), F32)
    w = w.at[0:GLA_LR, 0:LANES].set(gk_w[0]).at[GLA_LR:2 * GLA_LR, LANES:].set(gk_w[1])
    return w.astype(BF16), jnp.concatenate([gk_b[0], gk_b[1]]).reshape(1, 2 * LANES)


def _rope_tables(n_lat, n_ctx):
    rows = n_lat // GRID_W
    row = jnp.repeat(jnp.arange(rows, dtype=F32), GRID_W)
    col = jnp.tile(jnp.arange(GRID_W, dtype=F32), rows)
    n_freq = MLA_ROPE // 4
    inv = ROPE_THETA ** (-jnp.arange(n_freq, dtype=F32) / n_freq)
    ang = jnp.concatenate([row[:, None] * inv, col[:, None] * inv], axis=-1)
    cos = jnp.concatenate([jnp.cos(ang), jnp.ones((n_ctx, MLA_ROPE // 2), F32)], axis=0)
    sin = jnp.concatenate([jnp.sin(ang), jnp.zeros((n_ctx, MLA_ROPE // 2), F32)], axis=0)
    S = n_lat + n_ctx
    ones, zeros = jnp.ones((S, MLA_NOPE), F32), jnp.zeros((S, MLA_NOPE), F32)
    tail = jnp.zeros((S, LANES - MLA_NOPE - MLA_ROPE), F32)
    q_scale = MLA_SCALE * math.log2(math.e)
    cq = jnp.concatenate([ones, cos, cos, tail], axis=1) * q_scale
    sq = jnp.concatenate([zeros, sin, sin, tail], axis=1) * q_scale
    ck = jnp.concatenate([zeros, cos, cos, tail], axis=1)
    sk = jnp.concatenate([zeros, sin, sin, tail], axis=1)
    return cq, sq, ck, sk


def kernel(x, c, ctx, c_ctx, ada_w, ada_b, norm1_g, norm2_g, w_in, mla_q_g, mla_w_uq, mla_kv_g, mla_w_ukv, hy_conv_w, hy_conv_b, hy_w1, hy_b1, hy_w2, hy_b2, hy_w3, hy_freq, hy_bias, gla_gk_w, gla_gk_b, fnet_w, mix_g, w_out, router_w, router_b, moe_w_gu, moe_b_gu, moe_w_down, moe_b_down, final_g):
    B, L, D = x.shape
    Lc = ctx.shape[1]
    depth = ada_w.shape[0]
    S = L + Lc
    tm = TOK_TILE
    nlt = L // tm
    assert B == 2 and D == D_MODEL and Lc == tm and L % (2 * DFT_N2 * 8) == 0

    xs = jnp.concatenate([x, ctx], axis=1)
    cc = jnp.zeros((8, D), F32).at[0:B].set(c).at[B].set(c_ctx)
    mods = _mods(cc, ada_w, ada_b)
    tabs = _rope_tables(L, Lc)
    jc = np.arange(GROUP_W)
    ang_c = 2.0 * np.pi * ((jc[:, None] * jc[None, :]) % GROUP_W) / GROUP_W
    csc = jnp.asarray(np.concatenate([np.cos(ang_c), -np.sin(ang_c)], axis=1), F32)
    head_mean = jnp.asarray(np.kron(np.eye(GROUP_W // MLA_V), np.full((MLA_V, MLA_V), 1.0 / MLA_V)), F32)

    for i in range(depth):
        mod = mods[i].reshape(8, 6, D)
        gkw, gkb = _prep_gk(gla_gk_w[i], gla_gk_b[i])
        q, k, v, hyz, gqk, gv, gg, og, fre, fim = _proj(
            xs, mod, norm1_g[i].reshape(1, D), _prep_w_in(w_in[i]), mla_q_g[i].reshape(1, -1), _prep_wq(mla_w_uq[i]),
            mla_kv_g[i].reshape(1, -1), *_prep_wkv(mla_w_ukv[i]), gkw, gkb, csc, tabs, nlt)

        a = _attention(q, k, v, L, Lc)

        filt = (hy_w1[i], hy_b1[i], hy_w2[i], hy_b2[i], hy_w3[i], hy_freq[i])
        vg_l, x0_l = _hy_pre(hyz, hy_conv_w[i], hy_conv_b[i], 0, L)
        hy = _hyena_long(vg_l, x0_l, *_hyena_filter(L, *filt), hy_bias[i], S)
        vg_c, x0_c = _hy_pre(hyz, hy_conv_w[i], hy_conv_b[i], L, Lc)
        hy = _hyena_ctx(vg_c, x0_c, *_hyena_filter(Lc, *filt), hy_bias[i], hy, L)

        o_f, o_b = _gla(gqk, gv, gg, L)

        fy = _fnet_ctx(fre, fim, L, Lc, _fnet_long(fre, fim, L))

        rw = jnp.pad(router_w[i], ((0, 0), (0, LANES - N_EXPERTS)))
        rb = jnp.concatenate([router_b[i], jnp.full((LANES - N_EXPERTS,), -1e30, F32)]).reshape(1, LANES)
        xs, h2w, top_i, top_g, rank, cnt = _mix(
            xs, mod, a, hy, o_f, o_b, og, fy, fnet_w[i].astype(BF16), mix_g[i].reshape(1, D), head_mean,
            w_out[i].astype(BF16), norm2_g[i].reshape(1, D), rw, rb, nlt)

        pos, blk_e, n_used, n_rows = _route(top_i.reshape(B * S, LANES)[:, :TOP_K],
                                            rank.reshape(B * S, LANES)[:, :TOP_K],
                                            cnt[0, :N_EXPERTS].astype(jnp.int32))
        xg = _dispatch(h2w.reshape(B * S, D // 2), pos, n_rows)
        wg, wu = _gu_prep(moe_w_gu, i)
        yb = _experts(xg, blk_e, n_used, wg, wu, moe_b_gu[i][:, None, 0::2], moe_b_gu[i][:, None, 1::2],
                      _down_prep(moe_w_down, i), moe_b_down[i][:, None, :])
        xs = _combine(xs, mod, top_g, yb, pos, nlt, final_g if i == depth - 1 else None)

    return xs
```

```python
import functools
import math

import numpy as np
import jax
import jax.numpy as jnp
from jax import lax
from jax.experimental import pallas as pl
from jax.experimental.pallas import tpu as pltpu

F32 = jnp.float32
BF16 = jnp.bfloat16
HIGHEST = lax.Precision.HIGHEST

EPS = 1e-6
D_MODEL = 1024
GROUP_W = 256
MLA_HEADS = 4
MLA_NOPE = 64
MLA_ROPE = 32
MLA_V = 64
MLA_Q_LORA = 256
MLA_KV_LORA = 128
MLA_SCALE = (MLA_NOPE + MLA_ROPE) ** -0.5
ROPE_THETA = 10000.0
GRID_W = 64
HY_EMB = 33
HY_FFN = 64
HY_TARGET = 1e-2
HY_FAST_PCT = 0.3
HY_SLOW_PCT = 1.5
GLA_HEADS = 4
GLA_DK = 32
GLA_DV = 64
GLA_LR = 16
GLA_TAU = 16.0
GLA_CHUNK = 64
N_EXPERTS = 32
TOP_K = 4
SWIGLU_ALPHA = 1.702
SWIGLU_LIMIT = 7.0

LANES = 128
TOK_TILE = 256
DFT_N2 = 128
MOE_BM = 512
VMEM_LIMIT = 56 * 1024 * 1024

_O_CQ, _O_CKV, _O_KRP, _O_KRS, _O_HY, _O_GQ, _O_GK, _O_GV, _O_GLR, _O_OG, _O_FN, _W_ALL = (
    0, 256, 384, 896, 1408, 2176, 2304, 2432, 2688, 2816, 3072, 3328)


def _dot(a, b):
    return jnp.dot(a.astype(BF16), b.astype(BF16), preferred_element_type=F32)


def _dot_hi(a, b):
    return jnp.dot(a, b, precision=HIGHEST, preferred_element_type=F32)


def _split(a):
    hi = a.astype(BF16)
    return hi, (a - hi.astype(F32)).astype(BF16)


def _dot_x3(a, b_hi, b_lo):
    a_hi, a_lo = _split(a)
    mm = lambda u, w: jnp.dot(u, w, preferred_element_type=F32)
    return mm(a_hi, b_hi) + (mm(a_lo, b_hi) + mm(a_hi, b_lo))


def _pack_pairs(x):
    bits = pltpu.bitcast(x.astype(BF16).astype(F32), jnp.uint32)
    w = bits.shape[1] // 2
    return (bits[:, :w] >> 16) | (bits[:, w:] & jnp.uint32(0xFFFF0000))


def _unpack_pairs(words):
    return pltpu.bitcast(words << 16, F32), pltpu.bitcast(words & jnp.uint32(0xFFFF0000), F32)


def _params(*sem):
    return pltpu.CompilerParams(dimension_semantics=sem, vmem_limit_bytes=VMEM_LIMIT)


def _rms_rows(x):
    return x * lax.rsqrt(jnp.mean(x * x, axis=-1, keepdims=True) + EPS)


def _full(shape):
    n = len(shape)
    return pl.BlockSpec(shape, lambda *_: (0,) * n)


def _mods_kernel(c_ref, w_ref, b_ref, o_ref):
    c = c_ref[...]
    s = c / (1.0 + jnp.exp(-c))
    o_ref[...] = _dot_hi(s, w_ref[...]) + b_ref[...]


def _mods(cc, ada_w, ada_b):
    depth, d, n = ada_w.shape
    tn = 1024
    return pl.pallas_call(
        _mods_kernel,
        grid=(depth, n // tn),
        in_specs=[pl.BlockSpec((8, d), lambda i, j: (0, 0)),
                  pl.BlockSpec((None, d, tn), lambda i, j: (i, 0, j)),
                  pl.BlockSpec((None, 1, tn), lambda i, j: (i, 0, j))],
        out_specs=pl.BlockSpec((None, 8, tn), lambda i, j: (i, 0, j)),
        out_shape=jax.ShapeDtypeStruct((depth, 8, n), F32),
        compiler_params=_params("arbitrary", "arbitrary"),
        name="ada_mods",
    )(cc, ada_w, ada_b.reshape(depth, 1, n))


def _proj_kernel(x_ref, mod_ref, g_ref, win_ref, qg_ref, wq_ref, kvg_ref, wkv_ref, wvt_ref, gkw_ref, gkb_ref,
                 csh_ref, csl_ref, cq_ref, sq_ref, ck_ref, sk_ref,
                 q_out, k_out, v_out, hy_out, gqk_out, gv_out, gg_out, og_out, fre_out, fim_out):
    x = x_ref[...]
    mod = mod_ref[...]
    h = _rms_rows(x) * g_ref[...] * (1.0 + mod[1:2]) + mod[0:1]
    z = _dot(h, win_ref[...])

    nq = _rms_rows(z[:, _O_CQ:_O_CKV]) * qg_ref[...]
    qq = _dot(nq, wq_ref[...])
    nkv = _rms_rows(z[:, _O_CKV:_O_KRP]) * kvg_ref[...]
    nkv = nkv.astype(BF16)
    kvu = jnp.dot(nkv, wkv_ref[...], preferred_element_type=F32)
    cq, sq, ck, sk = cq_ref[...], sq_ref[...], ck_ref[...], sk_ref[...]
    for hd in range(MLA_HEADS):
        a, b = hd * LANES, (hd + 1) * LANES
        q_out[:, a:b] = (qq[:, a:b] * cq + qq[:, 512 + a:512 + b] * sq).astype(BF16)
        k_out[:, a:b] = (kvu[:, a:b] + z[:, _O_KRP + a:_O_KRP + b] * ck
                         + z[:, _O_KRS + a:_O_KRS + b] * sk).astype(BF16)
    vt = lax.dot_general(wvt_ref[...], nkv, (((1,), (1,)), ((), ())), preferred_element_type=F32)
    vrow = lax.broadcasted_iota(jnp.int32, vt.shape, 0)
    v_out[...] = jnp.where(vrow % LANES == MLA_V, 1.0, vt).astype(BF16)

    hy_out[...] = z[:, _O_HY:_O_GQ]
    gqk_out[:, :LANES] = z[:, _O_GQ:_O_GK] * (GLA_DK ** -0.5)
    gqk_out[:, LANES:] = z[:, _O_GK:_O_GV]
    gv_out[...] = z[:, _O_GV:_O_GLR]
    gates = _dot(z[:, _O_GLR:_O_OG], gkw_ref[...]) + gkb_ref[...]
    gg_out[...] = (jnp.minimum(gates, 0.0) - jnp.log(1.0 + jnp.exp(-jnp.abs(gates)))) * (1.0 / GLA_TAU)
    og_out[...] = z[:, _O_OG:_O_FN]
    fcs = _dot_x3(z[:, _O_FN:_W_ALL], csh_ref[...], csl_ref[...])
    fre_out[...] = fcs[:, :GROUP_W]
    fim_out[...] = fcs[:, GROUP_W:]


def _proj(xs, mod, g1, win, qg, wq, kvg, wkv, wvt, gkw, gkb, csc, tabs, n_lat_tiles):
    B, S, D = xs.shape
    tm = TOK_TILE
    csh, csl = _split(csc)
    tok = lambda w: pl.BlockSpec((None, tm, w), lambda b, t: (b, t, 0))
    tab = pl.BlockSpec((tm, LANES), lambda b, t: (t, 0))
    shp = lambda w, dt: jax.ShapeDtypeStruct((B, S, w), dt)
    return pl.pallas_call(
        _proj_kernel,
        grid=(B, S // tm),
        in_specs=[tok(D),
                  pl.BlockSpec((None, 6, D), lambda b, t: (jnp.where(t >= n_lat_tiles, 2, b), 0, 0)),
                  _full((1, D)), _full(win.shape), _full((1, MLA_Q_LORA)), _full(wq.shape),
                  _full((1, MLA_KV_LORA)), _full(wkv.shape), _full(wvt.shape), _full(gkw.shape), _full(gkb.shape),
                  _full(csc.shape), _full(csc.shape), tab, tab, tab, tab],
        out_specs=[tok(512), tok(512), pl.BlockSpec((None, None, MLA_HEADS * LANES, tm), lambda b, t: (b, t, 0, 0)),
                   tok(768), tok(256), tok(256), tok(256), tok(256), tok(256), tok(256)],
        out_shape=[shp(512, BF16), shp(512, BF16), jax.ShapeDtypeStruct((B, S // tm, MLA_HEADS * LANES, tm), BF16),
                   shp(768, F32), shp(256, F32), shp(256, F32),
                   shp(256, F32), shp(256, F32), shp(256, F32), shp(256, F32)],
        compiler_params=_params("arbitrary", "arbitrary"),
        name="in_proj",
    )(xs, mod, g1, win, qg, wq, kvg, wkv, wvt, gkw, gkb, csh, csl, *tabs)


def _attn_kernel(q_ref, k_ref, vt_ref, o_ref, *, n_lat_tiles, tiles_per_chunk, n_lat, n_ctx):
    qi = pl.program_id(1)
    tq = q_ref.shape[0]
    tile = vt_ref.shape[2]
    n_hd = q_ref.shape[1] // LANES

    def heads(tile0, n_chunks, n_tiles):
        sub = LANES

        def body(c, carry):
            t0 = tile0 + c * n_tiles
            off = pl.multiple_of(t0 * tile, tile)
            state = list(carry)
            for j in range(n_tiles * tile // sub):
                for hd in range(n_hd):
                    m, acc = state[hd]
                    q = q_ref[:, hd * LANES:(hd + 1) * LANES]
                    kc = k_ref[pl.ds(off + j * sub, sub), hd * LANES:(hd + 1) * LANES]
                    s = lax.dot_general(kc, q, (((1,), (1,)), ((), ())), preferred_element_type=F32)
                    m_new = jnp.maximum(m, jnp.max(s, axis=0, keepdims=True))
                    alpha = jnp.exp2(m - m_new)
                    p = jnp.exp2(s - m_new).astype(BF16)
                    lo = (j * sub) % tile
                    vt = vt_ref[t0 + (j * sub) // tile, hd * LANES:(hd + 1) * LANES, lo:lo + sub]
                    state[hd] = (m_new, alpha * acc + jnp.dot(vt, p, preferred_element_type=F32))
            return tuple(state)

        one = (jnp.full((1, tq), -1e30, F32), jnp.zeros((LANES, tq), F32))
        res = lax.fori_loop(0, n_chunks, body, (one,) * n_hd)
        for hd in range(n_hd):
            acc_t = res[hd][1].T
            o_ref[:, hd * MLA_V:(hd + 1) * MLA_V] = acc_t[:, :MLA_V] / acc_t[:, MLA_V:MLA_V + 1]

    n_all = (n_lat + n_ctx) // tile

    @pl.when(qi < n_lat_tiles)
    def _():
        heads(0, n_all // tiles_per_chunk, tiles_per_chunk)

    @pl.when(qi >= n_lat_tiles)
    def _():
        heads(n_lat // tile, 1, n_ctx // tile)


def _attention(q, k, vt, n_lat, n_ctx):
    B, S, _ = q.shape
    tq = TOK_TILE
    n_tiles = vt.shape[1]
    tiles_per_chunk = 13 if n_tiles % 13 == 0 else 1
    kern = functools.partial(_attn_kernel, n_lat_tiles=n_lat // tq, tiles_per_chunk=tiles_per_chunk,
                             n_lat=n_lat, n_ctx=n_ctx)
    return pl.pallas_call(
        kern,
        grid=(B, S // tq),
        in_specs=[pl.BlockSpec((None, tq, 512), lambda b, t: (b, t, 0)),
                  pl.BlockSpec((None, S, 512), lambda b, t: (b, 0, 0), pipeline_mode=pl.Buffered(1)),
                  pl.BlockSpec((None, n_tiles, 512, vt.shape[3]), lambda b, t: (b, 0, 0, 0),
                               pipeline_mode=pl.Buffered(1))],
        out_specs=pl.BlockSpec((None, tq, 256), lambda b, t: (b, t, 0)),
        out_shape=jax.ShapeDtypeStruct((B, S, 256), F32),
        compiler_params=_params("arbitrary", "arbitrary"),
        name="mla_attention",
    )(q, k, vt)


def _hy_pre_kernel(z_ref, zp_ref, zn_ref, w_ref, b_ref, vg_ref, x0_ref, *, n_tiles):
    i = pl.program_id(1)
    z = z_ref[...]
    tm = z.shape[0]
    rows = lax.broadcasted_iota(jnp.int32, z.shape, 0)
    prev_row = jnp.where(i == 0, 0.0, zp_ref[7:8, :])
    next_row = jnp.where(i == n_tiles - 1, 0.0, zn_ref[0:1, :])
    z_m = jnp.where(rows == 0, prev_row, pltpu.roll(z, 1, 0))
    z_p = jnp.where(rows == tm - 1, next_row, pltpu.roll(z, tm - 1, 0))
    w = w_ref[...]
    u = z_m * w[0:1] + z * w[1:2] + z_p * w[2:3] + b_ref[...]
    vg_ref[...] = u[:, 2 * GROUP_W:] * u[:, GROUP_W:2 * GROUP_W]
    x0_ref[...] = u[:, :GROUP_W]


def _hy_pre(hyz, conv_w, conv_b, row0, n_rows):
    B, S, W = hyz.shape
    tm = TOK_TILE
    nt = n_rows // tm
    t0, r8, last8 = row0 // tm, row0 // 8, S // 8 - 1
    kern = functools.partial(_hy_pre_kernel, n_tiles=nt)
    out = jax.ShapeDtypeStruct((B, n_rows, GROUP_W), F32)
    return pl.pallas_call(
        kern,
        grid=(B, nt),
        in_specs=[pl.BlockSpec((None, tm, W), lambda b, i: (b, t0 + i, 0)),
                  pl.BlockSpec((None, 8, W), lambda b, i: (b, jnp.maximum(r8 + i * (tm // 8) - 1, 0), 0)),
                  pl.BlockSpec((None, 8, W), lambda b, i: (b, jnp.minimum(r8 + (i + 1) * (tm // 8), last8), 0)),
                  _full((3, W)), _full((1, W))],
        out_specs=[pl.BlockSpec((None, tm, GROUP_W), lambda b, i: (b, i, 0))] * 2,
        out_shape=[out, out],
        compiler_params=_params("arbitrary", "arbitrary"),
        name="hyena_pre",
    )(hyz, hyz, hyz, conv_w, conv_b.reshape(1, W))


def _filter_kernel(step_ref, cb_ref, sb_ref, w1_ref, b1_ref, w2_ref, b2_ref, w3_ref, fr_ref, dl_ref, h_ref, ss_ref, *,
                   n_pos):
    i = pl.program_id(0)
    tl = h_ref.shape[0]
    row = lax.broadcasted_iota(jnp.int32, (tl, LANES), 0) + i * tl
    pos = jnp.where(row <= n_pos, row, 2 * n_pos - row).astype(F32)
    lane = lax.broadcasted_iota(jnp.int32, (tl, LANES), 1)
    t = pos * (1.0 / (n_pos - 1))
    st = step_ref[...]
    cb, sb = cb_ref[...], sb_ref[...]
    cos_a = st[0:1] * cb - st[3:4] * sb
    sin_a = st[1:2] * cb + st[2:3] * sb
    feat = jnp.where(lane == 0, t, jnp.where(lane < 17, cos_a, jnp.where(lane < HY_EMB, -sin_a, 0.0)))
    fr = fr_ref[...]
    h = jnp.sin(fr * (_dot_hi(feat, w1_ref[...]) + b1_ref[...]))
    h = jnp.sin(fr * (_dot_hi(h, w2_ref[...]) + b2_ref[...]))
    h = _dot_hi(h, w3_ref[...]) * jnp.exp(-t[:, 0:1] * dl_ref[...])
    r1 = row[:, 0:1]
    h = jnp.where(r1 < n_pos, h[:, :GROUP_W], jnp.where(r1 == n_pos, 0.0, h[:, GROUP_W:]))
    h_ref[...] = h

    @pl.when(i == 0)
    def _():
        ss_ref[...] = jnp.zeros_like(ss_ref)

    ss_ref[...] += jnp.sum(h * h, axis=0, keepdims=True)


def _hyena_filter(n_pos, w1, b1, w2, b2, w3, freq):
    tl = min(n_pos, 512)
    n_steps = 2 * n_pos // tl
    bands = (HY_EMB - 1) // 2
    f = np.linspace(1e-4, bands - 1, bands)
    fv = np.zeros((LANES,))
    fv[1:17] = f
    fv[17:33] = f
    step0 = np.arange(n_steps) * tl
    mirrored = step0 >= n_pos
    base = 2.0 * np.pi * np.where(mirrored, 2 * n_pos - step0, step0)[:, None] * fv[None, :] / n_pos
    sign = np.where(mirrored, -1.0, 1.0)[:, None]
    steps = np.zeros((n_steps, 8, LANES))
    steps[:, 0], steps[:, 1] = np.cos(base), np.sin(base)
    steps[:, 2], steps[:, 3] = sign * np.cos(base), sign * np.sin(base)
    inner = 2.0 * np.pi * np.arange(tl)[:, None] * fv[None, :] / n_pos
    w1p = jnp.zeros((LANES, HY_FFN), F32).at[:HY_EMB].set(w1)
    max_decay = math.log(HY_TARGET) / HY_FAST_PCT
    min_decay = math.log(HY_TARGET) / HY_SLOW_PCT
    deltas = np.abs(np.linspace(min_decay, max_decay, GROUP_W)).astype(np.float32)
    dl = jnp.asarray(np.concatenate([deltas, deltas])[None, :])
    kern = functools.partial(_filter_kernel, n_pos=n_pos)
    k2, ss = pl.pallas_call(
        kern,
        grid=(n_steps,),
        in_specs=[pl.BlockSpec((None, 8, LANES), lambda i: (i, 0, 0)), _full((tl, LANES)), _full((tl, LANES)),
                  _full((LANES, HY_FFN)), _full((1, HY_FFN)), _full((HY_FFN, HY_FFN)),
                  _full((1, HY_FFN)), _full((HY_FFN, 2 * GROUP_W)), _full((1, HY_FFN)), _full((1, 2 * GROUP_W))],
        out_specs=[pl.BlockSpec((tl, GROUP_W), lambda i: (i, 0)), _full((1, GROUP_W))],
        out_shape=[jax.ShapeDtypeStruct((2 * n_pos, GROUP_W), F32), jax.ShapeDtypeStruct((1, GROUP_W), F32)],
        compiler_params=_params("arbitrary"),
        name="hyena_filter",
    )(jnp.asarray(steps, F32), jnp.asarray(np.cos(inner), F32), jnp.asarray(np.sin(inner), F32),
      w1p, b1.reshape(1, -1), w2, b2.reshape(1, -1), w3, freq.reshape(1, -1), dl)
    return k2, lax.rsqrt(ss)


def _dft_mats(n, sign):
    j = np.arange(n)
    ang = 2.0 * np.pi * ((j[:, None] * j[None, :]) % n) / n
    return np.cos(ang), sign * np.sin(ang)


def _twiddle(n1, n2, sign):
    ang = 2.0 * np.pi * ((np.arange(n1)[:, None] * np.arange(n2)[None, :]) % (n1 * n2)) / (n1 * n2)
    return np.cos(ang), sign * np.sin(ang)


def _block_complex(re, im):
    return np.block([[re, -im], [im, re]])


N2_BLK = 8


def _stage1_kernel(m_ref, zr_ref, zi_ref, ar_ref, ai_ref):
    half = ar_ref.shape[0]
    for j in range(N2_BLK):
        a = _dot_hi(m_ref[...], jnp.concatenate([zr_ref[:, j, :], zi_ref[:, j, :]], axis=0))
        ar_ref[:, j, :] = a[:half]
        ai_ref[:, j, :] = a[half:]


def _stage1_real_kernel(m_ref, zr_ref, ar_ref, ai_ref):
    half = ar_ref.shape[0]
    for j in range(N2_BLK):
        a = _dot_hi(m_ref[...], zr_ref[:, j, :])
        ar_ref[:, j, :] = a[:half]
        ai_ref[:, j, :] = a[half:]


def _hy_stage1(vg4, n1):
    _, r, n2, c = vg4.shape
    cr, ci = _dft_mats(n1, -1.0)
    m = jnp.asarray(_block_complex(cr[:, :r], ci[:, :r]), F32)
    out = jax.ShapeDtypeStruct((n1, n2, c), F32)
    return pl.pallas_call(
        _stage1_kernel,
        grid=(n2 // N2_BLK,),
        in_specs=[_full(m.shape), pl.BlockSpec((None, r, N2_BLK, c), lambda j: (0, 0, j, 0)),
                  pl.BlockSpec((None, r, N2_BLK, c), lambda j: (1, 0, j, 0))],
        out_specs=[pl.BlockSpec((n1, N2_BLK, c), lambda j: (0, j, 0))] * 2,
        out_shape=[out, out],
        compiler_params=_params("arbitrary"),
        name="hyena_fwd_stage1",
    )(m, vg4, vg4)


def _filter_stage1(k3, n1):
    _, n2, c = k3.shape
    cr, ci = _dft_mats(n1, -1.0)
    m = jnp.asarray(np.concatenate([cr, ci], axis=0), F32)
    out = jax.ShapeDtypeStruct((n1, n2, c), F32)
    blk = pl.BlockSpec((n1, N2_BLK, c), lambda j: (0, j, 0))
    return pl.pallas_call(
        _stage1_real_kernel,
        grid=(n2 // N2_BLK,),
        in_specs=[_full(m.shape), blk],
        out_specs=[blk, blk],
        out_shape=[out, out],
        compiler_params=_params("arbitrary"),
        name="hyena_filter_stage1",
    )(m, k3)


def _twiddled(f_re, f_im, t_re, t_im):
    g_re = f_re * t_re - f_im * t_im
    g_im = f_re * t_im + f_im * t_re
    return jnp.concatenate([jnp.concatenate([g_re, -g_im], axis=1), jnp.concatenate([g_im, g_re], axis=1)], axis=0)


def _filter_stage2_kernel(ar_ref, ai_ref, twr_ref, twi_ref, fr_ref, fi_ref, sc_ref, kr_ref, ki_ref):
    n2 = kr_ref.shape[1]
    for j in range(N2_BLK):
        g = _twiddled(fr_ref[...], fi_ref[...], twr_ref[j], twi_ref[j])
        x = _dot_hi(g, jnp.concatenate([ar_ref[j], ai_ref[j]], axis=0)) * sc_ref[...]
        kr_ref[j] = x[:n2]
        ki_ref[j] = x[n2:]


def _filter_stage2(ar, ai, inv_norm, n1):
    _, n2, c = ar.shape
    fr, fi = _dft_mats(n2, -1.0)
    twr, twi = _twiddle(n1, n2, -1.0)
    blk = pl.BlockSpec((N2_BLK, n2, c), lambda k: (k, 0, 0))
    tw = pl.BlockSpec((N2_BLK, 1, n2), lambda k: (k, 0, 0))
    out = jax.ShapeDtypeStruct((n1, n2, c), F32)
    return pl.pallas_call(
        _filter_stage2_kernel,
        grid=(n1 // N2_BLK,),
        in_specs=[blk, blk, tw, tw, _full((n2, n2)), _full((n2, n2)), _full((1, c))],
        out_specs=[blk, blk],
        out_shape=[out, out],
        compiler_params=_params("arbitrary"),
        name="hyena_filter_stage2",
    )(ar, ai, jnp.asarray(twr.reshape(n1, 1, n2), F32),
      jnp.asarray(twi.reshape(n1, 1, n2), F32), jnp.asarray(fr, F32), jnp.asarray(fi, F32), inv_norm)


def _hy_stage2_kernel(ar_ref, ai_ref, kr_ref, ki_ref, twr_ref, twi_ref, tcr_ref, tci_ref, fr_ref, fi_ref,
                      br_ref, bi_ref):
    f_re, f_im = fr_ref[...], fi_ref[...]
    n2 = f_re.shape[0]
    for j in range(N2_BLK):
        g = _twiddled(f_re, f_im, twr_ref[j], twi_ref[j])
        x = _dot_hi(g, jnp.concatenate([ar_ref[j], ai_ref[j]], axis=0))
        x_re, x_im = x[:n2], x[n2:]
        k_re, k_im = kr_ref[j], ki_ref[j]
        y = jnp.concatenate([x_re * k_re - x_im * k_im, x_re * k_im + x_im * k_re], axis=0)
        g_inv = _twiddled(f_re, -f_im, tcr_ref[j], -tci_ref[j])
        b = _dot_hi(g_inv, y)
        br_ref[j] = b[:n2]
        bi_ref[j] = b[n2:]


def _hy_stage2(ar, ai, kr, ki, n1):
    n2 = DFT_N2
    c = kr.shape[2]
    fr, fi = _dft_mats(n2, -1.0)
    twr, twi = _twiddle(n1, n2, -1.0)
    blk = pl.BlockSpec((N2_BLK, n2, c), lambda k: (k, 0, 0))
    tw = pl.BlockSpec((N2_BLK, 1, n2), lambda k: (k, 0, 0))
    twc = pl.BlockSpec((N2_BLK, n2, 1), lambda k: (k, 0, 0))
    out = jax.ShapeDtypeStruct((n1, n2, c), F32)
    return pl.pallas_call(
        _hy_stage2_kernel,
        grid=(n1 // N2_BLK,),
        in_specs=[blk, blk, blk, blk, tw, tw, twc, twc, _full((n2, n2)), _full((n2, n2))],
        out_specs=[blk, blk],
        out_shape=[out, out],
        compiler_params=_params("arbitrary"),
        name="hyena_conv_stage2",
    )(ar, ai, kr, ki,
      jnp.asarray(twr.reshape(n1, 1, n2), F32), jnp.asarray(twi.reshape(n1, 1, n2), F32),
      jnp.asarray(twr.reshape(n1, n2, 1), F32), jnp.asarray(twi.reshape(n1, n2, 1), F32),
      jnp.asarray(fr, F32), jnp.asarray(fi, F32))


def _hy_stage3_kernel(m_ref, br_ref, bi_ref, vg0_ref, vg1_ref, x00_ref, x01_ref, bias_ref, o_ref):
    r = vg0_ref.shape[0]
    bias = bias_ref[...]
    for j in range(N2_BLK):
        conv = _dot_hi(m_ref[...], jnp.concatenate([br_ref[:, j, :], bi_ref[:, j, :]], axis=0))
        o_ref[0, :r, j, :] = (conv[:r] + vg0_ref[:, j, :] * bias) * x00_ref[:, j, :]
        o_ref[1, :r, j, :] = (conv[r:] + vg1_ref[:, j, :] * bias) * x01_ref[:, j, :]
    if o_ref.shape[1] > r:
        o_ref[:, r:] = jnp.zeros((2, o_ref.shape[1] - r) + o_ref.shape[2:], F32)


def _hy_stage3(br, bi, vg4, x04, bias, n1, rows_total):
    _, r, n2, c = vg4.shape
    cr, ci = _dft_mats(n1, 1.0)
    m = jnp.asarray(_block_complex(cr[:r], ci[:r]) / (n1 * n2), F32)
    plane = lambda p: pl.BlockSpec((None, r, N2_BLK, c), lambda j: (p, 0, j, 0))
    spec = pl.BlockSpec((n1, N2_BLK, c), lambda j: (0, j, 0))
    return pl.pallas_call(
        _hy_stage3_kernel,
        grid=(n2 // N2_BLK,),
        in_specs=[_full(m.shape), spec, spec, plane(0), plane(1), plane(0), plane(1), _full((1, c))],
        out_specs=pl.BlockSpec((2, rows_total, N2_BLK, c), lambda j: (0, 0, j, 0)),
        out_shape=jax.ShapeDtypeStruct((2, rows_total, n2, c), F32),
        compiler_params=_params("arbitrary"),
        name="hyena_inv_stage3",
    )(m, br, bi, vg4, vg4, x04, x04, bias)


def _hyena_long(vg, x0, k2, inv_norm, bias, n_all):
    B, L, C = vg.shape
    assert B == 2
    n2 = DFT_N2
    n1 = 2 * L // n2
    r = L // n2
    fa_r, fa_i = _filter_stage1(k2.reshape(n1, n2, C), n1)
    kr, ki = _filter_stage2(fa_r, fa_i, inv_norm, n1)
    vg4 = vg.reshape(2, r, n2, C)
    ar, ai = _hy_stage1(vg4, n1)
    br, bi = _hy_stage2(ar, ai, kr, ki, n1)
    y = _hy_stage3(br, bi, vg4, x0.reshape(2, r, n2, C), bias.reshape(1, C), n1, n_all // n2)
    return y.reshape(2, n_all, C)


def _hy_ctx_kernel(vg_ref, x0_ref, kext_ref, sc_ref, bias_ref, buf_ref, o_ref):
    del buf_ref
    n = vg_ref.shape[0]

    def body(a, acc):
        src = vg_ref[pl.ds(pl.multiple_of(8 * a, 8), 8), :]
        win = pl.multiple_of(n - 8 * a, 8)
        for r in range(8):
            acc = acc + kext_ref[r, pl.ds(win, n), :] * src[r:r + 1]
        return acc

    acc = lax.fori_loop(0, n // 8, body, jnp.zeros(vg_ref.shape, F32))
    o_ref[...] = (acc * sc_ref[...] + vg_ref[...] * bias_ref[...]) * x0_ref[...]


def _hyena_ctx(vg, x0, k2, inv_norm, bias, buf, row0):
    B, n, C = vg.shape
    kext = jnp.concatenate([k2[n:], k2[:n]], axis=0)
    kext = jnp.stack([jnp.roll(kext, r, axis=0) for r in range(8)])
    blk = pl.BlockSpec((None, n, C), lambda b: (b, 0, 0))
    return pl.pallas_call(
        _hy_ctx_kernel,
        grid=(B,),
        in_specs=[blk, blk, _full((8, 2 * n, C)), _full((1, C)), _full((1, C)), pl.BlockSpec(memory_space=pl.ANY)],
        out_specs=pl.BlockSpec((None, n, C), lambda b: (b, row0 // n, 0)),
        out_shape=jax.ShapeDtypeStruct(buf.shape, F32),
        input_output_aliases={5: 0},
        compiler_params=_params("arbitrary"),
        name="hyena_ctx",
    )(vg, x0, kext, inv_norm, bias.reshape(1, C), buf)


def _fn_stage1_kernel(c_ref, s_ref, re_ref, im_ref, ar_ref, ai_ref):
    c, s = c_ref[...], s_ref[...]
    for j in range(N2_BLK):
        re, im = re_ref[:, j, :], im_ref[:, j, :]
        ar_ref[:, j, :] = _dot_hi(c, re) + _dot_hi(s, im)
        ai_ref[:, j, :] = _dot_hi(c, im) - _dot_hi(s, re)


def _fn_stage2_kernel(ar_ref, ai_ref, twr_ref, twi_ref, fr_ref, fi_ref, o_ref):
    f_re, f_im = fr_ref[...], fi_ref[...]
    n2 = f_re.shape[0]
    for j in range(N2_BLK):
        t_re, t_im = twr_ref[j], twi_ref[j]
        g_re = f_re * t_re - f_im * t_im
        g_im = f_re * t_im + f_im * t_re
        o_ref[:n2, j, :] = _dot_hi(g_re, ar_ref[j]) - _dot_hi(g_im, ai_ref[j])
    if o_ref.shape[0] > n2:
        o_ref[n2:] = jnp.zeros((o_ref.shape[0] - n2,) + o_ref.shape[1:], F32)


def _fnet_long(fre, fim, n_lat):
    B, S, C = fre.shape
    n2 = DFT_N2
    n1 = n_lat // n2
    c1, s1 = _dft_mats(n1, 1.0)
    rows = pl.BlockSpec((None, n1, N2_BLK, C), lambda b, j: (b, 0, j, 0))
    a_shape = jax.ShapeDtypeStruct((B, n1, n2, C), F32)
    ar, ai = pl.pallas_call(
        _fn_stage1_kernel,
        grid=(B, n2 // N2_BLK),
        in_specs=[_full((n1, n1)), _full((n1, n1)), rows, rows],
        out_specs=[rows, rows],
        out_shape=[a_shape, a_shape],
        compiler_params=_params("arbitrary", "arbitrary"),
        name="fnet_stage1",
    )(jnp.asarray(c1, F32), jnp.asarray(s1, F32), fre.reshape(B, S // n2, n2, C), fim.reshape(B, S // n2, n2, C))
    fr, fi = _dft_mats(n2, -1.0)
    scale = 1.0 / math.sqrt(n_lat * C)
    twr, twi = _twiddle(n1, n2, -1.0)
    blk = pl.BlockSpec((None, N2_BLK, n2, C), lambda b, k: (b, k, 0, 0))
    tw = pl.BlockSpec((N2_BLK, 1, n2), lambda b, k: (k, 0, 0))
    y = pl.pallas_call(
        _fn_stage2_kernel,
        grid=(B, n1 // N2_BLK),
        in_specs=[blk, blk, tw, tw, _full((n2, n2)), _full((n2, n2))],
        out_specs=pl.BlockSpec((None, S // n1, N2_BLK, C), lambda b, k: (b, 0, k, 0)),
        out_shape=jax.ShapeDtypeStruct((B, S // n1, n1, C), F32),
        compiler_params=_params("arbitrary", "arbitrary"),
        name="fnet_stage2",
    )(ar, ai, jnp.asarray(twr.reshape(n1, 1, n2), F32),
      jnp.asarray(twi.reshape(n1, 1, n2), F32), jnp.asarray(fr * scale, F32), jnp.asarray(fi * scale, F32))
    return y.reshape(B, S, C)


def _fn_ctx_kernel(c_ref, s_ref, re_ref, im_ref, buf_ref, o_ref):
    del buf_ref
    o_ref[...] = _dot_hi(c_ref[...], re_ref[...]) + _dot_hi(s_ref[...], im_ref[...])


def _fnet_ctx(fre, fim, n_lat, n_ctx, buf):
    B, S, C = fre.shape
    c1, s1 = _dft_mats(n_ctx, 1.0)
    scale = 1.0 / math.sqrt(n_ctx * C)
    blk = pl.BlockSpec((None, n_ctx, C), lambda b: (b, n_lat // n_ctx, 0))
    return pl.pallas_call(
        _fn_ctx_kernel,
        grid=(B,),
        in_specs=[_full((n_ctx, n_ctx)), _full((n_ctx, n_ctx)), blk, blk, pl.BlockSpec(memory_space=pl.ANY)],
        out_specs=blk,
        out_shape=jax.ShapeDtypeStruct((B, S, C), F32),
        input_output_aliases={4: 0},
        compiler_params=_params("arbitrary"),
        name="fnet_ctx",
    )(jnp.asarray(c1 * scale, F32), jnp.asarray(s1 * scale, F32), fre, fim, buf)


def _gla_kernel(qkf_ref, vf_ref, gf_ref, qkb_ref, vb_ref, gb_ref, of_ref, ob_ref, st_ref):
    n = pl.program_id(0)
    n_b, tm = qkf_ref.shape[0], qkf_ref.shape[1]
    ck = GLA_CHUNK

    @pl.when(n == 0)
    def _():
        st_ref[...] = jnp.zeros_like(st_ref)

    ri = lax.broadcasted_iota(jnp.int32, (ck, ck), 0)
    ci = lax.broadcasted_iota(jnp.int32, (ck, ck), 1)

    def sub_chunk(qk_ref, v_ref, g_ref, o_ref, states, r0, reverse):
        keep = (ci >= ri) if reverse else (ci <= ri)
        g = g_ref[r0:r0 + ck, :]
        b = _dot_hi(keep.astype(F32), g)
        b_end = b[0:1] if reverse else b[ck - 1:ck]
        q = qk_ref[r0:r0 + ck, :LANES] * jnp.exp(b)
        k = qk_ref[r0:r0 + ck, LANES:]
        k_in = k * jnp.exp(-b)
        k_out = k * jnp.exp(b_end - b)
        decay = jnp.exp(b_end)
        new_states = []
        for hd in range(GLA_HEADS):
            ks = slice(hd * GLA_DK, (hd + 1) * GLA_DK)
            vs = slice(hd * GLA_DV, (hd + 1) * GLA_DV)
            qh, vh = q[:, ks].astype(BF16), v_ref[r0:r0 + ck, vs].astype(BF16)
            a = lax.dot_general(qh, k_in[:, ks].astype(BF16), (((1,), (1,)), ((), ())), preferred_element_type=F32)
            a = jnp.where(keep, a, 0.0)
            st = states[hd]
            o = jnp.dot(a.astype(BF16), vh, preferred_element_type=F32)
            o += lax.dot_general(qh, st.astype(BF16), (((1,), (1,)), ((), ())), preferred_element_type=F32)
            o_ref[r0:r0 + ck, vs] = o
            kv = lax.dot_general(vh, k_out[:, ks].astype(BF16), (((0,), (0,)), ((), ())), preferred_element_type=F32)
            new_states.append(st * decay[:, ks] + kv)
        return new_states

    st_f = [[st_ref[0, b, hd] for hd in range(GLA_HEADS)] for b in range(n_b)]
    st_b = [[st_ref[1, b, hd] for hd in range(GLA_HEADS)] for b in range(n_b)]
    n_sub = tm // ck
    for i in range(n_sub):
        for b in range(n_b):
            st_f[b] = sub_chunk(qkf_ref.at[b], vf_ref.at[b], gf_ref.at[b], of_ref.at[b], st_f[b], i * ck, False)
            st_b[b] = sub_chunk(qkb_ref.at[b], vb_ref.at[b], gb_ref.at[b], ob_ref.at[b], st_b[b],
                                (n_sub - 1 - i) * ck, True)
    for b in range(n_b):
        for hd in range(GLA_HEADS):
            st_ref[0, b, hd] = st_f[b][hd]
            st_ref[1, b, hd] = st_b[b][hd]


def _gla(gqk, gv, gg, n_lat):
    B, S, _ = gqk.shape
    tm = TOK_TILE
    nl = n_lat // tm
    nt = S // tm
    assert nt == nl + 1
    fwd = lambda n: jnp.where(n == 0, nl, n - 1)
    bwd = lambda n: jnp.where(n == 0, nl, nl - n)
    out = jax.ShapeDtypeStruct((B, S, 256), F32)
    return pl.pallas_call(
        _gla_kernel,
        grid=(nt,),
        in_specs=[pl.BlockSpec((B, tm, 256), lambda n: (0, fwd(n), 0)),
                  pl.BlockSpec((B, tm, 256), lambda n: (0, fwd(n), 0)),
                  pl.BlockSpec((B, tm, 128), lambda n: (0, fwd(n), 0)),
                  pl.BlockSpec((B, tm, 256), lambda n: (0, bwd(n), 0)),
                  pl.BlockSpec((B, tm, 256), lambda n: (0, bwd(n), 0)),
                  pl.BlockSpec((B, tm, 128), lambda n: (0, bwd(n), 1))],
        out_specs=[pl.BlockSpec((B, tm, 256), lambda n: (0, fwd(n), 0)),
                   pl.BlockSpec((B, tm, 256), lambda n: (0, bwd(n), 0))],
        out_shape=[out, out],
        scratch_shapes=[pltpu.VMEM((2, B, GLA_HEADS, GLA_DV, GLA_DK), F32)],
        compiler_params=_params("arbitrary"),
        name="gla_scan",
    )(gqk, gv, gg, gqk, gv, gg)


def _mix_kernel(x_ref, mod_ref, a_ref, hy_ref, of_ref, ob_ref, og_ref, fy_ref, fw_ref, mg_ref, hm_ref, wo_ref,
                n2g_ref, rwh_ref, rwl_ref, rb_ref, xo_ref, h2_ref, ti_ref, tg_ref, rk_ref, cnt_ref, run_ref):
    first = jnp.logical_and(pl.program_id(0) == 0, pl.program_id(1) == 0)

    @pl.when(first)
    def _():
        run_ref[...] = jnp.zeros_like(run_ref)

    mod = mod_ref[...]
    mg = mg_ref[...]
    hm = hm_ref[...]

    def head_rms(t):
        sq_hi, sq_lo = _split(t * t)
        ms = jnp.dot(sq_hi, hm, preferred_element_type=F32) + jnp.dot(sq_lo, hm, preferred_element_type=F32)
        return t * lax.rsqrt(ms + EPS)

    a = head_rms(a_ref[...]) * mg[:, 0:256]
    hy = _rms_rows(hy_ref[...]) * mg[:, 256:512]
    og = og_ref[...]
    o = head_rms(of_ref[...] + ob_ref[...]) * mg[:, 512:768] * (og / (1.0 + jnp.exp(-og)))
    fn = _rms_rows(_dot(fy_ref[...], fw_ref[...])) * mg[:, 768:1024]
    wo = wo_ref[...]
    y = _dot(a, wo[0:256]) + _dot(hy, wo[256:512]) + _dot(o, wo[512:768]) + _dot(fn, wo[768:1024])
    x = x_ref[...] + mod[2:3] * y
    xo_ref[...] = x

    h2 = _rms_rows(x) * n2g_ref[...] * (1.0 + mod[4:5]) + mod[3:4]
    h2_ref[...] = _pack_pairs(h2)

    logits = _dot_x3(h2, rwh_ref[...], rwl_ref[...]) + rb_ref[...]
    tm = logits.shape[0]
    lane = lax.broadcasted_iota(jnp.int32, logits.shape, 1).astype(F32)
    idx_out = jnp.zeros(logits.shape, F32)
    val_out = jnp.zeros(logits.shape, F32)
    chosen = jnp.zeros(logits.shape, F32)
    picks = []
    top = None
    den = jnp.zeros((tm, 1), F32)
    for kk in range(TOP_K):
        m = jnp.max(logits, axis=-1, keepdims=True)
        idx = jnp.min(jnp.where(logits == m, lane, float(LANES)), axis=-1, keepdims=True)
        if top is None:
            top = m
        e = jnp.exp(m - top)
        den = den + e
        hit = lane == idx
        picks.append(hit)
        chosen = jnp.where(hit, 1.0, chosen)
        idx_out = jnp.where(lane == kk, idx, idx_out)
        val_out = jnp.where(lane == kk, e, val_out)
        logits = jnp.where(hit, -jnp.inf, logits)
    ti_ref[...] = idx_out.astype(jnp.int32)
    tg_ref[...] = val_out / den

    ri = lax.broadcasted_iota(jnp.int32, (tm, tm), 0)
    ci = lax.broadcasted_iota(jnp.int32, (tm, tm), 1)
    before = _dot((ci < ri).astype(F32), chosen) + run_ref[...]
    rank = jnp.zeros(logits.shape, F32)
    for kk in range(TOP_K):
        r = jnp.sum(jnp.where(picks[kk], before, 0.0), axis=-1, keepdims=True)
        rank = jnp.where(lane == kk, r, rank)
    rk_ref[...] = rank.astype(jnp.int32)
    run_ref[...] += jnp.sum(chosen, axis=0, keepdims=True)
    cnt_ref[...] = run_ref[...]


def _mix(xs, mod, a, hy, o_f, o_b, og, fy, fnet_w, mix_g, head_mean, w_out, n2g, rw, rb, n_lat_tiles):
    B, S, D = xs.shape
    tm = TOK_TILE
    rwh, rwl = _split(rw)
    tok = lambda w: pl.BlockSpec((None, tm, w), lambda b, t: (b, t, 0))
    return pl.pallas_call(
        _mix_kernel,
        grid=(B, S // tm),
        in_specs=[tok(D),
                  pl.BlockSpec((None, 6, D), lambda b, t: (jnp.where(t >= n_lat_tiles, 2, b), 0, 0)),
                  tok(256), tok(256), tok(256), tok(256), tok(256), tok(256),
                  _full((256, 256)), _full((1, D)), _full((256, 256)), _full((D, D)), _full((1, D)),
                  _full((D, LANES)), _full((D, LANES)), _full((1, LANES))],
        out_specs=[tok(D), tok(D // 2), tok(LANES), tok(LANES), tok(LANES), _full((1, LANES))],
        out_shape=[jax.ShapeDtypeStruct((B, S, D), F32), jax.ShapeDtypeStruct((B, S, D // 2), jnp.uint32),
                   jax.ShapeDtypeStruct((B, S, LANES), jnp.int32), jax.ShapeDtypeStruct((B, S, LANES), F32),
                   jax.ShapeDtypeStruct((B, S, LANES), jnp.int32), jax.ShapeDtypeStruct((1, LANES), F32)],
        scratch_shapes=[pltpu.VMEM((1, LANES), F32)],
        compiler_params=_params("arbitrary", "arbitrary"),
        name="mix_out_router",
    )(xs, mod, a, hy, o_f, o_b, og, fy, fnet_w, mix_g, head_mean.astype(BF16), w_out, n2g, rwh, rwl, rb)


def _route(top_i, rank, counts):
    T = top_i.shape[0]
    bm = MOE_BM
    padded = (counts + bm - 1) // bm * bm
    pend = jnp.cumsum(padded)
    pstart = pend - padded
    pos = (pstart[top_i] + rank).astype(jnp.int32)
    n_blocks = (T * TOP_K + bm - 1) // bm + N_EXPERTS
    blk_row0 = jnp.arange(n_blocks, dtype=pend.dtype) * bm
    blk_e = jnp.minimum(jnp.sum(pend[None, :] <= blk_row0[:, None], axis=1), N_EXPERTS - 1).astype(jnp.int32)
    n_used = (pend[-1] // bm).astype(jnp.int32).reshape(1)
    return pos, blk_e, n_used, n_blocks * bm


def _dispatch_kernel(pos_ref, h_ref, zero_ref, xg_ref, sem):
    del zero_ref
    tm = h_ref.shape[0]

    def copy(t, kk):
        return pltpu.make_async_copy(h_ref.at[pl.ds(t, 1)], xg_ref.at[pl.ds(pos_ref[0, t * TOP_K + kk], 1)], sem)

    def start(t, c):
        for kk in range(TOP_K):
            copy(t, kk).start()
        return c

    def wait(t, c):
        for kk in range(TOP_K):
            copy(t, kk).wait()
        return c

    lax.fori_loop(0, tm, start, 0, unroll=8)
    lax.fori_loop(0, tm, wait, 0, unroll=8)


def _dispatch(h2w, pos, n_rows):
    T, W = h2w.shape
    tm = TOK_TILE
    nb = T // tm
    return pl.pallas_call(
        _dispatch_kernel,
        grid=(nb,),
        in_specs=[pl.BlockSpec((None, 1, tm * TOP_K), lambda i: (i, 0, 0), memory_space=pltpu.SMEM),
                  pl.BlockSpec((tm, W), lambda i: (i, 0)),
                  pl.BlockSpec(memory_space=pl.ANY)],
        out_specs=pl.BlockSpec(memory_space=pl.ANY),
        out_shape=jax.ShapeDtypeStruct((n_rows, W), h2w.dtype),
        scratch_shapes=[pltpu.SemaphoreType.DMA(())],
        input_output_aliases={2: 0},
        compiler_params=_params("arbitrary"),
        name="moe_dispatch",
    )(pos.reshape(nb, 1, tm * TOP_K), h2w, jnp.zeros((n_rows, W), h2w.dtype))


def _gu_prep_kernel(w_ref, p_ref, g_ref, u_ref):
    y = jnp.dot(w_ref[...].astype(BF16), p_ref[...], preferred_element_type=F32)
    half = y.shape[1] // 2
    g_ref[...] = y[:, :half].astype(BF16)
    u_ref[...] = y[:, half:].astype(BF16)


def _cast_kernel(w_ref, o_ref):
    o_ref[...] = w_ref[...].astype(BF16)


def _down_prep(w_down, layer):
    _, E, F, D = w_down.shape
    return pl.pallas_call(
        _cast_kernel,
        grid=(E,),
        in_specs=[pl.BlockSpec((None, None, F, D), lambda e: (layer, e, 0, 0))],
        out_specs=pl.BlockSpec((None, F, D), lambda e: (e, 0, 0)),
        out_shape=jax.ShapeDtypeStruct((E, F, D), BF16),
        compiler_params=_params("arbitrary"),
        name="moe_down_prep",
    )(w_down)


def _gu_prep(w_gu, layer):
    _, E, D, F2 = w_gu.shape
    tn = 512
    perm = np.zeros((tn, tn), np.float32)
    perm[2 * np.arange(tn // 2), np.arange(tn // 2)] = 1.0
    perm[2 * np.arange(tn // 2) + 1, tn // 2 + np.arange(tn // 2)] = 1.0
    out = jax.ShapeDtypeStruct((E, D, F2 // 2), BF16)
    return pl.pallas_call(
        _gu_prep_kernel,
        grid=(E, F2 // tn),
        in_specs=[pl.BlockSpec((None, None, D, tn), lambda e, j: (layer, e, 0, j)), _full((tn, tn))],
        out_specs=[pl.BlockSpec((None, D, tn // 2), lambda e, j: (e, 0, j))] * 2,
        out_shape=[out, out],
        compiler_params=_params("arbitrary", "arbitrary"),
        name="moe_weight_prep",
    )(w_gu, jnp.asarray(perm, BF16))


def _expert_kernel(be_ref, nu_ref, x_ref, wg_ref, wu_ref, bg_ref, bu_ref, wd_ref, bd_ref, o_ref):
    i = pl.program_id(0)

    @pl.when(i < nu_ref[0])
    def _():
        x = jnp.concatenate(_unpack_pairs(x_ref[...]), axis=1).astype(BF16)
        gate = jnp.minimum(jnp.dot(x, wg_ref[...], preferred_element_type=F32) + bg_ref[...], SWIGLU_LIMIT)
        up = jnp.clip(jnp.dot(x, wu_ref[...], preferred_element_type=F32) + bu_ref[...], -SWIGLU_LIMIT, SWIGLU_LIMIT)
        glu = gate / (1.0 + jnp.exp(-gate * SWIGLU_ALPHA))
        o_ref[...] = _pack_pairs(_dot((up + 1.0) * glu, wd_ref[...]) + bd_ref[...])

    @pl.when(i >= nu_ref[0])
    def _():
        o_ref[...] = jnp.zeros_like(o_ref)


def _experts(xg, blk_e, n_used, wg, wu, bg, bu, wd, bd):
    n_rows, W = xg.shape
    bm = MOE_BM
    D, F = wg.shape[1], wg.shape[2]
    wsel = lambda r, c: pl.BlockSpec((None, r, c), lambda i, be, nu: (be[i], 0, 0))
    return pl.pallas_call(
        _expert_kernel,
        grid_spec=pltpu.PrefetchScalarGridSpec(
            num_scalar_prefetch=2,
            grid=(n_rows // bm,),
            in_specs=[pl.BlockSpec((bm, W), lambda i, be, nu: (i, 0)),
                      wsel(D, F), wsel(D, F), wsel(1, F), wsel(1, F), wsel(F, D), wsel(1, D)],
            out_specs=pl.BlockSpec((bm, W), lambda i, be, nu: (i, 0)),
        ),
        out_shape=jax.ShapeDtypeStruct((n_rows, W), jnp.uint32),
        compiler_params=_params("arbitrary"),
        name="moe_experts",
    )(blk_e, n_used, xg, wg, wu, bg, bu, wd, bd)


def _combine_kernel(pos_ref, posn_ref, x_ref, mod_ref, tg_ref, fg_ref, y_ref, o_ref, buf, sem, *, final):
    i = pl.program_id(0)
    tm = x_ref.shape[0]
    slot = i % 2

    def copy(p_ref, s, t, kk):
        return pltpu.make_async_copy(y_ref.at[pl.ds(p_ref[0, t * TOP_K + kk], 1)], buf.at[s, kk, pl.ds(t, 1)],
                                     sem.at[s])

    def fetch(p_ref, s):
        def body(t, c):
            for kk in range(TOP_K):
                copy(p_ref, s, t, kk).start()
            return c
        lax.fori_loop(0, tm, body, 0, unroll=8)

    @pl.when(i == 0)
    def _():
        fetch(pos_ref, 0)

    @pl.when(i + 1 < pl.num_programs(0))
    def _():
        fetch(posn_ref, 1 - slot)

    def wait(t, c):
        for kk in range(TOP_K):
            copy(pos_ref, slot, t, kk).wait()
        return c

    lax.fori_loop(0, tm, wait, 0, unroll=8)
    tg = tg_ref[...]
    f_lo = jnp.zeros((tm, x_ref.shape[1] // 2), F32)
    f_hi = f_lo
    for kk in range(TOP_K):
        y_lo, y_hi = _unpack_pairs(buf[slot, kk])
        f_lo = f_lo + tg[:, kk:kk + 1] * y_lo
        f_hi = f_hi + tg[:, kk:kk + 1] * y_hi
    x = x_ref[...] + mod_ref[...][5:6] * jnp.concatenate([f_lo, f_hi], axis=1)
    o_ref[...] = _rms_rows(x) * fg_ref[...] if final else x


def _combine(xs, mod, tg, yb, pos, n_lat_tiles, final_g=None):
    B, S, D = xs.shape
    tm = TOK_TILE
    nt = S // tm
    final = final_g is not None
    per_b = n_lat_tiles if final else nt
    nb = B * per_b
    tile = lambda i: (i // per_b) * nt + i % per_b
    pos3 = pos.reshape(B * nt, 1, tm * TOP_K)
    row = lambda w: pl.BlockSpec((tm, w), lambda i: (tile(i), 0))
    fg = final_g.reshape(1, D) if final else jnp.ones((1, D), F32)
    out = pl.pallas_call(
        functools.partial(_combine_kernel, final=final),
        grid=(nb,),
        in_specs=[pl.BlockSpec((None, 1, tm * TOP_K), lambda i: (tile(i), 0, 0), memory_space=pltpu.SMEM),
                  pl.BlockSpec((None, 1, tm * TOP_K), lambda i: (tile(jnp.minimum(i + 1, nb - 1)), 0, 0),
                               memory_space=pltpu.SMEM),
                  row(D),
                  pl.BlockSpec((None, 6, D),
                               lambda i: (jnp.where(i % per_b >= n_lat_tiles, 2, i // per_b), 0, 0)),
                  row(LANES),
                  _full((1, D)),
                  pl.BlockSpec(memory_space=pl.ANY)],
        out_specs=pl.BlockSpec((tm, D), lambda i: (i, 0)),
        out_shape=jax.ShapeDtypeStruct((nb * tm, D), F32),
        scratch_shapes=[pltpu.VMEM((2, TOP_K, tm, D // 2), jnp.uint32), pltpu.SemaphoreType.DMA((2,))],
        compiler_params=_params("arbitrary"),
        name="moe_combine",
    )(pos3, pos3, xs.reshape(B * S, D), mod, tg.reshape(B * S, LANES), fg, yb)
    return out.reshape(B, nb // B * tm, D)


def _prep_w_in(w_in):
    cq, ckv, kr = w_in[:, 0:256], w_in[:, 256:384], w_in[:, 384:416]
    hy, gq, gk = w_in[:, 416:1184], w_in[:, 1184:1312], w_in[:, 1312:1440]
    gv, glr, og, fn = w_in[:, 1440:1696], w_in[:, 1696:1728], w_in[:, 1728:1984], w_in[:, 1984:2240]
    half = MLA_ROPE // 2
    kr_sw = jnp.concatenate([-kr[:, half:], kr[:, :half]], axis=1)
    place = lambda w: jnp.tile(jnp.pad(w, ((0, 0), (MLA_NOPE, LANES - MLA_NOPE - MLA_ROPE))), (1, MLA_HEADS))
    glr_p = jnp.pad(glr, ((0, 0), (0, LANES - 2 * GLA_LR)))
    return jnp.concatenate([cq, ckv, place(kr), place(kr_sw), hy, gq, gk, gv, glr_p, og, fn], axis=1).astype(BF16)


def _prep_wq(w_uq):
    w = w_uq.reshape(MLA_Q_LORA, MLA_HEADS, MLA_NOPE + MLA_ROPE)
    nope, rope = w[..., :MLA_NOPE], w[..., MLA_NOPE:]
    half = MLA_ROPE // 2
    z_tail = jnp.zeros((MLA_Q_LORA, MLA_HEADS, LANES - MLA_NOPE - MLA_ROPE), F32)
    z_nope = jnp.zeros((MLA_Q_LORA, MLA_HEADS, MLA_NOPE), F32)
    plain = jnp.concatenate([nope, rope, z_tail], axis=-1).reshape(MLA_Q_LORA, MLA_HEADS * LANES)
    partner = jnp.concatenate([z_nope, -rope[..., half:], rope[..., :half], z_tail], axis=-1)
    return jnp.concatenate([plain, partner.reshape(MLA_Q_LORA, MLA_HEADS * LANES)], axis=1).astype(BF16)


def _prep_wkv(w_ukv):
    w = w_ukv.reshape(MLA_KV_LORA, MLA_HEADS, MLA_NOPE + MLA_V)
    k_nope, v = w[..., :MLA_NOPE], w[..., MLA_NOPE:]
    k_placed = jnp.pad(k_nope, ((0, 0), (0, 0), (0, LANES - MLA_NOPE))).reshape(MLA_KV_LORA, MLA_HEADS * LANES)
    v_t = jnp.pad(v, ((0, 0), (0, 0), (0, LANES - MLA_V))).reshape(MLA_KV_LORA, MLA_HEADS * LANES).T
    return k_placed.astype(BF16), v_t.astype(BF16)


def _prep_gk(gk_w, gk_b):
    w = jnp.zeros((LANES, 2 * LANES), F32)
    w = w.at[0:GLA_LR, 0:LANES].set(gk_w[0]).at[GLA_LR:2 * GLA_LR, LANES:].set(gk_w[1])
    return w.astype(BF16), jnp.concatenate([gk_b[0], gk_b[1]]).reshape(1, 2 * LANES)


def _rope_tables(n_lat, n_ctx):
    rows = n_lat // GRID_W
    row = jnp.repeat(jnp.arange(rows, dtype=F32), GRID_W)
    col = jnp.tile(jnp.arange(GRID_W, dtype=F32), rows)
    n_freq = MLA_ROPE // 4
    inv = ROPE_THETA ** (-jnp.arange(n_freq, dtype=F32) / n_freq)
    ang = jnp.concatenate([row[:, None] * inv, col[:, None] * inv], axis=-1)
    cos = jnp.concatenate([jnp.cos(ang), jnp.ones((n_ctx, MLA_ROPE // 2), F32)], axis=0)
    sin = jnp.concatenate([jnp.sin(ang), jnp.zeros((n_ctx, MLA_ROPE // 2), F32)], axis=0)
    S = n_lat + n_ctx
    ones, zeros = jnp.ones((S, MLA_NOPE), F32), jnp.zeros((S, MLA_NOPE), F32)
    tail = jnp.zeros((S, LANES - MLA_NOPE - MLA_ROPE), F32)
    q_scale = MLA_SCALE * math.log2(math.e)
    cq = jnp.concatenate([ones, cos, cos, tail], axis=1) * q_scale
    sq = jnp.concatenate([zeros, sin, sin, tail], axis=1) * q_scale
    ck = jnp.concatenate([zeros, cos, cos, tail], axis=1)
    sk = jnp.concatenate([zeros, sin, sin, tail], axis=1)
    return cq, sq, ck, sk


def kernel(x, c, ctx, c_ctx, ada_w, ada_b, norm1_g, norm2_g, w_in, mla_q_g, mla_w_uq, mla_kv_g, mla_w_ukv, hy_conv_w, hy_conv_b, hy_w1, hy_b1, hy_w2, hy_b2, hy_w3, hy_freq, hy_bias, gla_gk_w, gla_gk_b, fnet_w, mix_g, w_out, router_w, router_b, moe_w_gu, moe_b_gu, moe_w_down, moe_b_down, final_g):
    B, L, D = x.shape
    Lc = ctx.shape[1]
    depth = ada_w.shape[0]
    S = L + Lc
    tm = TOK_TILE
    nlt = L // tm
    assert B == 2 and D == D_MODEL and Lc == tm and L % (2 * DFT_N2 * 8) == 0

    xs = jnp.concatenate([x, ctx], axis=1)
    cc = jnp.zeros((8, D), F32).at[0:B].set(c).at[B].set(c_ctx)
    mods = _mods(cc, ada_w, ada_b)
    tabs = _rope_tables(L, Lc)
    jc = np.arange(GROUP_W)
    ang_c = 2.0 * np.pi * ((jc[:, None] * jc[None, :]) % GROUP_W) / GROUP_W
    csc = jnp.asarray(np.concatenate([np.cos(ang_c), -np.sin(ang_c)], axis=1), F32)
    head_mean = jnp.asarray(np.kron(np.eye(GROUP_W // MLA_V), np.full((MLA_V, MLA_V), 1.0 / MLA_V)), F32)

    for i in range(depth):
        mod = mods[i].reshape(8, 6, D)
        gkw, gkb = _prep_gk(gla_gk_w[i], gla_gk_b[i])
        q, k, v, hyz, gqk, gv, gg, og, fre, fim = _proj(
            xs, mod, norm1_g[i].reshape(1, D), _prep_w_in(w_in[i]), mla_q_g[i].reshape(1, -1), _prep_wq(mla_w_uq[i]),
            mla_kv_g[i].reshape(1, -1), *_prep_wkv(mla_w_ukv[i]), gkw, gkb, csc, tabs, nlt)

        a = _attention(q, k, v, L, Lc)

        filt = (hy_w1[i], hy_b1[i], hy_w2[i], hy_b2[i], hy_w3[i], hy_freq[i])
        vg_l, x0_l = _hy_pre(hyz, hy_conv_w[i], hy_conv_b[i], 0, L)
        hy = _hyena_long(vg_l, x0_l, *_hyena_filter(L, *filt), hy_bias[i], S)
        vg_c, x0_c = _hy_pre(hyz, hy_conv_w[i], hy_conv_b[i], L, Lc)
        hy = _hyena_ctx(vg_c, x0_c, *_hyena_filter(Lc, *filt), hy_bias[i], hy, L)

        o_f, o_b = _gla(gqk, gv, gg, L)

        fy = _fnet_ctx(fre, fim, L, Lc, _fnet_long(fre, fim, L))

        rw = jnp.pad(router_w[i], ((0, 0), (0, LANES - N_EXPERTS)))
        rb = jnp.concatenate([router_b[i], jnp.full((LANES - N_EXPERTS,), -1e30, F32)]).reshape(1, LANES)
        xs, h2w, top_i, top_g, rank, cnt = _mix(
            xs, mod, a, hy, o_f, o_b, og, fy, fnet_w[i].astype(BF16), mix_g[i].reshape(1, D), head_mean,
            w_out[i].astype(BF16), norm2_g[i].reshape(1, D), rw, rb, nlt)

        pos, blk_e, n_used, n_rows = _route(top_i.reshape(B * S, LANES)[:, :TOP_K],
                                            rank.reshape(B * S, LANES)[:, :TOP_K],
                                            cnt[0, :N_EXPERTS].astype(jnp.int32))
        xg = _dispatch(h2w.reshape(B * S, D // 2), pos, n_rows)
        wg, wu = _gu_prep(moe_w_gu, i)
        yb = _experts(xg, blk_e, n_used, wg, wu, moe_b_gu[i][:, None, 0::2], moe_b_gu[i][:, None, 1::2],
                      _down_prep(moe_w_down, i), moe_b_down[i][:, None, :])
        xs = _combine(xs, mod, top_g, yb, pos, nlt, final_g if i == depth - 1 else None)

    return xs
```

```python
import functools
import math

import numpy as np
import jax
import jax.numpy as jnp
from jax import lax
from jax.experimental import pallas as pl
from jax.experimental.pallas import tpu as pltpu

F32 = jnp.float32
BF16 = jnp.bfloat16
HIGHEST = lax.Precision.HIGHEST

EPS = 1e-6
D_MODEL = 1024
GROUP_W = 256
MLA_HEADS = 4
MLA_NOPE = 64
MLA_ROPE = 32
MLA_V = 64
MLA_Q_LORA = 256
MLA_KV_LORA = 128
MLA_SCALE = (MLA_NOPE + MLA_ROPE) ** -0.5
ROPE_THETA = 10000.0
GRID_W = 64
HY_EMB = 33
HY_FFN = 64
HY_TARGET = 1e-2
HY_FAST_PCT = 0.3
HY_SLOW_PCT = 1.5
GLA_HEADS = 4
GLA_DK = 32
GLA_DV = 64
GLA_LR = 16
GLA_TAU = 16.0
GLA_CHUNK = 64
N_EXPERTS = 32
TOP_K = 4
SWIGLU_ALPHA = 1.702
SWIGLU_LIMIT = 7.0

LANES = 128
TOK_TILE = 256
DFT_N2 = 128
MOE_BM = 512
VMEM_LIMIT = 56 * 1024 * 1024

_O_CQ, _O_CKV, _O_KRP, _O_KRS, _O_HY, _O_GQ, _O_GK, _O_GV, _O_GLR, _O_OG, _O_FN, _W_ALL = (
    0, 256, 384, 896, 1408, 2176, 2304, 2432, 2688, 2816, 3072, 3328)


def _dot(a, b):
    return jnp.dot(a.astype(BF16), b.astype(BF16), preferred_element_type=F32)


def _dot_hi(a, b):
    return jnp.dot(a, b, precision=HIGHEST, preferred_element_type=F32)


def _split(a):
    hi = a.astype(BF16)
    return hi, (a - hi.astype(F32)).astype(BF16)


def _dot_x3(a, b_hi, b_lo):
    a_hi, a_lo = _split(a)
    mm = lambda u, w: jnp.dot(u, w, preferred_element_type=F32)
    return mm(a_hi, b_hi) + (mm(a_lo, b_hi) + mm(a_hi, b_lo))


def _pack_pairs(x):
    bits = pltpu.bitcast(x.astype(BF16).astype(F32), jnp.uint32)
    w = bits.shape[1] // 2
    return (bits[:, :w] >> 16) | (bits[:, w:] & jnp.uint32(0xFFFF0000))


def _unpack_pairs(words):
    return pltpu.bitcast(words << 16, F32), pltpu.bitcast(words & jnp.uint32(0xFFFF0000), F32)


def _params(*sem):
    return pltpu.CompilerParams(dimension_semantics=sem, vmem_limit_bytes=VMEM_LIMIT)


def _rms_rows(x):
    return x * lax.rsqrt(jnp.mean(x * x, axis=-1, keepdims=True) + EPS)


def _full(shape):
    n = len(shape)
    return pl.BlockSpec(shape, lambda *_: (0,) * n)


def _mods_kernel(c_ref, w_ref, b_ref, o_ref):
    c = c_ref[...]
    s = c / (1.0 + jnp.exp(-c))
    o_ref[...] = _dot_hi(s, w_ref[...]) + b_ref[...]


def _mods(cc, ada_w, ada_b):
    depth, d, n = ada_w.shape
    tn = 1024
    return pl.pallas_call(
        _mods_kernel,
        grid=(depth, n // tn),
        in_specs=[pl.BlockSpec((8, d), lambda i, j: (0, 0)),
                  pl.BlockSpec((None, d, tn), lambda i, j: (i, 0, j)),
                  pl.BlockSpec((None, 1, tn), lambda i, j: (i, 0, j))],
        out_specs=pl.BlockSpec((None, 8, tn), lambda i, j: (i, 0, j)),
        out_shape=jax.ShapeDtypeStruct((depth, 8, n), F32),
        compiler_params=_params("arbitrary", "arbitrary"),
        name="ada_mods",
    )(cc, ada_w, ada_b.reshape(depth, 1, n))


def _proj_kernel(x_ref, mod_ref, g_ref, win_ref, qg_ref, wq_ref, kvg_ref, wkv_ref, wvt_ref, gkw_ref, gkb_ref,
                 csh_ref, csl_ref, cq_ref, sq_ref, ck_ref, sk_ref,
                 q_out, k_out, v_out, hy_out, gqk_out, gv_out, gg_out, og_out, fre_out, fim_out):
    x = x_ref[...]
    mod = mod_ref[...]
    h = _rms_rows(x) * g_ref[...] * (1.0 + mod[1:2]) + mod[0:1]
    z = _dot(h, win_ref[...])

    nq = _rms_rows(z[:, _O_CQ:_O_CKV]) * qg_ref[...]
    qq = _dot(nq, wq_ref[...])
    nkv = _rms_rows(z[:, _O_CKV:_O_KRP]) * kvg_ref[...]
    nkv = nkv.astype(BF16)
    kvu = jnp.dot(nkv, wkv_ref[...], preferred_element_type=F32)
    cq, sq, ck, sk = cq_ref[...], sq_ref[...], ck_ref[...], sk_ref[...]
    for hd in range(MLA_HEADS):
        a, b = hd * LANES, (hd + 1) * LANES
        q_out[:, a:b] = (qq[:, a:b] * cq + qq[:, 512 + a:512 + b] * sq).astype(BF16)
        k_out[:, a:b] = (kvu[:, a:b] + z[:, _O_KRP + a:_O_KRP + b] * ck
                         + z[:, _O_KRS + a:_O_KRS + b] * sk).astype(BF16)
    vt = lax.dot_general(wvt_ref[...], nkv, (((1,), (1,)), ((), ())), preferred_element_type=F32)
    vrow = lax.broadcasted_iota(jnp.int32, vt.shape, 0)
    v_out[...] = jnp.where(vrow % LANES == MLA_V, 1.0, vt).astype(BF16)

    hy_out[...] = z[:, _O_HY:_O_GQ]
    gqk_out[:, :LANES] = z[:, _O_GQ:_O_GK] * (GLA_DK ** -0.5)
    gqk_out[:, LANES:] = z[:, _O_GK:_O_GV]
    gv_out[...] = z[:, _O_GV:_O_GLR]
    gates = _dot(z[:, _O_GLR:_O_OG], gkw_ref[...]) + gkb_ref[...]
    gg_out[...] = (jnp.minimum(gates, 0.0) - jnp.log(1.0 + jnp.exp(-jnp.abs(gates)))) * (1.0 / GLA_TAU)
    og_out[...] = z[:, _O_OG:_O_FN]
    fcs = _dot_x3(z[:, _O_FN:_W_ALL], csh_ref[...], csl_ref[...])
    fre_out[...] = fcs[:, :GROUP_W]
    fim_out[...] = fcs[:, GROUP_W:]


def _proj(xs, mod, g1, win, qg, wq, kvg, wkv, wvt, gkw, gkb, csc, tabs, n_lat_tiles):
    B, S, D = xs.shape
    tm = TOK_TILE
    csh, csl = _split(csc)
    tok = lambda w: pl.BlockSpec((None, tm, w), lambda b, t: (b, t, 0))
    tab = pl.BlockSpec((tm, LANES), lambda b, t: (t, 0))
    shp = lambda w, dt: jax.ShapeDtypeStruct((B, S, w), dt)
    return pl.pallas_call(
        _proj_kernel,
        grid=(B, S // tm),
        in_specs=[tok(D),
                  pl.BlockSpec((None, 6, D), lambda b, t: (jnp.where(t >= n_lat_tiles, 2, b), 0, 0)),
                  _full((1, D)), _full(win.shape), _full((1, MLA_Q_LORA)), _full(wq.shape),
                  _full((1, MLA_KV_LORA)), _full(wkv.shape), _full(wvt.shape), _full(gkw.shape), _full(gkb.shape),
                  _full(csc.shape), _full(csc.shape), tab, tab, tab, tab],
        out_specs=[tok(512), tok(512), pl.BlockSpec((None, None, MLA_HEADS * LANES, tm), lambda b, t: (b, t, 0, 0)),
                   tok(768), tok(256), tok(256), tok(256), tok(256), tok(256), tok(256)],
        out_shape=[shp(512, BF16), shp(512, BF16), jax.ShapeDtypeStruct((B, S // tm, MLA_HEADS * LANES, tm), BF16),
                   shp(768, F32), shp(256, F32), shp(256, F32),
                   shp(256, F32), shp(256, F32), shp(256, F32), shp(256, F32)],
        compiler_params=_params("arbitrary", "arbitrary"),
        name="in_proj",
    )(xs, mod, g1, win, qg, wq, kvg, wkv, wvt, gkw, gkb, csh, csl, *tabs)


def _attn_kernel(q_ref, k_ref, vt_ref, o_ref, *, n_lat_tiles, tiles_per_chunk, n_lat, n_ctx):
    qi = pl.program_id(1)
    tq = q_ref.shape[0]
    tile = vt_ref.shape[2]
    n_hd = q_ref.shape[1] // LANES

    def heads(tile0, n_chunks, n_tiles):
        sub = LANES

        def body(c, carry):
            t0 = tile0 + c * n_tiles
            off = pl.multiple_of(t0 * tile, tile)
            state = list(carry)
            for j in range(n_tiles * tile // sub):
                for hd in range(n_hd):
                    m, acc = state[hd]
                    q = q_ref[:, hd * LANES:(hd + 1) * LANES]
                    kc = k_ref[pl.ds(off + j * sub, sub), hd * LANES:(hd + 1) * LANES]
                    s = lax.dot_general(kc, q, (((1,), (1,)), ((), ())), preferred_element_type=F32)
                    m_new = jnp.maximum(m, jnp.max(s, axis=0, keepdims=True))
                    alpha = jnp.exp2(m - m_new)
                    p = jnp.exp2(s - m_new).astype(BF16)
                    lo = (j * sub) % tile
                    vt = vt_ref[t0 + (j * sub) // tile, hd * LANES:(hd + 1) * LANES, lo:lo + sub]
                    state[hd] = (m_new, alpha * acc + jnp.dot(vt, p, preferred_element_type=F32))
            return tuple(state)

        one = (jnp.full((1, tq), -1e30, F32), jnp.zeros((LANES, tq), F32))
        res = lax.fori_loop(0, n_chunks, body, (one,) * n_hd)
        for hd in range(n_hd):
            acc_t = res[hd][1].T
            o_ref[:, hd * MLA_V:(hd + 1) * MLA_V] = acc_t[:, :MLA_V] / acc_t[:, MLA_V:MLA_V + 1]

    n_all = (n_lat + n_ctx) // tile

    @pl.when(qi < n_lat_tiles)
    def _():
        heads(0, n_all // tiles_per_chunk, tiles_per_chunk)

    @pl.when(qi >= n_lat_tiles)
    def _():
        heads(n_lat // tile, 1, n_ctx // tile)


def _attention(q, k, vt, n_lat, n_ctx):
    B, S, _ = q.shape
    tq = TOK_TILE
    n_tiles = vt.shape[1]
    tiles_per_chunk = 13 if n_tiles % 13 == 0 else 1
    kern = functools.partial(_attn_kernel, n_lat_tiles=n_lat // tq, tiles_per_chunk=tiles_per_chunk,
                             n_lat=n_lat, n_ctx=n_ctx)
    return pl.pallas_call(
        kern,
        grid=(B, S // tq),
        in_specs=[pl.BlockSpec((None, tq, 512), lambda b, t: (b, t, 0)),
                  pl.BlockSpec((None, S, 512), lambda b, t: (b, 0, 0), pipeline_mode=pl.Buffered(1)),
                  pl.BlockSpec((None, n_tiles, 512, vt.shape[3]), lambda b, t: (b, 0, 0, 0),
                               pipeline_mode=pl.Buffered(1))],
        out_specs=pl.BlockSpec((None, tq, 256), lambda b, t: (b, t, 0)),
        out_shape=jax.ShapeDtypeStruct((B, S, 256), F32),
        compiler_params=_params("arbitrary", "arbitrary"),
        name="mla_attention",
    )(q, k, vt)


def _hy_pre_kernel(z_ref, zp_ref, zn_ref, w_ref, b_ref, vg_ref, x0_ref, *, n_tiles):
    i = pl.program_id(1)
    z = z_ref[...]
    tm = z.shape[0]
    rows = lax.broadcasted_iota(jnp.int32, z.shape, 0)
    prev_row = jnp.where(i == 0, 0.0, zp_ref[7:8, :])
    next_row = jnp.where(i == n_tiles - 1, 0.0, zn_ref[0:1, :])
    z_m = jnp.where(rows == 0, prev_row, pltpu.roll(z, 1, 0))
    z_p = jnp.where(rows == tm - 1, next_row, pltpu.roll(z, tm - 1, 0))
    w = w_ref[...]
    u = z_m * w[0:1] + z * w[1:2] + z_p * w[2:3] + b_ref[...]
    vg_ref[...] = u[:, 2 * GROUP_W:] * u[:, GROUP_W:2 * GROUP_W]
    x0_ref[...] = u[:, :GROUP_W]


def _hy_pre(hyz, conv_w, conv_b, row0, n_rows):
    B, S, W = hyz.shape
    tm = min(n_rows, 4 * TOK_TILE)
    assert n_rows % tm == 0 and row0 % tm == 0
    nt = n_rows // tm
    t0, r8, last8 = row0 // tm, row0 // 8, S // 8 - 1
    kern = functools.partial(_hy_pre_kernel, n_tiles=nt)
    out = jax.ShapeDtypeStruct((B, n_rows, GROUP_W), F32)
    return pl.pallas_call(
        kern,
        grid=(B, nt),
        in_specs=[pl.BlockSpec((None, tm, W), lambda b, i: (b, t0 + i, 0)),
                  pl.BlockSpec((None, 8, W), lambda b, i: (b, jnp.maximum(r8 + i * (tm // 8) - 1, 0), 0)),
                  pl.BlockSpec((None, 8, W), lambda b, i: (b, jnp.minimum(r8 + (i + 1) * (tm // 8), last8), 0)),
                  _full((3, W)), _full((1, W))],
        out_specs=[pl.BlockSpec((None, tm, GROUP_W), lambda b, i: (b, i, 0))] * 2,
        out_shape=[out, out],
        compiler_params=_params("arbitrary", "arbitrary"),
        name="hyena_pre",
    )(hyz, hyz, hyz, conv_w, conv_b.reshape(1, W))


def _filter_kernel(step_ref, cb_ref, sb_ref, w1_ref, b1_ref, w2_ref, b2_ref, w3_ref, fr_ref, dl_ref, h_ref, ss_ref, *,
                   n_pos):
    i = pl.program_id(0)
    tl = h_ref.shape[0]
    row = lax.broadcasted_iota(jnp.int32, (tl, LANES), 0) + i * tl
    pos = jnp.where(row <= n_pos, row, 2 * n_pos - row).astype(F32)
    lane = lax.broadcasted_iota(jnp.int32, (tl, LANES), 1)
    t = pos * (1.0 / (n_pos - 1))
    st = step_ref[...]
    cb, sb = cb_ref[...], sb_ref[...]
    cos_a = st[0:1] * cb - st[3:4] * sb
    sin_a = st[1:2] * cb + st[2:3] * sb
    feat = jnp.where(lane == 0, t, jnp.where(lane < 17, cos_a, jnp.where(lane < HY_EMB, -sin_a, 0.0)))
    fr = fr_ref[...]
    h = jnp.sin(fr * (_dot_hi(feat, w1_ref[...]) + b1_ref[...]))
    h = jnp.sin(fr * (_dot_hi(h, w2_ref[...]) + b2_ref[...]))
    h = _dot_hi(h, w3_ref[...]) * jnp.exp(-t[:, 0:1] * dl_ref[...])
    r1 = row[:, 0:1]
    h = jnp.where(r1 < n_pos, h[:, :GROUP_W], jnp.where(r1 == n_pos, 0.0, h[:, GROUP_W:]))
    h_ref[...] = h

    @pl.when(i == 0)
    def _():
        ss_ref[...] = jnp.zeros_like(ss_ref)

    ss_ref[...] += jnp.sum(h * h, axis=0, keepdims=True)


def _hyena_filter(n_pos, w1, b1, w2, b2, w3, freq):
    tl = min(n_pos, 512)
    n_steps = 2 * n_pos // tl
    bands = (HY_EMB - 1) // 2
    f = np.linspace(1e-4, bands - 1, bands)
    fv = np.zeros((LANES,))
    fv[1:17] = f
    fv[17:33] = f
    step0 = np.arange(n_steps) * tl
    mirrored = step0 >= n_pos
    base = 2.0 * np.pi * np.where(mirrored, 2 * n_pos - step0, step0)[:, None] * fv[None, :] / n_pos
    sign = np.where(mirrored, -1.0, 1.0)[:, None]
    steps = np.zeros((n_steps, 8, LANES))
    steps[:, 0], steps[:, 1] = np.cos(base), np.sin(base)
    steps[:, 2], steps[:, 3] = sign * np.cos(base), sign * np.sin(base)
    inner = 2.0 * np.pi * np.arange(tl)[:, None] * fv[None, :] / n_pos
    w1p = jnp.zeros((LANES, HY_FFN), F32).at[:HY_EMB].set(w1)
    max_decay = math.log(HY_TARGET) / HY_FAST_PCT
    min_decay = math.log(HY_TARGET) / HY_SLOW_PCT
    deltas = np.abs(np.linspace(min_decay, max_decay, GROUP_W)).astype(np.float32)
    dl = jnp.asarray(np.concatenate([deltas, deltas])[None, :])
    kern = functools.partial(_filter_kernel, n_pos=n_pos)
    k2, ss = pl.pallas_call(
        kern,
        grid=(n_steps,),
        in_specs=[pl.BlockSpec((None, 8, LANES), lambda i: (i, 0, 0)), _full((tl, LANES)), _full((tl, LANES)),
                  _full((LANES, HY_FFN)), _full((1, HY_FFN)), _full((HY_FFN, HY_FFN)),
                  _full((1, HY_FFN)), _full((HY_FFN, 2 * GROUP_W)), _full((1, HY_FFN)), _full((1, 2 * GROUP_W))],
        out_specs=[pl.BlockSpec((tl, GROUP_W), lambda i: (i, 0)), _full((1, GROUP_W))],
        out_shape=[jax.ShapeDtypeStruct((2 * n_pos, GROUP_W), F32), jax.ShapeDtypeStruct((1, GROUP_W), F32)],
        compiler_params=_params("arbitrary"),
        name="hyena_filter",
    )(jnp.asarray(steps, F32), jnp.asarray(np.cos(inner), F32), jnp.asarray(np.sin(inner), F32),
      w1p, b1.reshape(1, -1), w2, b2.reshape(1, -1), w3, freq.reshape(1, -1), dl)
    return k2, lax.rsqrt(ss)


def _dft_mats(n, sign):
    j = np.arange(n)
    ang = 2.0 * np.pi * ((j[:, None] * j[None, :]) % n) / n
    return np.cos(ang), sign * np.sin(ang)


def _twiddle(n1, n2, sign):
    ang = 2.0 * np.pi * ((np.arange(n1)[:, None] * np.arange(n2)[None, :]) % (n1 * n2)) / (n1 * n2)
    return np.cos(ang), sign * np.sin(ang)


def _block_complex(re, im):
    return np.block([[re, -im], [im, re]])


N2_BLK = 8


def _stage1_kernel(m_ref, zr_ref, zi_ref, ar_ref, ai_ref):
    half = ar_ref.shape[0]
    for j in range(N2_BLK):
        a = _dot_hi(m_ref[...], jnp.concatenate([zr_ref[:, j, :], zi_ref[:, j, :]], axis=0))
        ar_ref[:, j, :] = a[:half]
        ai_ref[:, j, :] = a[half:]


def _stage1_real_kernel(m_ref, zr_ref, ar_ref, ai_ref):
    half = ar_ref.shape[0]
    for j in range(N2_BLK):
        a = _dot_hi(m_ref[...], zr_ref[:, j, :])
        ar_ref[:, j, :] = a[:half]
        ai_ref[:, j, :] = a[half:]


def _hy_stage1(vg4, n1):
    _, r, n2, c = vg4.shape
    cr, ci = _dft_mats(n1, -1.0)
    m = jnp.asarray(_block_complex(cr[:, :r], ci[:, :r]), F32)
    out = jax.ShapeDtypeStruct((n1, n2, c), F32)
    return pl.pallas_call(
        _stage1_kernel,
        grid=(n2 // N2_BLK,),
        in_specs=[_full(m.shape), pl.BlockSpec((None, r, N2_BLK, c), lambda j: (0, 0, j, 0)),
                  pl.BlockSpec((None, r, N2_BLK, c), lambda j: (1, 0, j, 0))],
        out_specs=[pl.BlockSpec((n1, N2_BLK, c), lambda j: (0, j, 0))] * 2,
        out_shape=[out, out],
        compiler_params=_params("arbitrary"),
        name="hyena_fwd_stage1",
    )(m, vg4, vg4)


def _filter_stage1(k3, n1):
    _, n2, c = k3.shape
    cr, ci = _dft_mats(n1, -1.0)
    m = jnp.asarray(np.concatenate([cr, ci], axis=0), F32)
    out = jax.ShapeDtypeStruct((n1, n2, c), F32)
    blk = pl.BlockSpec((n1, N2_BLK, c), lambda j: (0, j, 0))
    return pl.pallas_call(
        _stage1_real_kernel,
        grid=(n2 // N2_BLK,),
        in_specs=[_full(m.shape), blk],
        out_specs=[blk, blk],
        out_shape=[out, out],
        compiler_params=_params("arbitrary"),
        name="hyena_filter_stage1",
    )(m, k3)


def _twiddled(f_re, f_im, t_re, t_im):
    g_re = f_re * t_re - f_im * t_im
    g_im = f_re * t_im + f_im * t_re
    return jnp.concatenate([jnp.concatenate([g_re, -g_im], axis=1), jnp.concatenate([g_im, g_re], axis=1)], axis=0)


def _filter_stage2_kernel(ar_ref, ai_ref, twr_ref, twi_ref, fr_ref, fi_ref, sc_ref, kr_ref, ki_ref):
    n2 = kr_ref.shape[1]
    for j in range(N2_BLK):
        g = _twiddled(fr_ref[...], fi_ref[...], twr_ref[j], twi_ref[j])
        x = _dot_hi(g, jnp.concatenate([ar_ref[j], ai_ref[j]], axis=0)) * sc_ref[...]
        kr_ref[j] = x[:n2]
        ki_ref[j] = x[n2:]


def _filter_stage2(ar, ai, inv_norm, n1):
    _, n2, c = ar.shape
    fr, fi = _dft_mats(n2, -1.0)
    twr, twi = _twiddle(n1, n2, -1.0)
    blk = pl.BlockSpec((N2_BLK, n2, c), lambda k: (k, 0, 0))
    tw = pl.BlockSpec((N2_BLK, 1, n2), lambda k: (k, 0, 0))
    out = jax.ShapeDtypeStruct((n1, n2, c), F32)
    return pl.pallas_call(
        _filter_stage2_kernel,
        grid=(n1 // N2_BLK,),
        in_specs=[blk, blk, tw, tw, _full((n2, n2)), _full((n2, n2)), _full((1, c))],
        out_specs=[blk, blk],
        out_shape=[out, out],
        compiler_params=_params("arbitrary"),
        name="hyena_filter_stage2",
    )(ar, ai, jnp.asarray(twr.reshape(n1, 1, n2), F32),
      jnp.asarray(twi.reshape(n1, 1, n2), F32), jnp.asarray(fr, F32), jnp.asarray(fi, F32), inv_norm)


def _hy_stage2_kernel(ar_ref, ai_ref, kr_ref, ki_ref, twr_ref, twi_ref, tcr_ref, tci_ref, fr_ref, fi_ref,
                      br_ref, bi_ref):
    f_re, f_im = fr_ref[...], fi_ref[...]
    n2 = f_re.shape[0]
    for j in range(N2_BLK):
        g = _twiddled(f_re, f_im, twr_ref[j], twi_ref[j])
        x = _dot_hi(g, jnp.concatenate([ar_ref[j], ai_ref[j]], axis=0))
        x_re, x_im = x[:n2], x[n2:]
        k_re, k_im = kr_ref[j], ki_ref[j]
        y = jnp.concatenate([x_re * k_re - x_im * k_im, x_re * k_im + x_im * k_re], axis=0)
        g_inv = _twiddled(f_re, -f_im, tcr_ref[j], -tci_ref[j])
        b = _dot_hi(g_inv, y)
        br_ref[j] = b[:n2]
        bi_ref[j] = b[n2:]


def _hy_stage2(ar, ai, kr, ki, n1):
    n2 = DFT_N2
    c = kr.shape[2]
    fr, fi = _dft_mats(n2, -1.0)
    twr, twi = _twiddle(n1, n2, -1.0)
    blk = pl.BlockSpec((N2_BLK, n2, c), lambda k: (k, 0, 0))
    tw = pl.BlockSpec((N2_BLK, 1, n2), lambda k: (k, 0, 0))
    twc = pl.BlockSpec((N2_BLK, n2, 1), lambda k: (k, 0, 0))
    out = jax.ShapeDtypeStruct((n1, n2, c), F32)
    return pl.pallas_call(
        _hy_stage2_kernel,
        grid=(n1 // N2_BLK,),
        in_specs=[blk, blk, blk, blk, tw, tw, twc, twc, _full((n2, n2)), _full((n2, n2))],
        out_specs=[blk, blk],
        out_shape=[out, out],
        compiler_params=_params("arbitrary"),
        name="hyena_conv_stage2",
    )(ar, ai, kr, ki,
      jnp.asarray(twr.reshape(n1, 1, n2), F32), jnp.asarray(twi.reshape(n1, 1, n2), F32),
      jnp.asarray(twr.reshape(n1, n2, 1), F32), jnp.asarray(twi.reshape(n1, n2, 1), F32),
      jnp.asarray(fr, F32), jnp.asarray(fi, F32))


def _hy_stage3_kernel(m_ref, br_ref, bi_ref, vg0_ref, vg1_ref, x00_ref, x01_ref, bias_ref, o_ref):
    r = vg0_ref.shape[0]
    bias = bias_ref[...]
    for j in range(N2_BLK):
        conv = _dot_hi(m_ref[...], jnp.concatenate([br_ref[:, j, :], bi_ref[:, j, :]], axis=0))
        o_ref[0, :r, j, :] = (conv[:r] + vg0_ref[:, j, :] * bias) * x00_ref[:, j, :]
        o_ref[1, :r, j, :] = (conv[r:] + vg1_ref[:, j, :] * bias) * x01_ref[:, j, :]
    if o_ref.shape[1] > r:
        o_ref[:, r:] = jnp.zeros((2, o_ref.shape[1] - r) + o_ref.shape[2:], F32)


def _hy_stage3(br, bi, vg4, x04, bias, n1, rows_total):
    _, r, n2, c = vg4.shape
    cr, ci = _dft_mats(n1, 1.0)
    m = jnp.asarray(_block_complex(cr[:r], ci[:r]) / (n1 * n2), F32)
    plane = lambda p: pl.BlockSpec((None, r, N2_BLK, c), lambda j: (p, 0, j, 0))
    spec = pl.BlockSpec((n1, N2_BLK, c), lambda j: (0, j, 0))
    return pl.pallas_call(
        _hy_stage3_kernel,
        grid=(n2 // N2_BLK,),
        in_specs=[_full(m.shape), spec, spec, plane(0), plane(1), plane(0), plane(1), _full((1, c))],
        out_specs=pl.BlockSpec((2, rows_total, N2_BLK, c), lambda j: (0, 0, j, 0)),
        out_shape=jax.ShapeDtypeStruct((2, rows_total, n2, c), F32),
        compiler_params=_params("arbitrary"),
        name="hyena_inv_stage3",
    )(m, br, bi, vg4, vg4, x04, x04, bias)


def _hyena_long(vg, x0, k2, inv_norm, bias, n_all):
    B, L, C = vg.shape
    assert B == 2
    n2 = DFT_N2
    n1 = 2 * L // n2
    r = L // n2
    fa_r, fa_i = _filter_stage1(k2.reshape(n1, n2, C), n1)
    kr, ki = _filter_stage2(fa_r, fa_i, inv_norm, n1)
    vg4 = vg.reshape(2, r, n2, C)
    ar, ai = _hy_stage1(vg4, n1)
    br, bi = _hy_stage2(ar, ai, kr, ki, n1)
    y = _hy_stage3(br, bi, vg4, x0.reshape(2, r, n2, C), bias.reshape(1, C), n1, n_all // n2)
    return y.reshape(2, n_all, C)


def _hy_ctx_kernel(vg_ref, x0_ref, kext_ref, sc_ref, bias_ref, buf_ref, o_ref):
    del buf_ref
    n = vg_ref.shape[0]

    def body(a, acc):
        src = vg_ref[pl.ds(pl.multiple_of(8 * a, 8), 8), :]
        win = pl.multiple_of(n - 8 * a, 8)
        for r in range(8):
            acc = acc + kext_ref[r, pl.ds(win, n), :] * src[r:r + 1]
        return acc

    acc = lax.fori_loop(0, n // 8, body, jnp.zeros(vg_ref.shape, F32))
    o_ref[...] = (acc * sc_ref[...] + vg_ref[...] * bias_ref[...]) * x0_ref[...]


def _hyena_ctx(vg, x0, k2, inv_norm, bias, buf, row0):
    B, n, C = vg.shape
    kext = jnp.concatenate([k2[n:], k2[:n]], axis=0)
    kext = jnp.stack([jnp.roll(kext, r, axis=0) for r in range(8)])
    blk = pl.BlockSpec((None, n, C), lambda b: (b, 0, 0))
    return pl.pallas_call(
        _hy_ctx_kernel,
        grid=(B,),
        in_specs=[blk, blk, _full((8, 2 * n, C)), _full((1, C)), _full((1, C)), pl.BlockSpec(memory_space=pl.ANY)],
        out_specs=pl.BlockSpec((None, n, C), lambda b: (b, row0 // n, 0)),
        out_shape=jax.ShapeDtypeStruct(buf.shape, F32),
        input_output_aliases={5: 0},
        compiler_params=_params("arbitrary"),
        name="hyena_ctx",
    )(vg, x0, kext, inv_norm, bias.reshape(1, C), buf)


def _fn_stage1_kernel(c_ref, s_ref, re_ref, im_ref, ar_ref, ai_ref):
    c, s = c_ref[...], s_ref[...]
    for j in range(N2_BLK):
        re, im = re_ref[:, j, :], im_ref[:, j, :]
        ar_ref[:, j, :] = _dot_hi(c, re) + _dot_hi(s, im)
        ai_ref[:, j, :] = _dot_hi(c, im) - _dot_hi(s, re)


def _fn_stage2_kernel(ar_ref, ai_ref, twr_ref, twi_ref, fr_ref, fi_ref, o_ref):
    f_re, f_im = fr_ref[...], fi_ref[...]
    n2 = f_re.shape[0]
    for j in range(N2_BLK):
        t_re, t_im = twr_ref[j], twi_ref[j]
        g_re = f_re * t_re - f_im * t_im
        g_im = f_re * t_im + f_im * t_re
        o_ref[:n2, j, :] = _dot_hi(g_re, ar_ref[j]) - _dot_hi(g_im, ai_ref[j])
    if o_ref.shape[0] > n2:
        o_ref[n2:] = jnp.zeros((o_ref.shape[0] - n2,) + o_ref.shape[1:], F32)


def _fnet_long(fre, fim, n_lat):
    B, S, C = fre.shape
    n2 = DFT_N2
    n1 = n_lat // n2
    c1, s1 = _dft_mats(n1, 1.0)
    rows = pl.BlockSpec((None, n1, N2_BLK, C), lambda b, j: (b, 0, j, 0))
    a_shape = jax.ShapeDtypeStruct((B, n1, n2, C), F32)
    ar, ai = pl.pallas_call(
        _fn_stage1_kernel,
        grid=(B, n2 // N2_BLK),
        in_specs=[_full((n1, n1)), _full((n1, n1)), rows, rows],
        out_specs=[rows, rows],
        out_shape=[a_shape, a_shape],
        compiler_params=_params("arbitrary", "arbitrary"),
        name="fnet_stage1",
    )(jnp.asarray(c1, F32), jnp.asarray(s1, F32), fre.reshape(B, S // n2, n2, C), fim.reshape(B, S // n2, n2, C))
    fr, fi = _dft_mats(n2, -1.0)
    scale = 1.0 / math.sqrt(n_lat * C)
    twr, twi = _twiddle(n1, n2, -1.0)
    blk = pl.BlockSpec((None, N2_BLK, n2, C), lambda b, k: (b, k, 0, 0))
    tw = pl.BlockSpec((N2_BLK, 1, n2), lambda b, k: (k, 0, 0))
    y = pl.pallas_call(
        _fn_stage2_kernel,
        grid=(B, n1 // N2_BLK),
        in_specs=[blk, blk, tw, tw, _full((n2, n2)), _full((n2, n2))],
        out_specs=pl.BlockSpec((None, S // n1, N2_BLK, C), lambda b, k: (b, 0, k, 0)),
        out_shape=jax.ShapeDtypeStruct((B, S // n1, n1, C), F32),
        compiler_params=_params("arbitrary", "arbitrary"),
        name="fnet_stage2",
    )(ar, ai, jnp.asarray(twr.reshape(n1, 1, n2), F32),
      jnp.asarray(twi.reshape(n1, 1, n2), F32), jnp.asarray(fr * scale, F32), jnp.asarray(fi * scale, F32))
    return y.reshape(B, S, C)


def _fn_ctx_kernel(c_ref, s_ref, re_ref, im_ref, buf_ref, o_ref):
    del buf_ref
    o_ref[...] = _dot_hi(c_ref[...], re_ref[...]) + _dot_hi(s_ref[...], im_ref[...])


def _fnet_ctx(fre, fim, n_lat, n_ctx, buf):
    B, S, C = fre.shape
    c1, s1 = _dft_mats(n_ctx, 1.0)
    scale = 1.0 / math.sqrt(n_ctx * C)
    blk = pl.BlockSpec((None, n_ctx, C), lambda b: (b, n_lat // n_ctx, 0))
    return pl.pallas_call(
        _fn_ctx_kernel,
        grid=(B,),
        in_specs=[_full((n_ctx, n_ctx)), _full((n_ctx, n_ctx)), blk, blk, pl.BlockSpec(memory_space=pl.ANY)],
        out_specs=blk,
        out_shape=jax.ShapeDtypeStruct((B, S, C), F32),
        input_output_aliases={4: 0},
        compiler_params=_params("arbitrary"),
        name="fnet_ctx",
    )(jnp.asarray(c1 * scale, F32), jnp.asarray(s1 * scale, F32), fre, fim, buf)


def _gla_kernel(qkf_ref, vf_ref, gf_ref, qkb_ref, vb_ref, gb_ref, of_ref, ob_ref, st_ref):
    n = pl.program_id(0)
    n_b, tm = qkf_ref.shape[0], qkf_ref.shape[1]
    ck = GLA_CHUNK

    @pl.when(n == 0)
    def _():
        st_ref[...] = jnp.zeros_like(st_ref)

    ri = lax.broadcasted_iota(jnp.int32, (ck, ck), 0)
    ci = lax.broadcasted_iota(jnp.int32, (ck, ck), 1)

    def sub_chunk(qk_ref, v_ref, g_ref, o_ref, states, r0, reverse):
        keep = (ci >= ri) if reverse else (ci <= ri)
        g = g_ref[r0:r0 + ck, :]
        b = _dot_hi(keep.astype(F32), g)
        b_end = b[0:1] if reverse else b[ck - 1:ck]
        q = qk_ref[r0:r0 + ck, :LANES] * jnp.exp(b)
        k = qk_ref[r0:r0 + ck, LANES:]
        k_in = k * jnp.exp(-b)
        k_out = k * jnp.exp(b_end - b)
        decay = jnp.exp(b_end)
        new_states = []
        for hd in range(GLA_HEADS):
            ks = slice(hd * GLA_DK, (hd + 1) * GLA_DK)
            vs = slice(hd * GLA_DV, (hd + 1) * GLA_DV)
            qh, vh = q[:, ks].astype(BF16), v_ref[r0:r0 + ck, vs].astype(BF16)
            a = lax.dot_general(qh, k_in[:, ks].astype(BF16), (((1,), (1,)), ((), ())), preferred_element_type=F32)
            a = jnp.where(keep, a, 0.0)
            st = states[hd]
            o = jnp.dot(a.astype(BF16), vh, preferred_element_type=F32)
            o += lax.dot_general(qh, st.astype(BF16), (((1,), (1,)), ((), ())), preferred_element_type=F32)
            o_ref[r0:r0 + ck, vs] = o
            kv = lax.dot_general(vh, k_out[:, ks].astype(BF16), (((0,), (0,)), ((), ())), preferred_element_type=F32)
            new_states.append(st * decay[:, ks] + kv)
        return new_states

    st_f = [[st_ref[0, b, hd] for hd in range(GLA_HEADS)] for b in range(n_b)]
    st_b = [[st_ref[1, b, hd] for hd in range(GLA_HEADS)] for b in range(n_b)]
    n_sub = tm // ck
    for i in range(n_sub):
        for b in range(n_b):
            st_f[b] = sub_chunk(qkf_ref.at[b], vf_ref.at[b], gf_ref.at[b], of_ref.at[b], st_f[b], i * ck, False)
            st_b[b] = sub_chunk(qkb_ref.at[b], vb_ref.at[b], gb_ref.at[b], ob_ref.at[b], st_b[b],
                                (n_sub - 1 - i) * ck, True)
    for b in range(n_b):
        for hd in range(GLA_HEADS):
            st_ref[0, b, hd] = st_f[b][hd]
            st_ref[1, b, hd] = st_b[b][hd]


def _gla(gqk, gv, gg, n_lat):
    B, S, _ = gqk.shape
    tm = TOK_TILE
    nl = n_lat // tm
    nt = S // tm
    assert nt == nl + 1
    fwd = lambda n: jnp.where(n == 0, nl, n - 1)
    bwd = lambda n: jnp.where(n == 0, nl, nl - n)
    out = jax.ShapeDtypeStruct((B, S, 256), F32)
    return pl.pallas_call(
        _gla_kernel,
        grid=(nt,),
        in_specs=[pl.BlockSpec((B, tm, 256), lambda n: (0, fwd(n), 0)),
                  pl.BlockSpec((B, tm, 256), lambda n: (0, fwd(n), 0)),
                  pl.BlockSpec((B, tm, 128), lambda n: (0, fwd(n), 0)),
                  pl.BlockSpec((B, tm, 256), lambda n: (0, bwd(n), 0)),
                  pl.BlockSpec((B, tm, 256), lambda n: (0, bwd(n), 0)),
                  pl.BlockSpec((B, tm, 128), lambda n: (0, bwd(n), 1))],
        out_specs=[pl.BlockSpec((B, tm, 256), lambda n: (0, fwd(n), 0)),
                   pl.BlockSpec((B, tm, 256), lambda n: (0, bwd(n), 0))],
        out_shape=[out, out],
        scratch_shapes=[pltpu.VMEM((2, B, GLA_HEADS, GLA_DV, GLA_DK), F32)],
        compiler_params=_params("arbitrary"),
        name="gla_scan",
    )(gqk, gv, gg, gqk, gv, gg)


def _mix_kernel(x_ref, mod_ref, a_ref, hy_ref, of_ref, ob_ref, og_ref, fy_ref, fw_ref, mg_ref, hm_ref, wo_ref,
                n2g_ref, rwh_ref, rwl_ref, rb_ref, xo_ref, h2_ref, ti_ref, tg_ref, rk_ref, cnt_ref, run_ref):
    first = jnp.logical_and(pl.program_id(0) == 0, pl.program_id(1) == 0)

    @pl.when(first)
    def _():
        run_ref[...] = jnp.zeros_like(run_ref)

    mod = mod_ref[...]
    mg = mg_ref[...]
    hm = hm_ref[...]

    def head_rms(t):
        sq_hi, sq_lo = _split(t * t)
        ms = jnp.dot(sq_hi, hm, preferred_element_type=F32) + jnp.dot(sq_lo, hm, preferred_element_type=F32)
        return t * lax.rsqrt(ms + EPS)

    a = head_rms(a_ref[...]) * mg[:, 0:256]
    hy = _rms_rows(hy_ref[...]) * mg[:, 256:512]
    og = og_ref[...]
    o = head_rms(of_ref[...] + ob_ref[...]) * mg[:, 512:768] * (og / (1.0 + jnp.exp(-og)))
    fn = _rms_rows(_dot(fy_ref[...], fw_ref[...])) * mg[:, 768:1024]
    wo = wo_ref[...]
    y = _dot(a, wo[0:256]) + _dot(hy, wo[256:512]) + _dot(o, wo[512:768]) + _dot(fn, wo[768:1024])
    x = x_ref[...] + mod[2:3] * y
    xo_ref[...] = x

    h2 = _rms_rows(x) * n2g_ref[...] * (1.0 + mod[4:5]) + mod[3:4]
    h2_ref[...] = _pack_pairs(h2)

    logits = _dot_x3(h2, rwh_ref[...], rwl_ref[...]) + rb_ref[...]
    tm = logits.shape[0]
    lane = lax.broadcasted_iota(jnp.int32, logits.shape, 1).astype(F32)
    idx_out = jnp.zeros(logits.shape, F32)
    val_out = jnp.zeros(logits.shape, F32)
    chosen = jnp.zeros(logits.shape, F32)
    picks = []
    top = None
    den = jnp.zeros((tm, 1), F32)
    for kk in range(TOP_K):
        m = jnp.max(logits, axis=-1, keepdims=True)
        idx = jnp.min(jnp.where(logits == m, lane, float(LANES)), axis=-1, keepdims=True)
        if top is None:
            top = m
        e = jnp.exp(m - top)
        den = den + e
        hit = lane == idx
        picks.append(hit)
        chosen = jnp.where(hit, 1.0, chosen)
        idx_out = jnp.where(lane == kk, idx, idx_out)
        val_out = jnp.where(lane == kk, e, val_out)
        logits = jnp.where(hit, -jnp.inf, logits)
    ti_ref[...] = idx_out.astype(jnp.int32)
    tg_ref[...] = val_out / den

    ri = lax.broadcasted_iota(jnp.int32, (tm, tm), 0)
    ci = lax.broadcasted_iota(jnp.int32, (tm, tm), 1)
    before = _dot((ci < ri).astype(F32), chosen) + run_ref[...]
    rank = jnp.zeros(logits.shape, F32)
    for kk in range(TOP_K):
        r = jnp.sum(jnp.where(picks[kk], before, 0.0), axis=-1, keepdims=True)
        rank = jnp.where(lane == kk, r, rank)
    rk_ref[...] = rank.astype(jnp.int32)
    run_ref[...] += jnp.sum(chosen, axis=0, keepdims=True)
    cnt_ref[...] = run_ref[...]


def _mix(xs, mod, a, hy, o_f, o_b, og, fy, fnet_w, mix_g, head_mean, w_out, n2g, rw, rb, n_lat_tiles):
    B, S, D = xs.shape
    tm = TOK_TILE
    rwh, rwl = _split(rw)
    tok = lambda w: pl.BlockSpec((None, tm, w), lambda b, t: (b, t, 0))
    return pl.pallas_call(
        _mix_kernel,
        grid=(B, S // tm),
        in_specs=[tok(D),
                  pl.BlockSpec((None, 6, D), lambda b, t: (jnp.where(t >= n_lat_tiles, 2, b), 0, 0)),
                  tok(256), tok(256), tok(256), tok(256), tok(256), tok(256),
                  _full((256, 256)), _full((1, D)), _full((256, 256)), _full((D, D)), _full((1, D)),
                  _full((D, LANES)), _full((D, LANES)), _full((1, LANES))],
        out_specs=[tok(D), tok(D // 2), tok(LANES), tok(LANES), tok(LANES), _full((1, LANES))],
        out_shape=[jax.ShapeDtypeStruct((B, S, D), F32), jax.ShapeDtypeStruct((B, S, D // 2), jnp.uint32),
                   jax.ShapeDtypeStruct((B, S, LANES), jnp.int32), jax.ShapeDtypeStruct((B, S, LANES), F32),
                   jax.ShapeDtypeStruct((B, S, LANES), jnp.int32), jax.ShapeDtypeStruct((1, LANES), F32)],
        scratch_shapes=[pltpu.VMEM((1, LANES), F32)],
        compiler_params=_params("arbitrary", "arbitrary"),
        name="mix_out_router",
    )(xs, mod, a, hy, o_f, o_b, og, fy, fnet_w, mix_g, head_mean.astype(BF16), w_out, n2g, rwh, rwl, rb)


def _route(top_i, rank, counts):
    T = top_i.shape[0]
    bm = MOE_BM
    padded = (counts + bm - 1) // bm * bm
    pend = jnp.cumsum(padded)
    pstart = pend - padded
    pos = (pstart[top_i] + rank).astype(jnp.int32)
    n_blocks = (T * TOP_K + bm - 1) // bm + N_EXPERTS
    blk_row0 = jnp.arange(n_blocks, dtype=pend.dtype) * bm
    blk_e = jnp.minimum(jnp.sum(pend[None, :] <= blk_row0[:, None], axis=1), N_EXPERTS - 1).astype(jnp.int32)
    n_used = (pend[-1] // bm).astype(jnp.int32).reshape(1)
    return pos, blk_e, n_used, n_blocks * bm


def _dispatch_kernel(pos_ref, h_ref, zero_ref, xg_ref, sem):
    del zero_ref
    tm = h_ref.shape[0]

    def copy(t, kk):
        return pltpu.make_async_copy(h_ref.at[pl.ds(t, 1)], xg_ref.at[pl.ds(pos_ref[0, t * TOP_K + kk], 1)], sem)

    def start(t, c):
        for kk in range(TOP_K):
            copy(t, kk).start()
        return c

    def wait(t, c):
        for kk in range(TOP_K):
            copy(t, kk).wait()
        return c

    lax.fori_loop(0, tm, start, 0, unroll=8)
    lax.fori_loop(0, tm, wait, 0, unroll=8)


def _dispatch(h2w, pos, n_rows):
    T, W = h2w.shape
    tm = TOK_TILE
    nb = T // tm
    return pl.pallas_call(
        _dispatch_kernel,
        grid=(nb,),
        in_specs=[pl.BlockSpec((None, 1, tm * TOP_K), lambda i: (i, 0, 0), memory_space=pltpu.SMEM),
                  pl.BlockSpec((tm, W), lambda i: (i, 0)),
                  pl.BlockSpec(memory_space=pl.ANY)],
        out_specs=pl.BlockSpec(memory_space=pl.ANY),
        out_shape=jax.ShapeDtypeStruct((n_rows, W), h2w.dtype),
        scratch_shapes=[pltpu.SemaphoreType.DMA(())],
        input_output_aliases={2: 0},
        compiler_params=_params("arbitrary"),
        name="moe_dispatch",
    )(pos.reshape(nb, 1, tm * TOP_K), h2w, jnp.zeros((n_rows, W), h2w.dtype))


def _gu_prep_kernel(w_ref, p_ref, g_ref, u_ref):
    y = jnp.dot(w_ref[...].astype(BF16), p_ref[...], preferred_element_type=F32)
    half = y.shape[1] // 2
    g_ref[...] = y[:, :half].astype(BF16)
    u_ref[...] = y[:, half:].astype(BF16)


def _cast_kernel(w_ref, o_ref):
    o_ref[...] = w_ref[...].astype(BF16)


def _down_prep(w_down, layer):
    _, E, F, D = w_down.shape
    return pl.pallas_call(
        _cast_kernel,
        grid=(E,),
        in_specs=[pl.BlockSpec((None, None, F, D), lambda e: (layer, e, 0, 0))],
        out_specs=pl.BlockSpec((None, F, D), lambda e: (e, 0, 0)),
        out_shape=jax.ShapeDtypeStruct((E, F, D), BF16),
        compiler_params=_params("arbitrary"),
        name="moe_down_prep",
    )(w_down)


def _gu_prep(w_gu, layer):
    _, E, D, F2 = w_gu.shape
    tn = 512
    perm = np.zeros((tn, tn), np.float32)
    perm[2 * np.arange(tn // 2), np.arange(tn // 2)] = 1.0
    perm[2 * np.arange(tn // 2) + 1, tn // 2 + np.arange(tn // 2)] = 1.0
    out = jax.ShapeDtypeStruct((E, D, F2 // 2), BF16)
    return pl.pallas_call(
        _gu_prep_kernel,
        grid=(E, F2 // tn),
        in_specs=[pl.BlockSpec((None, None, D, tn), lambda e, j: (layer, e, 0, j)), _full((tn, tn))],
        out_specs=[pl.BlockSpec((None, D, tn // 2), lambda e, j: (e, 0, j))] * 2,
        out_shape=[out, out],
        compiler_params=_params("arbitrary", "arbitrary"),
        name="moe_weight_prep",
    )(w_gu, jnp.asarray(perm, BF16))


def _expert_kernel(be_ref, nu_ref, x_ref, wg_ref, wu_ref, bg_ref, bu_ref, wd_ref, bd_ref, o_ref):
    i = pl.program_id(0)

    @pl.when(i < nu_ref[0])
    def _():
        x = jnp.concatenate(_unpack_pairs(x_ref[...]), axis=1).astype(BF16)
        gate = jnp.minimum(jnp.dot(x, wg_ref[...], preferred_element_type=F32) + bg_ref[...], SWIGLU_LIMIT)
        up = jnp.clip(jnp.dot(x, wu_ref[...], preferred_element_type=F32) + bu_ref[...], -SWIGLU_LIMIT, SWIGLU_LIMIT)
        glu = gate / (1.0 + jnp.exp(-gate * SWIGLU_ALPHA))
        o_ref[...] = _pack_pairs(_dot((up + 1.0) * glu, wd_ref[...]) + bd_ref[...])

    @pl.when(i >= nu_ref[0])
    def _():
        o_ref[...] = jnp.zeros_like(o_ref)


def _experts(xg, blk_e, n_used, wg, wu, bg, bu, wd, bd):
    n_rows, W = xg.shape
    bm = MOE_BM
    D, F = wg.shape[1], wg.shape[2]
    wsel = lambda r, c: pl.BlockSpec((None, r, c), lambda i, be, nu: (be[i], 0, 0))
    return pl.pallas_call(
        _expert_kernel,
        grid_spec=pltpu.PrefetchScalarGridSpec(
            num_scalar_prefetch=2,
            grid=(n_rows // bm,),
            in_specs=[pl.BlockSpec((bm, W), lambda i, be, nu: (i, 0)),
                      wsel(D, F), wsel(D, F), wsel(1, F), wsel(1, F), wsel(F, D), wsel(1, D)],
            out_specs=pl.BlockSpec((bm, W), lambda i, be, nu: (i, 0)),
        ),
        out_shape=jax.ShapeDtypeStruct((n_rows, W), jnp.uint32),
        compiler_params=_params("arbitrary"),
        name="moe_experts",
    )(blk_e, n_used, xg, wg, wu, bg, bu, wd, bd)


def _combine_kernel(pos_ref, posn_ref, x_ref, mod_ref, tg_ref, fg_ref, y_ref, o_ref, buf, sem, *, final):
    i = pl.program_id(0)
    tm = x_ref.shape[0]
    slot = i % 2

    def copy(p_ref, s, t, kk):
        return pltpu.make_async_copy(y_ref.at[pl.ds(p_ref[0, t * TOP_K + kk], 1)], buf.at[s, kk, pl.ds(t, 1)],
                                     sem.at[s])

    def fetch(p_ref, s):
        def body(t, c):
            for kk in range(TOP_K):
                copy(p_ref, s, t, kk).start()
            return c
        lax.fori_loop(0, tm, body, 0, unroll=8)

    @pl.when(i == 0)
    def _():
        fetch(pos_ref, 0)

    @pl.when(i + 1 < pl.num_programs(0))
    def _():
        fetch(posn_ref, 1 - slot)

    def wait(t, c):
        for kk in range(TOP_K):
            copy(pos_ref, slot, t, kk).wait()
        return c

    lax.fori_loop(0, tm, wait, 0, unroll=8)
    tg = tg_ref[...]
    f_lo = jnp.zeros((tm, x_ref.shape[1] // 2), F32)
    f_hi = f_lo
    for kk in range(TOP_K):
        y_lo, y_hi = _unpack_pairs(buf[slot, kk])
        f_lo = f_lo + tg[:, kk:kk + 1] * y_lo
        f_hi = f_hi + tg[:, kk:kk + 1] * y_hi
    x = x_ref[...] + mod_ref[...][5:6] * jnp.concatenate([f_lo, f_hi], axis=1)
    o_ref[...] = _rms_rows(x) * fg_ref[...] if final else x


def _combine(xs, mod, tg, yb, pos, n_lat_tiles, final_g=None):
    B, S, D = xs.shape
    tm = TOK_TILE
    nt = S // tm
    final = final_g is not None
    per_b = n_lat_tiles if final else nt
    nb = B * per_b
    tile = lambda i: (i // per_b) * nt + i % per_b
    pos3 = pos.reshape(B * nt, 1, tm * TOP_K)
    row = lambda w: pl.BlockSpec((tm, w), lambda i: (tile(i), 0))
    fg = final_g.reshape(1, D) if final else jnp.ones((1, D), F32)
    out = pl.pallas_call(
        functools.partial(_combine_kernel, final=final),
        grid=(nb,),
        in_specs=[pl.BlockSpec((None, 1, tm * TOP_K), lambda i: (tile(i), 0, 0), memory_space=pltpu.SMEM),
                  pl.BlockSpec((None, 1, tm * TOP_K), lambda i: (tile(jnp.minimum(i + 1, nb - 1)), 0, 0),
                               memory_space=pltpu.SMEM),
                  row(D),
                  pl.BlockSpec((None, 6, D),
                               lambda i: (jnp.where(i % per_b >= n_lat_tiles, 2, i // per_b), 0, 0)),
                  row(LANES),
                  _full((1, D)),
                  pl.BlockSpec(memory_space=pl.ANY)],
        out_specs=pl.BlockSpec((tm, D), lambda i: (i, 0)),
        out_shape=jax.ShapeDtypeStruct((nb * tm, D), F32),
        scratch_shapes=[pltpu.VMEM((2, TOP_K, tm, D // 2), jnp.uint32), pltpu.SemaphoreType.DMA((2,))],
        compiler_params=_params("arbitrary"),
        name="moe_combine",
    )(pos3, pos3, xs.reshape(B * S, D), mod, tg.reshape(B * S, LANES), fg, yb)
    return out.reshape(B, nb // B * tm, D)


def _prep_w_in(w_in):
    cq, ckv, kr = w_in[:, 0:256], w_in[:, 256:384], w_in[:, 384:416]
    hy, gq, gk = w_in[:, 416:1184], w_in[:, 1184:1312], w_in[:, 1312:1440]
    gv, glr, og, fn = w_in[:, 1440:1696], w_in[:, 1696:1728], w_in[:, 1728:1984], w_in[:, 1984:2240]
    half = MLA_ROPE // 2
    kr_sw = jnp.concatenate([-kr[:, half:], kr[:, :half]], axis=1)
    place = lambda w: jnp.tile(jnp.pad(w, ((0, 0), (MLA_NOPE, LANES - MLA_NOPE - MLA_ROPE))), (1, MLA_HEADS))
    glr_p = jnp.pad(glr, ((0, 0), (0, LANES - 2 * GLA_LR)))
    return jnp.concatenate([cq, ckv, place(kr), place(kr_sw), hy, gq, gk, gv, glr_p, og, fn], axis=1).astype(BF16)


def _prep_wq(w_uq):
    w = w_uq.reshape(MLA_Q_LORA, MLA_HEADS, MLA_NOPE + MLA_ROPE)
    nope, rope = w[..., :MLA_NOPE], w[..., MLA_NOPE:]
    half = MLA_ROPE // 2
    z_tail = jnp.zeros((MLA_Q_LORA, MLA_HEADS, LANES - MLA_NOPE - MLA_ROPE), F32)
    z_nope = jnp.zeros((MLA_Q_LORA, MLA_HEADS, MLA_NOPE), F32)
    plain = jnp.concatenate([nope, rope, z_tail], axis=-1).reshape(MLA_Q_LORA, MLA_HEADS * LANES)
    partner = jnp.concatenate([z_nope, -rope[..., half:], rope[..., :half], z_tail], axis=-1)
    return jnp.concatenate([plain, partner.reshape(MLA_Q_LORA, MLA_HEADS * LANES)], axis=1).astype(BF16)


def _prep_wkv(w_ukv):
    w = w_ukv.reshape(MLA_KV_LORA, MLA_HEADS, MLA_NOPE + MLA_V)
    k_nope, v = w[..., :MLA_NOPE], w[..., MLA_NOPE:]
    k_placed = jnp.pad(k_nope, ((0, 0), (0, 0), (0, LANES - MLA_NOPE))).reshape(MLA_KV_LORA, MLA_HEADS * LANES)
    v_t = jnp.pad(v, ((0, 0), (0, 0), (0, LANES - MLA_V))).reshape(MLA_KV_LORA, MLA_HEADS * LANES).T
    return k_placed.astype(BF16), v_t.astype(BF16)


def _prep_gk(gk_w, gk_b):
    w = jnp.zeros((LANES, 2 * LANES), F32)
    w = w.at[0:GLA_LR, 0:LANES].set(gk_w[0]).at[GLA_LR:2 * GLA_LR, LANES:].set(gk_w[1])
    return w.astype(BF16), jnp.concatenate([gk_b[0], gk_b[1]]).reshape(1, 2 * LANES)


def _rope_tables(n_lat, n_ctx):
    rows = n_lat // GRID_W
    row = jnp.repeat(jnp.arange(rows, dtype=F32), GRID_W)
    col = jnp.tile(jnp.arange(GRID_W, dtype=F32), rows)
    n_freq = MLA_ROPE // 4
    inv = ROPE_THETA ** (-jnp.arange(n_freq, dtype=F32) / n_freq)
    ang = jnp.concatenate([row[:, None] * inv, col[:, None] * inv], axis=-1)
    cos = jnp.concatenate([jnp.cos(ang), jnp.ones((n_ctx, MLA_ROPE // 2), F32)], axis=0)
    sin = jnp.concatenate([jnp.sin(ang), jnp.zeros((n_ctx, MLA_ROPE // 2), F32)], axis=0)
    S = n_lat + n_ctx
    ones, zeros = jnp.ones((S, MLA_NOPE), F32), jnp.zeros((S, MLA_NOPE), F32)
    tail = jnp.zeros((S, LANES - MLA_NOPE - MLA_ROPE), F32)
    q_scale = MLA_SCALE * math.log2(math.e)
    cq = jnp.concatenate([ones, cos, cos, tail], axis=1) * q_scale
    sq = jnp.concatenate([zeros, sin, sin, tail], axis=1) * q_scale
    ck = jnp.concatenate([zeros, cos, cos, tail], axis=1)
    sk = jnp.concatenate([zeros, sin, sin, tail], axis=1)
    return cq, sq, ck, sk


def kernel(x, c, ctx, c_ctx, ada_w, ada_b, norm1_g, norm2_g, w_in, mla_q_g, mla_w_uq, mla_kv_g, mla_w_ukv, hy_conv_w, hy_conv_b, hy_w1, hy_b1, hy_w2, hy_b2, hy_w3, hy_freq, hy_bias, gla_gk_w, gla_gk_b, fnet_w, mix_g, w_out, router_w, router_b, moe_w_gu, moe_b_gu, moe_w_down, moe_b_down, final_g):
    B, L, D = x.shape
    Lc = ctx.shape[1]
    depth = ada_w.shape[0]
    S = L + Lc
    tm = TOK_TILE
    nlt = L // tm
    assert B == 2 and D == D_MODEL and Lc == tm and L % (2 * DFT_N2 * 8) == 0

    xs = jnp.concatenate([x, ctx], axis=1)
    cc = jnp.zeros((8, D), F32).at[0:B].set(c).at[B].set(c_ctx)
    mods = _mods(cc, ada_w, ada_b)
    tabs = _rope_tables(L, Lc)
    jc = np.arange(GROUP_W)
    ang_c = 2.0 * np.pi * ((jc[:, None] * jc[None, :]) % GROUP_W) / GROUP_W
    csc = jnp.asarray(np.concatenate([np.cos(ang_c), -np.sin(ang_c)], axis=1), F32)
    head_mean = jnp.asarray(np.kron(np.eye(GROUP_W // MLA_V), np.full((MLA_V, MLA_V), 1.0 / MLA_V)), F32)

    for i in range(depth):
        mod = mods[i].reshape(8, 6, D)
        gkw, gkb = _prep_gk(gla_gk_w[i], gla_gk_b[i])
        q, k, v, hyz, gqk, gv, gg, og, fre, fim = _proj(
            xs, mod, norm1_g[i].reshape(1, D), _prep_w_in(w_in[i]), mla_q_g[i].reshape(1, -1), _prep_wq(mla_w_uq[i]),
            mla_kv_g[i].reshape(1, -1), *_prep_wkv(mla_w_ukv[i]), gkw, gkb, csc, tabs, nlt)

        a = _attention(q, k, v, L, Lc)

        filt = (hy_w1[i], hy_b1[i], hy_w2[i], hy_b2[i], hy_w3[i], hy_freq[i])
        vg_l, x0_l = _hy_pre(hyz, hy_conv_w[i], hy_conv_b[i], 0, L)
        hy = _hyena_long(vg_l, x0_l, *_hyena_filter(L, *filt), hy_bias[i], S)
        vg_c, x0_c = _hy_pre(hyz, hy_conv_w[i], hy_conv_b[i], L, Lc)
        hy = _hyena_ctx(vg_c, x0_c, *_hyena_filter(Lc, *filt), hy_bias[i], hy, L)

        o_f, o_b = _gla(gqk, gv, gg, L)

        fy = _fnet_ctx(fre, fim, L, Lc, _fnet_long(fre, fim, L))

        rw = jnp.pad(router_w[i], ((0, 0), (0, LANES - N_EXPERTS)))
        rb = jnp.concatenate([router_b[i], jnp.full((LANES - N_EXPERTS,), -1e30, F32)]).reshape(1, LANES)
        xs, h2w, top_i, top_g, rank, cnt = _mix(
            xs, mod, a, hy, o_f, o_b, og, fy, fnet_w[i].astype(BF16), mix_g[i].reshape(1, D), head_mean,
            w_out[i].astype(BF16), norm2_g[i].reshape(1, D), rw, rb, nlt)

        pos, blk_e, n_used, n_rows = _route(top_i.reshape(B * S, LANES)[:, :TOP_K],
                                            rank.reshape(B * S, LANES)[:, :TOP_K],
                                            cnt[0, :N_EXPERTS].astype(jnp.int32))
        xg = _dispatch(h2w.reshape(B * S, D // 2), pos, n_rows)
        wg, wu = _gu_prep(moe_w_gu, i)
        yb = _experts(xg, blk_e, n_used, wg, wu, moe_b_gu[i][:, None, 0::2], moe_b_gu[i][:, None, 1::2],
                      _down_prep(moe_w_down, i), moe_b_down[i][:, None, :])
        xs = _combine(xs, mod, top_g, yb, pos, nlt, final_g if i == depth - 1 else None)

    return xs
```
